```python
import math
import jax, jax.numpy as jnp
from jax import lax
import numpy as np

D_MODEL = 1024
BATCH = 8
SEQ = 8192
DEPTH = 2

HEAD_DIM = 64
NSA_HEADS = 8
NSA_KV_HEADS = 2
NSA_GROUP = NSA_HEADS // NSA_KV_HEADS
RET_HEADS = 8
CMP_BLOCK = 32
CMP_STRIDE = 16
CMP_HIDDEN = 256
SEL_BLOCK = 64
N_SELECT = 16
N_LOCAL_SEL = 2
WINDOW = 512
Q_BLOCK = 128
RET_CHUNK = 128
D_FF = 3584
N_EXPERTS = 8
TOP_K = 2
PLE_DIM = 256
EPS = 1e-6
NEG_INF = -1e30
BIG = 1e9
N_DENSE = (DEPTH + 1) // 2
N_MOE = DEPTH // 2

NSA_Q_COLS = NSA_HEADS * HEAD_DIM
NSA_KV_COLS = NSA_KV_HEADS * HEAD_DIM
NSA_GATE_COLS = 3 * NSA_HEADS
RET_COLS = RET_HEADS * HEAD_DIM
SPLIT_SIZES = (NSA_Q_COLS,) + (NSA_KV_COLS,) * 6 + (NSA_GATE_COLS,) + (RET_COLS,) * 4
IN_COLS = NSA_Q_COLS + 6 * NSA_KV_COLS + NSA_GATE_COLS + 4 * RET_COLS
MIX_WIDTH = NSA_Q_COLS + RET_COLS

kernel_name = "hybrid_nsa_retention_moe_ple"


def rmsnorm(x, g):
    xf = x.astype(jnp.float32)
    y = xf * lax.rsqrt(jnp.mean(xf * xf, axis=-1, keepdims=True) + EPS)
    return (y * g.astype(jnp.float32)).astype(x.dtype)


def alibi_slopes(n):
    return jnp.exp2(-8.0 * jnp.arange(1, n + 1, dtype=jnp.float32) / n)


def masked_softmax(logits, valid):
    l = jnp.where(valid, logits, NEG_INF)
    m = jnp.max(l, axis=-1, keepdims=True)
    e = jnp.where(valid, jnp.exp(l - m), 0.0)
    return e / jnp.maximum(jnp.sum(e, axis=-1, keepdims=True), 1e-30)


def compress(k, pos, w1, w2):
    b, g, s, d = k.shape
    r = CMP_BLOCK // CMP_STRIDE
    nch = s // CMP_STRIDE
    nc = nch - r + 1
    chunks = k.reshape(b, g, nch, CMP_STRIDE, d)
    w1r = w1.reshape(r, CMP_STRIDE, d, CMP_HIDDEN)
    u = jnp.einsum('bgncd,rcdh->bgnrh', chunks, w1r)
    hid = jnp.einsum('ld,ldh->h', pos, w1)
    for a in range(r):
        hid = hid + u[:, :, a:a + nc, a, :]
    return jnp.einsum('bgnh,hd->bgnd', jax.nn.gelu(hid), w2)


def nsa_mixer(q, kc, vc, ks, vs, kw, vw, gate_logits, cmp_pos, cmp_w1, cmp_w2):
    f32 = jnp.float32
    b, s, _, d = q.shape
    G, R = NSA_KV_HEADS, NSA_GROUP
    q = q.astype(f32) * (d ** -0.5)
    to_bgsd = lambda t: jnp.transpose(t.astype(f32), (0, 2, 1, 3))
    kc, vc, ks, vs, kw, vw = [to_bgsd(t) for t in (kc, vc, ks, vs, kw, vw)]
    k_cmp = compress(kc, cmp_pos[0].astype(f32), cmp_w1[0].astype(f32), cmp_w2[0].astype(f32))
    v_cmp = compress(vc, cmp_pos[1].astype(f32), cmp_w1[1].astype(f32), cmp_w2[1].astype(f32))
    nc = k_cmp.shape[2]
    nsel = s // SEL_BLOCK
    n_pick = min(N_SELECT, nsel)
    cmp_start = jnp.arange(nc) * CMP_STRIDE
    cmp_end = cmp_start + CMP_BLOCK - 1
    sel_start = jnp.arange(nsel) * SEL_BLOCK
    overlap = (jnp.minimum(cmp_start[:, None] + CMP_BLOCK, sel_start[None, :] + SEL_BLOCK)
               > jnp.maximum(cmp_start[:, None], sel_start[None, :])).astype(f32)
    ks_blk = ks.reshape(b, G, nsel, SEL_BLOCK, d)
    vs_blk = vs.reshape(b, G, nsel, SEL_BLOCK, d)
    kw_pad = jnp.pad(kw, ((0, 0), (0, 0), (WINDOW, 0), (0, 0)))
    vw_pad = jnp.pad(vw, ((0, 0), (0, 0), (WINDOW, 0), (0, 0)))
    slopes = alibi_slopes(NSA_HEADS).reshape(G, R)
    gates = jax.nn.sigmoid(gate_logits.astype(f32)).reshape(b, s, 3, G, R)
    gather = jax.vmap(jax.vmap(lambda blocks, ix: blocks[ix]))
    blk = jnp.arange(nsel)

    def block(i):
        t0 = i * Q_BLOCK
        t = t0 + jnp.arange(Q_BLOCK)
        qb = lax.dynamic_slice_in_dim(q, t0, Q_BLOCK, axis=1).reshape(b, Q_BLOCK, G, R, d)
        dist_c = t[:, None] - cmp_end[None, :]
        s_c = jnp.einsum('bqgrd,bgnd->bgrqn', qb, k_cmp)
        p_c = masked_softmax(s_c - slopes[:, :, None, None] * dist_c.astype(f32), dist_c >= 0)
        o_c = jnp.einsum('bgrqn,bgnd->bgrqd', p_c, v_cmp)
        imp = jnp.einsum('bgrqn,nm->bgqm', p_c, overlap)
        back = (t // SEL_BLOCK)[:, None] - blk[None, :]
        forced = (blk[None, :] == 0) | ((back >= 0) & (back < N_LOCAL_SEL))
        score = jnp.where(forced, BIG, jnp.where(back >= 0, imp, -BIG))
        _, idx = lax.top_k(score, n_pick)
        k_pick = gather(ks_blk, idx)
        v_pick = gather(vs_blk, idx)
        pos = idx[..., None] * SEL_BLOCK + jnp.arange(SEL_BLOCK)
        dist_s = t[None, None, :, None, None] - pos
        s_s = jnp.einsum('bqgrd,bgqkld->bgrqkl', qb, k_pick)
        logit_s = (s_s - slopes[None, :, :, None, None, None] * dist_s[:, :, None].astype(f32))
        logit_s = logit_s.reshape(b, G, R, Q_BLOCK, n_pick * SEL_BLOCK)
        valid_s = (dist_s >= 0)[:, :, None].reshape(b, G, 1, Q_BLOCK, n_pick * SEL_BLOCK)
        p_s = masked_softmax(logit_s, valid_s).reshape(b, G, R, Q_BLOCK, n_pick, SEL_BLOCK)
        o_s = jnp.einsum('bgrqkl,bgqkld->bgrqd', p_s, v_pick)
        kwb = lax.dynamic_slice_in_dim(kw_pad, t0, WINDOW + Q_BLOCK, axis=2)
        vwb = lax.dynamic_slice_in_dim(vw_pad, t0, WINDOW + Q_BLOCK, axis=2)
        kpos = t0 - WINDOW + jnp.arange(WINDOW + Q_BLOCK)
        dist_w = t[:, None] - kpos[None, :]
        valid_w = (dist_w >= 0) & (dist_w < WINDOW) & (kpos[None, :] >= 0)
        s_w = jnp.einsum('bqgrd,bgkd->bgrqk', qb, kwb)
        p_w = masked_softmax(s_w - slopes[:, :, None, None] * dist_w.astype(f32), valid_w)
        o_w = jnp.einsum('bgrqk,bgkd->bgrqd', p_w, vwb)
        gb = lax.dynamic_slice_in_dim(gates, t0, Q_BLOCK, axis=1)
        gb = jnp.transpose(gb, (2, 0, 3, 4, 1))[..., None]
        o = gb[0] * o_c + gb[1] * o_s + gb[2] * o_w
        return jnp.transpose(o, (0, 3, 1, 2, 4)).reshape(b, Q_BLOCK, NSA_HEADS * d)

    out = lax.map(block, jnp.arange(s // Q_BLOCK))
    return jnp.transpose(out, (1, 0, 2, 3)).reshape(b, s, NSA_HEADS * d)


def retention(q, k, v, g, gn_gain):
    f32 = jnp.float32
    b, s, h, d = q.shape
    C = RET_CHUNK
    n = s // C
    log_gamma = jnp.log1p(-jnp.exp2(-5.0 - jnp.arange(h, dtype=f32)))
    pos = jnp.arange(C, dtype=f32)
    diff = pos[:, None] - pos[None, :]
    decay_in = jnp.where(diff >= 0, jnp.exp(jnp.maximum(diff, 0.0)[None] * log_gamma[:, None, None]), 0.0)
    xi = jnp.exp((pos + 1.0)[None] * log_gamma[:, None])[..., None]
    zeta = jnp.exp((C - 1.0 - pos)[None] * log_gamma[:, None])[..., None]
    g_chunk = jnp.exp(C * log_gamma)[:, None, None]

    def chunks(t):
        return jnp.transpose(t.astype(f32).reshape(b, n, C, h, d), (1, 0, 3, 2, 4))

    qc, kc, vc = chunks(q), chunks(k) * (d ** -0.5), chunks(v)

    def step(state, inp):
        qi, ki, vi = inp
        inner = jnp.einsum('bhnd,bhmd->bhnm', qi, ki) * decay_in
        o = (jnp.einsum('bhnm,bhme->bhne', inner, vi)
             + jnp.einsum('bhnd,bhde->bhne', qi, state) * xi)
        state = g_chunk * state + jnp.einsum('bhmd,bhme->bhde', ki * zeta, vi)
        return state, o

    state0 = jnp.zeros((b, h, d, d), f32)
    _, o = lax.scan(step, state0, (qc, kc, vc))
    o = jnp.transpose(o, (1, 0, 3, 2, 4)).reshape(b, s, h, d)
    mu = jnp.mean(o, axis=-1, keepdims=True)
    var = jnp.mean(jnp.square(o - mu), axis=-1, keepdims=True)
    o = ((o - mu) * lax.rsqrt(var + EPS)).reshape(b, s, h * d) * gn_gain.astype(f32)
    return (jax.nn.silu(g.astype(f32)).reshape(b, s, h * d) * o).astype(q.dtype)


def swiglu(x, wg, wu, wd):
    return (jax.nn.silu(x @ wg) * (x @ wu)) @ wd


def moe(x, router, wg, wu, wd):
    b, s, dm = x.shape
    xt = x.reshape(-1, dm)
    logits = (xt @ router).astype(jnp.float32)
    top_v, top_i = lax.top_k(logits, TOP_K)
    w = jax.nn.softmax(top_v, axis=-1)
    gate = jnp.sum(jax.nn.one_hot(top_i, N_EXPERTS, dtype=jnp.float32) * w[..., None], axis=1)
    out = jnp.zeros(xt.shape, jnp.float32)
    for e in range(N_EXPERTS):
        out = out + gate[:, e:e + 1] * swiglu(xt, wg[e], wu[e], wd[e]).astype(jnp.float32)
    return out.astype(x.dtype).reshape(b, s, dm)


def setup_inputs(seed: int = 0) -> dict:
    key = jax.random.key(seed)
    k = jax.random.split(key, 21)
    nrm = lambda kk, shape, scale: jax.random.normal(kk, shape, jnp.float32) * scale
    gain = lambda kk, shape: 1.0 + 0.01 * jax.random.normal(kk, shape, jnp.float32)
    return {
        "x": nrm(k[0], (BATCH, SEQ, D_MODEL), 1.0),
        "p": nrm(k[1], (DEPTH, BATCH, SEQ, PLE_DIM), 1.0),
        "w_in": nrm(k[2], (DEPTH, D_MODEL, IN_COLS), D_MODEL ** -0.5),
        "w_out": nrm(k[3], (DEPTH, MIX_WIDTH, D_MODEL), MIX_WIDTH ** -0.5),
        "g_mix": gain(k[4], (DEPTH, D_MODEL)),
        "g_ffn": gain(k[5], (DEPTH, D_MODEL)),
        "g_ple": gain(k[6], (DEPTH, D_MODEL)),
        "g_final": gain(k[7], (D_MODEL,)),
        "cmp_pos": nrm(k[8], (DEPTH, 2, CMP_BLOCK, HEAD_DIM), 0.1),
        "cmp_w1": nrm(k[9], (DEPTH, 2, CMP_BLOCK, HEAD_DIM, CMP_HIDDEN), (CMP_BLOCK * HEAD_DIM) ** -0.5),
        "cmp_w2": nrm(k[10], (DEPTH, 2, CMP_HIDDEN, HEAD_DIM), CMP_HIDDEN ** -0.5),
        "ret_gn": gain(k[11], (DEPTH, RET_COLS)),
        "ffn_gate": nrm(k[12], (N_DENSE, D_MODEL, D_FF), D_MODEL ** -0.5),
        "ffn_up": nrm(k[13], (N_DENSE, D_MODEL, D_FF), D_MODEL ** -0.5),
        "ffn_down": nrm(k[14], (N_DENSE, D_FF, D_MODEL), D_FF ** -0.5),
        "moe_router": nrm(k[15], (N_MOE, D_MODEL, N_EXPERTS), D_MODEL ** -0.5),
        "moe_gate": nrm(k[16], (N_MOE, N_EXPERTS, D_MODEL, D_FF), D_MODEL ** -0.5),
        "moe_up": nrm(k[17], (N_MOE, N_EXPERTS, D_MODEL, D_FF), D_MODEL ** -0.5),
        "moe_down": nrm(k[18], (N_MOE, N_EXPERTS, D_FF, D_MODEL), D_FF ** -0.5),
        "ple_proj": nrm(k[19], (DEPTH, PLE_DIM, D_MODEL), PLE_DIM ** -0.5),
        "ple_gate": nrm(k[20], (DEPTH, D_MODEL, D_MODEL), D_MODEL ** -0.5),
    }


def reference(x, p, w_in, w_out, g_mix, g_ffn, g_ple, g_final, cmp_pos, cmp_w1, cmp_w2, ret_gn,
              ffn_gate, ffn_up, ffn_down, moe_router, moe_gate, moe_up, moe_down, ple_proj, ple_gate):
    b, s, _ = x.shape
    split_points = np.cumsum(SPLIT_SIZES)[:-1].tolist()
    h = x
    for i in range(DEPTH):
        hn = rmsnorm(h, g_mix[i])
        z = hn @ w_in[i]
        (q_n, kc, vc, ks, vs, kw, vw, gl, q_r, k_r, v_r, g_r) = jnp.split(z, split_points, axis=-1)
        kvr = lambda t: t.reshape(b, s, NSA_KV_HEADS, HEAD_DIM)
        rr = lambda t: t.reshape(b, s, RET_HEADS, HEAD_DIM)
        a = nsa_mixer(q_n.reshape(b, s, NSA_HEADS, HEAD_DIM), kvr(kc), kvr(vc), kvr(ks), kvr(vs),
                      kvr(kw), kvr(vw), gl, cmp_pos[i], cmp_w1[i], cmp_w2[i]).astype(h.dtype)
        r = retention(rr(q_r), rr(k_r), rr(v_r), rr(g_r), ret_gn[i])
        h = h + jnp.concatenate([a, r], axis=-1) @ w_out[i]
        hn = rmsnorm(h, g_ffn[i])
        if i % 2 == 0:
            f = swiglu(hn, ffn_gate[i // 2], ffn_up[i // 2], ffn_down[i // 2])
        else:
            f = moe(hn, moe_router[i // 2], moe_gate[i // 2], moe_up[i // 2], moe_down[i // 2])
        h = h + f
        h = h + (p[i] @ ple_proj[i]) * jax.nn.sigmoid(rmsnorm(h, g_ple[i]) @ ple_gate[i])
    return rmsnorm(h, g_final)
```

```python
import functools

import jax
import jax.numpy as jnp
from jax import lax
from jax.experimental import pallas as pl
from jax.experimental.pallas import tpu as pltpu

F32 = jnp.float32
BF16 = jnp.bfloat16

D_MODEL = 1024
HEAD_DIM = 64
NSA_HEADS = 8
NSA_KV_HEADS = 2
NSA_GROUP = NSA_HEADS // NSA_KV_HEADS
RET_HEADS = 8
CMP_BLOCK = 32
CMP_STRIDE = 16
CMP_HIDDEN = 256
SEL_BLOCK = 64
N_SELECT = 16
N_LOCAL_SEL = 2
WINDOW = 512
Q_BLOCK = 128
RET_CHUNK = 128
D_FF = 3584
N_EXPERTS = 8
TOP_K = 2
PLE_DIM = 256
EPS = 1e-6
NEG_INF = -1e30
BIG = 1e9

NSA_Q_COLS = NSA_HEADS * HEAD_DIM
NSA_KV_COLS = NSA_KV_HEADS * HEAD_DIM
NSA_GATE_COLS = 3 * NSA_HEADS
RET_COLS = RET_HEADS * HEAD_DIM
MIX_WIDTH = NSA_Q_COLS + RET_COLS

V7X_LANES = 128
V7X_VMEM_BYTES = 64 * 1024 * 1024
VMEM_LIMIT = V7X_VMEM_BYTES * 3 // 4

ROW_TILE = 512
SEL_KV_TILE = 512
FFN_ROW_TILE = 1024
FFN_COL_TILE = 512


def _params(sem):
    return pltpu.CompilerParams(dimension_semantics=sem, vmem_limit_bytes=VMEM_LIMIT)


def _dot(a, b):
    return jnp.dot(a, b, preferred_element_type=F32)


def _dot_nt(a, b):
    return lax.dot_general(a, b, (((1,), (1,)), ((), ())), preferred_element_type=F32)


def _dot_tn(a, b):
    return lax.dot_general(a, b, (((0,), (0,)), ((), ())), preferred_element_type=F32)


def _rms(x, g):
    return x * lax.rsqrt(jnp.mean(x * x, axis=-1, keepdims=True) + EPS) * g


def _in_proj_kernel(h_ref, g_ref, w_ref, qn_ref, kcv_ref, kv4_ref, gl_ref, ret_ref):
    xn = _rms(h_ref[0], g_ref[...]).astype(BF16)
    pair = 2 * HEAD_DIM
    for j in range(NSA_HEADS // 2):
        z = _dot(xn, w_ref[:, j * pair:(j + 1) * pair]).astype(BF16)
        qn_ref[0, 2 * j] = z[:, :HEAD_DIM]
        qn_ref[0, 2 * j + 1] = z[:, HEAD_DIM:]
    base = NSA_Q_COLS
    for a in range(2):
        kcv_ref[a, 0] = _dot(xn, w_ref[:, base + a * pair:base + (a + 1) * pair]).astype(BF16)
    base += 2 * pair
    for a in range(4):
        z = _dot(xn, w_ref[:, base + a * pair:base + (a + 1) * pair]).astype(BF16)
        kv4_ref[a, 0, 0] = z[:, :HEAD_DIM]
        kv4_ref[a, 0, 1] = z[:, HEAD_DIM:]
    base += 4 * pair
    for a in range(4):
        ret_ref[0, :, a * RET_COLS:(a + 1) * RET_COLS] = _dot(
            xn, w_ref[:, base + a * RET_COLS:base + (a + 1) * RET_COLS]).astype(BF16)
    base += 4 * RET_COLS
    gl_ref[0] = _dot(xn, w_ref[:, base:base + NSA_KV_HEADS * V7X_LANES])


def _arrange_w_in(w):
    q_end = NSA_Q_COLS
    kv_end = q_end + 6 * NSA_KV_COLS
    gl_end = kv_end + NSA_GATE_COLS
    gl = w[:, kv_end:gl_end].reshape(D_MODEL, 3, NSA_KV_HEADS, NSA_GROUP)
    gl = jnp.transpose(gl, (0, 2, 1, 3)).reshape(D_MODEL, NSA_KV_HEADS, 3 * NSA_GROUP)
    gl = jnp.pad(gl, ((0, 0), (0, 0), (0, V7X_LANES - 3 * NSA_GROUP)))
    gl = gl.reshape(D_MODEL, NSA_KV_HEADS * V7X_LANES)
    return jnp.concatenate([w[:, :kv_end], w[:, gl_end:], gl], axis=1).astype(BF16)


def _in_proj(h, g, w):
    b, s, d = h.shape
    tm = min(ROW_TILE, s)
    ncols = w.shape[1]
    return pl.pallas_call(
        _in_proj_kernel,
        grid=(b, s // tm),
        in_specs=[
            pl.BlockSpec((1, tm, d), lambda bi, si: (bi, si, 0)),
            pl.BlockSpec((1, d), lambda bi, si: (0, 0)),
            pl.BlockSpec((d, ncols), lambda bi, si: (0, 0)),
        ],
        out_specs=[
            pl.BlockSpec((1, NSA_HEADS, tm, HEAD_DIM), lambda bi, si: (bi, 0, si, 0)),
            pl.BlockSpec((2, 1, tm, NSA_KV_COLS), lambda bi, si: (0, bi, si, 0)),
            pl.BlockSpec((4, 1, NSA_KV_HEADS, tm, HEAD_DIM), lambda bi, si: (0, bi, 0, si, 0)),
            pl.BlockSpec((1, tm, NSA_KV_HEADS * V7X_LANES), lambda bi, si: (bi, si, 0)),
            pl.BlockSpec((1, tm, 4 * RET_COLS), lambda bi, si: (bi, si, 0)),
        ],
        out_shape=[
            jax.ShapeDtypeStruct((b, NSA_HEADS, s, HEAD_DIM), BF16),
            jax.ShapeDtypeStruct((2, b, s, NSA_KV_COLS), BF16),
            jax.ShapeDtypeStruct((4, b, NSA_KV_HEADS, s, HEAD_DIM), BF16),
            jax.ShapeDtypeStruct((b, s, NSA_KV_HEADS * V7X_LANES), F32),
            jax.ShapeDtypeStruct((b, s, 4 * RET_COLS), BF16),
        ],
        compiler_params=_params(("parallel", "parallel")),
        name="in_proj",
    )(h, g.reshape(1, d), w)


def _compress_kernel(x_ref, wbig_ref, pos_ref, w1_ref, w2_ref, o_ref):
    nch = x_ref.shape[2]
    u = _dot(x_ref[0, 0], wbig_ref[0])
    hid0 = _dot(pos_ref[0], w1_ref[0])[0:1]
    w2 = w2_ref[0]
    for g in range(NSA_KV_HEADS):
        c0 = g * 2 * CMP_HIDDEN
        first = u[:, c0:c0 + CMP_HIDDEN]
        second = u[:, c0 + CMP_HIDDEN:c0 + 2 * CMP_HIDDEN]
        hid = first + pltpu.roll(second, nch - 1, 0) + hid0
        o_ref[0, 0, g] = _dot(jax.nn.gelu(hid).astype(BF16), w2).astype(BF16)


def _arrange_cmp_w1(w1):
    r = CMP_BLOCK // CMP_STRIDE
    w1r = w1.reshape(2, r, CMP_STRIDE, HEAD_DIM, CMP_HIDDEN)
    eye = jnp.eye(NSA_KV_HEADS, dtype=w1.dtype)
    big = jnp.einsum('krcdh,gf->kcgdfrh', w1r, eye)
    return big.reshape(2, CMP_STRIDE * NSA_KV_COLS, NSA_KV_HEADS * r * CMP_HIDDEN).astype(BF16)


def _compress(kcv, cmp_pos, cmp_w1, cmp_w2):
    _, b, s, _ = kcv.shape
    nch = s // CMP_STRIDE
    x = kcv.reshape(2, b, nch, CMP_STRIDE * NSA_KV_COLS)
    wbig = _arrange_cmp_w1(cmp_w1)
    pos = jnp.broadcast_to(cmp_pos.reshape(2, 1, CMP_BLOCK * HEAD_DIM), (2, 8, CMP_BLOCK * HEAD_DIM)).astype(BF16)
    w1 = cmp_w1.reshape(2, CMP_BLOCK * HEAD_DIM, CMP_HIDDEN).astype(BF16)
    w2 = cmp_w2.astype(BF16)
    kdim = CMP_STRIDE * NSA_KV_COLS
    return pl.pallas_call(
        _compress_kernel,
        grid=(2, b),
        in_specs=[
            pl.BlockSpec((1, 1, nch, kdim), lambda a, bi: (a, bi, 0, 0)),
            pl.BlockSpec((1, kdim, wbig.shape[2]), lambda a, bi: (a, 0, 0)),
            pl.BlockSpec((1, 8, CMP_BLOCK * HEAD_DIM), lambda a, bi: (a, 0, 0)),
            pl.BlockSpec((1, CMP_BLOCK * HEAD_DIM, CMP_HIDDEN), lambda a, bi: (a, 0, 0)),
            pl.BlockSpec((1, CMP_HIDDEN, HEAD_DIM), lambda a, bi: (a, 0, 0)),
        ],
        out_specs=pl.BlockSpec((1, 1, NSA_KV_HEADS, nch, HEAD_DIM), lambda a, bi: (a, bi, 0, 0, 0)),
        out_shape=jax.ShapeDtypeStruct((2, b, NSA_KV_HEADS, nch, HEAD_DIM), BF16),
        compiler_params=_params(("parallel", "parallel")),
        name="compress",
    )(x, wbig, pos, w1, w2)


def _softmax_rows(logit, valid):
    m = jnp.max(logit, axis=-1, keepdims=True)
    e = jnp.where(valid, jnp.exp(logit - m), 0.0)
    return e * (1.0 / jnp.maximum(jnp.sum(e, axis=-1, keepdims=True), 1e-30))


def _attn_kernel(q_ref, kc_ref, vc_ref, ks_ref, vs_ref, kw_ref, vw_ref, gl_ref, slope_ref, ov_ref, o_ref,
                 *, n_pick, tk):
    rows = NSA_GROUP * Q_BLOCK
    t0 = pl.program_id(2) * Q_BLOCK
    q = q_ref[0].reshape(rows, HEAD_DIM) * jnp.asarray(HEAD_DIM ** -0.5, BF16)
    slope = slope_ref[0]
    tq = t0 + (lax.broadcasted_iota(jnp.int32, (rows, 1), 0) & (Q_BLOCK - 1))

    nc = kc_ref.shape[3]
    s_c = _dot_nt(q, kc_ref[0, 0, 0])
    cmp_end = lax.broadcasted_iota(jnp.int32, (1, nc), 1) * CMP_STRIDE + (CMP_BLOCK - 1)
    dist_c = tq - cmp_end
    valid_c = dist_c >= 0
    p_c = _softmax_rows(jnp.where(valid_c, s_c - slope * dist_c.astype(F32), NEG_INF), valid_c)
    o_c = _dot(p_c.astype(BF16), vc_ref[0, 0, 0])

    p_sum = p_c[0:Q_BLOCK]
    for r in range(1, NSA_GROUP):
        p_sum = p_sum + p_c[r * Q_BLOCK:(r + 1) * Q_BLOCK]
    p_hi = p_sum.astype(BF16)
    p_lo = (p_sum - p_hi.astype(F32)).astype(BF16)
    imp = _dot(p_hi, ov_ref[...]) + _dot(p_lo, ov_ref[...])
    nsel = ov_ref.shape[1]
    blk = lax.broadcasted_iota(jnp.int32, (Q_BLOCK, nsel), 1)
    tq1 = t0 + lax.broadcasted_iota(jnp.int32, (Q_BLOCK, nsel), 0)
    back = tq1 // SEL_BLOCK - blk
    forced = (blk == 0) | ((back >= 0) & (back < N_LOCAL_SEL))
    score = jnp.where(forced, BIG, jnp.where(back >= 0, imp, -BIG))

    def pick(_, carry):
        work, sel = carry
        m = jnp.max(work, axis=-1, keepdims=True)
        first = jnp.min(jnp.where(work == m, blk, nsel), axis=-1, keepdims=True)
        hit = blk == first
        return jnp.where(hit, -jnp.inf, work), jnp.where(hit, 1.0, sel)

    _, sel = lax.fori_loop(0, n_pick, pick, (score, jnp.zeros((Q_BLOCK, nsel), F32)))
    sel = sel.astype(BF16)

    def sweep(j, carry):
        m_i, l_i, acc = carry
        k0 = pl.multiple_of(j * tk, tk)
        s_s = _dot_nt(q, ks_ref[0, 0, 0, pl.ds(k0, tk), :])
        kpos = k0 + lax.broadcasted_iota(jnp.int32, (1, tk), 1)
        expand = lax.broadcasted_iota(jnp.int32, (nsel, tk), 0) == (kpos // SEL_BLOCK)
        picked = _dot(sel, jnp.where(expand, 1.0, 0.0).astype(BF16))
        picked = jnp.concatenate([picked] * NSA_GROUP, axis=0)
        dist = tq - kpos
        valid = (picked > 0.5) & (dist >= 0)
        logit = jnp.where(valid, s_s - slope * dist.astype(F32), NEG_INF)
        m_new = jnp.maximum(m_i, jnp.max(logit, axis=-1, keepdims=True))
        alpha = jnp.exp(m_i - m_new)
        p = jnp.where(valid, jnp.exp(logit - m_new), 0.0)
        l_new = alpha * l_i + jnp.sum(p, axis=-1, keepdims=True)
        acc = alpha * acc + _dot(p.astype(BF16), vs_ref[0, 0, 0, pl.ds(k0, tk), :])
        return m_new, l_new, acc

    n_tiles = (t0 + Q_BLOCK + tk - 1) // tk
    init = (jnp.full((rows, 1), NEG_INF, F32), jnp.zeros((rows, 1), F32), jnp.zeros((rows, HEAD_DIM), F32))
    _, l_s, acc_s = lax.fori_loop(0, n_tiles, sweep, init)
    o_s = acc_s * (1.0 / jnp.maximum(l_s, 1e-30))

    band = WINDOW + Q_BLOCK
    w0 = pl.multiple_of(jnp.maximum(t0 - WINDOW, 0), Q_BLOCK)
    s_w = _dot_nt(q, kw_ref[0, 0, 0, pl.ds(w0, band), :])
    dist_w = tq - (w0 + lax.broadcasted_iota(jnp.int32, (1, band), 1))
    valid_w = (dist_w >= 0) & (dist_w < WINDOW)
    p_w = _softmax_rows(jnp.where(valid_w, s_w - slope * dist_w.astype(F32), NEG_INF), valid_w)
    o_w = _dot(p_w.astype(BF16), vw_ref[0, 0, 0, pl.ds(w0, band), :])

    gate = jax.nn.sigmoid(gl_ref[0])
    outs = []
    for r in range(NSA_GROUP):
        sl = slice(r * Q_BLOCK, (r + 1) * Q_BLOCK)
        outs.append(gate[:, r:r + 1] * o_c[sl]
                    + gate[:, NSA_GROUP + r:NSA_GROUP + r + 1] * o_s[sl]
                    + gate[:, 2 * NSA_GROUP + r:2 * NSA_GROUP + r + 1] * o_w[sl])
    o_ref[0] = jnp.concatenate(outs, axis=1).astype(o_ref.dtype)


def _nsa_attention(qn, kv_cmp, kv4, gl):
    b, _, s, _ = qn.shape
    nc = kv_cmp.shape[3]
    nsel = s // SEL_BLOCK
    n_pick = min(N_SELECT, nsel)
    tk = min(SEL_KV_TILE, s)
    rows = NSA_GROUP * Q_BLOCK
    slopes = jnp.exp2(-8.0 * jnp.arange(1, NSA_HEADS + 1, dtype=F32) / NSA_HEADS)
    slopes = jnp.repeat(slopes.reshape(NSA_KV_HEADS, NSA_GROUP), Q_BLOCK, axis=1).reshape(NSA_KV_HEADS, rows, 1)
    cmp_start = jnp.arange(nc) * CMP_STRIDE
    sel_start = jnp.arange(nsel) * SEL_BLOCK
    overlap = (jnp.minimum(cmp_start[:, None] + CMP_BLOCK, sel_start[None, :] + SEL_BLOCK)
               > jnp.maximum(cmp_start[:, None], sel_start[None, :])).astype(BF16)
    kv_spec = lambda a: pl.BlockSpec((1, 1, 1, s, HEAD_DIM), lambda bi, g, i, a=a: (a, bi, g, 0, 0))
    cmp_spec = lambda a: pl.BlockSpec((1, 1, 1, nc, HEAD_DIM), lambda bi, g, i, a=a: (a, bi, g, 0, 0))
    return pl.pallas_call(
        functools.partial(_attn_kernel, n_pick=n_pick, tk=tk),
        grid=(b, NSA_KV_HEADS, s // Q_BLOCK),
        in_specs=[
            pl.BlockSpec((1, NSA_GROUP, Q_BLOCK, HEAD_DIM), lambda bi, g, i: (bi, g, i, 0)),
            cmp_spec(0), cmp_spec(1),
            kv_spec(0), kv_spec(1), kv_spec(2), kv_spec(3),
            pl.BlockSpec((1, Q_BLOCK, V7X_LANES), lambda bi, g, i: (bi, i, g)),
            pl.BlockSpec((1, rows, 1), lambda bi, g, i: (g, 0, 0)),
            pl.BlockSpec((nc, nsel), lambda bi, g, i: (0, 0)),
        ],
        out_specs=pl.BlockSpec((1, Q_BLOCK, NSA_GROUP * HEAD_DIM), lambda bi, g, i: (bi, i, g)),
        out_shape=jax.ShapeDtypeStruct((b, s, NSA_Q_COLS), BF16),
        compiler_params=_params(("parallel", "parallel", "arbitrary")),
        name="nsa_attention",
    )(qn, kv_cmp, kv_cmp, kv4, kv4, kv4, kv4, gl, slopes, overlap)


def _retention_kernel(q_ref, k_ref, v_ref, g_ref, decay_ref, xi_ref, zeta_ref, gch_ref, gn_ref, o_ref, state_ref):
    @pl.when(pl.program_id(1) == 0)
    def _():
        state_ref[...] = jnp.zeros_like(state_ref)

    scale = jnp.asarray(HEAD_DIM ** -0.5, BF16)
    outs = []
    for h in range(RET_HEADS):
        cols = slice(h * HEAD_DIM, (h + 1) * HEAD_DIM)
        q = q_ref[0, :, cols]
        k = k_ref[0, :, cols] * scale
        v = v_ref[0, :, cols]
        state = state_ref[h]
        inner = _dot_nt(q, k) * decay_ref[h]
        o = _dot(inner.astype(BF16), v) + _dot(q, state.astype(BF16)) * xi_ref[h]
        kz = (k.astype(F32) * zeta_ref[h]).astype(BF16)
        state_ref[h] = gch_ref[h] * state + _dot_tn(kz, v)
        mu = jnp.mean(o, axis=-1, keepdims=True)
        var = jnp.mean(jnp.square(o - mu), axis=-1, keepdims=True)
        outs.append((o - mu) * lax.rsqrt(var + EPS))
    o = jnp.concatenate(outs, axis=1) * gn_ref[...]
    o_ref[0] = (jax.nn.silu(g_ref[0].astype(F32)) * o).astype(o_ref.dtype)


def _retention(ret, gn_gain):
    b, s, _ = ret.shape
    c = RET_CHUNK
    hh = RET_HEADS
    log_gamma = jnp.log1p(-jnp.exp2(-5.0 - jnp.arange(hh, dtype=F32)))
    pos = jnp.arange(c, dtype=F32)
    diff = pos[:, None] - pos[None, :]
    decay = jnp.where(diff >= 0, jnp.exp(jnp.maximum(diff, 0.0)[None] * log_gamma[:, None, None]), 0.0)
    xi = jnp.exp((pos + 1.0)[None] * log_gamma[:, None])[..., None]
    zeta = jnp.exp((c - 1.0 - pos)[None] * log_gamma[:, None])[..., None]
    g_chunk = jnp.exp(c * log_gamma)[:, None, None]
    xi = jnp.broadcast_to(xi, (hh, c, HEAD_DIM))
    zeta = jnp.broadcast_to(zeta, (hh, c, HEAD_DIM))
    g_chunk = jnp.broadcast_to(g_chunk, (hh, HEAD_DIM, HEAD_DIM))
    part = lambda a: pl.BlockSpec((1, c, RET_COLS), lambda bi, n, a=a: (bi, n, a))
    full = lambda shape: pl.BlockSpec(shape, lambda bi, n: (0,) * len(shape))
    return pl.pallas_call(
        _retention_kernel,
        grid=(b, s // c),
        in_specs=[part(0), part(1), part(2), part(3),
                  full((hh, c, c)), full((hh, c, HEAD_DIM)), full((hh, c, HEAD_DIM)),
                  full((hh, HEAD_DIM, HEAD_DIM)), full((1, RET_COLS))],
        out_specs=pl.BlockSpec((1, c, RET_COLS), lambda bi, n: (bi, n, 0)),
        out_shape=jax.ShapeDtypeStruct((b, s, RET_COLS), BF16),
        scratch_shapes=[pltpu.VMEM((hh, HEAD_DIM, HEAD_DIM), F32)],
        compiler_params=_params(("parallel", "arbitrary")),
        name="retention",
    )(ret, ret, ret, ret, decay, xi, zeta, g_chunk, gn_gain.reshape(1, RET_COLS).astype(F32))


def _out_proj_kernel(h_ref, a_ref, r_ref, w_ref, g_ref, h_out_ref, hn_ref):
    h = h_ref[...] + _dot(a_ref[...], w_ref[:NSA_Q_COLS]) + _dot(r_ref[...], w_ref[NSA_Q_COLS:])
    h_out_ref[...] = h
    hn_ref[...] = _rms(h, g_ref[...]).astype(BF16)


def _out_proj(h, a, r, w, g):
    t, d = h.shape
    tm = min(ROW_TILE, t)
    row = lambda n: pl.BlockSpec((tm, n), lambda i: (i, 0))
    return pl.pallas_call(
        _out_proj_kernel,
        grid=(t // tm,),
        in_specs=[row(d), row(NSA_Q_COLS), row(RET_COLS),
                  pl.BlockSpec((MIX_WIDTH, d), lambda i: (0, 0)),
                  pl.BlockSpec((1, d), lambda i: (0, 0))],
        out_specs=[row(d), row(d)],
        out_shape=[jax.ShapeDtypeStruct((t, d), F32), jax.ShapeDtypeStruct((t, d), BF16)],
        compiler_params=_params(("parallel",)),
        name="out_proj",
    )(h, a, r, w.astype(BF16), g.reshape(1, d))


def _swiglu_chunk(x, wg, wu, wd):
    hid = jax.nn.silu(_dot(x, wg)) * _dot(x, wu)
    return _dot(hid.astype(BF16), wd)


def _ffn_kernel(h_ref, x_ref, wg_ref, wu_ref, wd_ref, o_ref, acc_ref):
    f = pl.program_id(1)

    @pl.when(f == 0)
    def _():
        acc_ref[...] = jnp.zeros_like(acc_ref)

    acc_ref[...] += _swiglu_chunk(x_ref[...], wg_ref[...], wu_ref[...], wd_ref[...])

    @pl.when(f == pl.num_programs(1) - 1)
    def _():
        o_ref[...] = h_ref[...] + acc_ref[...]


def _dense_ffn(h, hn, wg, wu, wd):
    t, d = h.shape
    tm = min(FFN_ROW_TILE, t)
    tf = FFN_COL_TILE
    return pl.pallas_call(
        _ffn_kernel,
        grid=(t // tm, D_FF // tf),
        in_specs=[pl.BlockSpec((tm, d), lambda i, f: (i, 0)),
                  pl.BlockSpec((tm, d), lambda i, f: (i, 0)),
                  pl.BlockSpec((d, tf), lambda i, f: (0, f)),
                  pl.BlockSpec((d, tf), lambda i, f: (0, f)),
                  pl.BlockSpec((tf, d), lambda i, f: (f, 0))],
        out_specs=pl.BlockSpec((tm, d), lambda i, f: (i, 0)),
        out_shape=jax.ShapeDtypeStruct((t, d), F32),
        scratch_shapes=[pltpu.VMEM((tm, d), F32)],
        compiler_params=_params(("parallel", "arbitrary")),
        name="dense_ffn",
    )(h, hn, wg.astype(BF16), wu.astype(BF16), wd.astype(BF16))


def _moe_kernel(h_ref, x_ref, router_ref, wg_ref, wu_ref, wd_ref, o_ref, acc_ref, gate_ref):
    e = pl.program_id(1)
    f = pl.program_id(2)
    first = (e == 0) & (f == 0)
    last = (e == pl.num_programs(1) - 1) & (f == pl.num_programs(2) - 1)

    @pl.when(first)
    def _():
        acc_ref[...] = jnp.zeros_like(acc_ref)
        logits = _dot(x_ref[...], router_ref[...])
        lane = lax.broadcasted_iota(jnp.int32, logits.shape, 1)
        logits = jnp.where(lane < N_EXPERTS, logits, -jnp.inf)
        v1 = jnp.max(logits, axis=-1, keepdims=True)
        i1 = jnp.min(jnp.where(logits == v1, lane, V7X_LANES), axis=-1, keepdims=True)
        rest = jnp.where(lane == i1, -jnp.inf, logits)
        v2 = jnp.max(rest, axis=-1, keepdims=True)
        i2 = jnp.min(jnp.where(rest == v2, lane, V7X_LANES), axis=-1, keepdims=True)
        e2 = jnp.exp(v2 - v1)
        inv = 1.0 / (1.0 + e2)
        gate_ref[...] = jnp.where(lane == i1, inv, 0.0) + jnp.where(lane == i2, e2 * inv, 0.0)

    lane = lax.broadcasted_iota(jnp.int32, gate_ref.shape, 1)
    g_e = jnp.sum(jnp.where(lane == e, gate_ref[...], 0.0), axis=-1, keepdims=True)
    acc_ref[...] += g_e * _swiglu_chunk(x_ref[...], wg_ref[0], wu_ref[0], wd_ref[0])

    @pl.when(last)
    def _():
        o_ref[...] = h_ref[...] + acc_ref[...]


def _moe_ffn(h, hn, router, wg, wu, wd):
    t, d = h.shape
    tm = min(FFN_ROW_TILE, t)
    tf = FFN_COL_TILE
    router = jnp.pad(router, ((0, 0), (0, V7X_LANES - N_EXPERTS))).astype(BF16)
    return pl.pallas_call(
        _moe_kernel,
        grid=(t // tm, N_EXPERTS, D_FF // tf),
        in_specs=[pl.BlockSpec((tm, d), lambda i, e, f: (i, 0)),
                  pl.BlockSpec((tm, d), lambda i, e, f: (i, 0)),
                  pl.BlockSpec((d, V7X_LANES), lambda i, e, f: (0, 0)),
                  pl.BlockSpec((1, d, tf), lambda i, e, f: (e, 0, f)),
                  pl.BlockSpec((1, d, tf), lambda i, e, f: (e, 0, f)),
                  pl.BlockSpec((1, tf, d), lambda i, e, f: (e, f, 0))],
        out_specs=pl.BlockSpec((tm, d), lambda i, e, f: (i, 0)),
        out_shape=jax.ShapeDtypeStruct((t, d), F32),
        scratch_shapes=[pltpu.VMEM((tm, d), F32), pltpu.VMEM((tm, V7X_LANES), F32)],
        compiler_params=_params(("parallel", "arbitrary", "arbitrary")),
        name="moe_ffn",
    )(h, hn, router, wg.astype(BF16), wu.astype(BF16), wd.astype(BF16))


def _ple_kernel(h_ref, p_ref, g_ref, proj_ref, gate_ref, gf_ref, o_ref, *, final_norm):
    h = h_ref[...]
    emb = _dot(p_ref[...].astype(BF16), proj_ref[...])
    sig = jax.nn.sigmoid(_dot(_rms(h, g_ref[...]).astype(BF16), gate_ref[...]))
    h = h + emb * sig
    if final_norm:
        h = _rms(h, gf_ref[...])
    o_ref[...] = h


def _ple(h, p, g, proj, gate, g_final, final_norm):
    t, d = h.shape
    tm = min(ROW_TILE, t)
    vec = pl.BlockSpec((1, d), lambda i: (0, 0))
    return pl.pallas_call(
        functools.partial(_ple_kernel, final_norm=final_norm),
        grid=(t // tm,),
        in_specs=[pl.BlockSpec((tm, d), lambda i: (i, 0)),
                  pl.BlockSpec((tm, PLE_DIM), lambda i: (i, 0)),
                  vec,
                  pl.BlockSpec((PLE_DIM, d), lambda i: (0, 0)),
                  pl.BlockSpec((d, d), lambda i: (0, 0)),
                  vec],
        out_specs=pl.BlockSpec((tm, d), lambda i: (i, 0)),
        out_shape=jax.ShapeDtypeStruct((t, d), F32),
        compiler_params=_params(("parallel",)),
        name="ple",
    )(h, p, g.reshape(1, d), proj.astype(BF16), gate.astype(BF16), g_final.reshape(1, d))


def kernel(x, p, w_in, w_out, g_mix, g_ffn, g_ple, g_final, cmp_pos, cmp_w1, cmp_w2, ret_gn,
           ffn_gate, ffn_up, ffn_down, moe_router, moe_gate, moe_up, moe_down, ple_proj, ple_gate):
    b, s, d = x.shape
    depth = w_in.shape[0]
    t = b * s
    h = x
    for i in range(depth):
        qn, kcv, kv4, gl, ret = _in_proj(h.reshape(b, s, d), g_mix[i], _arrange_w_in(w_in[i]))
        kv_cmp = _compress(kcv, cmp_pos[i], cmp_w1[i], cmp_w2[i])
        a = _nsa_attention(qn, kv_cmp, kv4, gl)
        r = _retention(ret, ret_gn[i])
        h, hn = _out_proj(h.reshape(t, d), a.reshape(t, NSA_Q_COLS), r.reshape(t, RET_COLS), w_out[i], g_ffn[i])
        if i % 2 == 0:
            h = _dense_ffn(h, hn, ffn_gate[i // 2], ffn_up[i // 2], ffn_down[i // 2])
        else:
            h = _moe_ffn(h, hn, moe_router[i // 2], moe_gate[i // 2], moe_up[i // 2], moe_down[i // 2])
        h = _ple(h, p[i].reshape(t, PLE_DIM), g_ple[i], ple_proj[i], ple_gate[i], g_final, i == depth - 1)
    return h.reshape(b, s, d)
```

```python
import functools

import jax
import jax.numpy as jnp
from jax import lax
from jax.experimental import pallas as pl
from jax.experimental.pallas import tpu as pltpu

F32 = jnp.float32
BF16 = jnp.bfloat16

D_MODEL = 1024
HEAD_DIM = 64
NSA_HEADS = 8
NSA_KV_HEADS = 2
NSA_GROUP = NSA_HEADS // NSA_KV_HEADS
RET_HEADS = 8
CMP_BLOCK = 32
CMP_STRIDE = 16
CMP_HIDDEN = 256
SEL_BLOCK = 64
N_SELECT = 16
N_LOCAL_SEL = 2
WINDOW = 512
Q_BLOCK = 128
RET_CHUNK = 128
D_FF = 3584
N_EXPERTS = 8
TOP_K = 2
PLE_DIM = 256
EPS = 1e-6
NEG_INF = -1e30
BIG = 1e9

NSA_Q_COLS = NSA_HEADS * HEAD_DIM
NSA_KV_COLS = NSA_KV_HEADS * HEAD_DIM
NSA_GATE_COLS = 3 * NSA_HEADS
RET_COLS = RET_HEADS * HEAD_DIM
MIX_WIDTH = NSA_Q_COLS + RET_COLS

V7X_LANES = 128
V7X_VMEM_BYTES = 64 * 1024 * 1024
VMEM_LIMIT = V7X_VMEM_BYTES * 3 // 4

ROW_TILE = 512
SEL_KV_TILE = 256
FFN_ROW_TILE = 1024
FFN_COL_TILE = 512

POS_SHIFT = 6
POS_SPLIT = 1 << POS_SHIFT
SEL_SHIFT = SEL_BLOCK.bit_length() - 1
MAX_SEL_BLOCKS = V7X_LANES


def _params(sem):
    return pltpu.CompilerParams(dimension_semantics=sem, vmem_limit_bytes=VMEM_LIMIT)


def _dot(a, b):
    return jnp.dot(a, b, preferred_element_type=F32)


def _dot_nt(a, b):
    return lax.dot_general(a, b, (((1,), (1,)), ((), ())), preferred_element_type=F32)


def _dot_tn(a, b):
    return lax.dot_general(a, b, (((0,), (0,)), ((), ())), preferred_element_type=F32)


def _rms(x, g):
    return x * lax.rsqrt(jnp.mean(x * x, axis=-1, keepdims=True) + EPS) * g


def _lane_features(shape, first, second):
    lane = lax.broadcasted_iota(jnp.int32, shape, 1)
    return jnp.where(lane == 0, first, jnp.where(lane == 1, second, 0))


def _key_features(pos, width):
    return _lane_features((pos.shape[0], width), pos >> POS_SHIFT, pos & (POS_SPLIT - 1))


def _ones_features(n, width):
    return _lane_features((n, width), 1, 0)


def _int_to_bf16(x):
    return x.astype(F32).astype(BF16)


def _in_proj_kernel(h_ref, g_ref, w_ref, qn_ref, kcv_ref, ks_ref, kv3_ref, gl_ref, ret_ref):
    tm = h_ref.shape[1]
    xn = _rms(h_ref[0], g_ref[...]).astype(BF16)
    pair = 2 * HEAD_DIM
    pos = pl.program_id(1) * tm + lax.broadcasted_iota(jnp.int32, (tm, 1), 0)
    kfeat = _int_to_bf16(_key_features(pos, HEAD_DIM))
    vfeat = _int_to_bf16(_ones_features(tm, HEAD_DIM))
    lane = lax.broadcasted_iota(jnp.int32, (tm, MAX_SEL_BLOCKS), 1)
    block_onehot = jnp.where(lane == (pos >> SEL_SHIFT), 1.0, 0.0).astype(BF16)

    for j in range(NSA_HEADS // 2):
        z = (_dot(xn, w_ref[:, j * pair:(j + 1) * pair]) * HEAD_DIM ** -0.5).astype(BF16)
        for k in range(2):
            slope = 2.0 ** -(2 * j + k + 1)
            qfeat = _lane_features((tm, HEAD_DIM), POS_SPLIT * slope, slope).astype(BF16)
            qn_ref[0, 2 * j + k] = jnp.concatenate([z[:, k * HEAD_DIM:(k + 1) * HEAD_DIM], qfeat], axis=1)
    base = NSA_Q_COLS
    for a in range(2):
        kcv_ref[a, 0] = _dot(xn, w_ref[:, base + a * pair:base + (a + 1) * pair]).astype(BF16)
    base += 2 * pair
    for a in range(4):
        z = _dot(xn, w_ref[:, base + a * pair:base + (a + 1) * pair]).astype(BF16)
        for g in range(NSA_KV_HEADS):
            zg = z[:, g * HEAD_DIM:(g + 1) * HEAD_DIM]
            if a == 0:
                ks_ref[0, g] = jnp.concatenate([zg, kfeat, block_onehot], axis=1)
            else:
                kv3_ref[a - 1, 0, g] = jnp.concatenate([zg, kfeat if a == 2 else vfeat], axis=1)
    base += 4 * pair
    for a in range(4):
        ret_ref[0, :, a * RET_COLS:(a + 1) * RET_COLS] = _dot(
            xn, w_ref[:, base + a * RET_COLS:base + (a + 1) * RET_COLS]).astype(BF16)
    base += 4 * RET_COLS
    gl_ref[0] = _dot(xn, w_ref[:, base:base + NSA_KV_HEADS * V7X_LANES])


def _arrange_w_in(w):
    q_end = NSA_Q_COLS
    kv_end = q_end + 6 * NSA_KV_COLS
    gl_end = kv_end + NSA_GATE_COLS
    gl = w[:, kv_end:gl_end].reshape(D_MODEL, 3, NSA_KV_HEADS, NSA_GROUP)
    gl = jnp.transpose(gl, (0, 2, 1, 3)).reshape(D_MODEL, NSA_KV_HEADS, 3 * NSA_GROUP)
    gl = jnp.pad(gl, ((0, 0), (0, 0), (0, V7X_LANES - 3 * NSA_GROUP)))
    gl = gl.reshape(D_MODEL, NSA_KV_HEADS * V7X_LANES)
    return jnp.concatenate([w[:, :kv_end], w[:, gl_end:], gl], axis=1).astype(BF16)


def _in_proj(h, g, w):
    b, s, d = h.shape
    assert s // SEL_BLOCK <= MAX_SEL_BLOCKS
    tm = min(ROW_TILE, s)
    ncols = w.shape[1]
    aug = 2 * HEAD_DIM
    return pl.pallas_call(
        _in_proj_kernel,
        grid=(b, s // tm),
        in_specs=[
            pl.BlockSpec((1, tm, d), lambda bi, si: (bi, si, 0)),
            pl.BlockSpec((1, d), lambda bi, si: (0, 0)),
            pl.BlockSpec((d, ncols), lambda bi, si: (0, 0)),
        ],
        out_specs=[
            pl.BlockSpec((1, NSA_HEADS, tm, aug), lambda bi, si: (bi, 0, si, 0)),
            pl.BlockSpec((2, 1, tm, NSA_KV_COLS), lambda bi, si: (0, bi, si, 0)),
            pl.BlockSpec((1, NSA_KV_HEADS, tm, aug + MAX_SEL_BLOCKS), lambda bi, si: (bi, 0, si, 0)),
            pl.BlockSpec((3, 1, NSA_KV_HEADS, tm, aug), lambda bi, si: (0, bi, 0, si, 0)),
            pl.BlockSpec((1, tm, NSA_KV_HEADS * V7X_LANES), lambda bi, si: (bi, si, 0)),
            pl.BlockSpec((1, tm, 4 * RET_COLS), lambda bi, si: (bi, si, 0)),
        ],
        out_shape=[
            jax.ShapeDtypeStruct((b, NSA_HEADS, s, aug), BF16),
            jax.ShapeDtypeStruct((2, b, s, NSA_KV_COLS), BF16),
            jax.ShapeDtypeStruct((b, NSA_KV_HEADS, s, aug + MAX_SEL_BLOCKS), BF16),
            jax.ShapeDtypeStruct((3, b, NSA_KV_HEADS, s, aug), BF16),
            jax.ShapeDtypeStruct((b, s, NSA_KV_HEADS * V7X_LANES), F32),
            jax.ShapeDtypeStruct((b, s, 4 * RET_COLS), BF16),
        ],
        compiler_params=_params(("parallel", "parallel")),
        name="in_proj",
    )(h, g.reshape(1, d), w)


def _compress_kernel(x_ref, wbig_ref, pos_ref, w1_ref, w2_ref, o_ref):
    nch = x_ref.shape[2]
    u = _dot(x_ref[0, 0], wbig_ref[0])
    hid0 = _dot(pos_ref[0], w1_ref[0])[0:1]
    w2 = w2_ref[0]
    cmp_end = lax.broadcasted_iota(jnp.int32, (nch, 1), 0) * CMP_STRIDE + (CMP_BLOCK - 1)
    feat = _int_to_bf16(jnp.where(pl.program_id(0) == 0, _key_features(cmp_end, HEAD_DIM),
                                  _ones_features(nch, HEAD_DIM)))
    for g in range(NSA_KV_HEADS):
        c0 = g * 2 * CMP_HIDDEN
        first = u[:, c0:c0 + CMP_HIDDEN]
        second = u[:, c0 + CMP_HIDDEN:c0 + 2 * CMP_HIDDEN]
        hid = first + pltpu.roll(second, nch - 1, 0) + hid0
        out = _dot(jax.nn.gelu(hid).astype(BF16), w2).astype(BF16)
        o_ref[0, 0, g] = jnp.concatenate([out, feat], axis=1)


def _arrange_cmp_w1(w1):
    r = CMP_BLOCK // CMP_STRIDE
    w1r = w1.reshape(2, r, CMP_STRIDE, HEAD_DIM, CMP_HIDDEN)
    eye = jnp.eye(NSA_KV_HEADS, dtype=w1.dtype)
    big = jnp.einsum('krcdh,gf->kcgdfrh', w1r, eye)
    return big.reshape(2, CMP_STRIDE * NSA_KV_COLS, NSA_KV_HEADS * r * CMP_HIDDEN).astype(BF16)


def _compress(kcv, cmp_pos, cmp_w1, cmp_w2):
    _, b, s, _ = kcv.shape
    nch = s // CMP_STRIDE
    x = kcv.reshape(2, b, nch, CMP_STRIDE * NSA_KV_COLS)
    wbig = _arrange_cmp_w1(cmp_w1)
    pos = jnp.broadcast_to(cmp_pos.reshape(2, 1, CMP_BLOCK * HEAD_DIM), (2, 8, CMP_BLOCK * HEAD_DIM)).astype(BF16)
    w1 = cmp_w1.reshape(2, CMP_BLOCK * HEAD_DIM, CMP_HIDDEN).astype(BF16)
    w2 = cmp_w2.astype(BF16)
    kdim = CMP_STRIDE * NSA_KV_COLS
    return pl.pallas_call(
        _compress_kernel,
        grid=(2, b),
        in_specs=[
            pl.BlockSpec((1, 1, nch, kdim), lambda a, bi: (a, bi, 0, 0)),
            pl.BlockSpec((1, kdim, wbig.shape[2]), lambda a, bi: (a, 0, 0)),
            pl.BlockSpec((1, 8, CMP_BLOCK * HEAD_DIM), lambda a, bi: (a, 0, 0)),
            pl.BlockSpec((1, CMP_BLOCK * HEAD_DIM, CMP_HIDDEN), lambda a, bi: (a, 0, 0)),
            pl.BlockSpec((1, CMP_HIDDEN, HEAD_DIM), lambda a, bi: (a, 0, 0)),
        ],
        out_specs=pl.BlockSpec((1, 1, NSA_KV_HEADS, nch, 2 * HEAD_DIM), lambda a, bi: (a, bi, 0, 0, 0)),
        out_shape=jax.ShapeDtypeStruct((2, b, NSA_KV_HEADS, nch, 2 * HEAD_DIM), BF16),
        compiler_params=_params(("parallel", "parallel")),
        name="compress",
    )(x, wbig, pos, w1, w2)


def _normalise(acc):
    return acc[:, :HEAD_DIM] * (1.0 / jnp.maximum(acc[:, HEAD_DIM:HEAD_DIM + 1], 1e-30))


def _attn_kernel(q_ref, kc_ref, vc_ref, ks_ref, vs_ref, kw_ref, vw_ref, gl_ref, ovt_ref, o_ref, tiles_ref,
                 *, n_pick, tk):
    rows = NSA_GROUP * Q_BLOCK
    t0 = pl.program_id(2) * Q_BLOCK
    q = q_ref[0].reshape(rows, 2 * HEAD_DIM)
    tq = t0 + (lax.broadcasted_iota(jnp.int32, (rows, 1), 0) & (Q_BLOCK - 1))

    nc = kc_ref.shape[3]
    cmp_end = lax.broadcasted_iota(jnp.int32, (1, nc), 1) * CMP_STRIDE + (CMP_BLOCK - 1)
    valid_c = cmp_end <= tq
    logit_c = jnp.where(valid_c, _dot_nt(q, kc_ref[0, 0, 0]), NEG_INF)
    e_c = jnp.where(valid_c, jnp.exp(logit_c - jnp.max(logit_c, axis=-1, keepdims=True)), 0.0)
    p_c = e_c * (1.0 / jnp.maximum(jnp.sum(e_c, axis=-1, keepdims=True), 1e-30))
    o_c = _dot(p_c.astype(BF16), vc_ref[0, 0, 0])[:, :HEAD_DIM]

    p_sum = p_c[0:Q_BLOCK]
    for r in range(1, NSA_GROUP):
        p_sum = p_sum + p_c[r * Q_BLOCK:(r + 1) * Q_BLOCK]
    p_hi = p_sum.astype(BF16)
    p_lo = (p_sum - p_hi.astype(F32)).astype(BF16)
    imp = _dot_nt(ovt_ref[...], p_hi) + _dot_nt(ovt_ref[...], p_lo)
    shape = (MAX_SEL_BLOCKS, Q_BLOCK)
    blk = lax.broadcasted_iota(jnp.int32, shape, 0)
    back = ((t0 + lax.broadcasted_iota(jnp.int32, shape, 1)) >> SEL_SHIFT) - blk
    forced = (blk == 0) | ((back >= 0) & (back < N_LOCAL_SEL))
    score = jnp.where(forced, BIG, jnp.where(back >= 0, imp, -BIG))

    def pick(_, carry):
        work, sel = carry
        m = jnp.max(work, axis=0, keepdims=True)
        first = jnp.min(jnp.where(work == m, blk, MAX_SEL_BLOCKS), axis=0, keepdims=True)
        hit = blk == first
        return jnp.where(hit, -jnp.inf, work), jnp.where(hit, 1.0, sel)

    _, sel_t = lax.fori_loop(0, n_pick, pick, (score, jnp.zeros(shape, F32)))
    sel_bias = ((sel_t.T - 1.0) * -NEG_INF).astype(BF16)
    q_sel = jnp.concatenate([q, jnp.concatenate([sel_bias] * NSA_GROUP, axis=0)], axis=1)

    def sweep_tile(j, carry, causal):
        m_i, acc = carry
        k0 = pl.multiple_of(j * tk, tk)
        s = _dot_nt(q_sel, ks_ref[0, 0, pl.ds(k0, tk), :])
        if causal:
            s = jnp.where(k0 + lax.broadcasted_iota(jnp.int32, (1, tk), 1) <= tq, s, NEG_INF)
        m_new = jnp.maximum(m_i, jnp.max(s, axis=-1, keepdims=True))
        p = jnp.exp(s - m_new)
        acc = jnp.exp(m_i - m_new) * acc + _dot(p.astype(BF16), vs_ref[0, 0, 0, pl.ds(k0, tk), :])
        return m_new, acc

    n_full = t0 // tk
    blocks_per_tile = tk // SEL_BLOCK
    block_used = jnp.max(sel_t, axis=1, keepdims=True)
    n_visit = jnp.int32(0)
    for j in range(tiles_ref.shape[0]):
        used = jnp.max(block_used[j * blocks_per_tile:(j + 1) * blocks_per_tile]) > 0.0
        tiles_ref[n_visit] = j
        n_visit = n_visit + (used & (j < n_full)).astype(jnp.int32)
    init = (jnp.full((rows, 1), NEG_INF, F32), jnp.zeros((rows, 2 * HEAD_DIM), F32))
    carry = lax.fori_loop(0, n_visit, lambda i, c: sweep_tile(tiles_ref[i], c, causal=False), init)
    _, acc_s = sweep_tile(n_full, carry, causal=True)
    o_s = _normalise(acc_s)

    band = WINDOW + Q_BLOCK
    w0 = pl.multiple_of(jnp.maximum(t0 - WINDOW, 0), Q_BLOCK)
    kpos = w0 + lax.broadcasted_iota(jnp.int32, (1, band), 1)
    valid_w = (kpos <= tq) & (kpos > tq - WINDOW)
    logit_w = jnp.where(valid_w, _dot_nt(q, kw_ref[0, 0, 0, pl.ds(w0, band), :]), NEG_INF)
    e_w = jnp.exp(logit_w - jnp.max(logit_w, axis=-1, keepdims=True))
    o_w = _normalise(_dot(e_w.astype(BF16), vw_ref[0, 0, 0, pl.ds(w0, band), :]))

    gate = jax.nn.sigmoid(gl_ref[0])
    outs = []
    for r in range(NSA_GROUP):
        sl = slice(r * Q_BLOCK, (r + 1) * Q_BLOCK)
        outs.append(gate[:, r:r + 1] * o_c[sl]
                    + gate[:, NSA_GROUP + r:NSA_GROUP + r + 1] * o_s[sl]
                    + gate[:, 2 * NSA_GROUP + r:2 * NSA_GROUP + r + 1] * o_w[sl])
    o_ref[0] = jnp.concatenate(outs, axis=1).astype(o_ref.dtype)


def _nsa_attention(qn, kv_cmp, ks, kv3, gl):
    b, _, s, _ = qn.shape
    nc = kv_cmp.shape[3]
    nsel = s // SEL_BLOCK
    n_pick = min(N_SELECT, nsel)
    tk = min(SEL_KV_TILE, s)
    aug = 2 * HEAD_DIM
    cmp_start = jnp.arange(nc) * CMP_STRIDE
    sel_start = jnp.arange(nsel) * SEL_BLOCK
    overlap = (jnp.minimum(cmp_start[:, None] + CMP_BLOCK, sel_start[None, :] + SEL_BLOCK)
               > jnp.maximum(cmp_start[:, None], sel_start[None, :]))
    overlap_t = jnp.pad(overlap.T.astype(BF16), ((0, MAX_SEL_BLOCKS - nsel), (0, 0)))
    kv_spec = lambda a: pl.BlockSpec((1, 1, 1, s, aug), lambda bi, g, i, a=a: (a, bi, g, 0, 0))
    cmp_spec = lambda a: pl.BlockSpec((1, 1, 1, nc, aug), lambda bi, g, i, a=a: (a, bi, g, 0, 0))
    return pl.pallas_call(
        functools.partial(_attn_kernel, n_pick=n_pick, tk=tk),
        grid=(b, NSA_KV_HEADS, s // Q_BLOCK),
        in_specs=[
            pl.BlockSpec((1, NSA_GROUP, Q_BLOCK, aug), lambda bi, g, i: (bi, g, i, 0)),
            cmp_spec(0), cmp_spec(1),
            pl.BlockSpec((1, 1, s, aug + MAX_SEL_BLOCKS), lambda bi, g, i: (bi, g, 0, 0)),
            kv_spec(0), kv_spec(1), kv_spec(2),
            pl.BlockSpec((1, Q_BLOCK, V7X_LANES), lambda bi, g, i: (bi, i, g)),
            pl.BlockSpec((MAX_SEL_BLOCKS, nc), lambda bi, g, i: (0, 0)),
        ],
        out_specs=pl.BlockSpec((1, Q_BLOCK, NSA_GROUP * HEAD_DIM), lambda bi, g, i: (bi, i, g)),
        out_shape=jax.ShapeDtypeStruct((b, s, NSA_Q_COLS), BF16),
        scratch_shapes=[pltpu.SMEM((s // tk,), jnp.int32)],
        compiler_params=_params(("parallel", "parallel", "arbitrary")),
        name="nsa_attention",
    )(qn, kv_cmp, kv_cmp, ks, kv3, kv3, kv3, gl, overlap_t)


def _retention_kernel(q_ref, k_ref, v_ref, g_ref, decay_ref, xi_ref, zeta_ref, gch_ref, gn_ref, o_ref, state_ref):
    @pl.when(pl.program_id(1) == 0)
    def _():
        state_ref[...] = jnp.zeros_like(state_ref)

    scale = jnp.asarray(HEAD_DIM ** -0.5, BF16)
    outs = []
    for h in range(RET_HEADS):
        cols = slice(h * HEAD_DIM, (h + 1) * HEAD_DIM)
        q = q_ref[0, :, cols]
        k = k_ref[0, :, cols] * scale
        v = v_ref[0, :, cols]
        state = state_ref[h]
        inner = _dot_nt(q, k) * decay_ref[h]
        o = _dot(inner.astype(BF16), v) + _dot(q, state.astype(BF16)) * xi_ref[h]
        kz = (k.astype(F32) * zeta_ref[h]).astype(BF16)
        state_ref[h] = gch_ref[h] * state + _dot_tn(kz, v)
        mu = jnp.mean(o, axis=-1, keepdims=True)
        var = jnp.mean(jnp.square(o - mu), axis=-1, keepdims=True)
        outs.append((o - mu) * lax.rsqrt(var + EPS))
    o = jnp.concatenate(outs, axis=1) * gn_ref[...]
    o_ref[0] = (jax.nn.silu(g_ref[0].astype(F32)) * o).astype(o_ref.dtype)


def _retention(ret, gn_gain):
    b, s, _ = ret.shape
    c = RET_CHUNK
    hh = RET_HEADS
    log_gamma = jnp.log1p(-jnp.exp2(-5.0 - jnp.arange(hh, dtype=F32)))
    pos = jnp.arange(c, dtype=F32)
    diff = pos[:, None] - pos[None, :]
    decay = jnp.where(diff >= 0, jnp.exp(jnp.maximum(diff, 0.0)[None] * log_gamma[:, None, None]), 0.0)
    xi = jnp.exp((pos + 1.0)[None] * log_gamma[:, None])[..., None]
    zeta = jnp.exp((c - 1.0 - pos)[None] * log_gamma[:, None])[..., None]
    g_chunk = jnp.exp(c * log_gamma)[:, None, None]
    xi = jnp.broadcast_to(xi, (hh, c, HEAD_DIM))
    zeta = jnp.broadcast_to(zeta, (hh, c, HEAD_DIM))
    g_chunk = jnp.broadcast_to(g_chunk, (hh, HEAD_DIM, HEAD_DIM))
    part = lambda a: pl.BlockSpec((1, c, RET_COLS), lambda bi, n, a=a: (bi, n, a))
    full = lambda shape: pl.BlockSpec(shape, lambda bi, n: (0,) * len(shape))
    return pl.pallas_call(
        _retention_kernel,
        grid=(b, s // c),
        in_specs=[part(0), part(1), part(2), part(3),
                  full((hh, c, c)), full((hh, c, HEAD_DIM)), full((hh, c, HEAD_DIM)),
                  full((hh, HEAD_DIM, HEAD_DIM)), full((1, RET_COLS))],
        out_specs=pl.BlockSpec((1, c, RET_COLS), lambda bi, n: (bi, n, 0)),
        out_shape=jax.ShapeDtypeStruct((b, s, RET_COLS), BF16),
        scratch_shapes=[pltpu.VMEM((hh, HEAD_DIM, HEAD_DIM), F32)],
        compiler_params=_params(("parallel", "arbitrary")),
        name="retention",
    )(ret, ret, ret, ret, decay, xi, zeta, g_chunk, gn_gain.reshape(1, RET_COLS).astype(F32))


def _out_proj_kernel(h_ref, a_ref, r_ref, w_ref, g_ref, h_out_ref, hn_ref):
    h = h_ref[...] + _dot(a_ref[...], w_ref[:NSA_Q_COLS]) + _dot(r_ref[...], w_ref[NSA_Q_COLS:])
    h_out_ref[...] = h
    hn_ref[...] = _rms(h, g_ref[...]).astype(BF16)


def _out_proj(h, a, r, w, g):
    t, d = h.shape
    tm = min(ROW_TILE, t)
    row = lambda n: pl.BlockSpec((tm, n), lambda i: (i, 0))
    return pl.pallas_call(
        _out_proj_kernel,
        grid=(t // tm,),
        in_specs=[row(d), row(NSA_Q_COLS), row(RET_COLS),
                  pl.BlockSpec((MIX_WIDTH, d), lambda i: (0, 0)),
                  pl.BlockSpec((1, d), lambda i: (0, 0))],
        out_specs=[row(d), row(d)],
        out_shape=[jax.ShapeDtypeStruct((t, d), F32), jax.ShapeDtypeStruct((t, d), BF16)],
        compiler_params=_params(("parallel",)),
        name="out_proj",
    )(h, a, r, w.astype(BF16), g.reshape(1, d))


def _swiglu_chunk(x, wg, wu, wd):
    hid = jax.nn.silu(_dot(x, wg)) * _dot(x, wu)
    return _dot(hid.astype(BF16), wd)


def _ffn_kernel(h_ref, x_ref, wg_ref, wu_ref, wd_ref, o_ref, acc_ref):
    f = pl.program_id(1)

    @pl.when(f == 0)
    def _():
        acc_ref[...] = jnp.zeros_like(acc_ref)

    acc_ref[...] += _swiglu_chunk(x_ref[...], wg_ref[...], wu_ref[...], wd_ref[...])

    @pl.when(f == pl.num_programs(1) - 1)
    def _():
        o_ref[...] = h_ref[...] + acc_ref[...]


def _dense_ffn(h, hn, wg, wu, wd):
    t, d = h.shape
    tm = min(FFN_ROW_TILE, t)
    tf = FFN_COL_TILE
    return pl.pallas_call(
        _ffn_kernel,
        grid=(t // tm, D_FF // tf),
        in_specs=[pl.BlockSpec((tm, d), lambda i, f: (i, 0)),
                  pl.BlockSpec((tm, d), lambda i, f: (i, 0)),
                  pl.BlockSpec((d, tf), lambda i, f: (0, f)),
                  pl.BlockSpec((d, tf), lambda i, f: (0, f)),
                  pl.BlockSpec((tf, d), lambda i, f: (f, 0))],
        out_specs=pl.BlockSpec((tm, d), lambda i, f: (i, 0)),
        out_shape=jax.ShapeDtypeStruct((t, d), F32),
        scratch_shapes=[pltpu.VMEM((tm, d), F32)],
        compiler_params=_params(("parallel", "arbitrary")),
        name="dense_ffn",
    )(h, hn, wg.astype(BF16), wu.astype(BF16), wd.astype(BF16))


def _moe_kernel(h_ref, x_ref, router_ref, wg_ref, wu_ref, wd_ref, o_ref, acc_ref, gate_ref):
    e = pl.program_id(1)
    f = pl.program_id(2)
    first = (e == 0) & (f == 0)
    last = (e == pl.num_programs(1) - 1) & (f == pl.num_programs(2) - 1)

    @pl.when(first)
    def _():
        acc_ref[...] = jnp.zeros_like(acc_ref)
        logits = _dot(x_ref[...], router_ref[...])
        lane = lax.broadcasted_iota(jnp.int32, logits.shape, 1)
        logits = jnp.where(lane < N_EXPERTS, logits, -jnp.inf)
        v1 = jnp.max(logits, axis=-1, keepdims=True)
        i1 = jnp.min(jnp.where(logits == v1, lane, V7X_LANES), axis=-1, keepdims=True)
        rest = jnp.where(lane == i1, -jnp.inf, logits)
        v2 = jnp.max(rest, axis=-1, keepdims=True)
        i2 = jnp.min(jnp.where(rest == v2, lane, V7X_LANES), axis=-1, keepdims=True)
        e2 = jnp.exp(v2 - v1)
        inv = 1.0 / (1.0 + e2)
        gate_ref[...] = jnp.where(lane == i1, inv, 0.0) + jnp.where(lane == i2, e2 * inv, 0.0)

    lane = lax.broadcasted_iota(jnp.int32, gate_ref.shape, 1)
    g_e = jnp.sum(jnp.where(lane == e, gate_ref[...], 0.0), axis=-1, keepdims=True)
    acc_ref[...] += g_e * _swiglu_chunk(x_ref[...], wg_ref[0], wu_ref[0], wd_ref[0])

    @pl.when(last)
    def _():
        o_ref[...] = h_ref[...] + acc_ref[...]


def _moe_ffn(h, hn, router, wg, wu, wd):
    t, d = h.shape
    tm = min(FFN_ROW_TILE, t)
    tf = FFN_COL_TILE
    router = jnp.pad(router, ((0, 0), (0, V7X_LANES - N_EXPERTS))).astype(BF16)
    return pl.pallas_call(
        _moe_kernel,
        grid=(t // tm, N_EXPERTS, D_FF // tf),
        in_specs=[pl.BlockSpec((tm, d), lambda i, e, f: (i, 0)),
                  pl.BlockSpec((tm, d), lambda i, e, f: (i, 0)),
                  pl.BlockSpec((d, V7X_LANES), lambda i, e, f: (0, 0)),
                  pl.BlockSpec((1, d, tf), lambda i, e, f: (e, 0, f)),
                  pl.BlockSpec((1, d, tf), lambda i, e, f: (e, 0, f)),
                  pl.BlockSpec((1, tf, d), lambda i, e, f: (e, f, 0))],
        out_specs=pl.BlockSpec((tm, d), lambda i, e, f: (i, 0)),
        out_shape=jax.ShapeDtypeStruct((t, d), F32),
        scratch_shapes=[pltpu.VMEM((tm, d), F32), pltpu.VMEM((tm, V7X_LANES), F32)],
        compiler_params=_params(("parallel", "arbitrary", "arbitrary")),
        name="moe_ffn",
    )(h, hn, router, wg.astype(BF16), wu.astype(BF16), wd.astype(BF16))


def _ple_kernel(h_ref, p_ref, g_ref, proj_ref, gate_ref, gf_ref, o_ref, *, final_norm):
    h = h_ref[...]
    emb = _dot(p_ref[...].astype(BF16), proj_ref[...])
    sig = jax.nn.sigmoid(_dot(_rms(h, g_ref[...]).astype(BF16), gate_ref[...]))
    h = h + emb * sig
    if final_norm:
        h = _rms(h, gf_ref[...])
    o_ref[...] = h


def _ple(h, p, g, proj, gate, g_final, final_norm):
    t, d = h.shape
    tm = min(ROW_TILE, t)
    vec = pl.BlockSpec((1, d), lambda i: (0, 0))
    return pl.pallas_call(
        functools.partial(_ple_kernel, final_norm=final_norm),
        grid=(t // tm,),
        in_specs=[pl.BlockSpec((tm, d), lambda i: (i, 0)),
                  pl.BlockSpec((tm, PLE_DIM), lambda i: (i, 0)),
                  vec,
                  pl.BlockSpec((PLE_DIM, d), lambda i: (0, 0)),
                  pl.BlockSpec((d, d), lambda i: (0, 0)),
                  vec],
        out_specs=pl.BlockSpec((tm, d), lambda i: (i, 0)),
        out_shape=jax.ShapeDtypeStruct((t, d), F32),
        compiler_params=_params(("parallel",)),
        name="ple",
    )(h, p, g.reshape(1, d), proj.astype(BF16), gate.astype(BF16), g_final.reshape(1, d))


def kernel(x, p, w_in, w_out, g_mix, g_ffn, g_ple, g_final, cmp_pos, cmp_w1, cmp_w2, ret_gn,
           ffn_gate, ffn_up, ffn_down, moe_router, moe_gate, moe_up, moe_down, ple_proj, ple_gate):
    b, s, d = x.shape
    depth = w_in.shape[0]
    t = b * s
    h = x
    for i in range(depth):
        qn, kcv, ks, kv3, gl, ret = _in_proj(h.reshape(b, s, d), g_mix[i], _arrange_w_in(w_in[i]))
        kv_cmp = _compress(kcv, cmp_pos[i], cmp_w1[i], cmp_w2[i])
        a = _nsa_attention(qn, kv_cmp, ks, kv3, gl)
        r = _retention(ret, ret_gn[i])
        h, hn = _out_proj(h.reshape(t, d), a.reshape(t, NSA_Q_COLS), r.reshape(t, RET_COLS), w_out[i], g_ffn[i])
        if i % 2 == 0:
            h = _dense_ffn(h, hn, ffn_gate[i // 2], ffn_up[i // 2], ffn_down[i // 2])
        else:
            h = _moe_ffn(h, hn, moe_router[i // 2], moe_gate[i // 2], moe_up[i // 2], moe_down[i // 2])
        h = _ple(h, p[i].reshape(t, PLE_DIM), g_ple[i], ple_proj[i], ple_gate[i], g_final, i == depth - 1)
    return h.reshape(b, s, d)
```

```python
import functools

import jax
import jax.numpy as jnp
from jax import lax
from jax.experimental import pallas as pl
from jax.experimental.pallas import tpu as pltpu

F32 = jnp.float32
BF16 = jnp.bfloat16

D_MODEL = 1024
HEAD_DIM = 64
NSA_HEADS = 8
NSA_KV_HEADS = 2
NSA_GROUP = NSA_HEADS // NSA_KV_HEADS
RET_HEADS = 8
CMP_BLOCK = 32
CMP_STRIDE = 16
CMP_HIDDEN = 256
SEL_BLOCK = 64
N_SELECT = 16
N_LOCAL_SEL = 2
WINDOW = 512
Q_BLOCK = 128
RET_CHUNK = 128
D_FF = 3584
N_EXPERTS = 8
TOP_K = 2
PLE_DIM = 256
EPS = 1e-6
NEG_INF = -1e30
BIG = 1e9

NSA_Q_COLS = NSA_HEADS * HEAD_DIM
NSA_KV_COLS = NSA_KV_HEADS * HEAD_DIM
NSA_GATE_COLS = 3 * NSA_HEADS
RET_COLS = RET_HEADS * HEAD_DIM
MIX_WIDTH = NSA_Q_COLS + RET_COLS

V7X_LANES = 128
V7X_VMEM_BYTES = 64 * 1024 * 1024
VMEM_LIMIT = V7X_VMEM_BYTES * 3 // 4

ROW_TILE = 512
SEL_KV_TILE = 256
FFN_ROW_TILE = 1024
FFN_COL_TILE = 512
MOE_ROW_TILE = 2048
MOE_GROUP_ROWS = 256
MOE_RANK_CHUNK = 256
MOE_SCATTER_ROWS = 512
MOE_VMEM_LIMIT = V7X_VMEM_BYTES * 7 // 8

POS_SHIFT = 6
POS_SPLIT = 1 << POS_SHIFT
SEL_SHIFT = SEL_BLOCK.bit_length() - 1
MAX_SEL_BLOCKS = V7X_LANES


def _params(sem, vmem=VMEM_LIMIT):
    return pltpu.CompilerParams(dimension_semantics=sem, vmem_limit_bytes=vmem)


def _dot(a, b):
    return jnp.dot(a, b, preferred_element_type=F32)


def _dot_nt(a, b):
    return lax.dot_general(a, b, (((1,), (1,)), ((), ())), preferred_element_type=F32)


def _dot_tn(a, b):
    return lax.dot_general(a, b, (((0,), (0,)), ((), ())), preferred_element_type=F32)


def _rms(x, g):
    return x * lax.rsqrt(jnp.mean(x * x, axis=-1, keepdims=True) + EPS) * g


def _lane_features(shape, first, second):
    lane = lax.broadcasted_iota(jnp.int32, shape, 1)
    return jnp.where(lane == 0, first, jnp.where(lane == 1, second, 0))


def _key_features(pos, width):
    return _lane_features((pos.shape[0], width), pos >> POS_SHIFT, pos & (POS_SPLIT - 1))


def _ones_features(n, width):
    return _lane_features((n, width), 1, 0)


def _int_to_bf16(x):
    return x.astype(F32).astype(BF16)


def _in_proj_kernel(h_ref, g_ref, w_ref, qn_ref, kcv_ref, ks_ref, kv3_ref, gl_ref, ret_ref):
    tm = h_ref.shape[1]
    xn = _rms(h_ref[0], g_ref[...]).astype(BF16)
    pair = 2 * HEAD_DIM
    pos = pl.program_id(1) * tm + lax.broadcasted_iota(jnp.int32, (tm, 1), 0)
    kfeat = _int_to_bf16(_key_features(pos, HEAD_DIM))
    vfeat = _int_to_bf16(_ones_features(tm, HEAD_DIM))
    lane = lax.broadcasted_iota(jnp.int32, (tm, MAX_SEL_BLOCKS), 1)
    block_onehot = jnp.where(lane == (pos >> SEL_SHIFT), 1.0, 0.0).astype(BF16)

    for j in range(NSA_HEADS // 2):
        z = (_dot(xn, w_ref[:, j * pair:(j + 1) * pair]) * HEAD_DIM ** -0.5).astype(BF16)
        for k in range(2):
            slope = 2.0 ** -(2 * j + k + 1)
            qfeat = _lane_features((tm, HEAD_DIM), POS_SPLIT * slope, slope).astype(BF16)
            qn_ref[0, 2 * j + k] = jnp.concatenate([z[:, k * HEAD_DIM:(k + 1) * HEAD_DIM], qfeat], axis=1)
    base = NSA_Q_COLS
    for a in range(2):
        kcv_ref[a, 0] = _dot(xn, w_ref[:, base + a * pair:base + (a + 1) * pair]).astype(BF16)
    base += 2 * pair
    for a in range(4):
        z = _dot(xn, w_ref[:, base + a * pair:base + (a + 1) * pair]).astype(BF16)
        for g in range(NSA_KV_HEADS):
            zg = z[:, g * HEAD_DIM:(g + 1) * HEAD_DIM]
            if a == 0:
                ks_ref[0, g] = jnp.concatenate([zg, kfeat, block_onehot], axis=1)
            else:
                kv3_ref[a - 1, 0, g] = jnp.concatenate([zg, kfeat if a == 2 else vfeat], axis=1)
    base += 4 * pair
    for a in range(4):
        ret_ref[0, :, a * RET_COLS:(a + 1) * RET_COLS] = _dot(
            xn, w_ref[:, base + a * RET_COLS:base + (a + 1) * RET_COLS]).astype(BF16)
    base += 4 * RET_COLS
    gl_ref[0] = _dot(xn, w_ref[:, base:base + NSA_KV_HEADS * V7X_LANES])


def _arrange_w_in(w):
    q_end = NSA_Q_COLS
    kv_end = q_end + 6 * NSA_KV_COLS
    gl_end = kv_end + NSA_GATE_COLS
    gl = w[:, kv_end:gl_end].reshape(D_MODEL, 3, NSA_KV_HEADS, NSA_GROUP)
    gl = jnp.transpose(gl, (0, 2, 1, 3)).reshape(D_MODEL, NSA_KV_HEADS, 3 * NSA_GROUP)
    gl = jnp.pad(gl, ((0, 0), (0, 0), (0, V7X_LANES - 3 * NSA_GROUP)))
    gl = gl.reshape(D_MODEL, NSA_KV_HEADS * V7X_LANES)
    return jnp.concatenate([w[:, :kv_end], w[:, gl_end:], gl], axis=1).astype(BF16)


def _in_proj(h, g, w):
    b, s, d = h.shape
    assert s // SEL_BLOCK <= MAX_SEL_BLOCKS
    tm = min(ROW_TILE, s)
    ncols = w.shape[1]
    aug = 2 * HEAD_DIM
    return pl.pallas_call(
        _in_proj_kernel,
        grid=(b, s // tm),
        in_specs=[
            pl.BlockSpec((1, tm, d), lambda bi, si: (bi, si, 0)),
            pl.BlockSpec((1, d), lambda bi, si: (0, 0)),
            pl.BlockSpec((d, ncols), lambda bi, si: (0, 0)),
        ],
        out_specs=[
            pl.BlockSpec((1, NSA_HEADS, tm, aug), lambda bi, si: (bi, 0, si, 0)),
            pl.BlockSpec((2, 1, tm, NSA_KV_COLS), lambda bi, si: (0, bi, si, 0)),
            pl.BlockSpec((1, NSA_KV_HEADS, tm, aug + MAX_SEL_BLOCKS), lambda bi, si: (bi, 0, si, 0)),
            pl.BlockSpec((3, 1, NSA_KV_HEADS, tm, aug), lambda bi, si: (0, bi, 0, si, 0)),
            pl.BlockSpec((1, tm, NSA_KV_HEADS * V7X_LANES), lambda bi, si: (bi, si, 0)),
            pl.BlockSpec((1, tm, 4 * RET_COLS), lambda bi, si: (bi, si, 0)),
        ],
        out_shape=[
            jax.ShapeDtypeStruct((b, NSA_HEADS, s, aug), BF16),
            jax.ShapeDtypeStruct((2, b, s, NSA_KV_COLS), BF16),
            jax.ShapeDtypeStruct((b, NSA_KV_HEADS, s, aug + MAX_SEL_BLOCKS), BF16),
            jax.ShapeDtypeStruct((3, b, NSA_KV_HEADS, s, aug), BF16),
            jax.ShapeDtypeStruct((b, s, NSA_KV_HEADS * V7X_LANES), F32),
            jax.ShapeDtypeStruct((b, s, 4 * RET_COLS), BF16),
        ],
        compiler_params=_params(("parallel", "parallel")),
        name="in_proj",
    )(h, g.reshape(1, d), w)


def _compress_kernel(x_ref, wbig_ref, pos_ref, w1_ref, w2_ref, o_ref):
    nch = x_ref.shape[2]
    u = _dot(x_ref[0, 0], wbig_ref[0])
    hid0 = _dot(pos_ref[0], w1_ref[0])[0:1]
    w2 = w2_ref[0]
    cmp_end = lax.broadcasted_iota(jnp.int32, (nch, 1), 0) * CMP_STRIDE + (CMP_BLOCK - 1)
    feat = _int_to_bf16(jnp.where(pl.program_id(0) == 0, _key_features(cmp_end, HEAD_DIM),
                                  _ones_features(nch, HEAD_DIM)))
    for g in range(NSA_KV_HEADS):
        c0 = g * 2 * CMP_HIDDEN
        first = u[:, c0:c0 + CMP_HIDDEN]
        second = u[:, c0 + CMP_HIDDEN:c0 + 2 * CMP_HIDDEN]
        hid = first + pltpu.roll(second, nch - 1, 0) + hid0
        out = _dot(jax.nn.gelu(hid).astype(BF16), w2).astype(BF16)
        o_ref[0, 0, g] = jnp.concatenate([out, feat], axis=1)


def _arrange_cmp_w1(w1):
    r = CMP_BLOCK // CMP_STRIDE
    w1r = w1.reshape(2, r, CMP_STRIDE, HEAD_DIM, CMP_HIDDEN)
    eye = jnp.eye(NSA_KV_HEADS, dtype=w1.dtype)
    big = jnp.einsum('krcdh,gf->kcgdfrh', w1r, eye)
    return big.reshape(2, CMP_STRIDE * NSA_KV_COLS, NSA_KV_HEADS * r * CMP_HIDDEN).astype(BF16)


def _compress(kcv, cmp_pos, cmp_w1, cmp_w2):
    _, b, s, _ = kcv.shape
    nch = s // CMP_STRIDE
    x = kcv.reshape(2, b, nch, CMP_STRIDE * NSA_KV_COLS)
    wbig = _arrange_cmp_w1(cmp_w1)
    pos = jnp.broadcast_to(cmp_pos.reshape(2, 1, CMP_BLOCK * HEAD_DIM), (2, 8, CMP_BLOCK * HEAD_DIM)).astype(BF16)
    w1 = cmp_w1.reshape(2, CMP_BLOCK * HEAD_DIM, CMP_HIDDEN).astype(BF16)
    w2 = cmp_w2.astype(BF16)
    kdim = CMP_STRIDE * NSA_KV_COLS
    return pl.pallas_call(
        _compress_kernel,
        grid=(2, b),
        in_specs=[
            pl.BlockSpec((1, 1, nch, kdim), lambda a, bi: (a, bi, 0, 0)),
            pl.BlockSpec((1, kdim, wbig.shape[2]), lambda a, bi: (a, 0, 0)),
            pl.BlockSpec((1, 8, CMP_BLOCK * HEAD_DIM), lambda a, bi: (a, 0, 0)),
            pl.BlockSpec((1, CMP_BLOCK * HEAD_DIM, CMP_HIDDEN), lambda a, bi: (a, 0, 0)),
            pl.BlockSpec((1, CMP_HIDDEN, HEAD_DIM), lambda a, bi: (a, 0, 0)),
        ],
        out_specs=pl.BlockSpec((1, 1, NSA_KV_HEADS, nch, 2 * HEAD_DIM), lambda a, bi: (a, bi, 0, 0, 0)),
        out_shape=jax.ShapeDtypeStruct((2, b, NSA_KV_HEADS, nch, 2 * HEAD_DIM), BF16),
        compiler_params=_params(("parallel", "parallel")),
        name="compress",
    )(x, wbig, pos, w1, w2)


def _normalise(acc):
    return acc[:, :HEAD_DIM] * (1.0 / jnp.maximum(acc[:, HEAD_DIM:HEAD_DIM + 1], 1e-30))


def _attn_kernel(q_ref, kc_ref, vc_ref, ks_ref, vs_ref, kw_ref, vw_ref, gl_ref, ovt_ref, o_ref, tiles_ref,
                 *, n_pick, tk):
    rows = NSA_GROUP * Q_BLOCK
    t0 = pl.program_id(2) * Q_BLOCK
    q = q_ref[0].reshape(rows, 2 * HEAD_DIM)
    tq = t0 + (lax.broadcasted_iota(jnp.int32, (rows, 1), 0) & (Q_BLOCK - 1))

    nc = kc_ref.shape[3]
    cmp_end = lax.broadcasted_iota(jnp.int32, (1, nc), 1) * CMP_STRIDE + (CMP_BLOCK - 1)
    valid_c = cmp_end <= tq
    logit_c = jnp.where(valid_c, _dot_nt(q, kc_ref[0, 0, 0]), NEG_INF)
    e_c = jnp.where(valid_c, jnp.exp(logit_c - jnp.max(logit_c, axis=-1, keepdims=True)), 0.0)
    p_c = e_c * (1.0 / jnp.maximum(jnp.sum(e_c, axis=-1, keepdims=True), 1e-30))
    o_c = _dot(p_c.astype(BF16), vc_ref[0, 0, 0])[:, :HEAD_DIM]

    p_sum = p_c[0:Q_BLOCK]
    for r in range(1, NSA_GROUP):
        p_sum = p_sum + p_c[r * Q_BLOCK:(r + 1) * Q_BLOCK]
    p_hi = p_sum.astype(BF16)
    p_lo = (p_sum - p_hi.astype(F32)).astype(BF16)
    imp = _dot_nt(ovt_ref[...], p_hi) + _dot_nt(ovt_ref[...], p_lo)
    shape = (MAX_SEL_BLOCKS, Q_BLOCK)
    blk = lax.broadcasted_iota(jnp.int32, shape, 0)
    back = ((t0 + lax.broadcasted_iota(jnp.int32, shape, 1)) >> SEL_SHIFT) - blk
    forced = (blk == 0) | ((back >= 0) & (back < N_LOCAL_SEL))
    score = jnp.where(forced, BIG, jnp.where(back >= 0, imp, -BIG))

    def pick(_, carry):
        work, sel = carry
        m = jnp.max(work, axis=0, keepdims=True)
        first = jnp.min(jnp.where(work == m, blk, MAX_SEL_BLOCKS), axis=0, keepdims=True)
        hit = blk == first
        return jnp.where(hit, -jnp.inf, work), jnp.where(hit, 1.0, sel)

    _, sel_t = lax.fori_loop(0, n_pick, pick, (score, jnp.zeros(shape, F32)))
    sel_bias = ((sel_t.T - 1.0) * -NEG_INF).astype(BF16)
    q_sel = jnp.concatenate([q, jnp.concatenate([sel_bias] * NSA_GROUP, axis=0)], axis=1)

    def sweep_tile(j, carry, causal):
        m_i, acc = carry
        k0 = pl.multiple_of(j * tk, tk)
        s = _dot_nt(q_sel, ks_ref[0, 0, pl.ds(k0, tk), :])
        if causal:
            s = jnp.where(k0 + lax.broadcasted_iota(jnp.int32, (1, tk), 1) <= tq, s, NEG_INF)
        m_new = jnp.maximum(m_i, jnp.max(s, axis=-1, keepdims=True))
        p = jnp.exp(s - m_new)
        acc = jnp.exp(m_i - m_new) * acc + _dot(p.astype(BF16), vs_ref[0, 0, 0, pl.ds(k0, tk), :])
        return m_new, acc

    n_full = t0 // tk
    blocks_per_tile = tk // SEL_BLOCK
    block_used = jnp.max(sel_t, axis=1, keepdims=True)
    n_visit = jnp.int32(0)
    for j in range(tiles_ref.shape[0]):
        used = jnp.max(block_used[j * blocks_per_tile:(j + 1) * blocks_per_tile]) > 0.0
        tiles_ref[n_visit] = j
        n_visit = n_visit + (used & (j < n_full)).astype(jnp.int32)
    init = (jnp.full((rows, 1), NEG_INF, F32), jnp.zeros((rows, 2 * HEAD_DIM), F32))
    carry = lax.fori_loop(0, n_visit, lambda i, c: sweep_tile(tiles_ref[i], c, causal=False), init)
    _, acc_s = sweep_tile(n_full, carry, causal=True)
    o_s = _normalise(acc_s)

    band = WINDOW + Q_BLOCK
    w0 = pl.multiple_of(jnp.maximum(t0 - WINDOW, 0), Q_BLOCK)
    kpos = w0 + lax.broadcasted_iota(jnp.int32, (1, band), 1)
    valid_w = (kpos <= tq) & (kpos > tq - WINDOW)
    logit_w = jnp.where(valid_w, _dot_nt(q, kw_ref[0, 0, 0, pl.ds(w0, band), :]), NEG_INF)
    e_w = jnp.exp(logit_w - jnp.max(logit_w, axis=-1, keepdims=True))
    o_w = _normalise(_dot(e_w.astype(BF16), vw_ref[0, 0, 0, pl.ds(w0, band), :]))

    gate = jax.nn.sigmoid(gl_ref[0])
    outs = []
    for r in range(NSA_GROUP):
        sl = slice(r * Q_BLOCK, (r + 1) * Q_BLOCK)
        outs.append(gate[:, r:r + 1] * o_c[sl]
                    + gate[:, NSA_GROUP + r:NSA_GROUP + r + 1] * o_s[sl]
                    + gate[:, 2 * NSA_GROUP + r:2 * NSA_GROUP + r + 1] * o_w[sl])
    o_ref[0] = jnp.concatenate(outs, axis=1).astype(o_ref.dtype)


def _nsa_attention(qn, kv_cmp, ks, kv3, gl):
    b, _, s, _ = qn.shape
    nc = kv_cmp.shape[3]
    nsel = s // SEL_BLOCK
    n_pick = min(N_SELECT, nsel)
    tk = min(SEL_KV_TILE, s)
    aug = 2 * HEAD_DIM
    cmp_start = jnp.arange(nc) * CMP_STRIDE
    sel_start = jnp.arange(nsel) * SEL_BLOCK
    overlap = (jnp.minimum(cmp_start[:, None] + CMP_BLOCK, sel_start[None, :] + SEL_BLOCK)
               > jnp.maximum(cmp_start[:, None], sel_start[None, :]))
    overlap_t = jnp.pad(overlap.T.astype(BF16), ((0, MAX_SEL_BLOCKS - nsel), (0, 0)))
    kv_spec = lambda a: pl.BlockSpec((1, 1, 1, s, aug), lambda bi, g, i, a=a: (a, bi, g, 0, 0))
    cmp_spec = lambda a: pl.BlockSpec((1, 1, 1, nc, aug), lambda bi, g, i, a=a: (a, bi, g, 0, 0))
    return pl.pallas_call(
        functools.partial(_attn_kernel, n_pick=n_pick, tk=tk),
        grid=(b, NSA_KV_HEADS, s // Q_BLOCK),
        in_specs=[
            pl.BlockSpec((1, NSA_GROUP, Q_BLOCK, aug), lambda bi, g, i: (bi, g, i, 0)),
            cmp_spec(0), cmp_spec(1),
            pl.BlockSpec((1, 1, s, aug + MAX_SEL_BLOCKS), lambda bi, g, i: (bi, g, 0, 0)),
            kv_spec(0), kv_spec(1), kv_spec(2),
            pl.BlockSpec((1, Q_BLOCK, V7X_LANES), lambda bi, g, i: (bi, i, g)),
            pl.BlockSpec((MAX_SEL_BLOCKS, nc), lambda bi, g, i: (0, 0)),
        ],
        out_specs=pl.BlockSpec((1, Q_BLOCK, NSA_GROUP * HEAD_DIM), lambda bi, g, i: (bi, i, g)),
        out_shape=jax.ShapeDtypeStruct((b, s, NSA_Q_COLS), BF16),
        scratch_shapes=[pltpu.SMEM((s // tk,), jnp.int32)],
        compiler_params=_params(("parallel", "parallel", "arbitrary")),
        name="nsa_attention",
    )(qn, kv_cmp, kv_cmp, ks, kv3, kv3, kv3, gl, overlap_t)


def _retention_kernel(q_ref, k_ref, v_ref, g_ref, decay_ref, xi_ref, zeta_ref, gch_ref, gn_ref, o_ref, state_ref):
    @pl.when(pl.program_id(1) == 0)
    def _():
        state_ref[...] = jnp.zeros_like(state_ref)

    scale = jnp.asarray(HEAD_DIM ** -0.5, BF16)
    outs = []
    for h in range(RET_HEADS):
        cols = slice(h * HEAD_DIM, (h + 1) * HEAD_DIM)
        q = q_ref[0, :, cols]
        k = k_ref[0, :, cols] * scale
        v = v_ref[0, :, cols]
        state = state_ref[h]
        inner = _dot_nt(q, k) * decay_ref[h]
        o = _dot(inner.astype(BF16), v) + _dot(q, state.astype(BF16)) * xi_ref[h]
        kz = (k.astype(F32) * zeta_ref[h]).astype(BF16)
        state_ref[h] = gch_ref[h] * state + _dot_tn(kz, v)
        mu = jnp.mean(o, axis=-1, keepdims=True)
        var = jnp.mean(jnp.square(o - mu), axis=-1, keepdims=True)
        outs.append((o - mu) * lax.rsqrt(var + EPS))
    o = jnp.concatenate(outs, axis=1) * gn_ref[...]
    o_ref[0] = (jax.nn.silu(g_ref[0].astype(F32)) * o).astype(o_ref.dtype)


def _retention(ret, gn_gain):
    b, s, _ = ret.shape
    c = RET_CHUNK
    hh = RET_HEADS
    log_gamma = jnp.log1p(-jnp.exp2(-5.0 - jnp.arange(hh, dtype=F32)))
    pos = jnp.arange(c, dtype=F32)
    diff = pos[:, None] - pos[None, :]
    decay = jnp.where(diff >= 0, jnp.exp(jnp.maximum(diff, 0.0)[None] * log_gamma[:, None, None]), 0.0)
    xi = jnp.exp((pos + 1.0)[None] * log_gamma[:, None])[..., None]
    zeta = jnp.exp((c - 1.0 - pos)[None] * log_gamma[:, None])[..., None]
    g_chunk = jnp.exp(c * log_gamma)[:, None, None]
    xi = jnp.broadcast_to(xi, (hh, c, HEAD_DIM))
    zeta = jnp.broadcast_to(zeta, (hh, c, HEAD_DIM))
    g_chunk = jnp.broadcast_to(g_chunk, (hh, HEAD_DIM, HEAD_DIM))
    part = lambda a: pl.BlockSpec((1, c, RET_COLS), lambda bi, n, a=a: (bi, n, a))
    full = lambda shape: pl.BlockSpec(shape, lambda bi, n: (0,) * len(shape))
    return pl.pallas_call(
        _retention_kernel,
        grid=(b, s // c),
        in_specs=[part(0), part(1), part(2), part(3),
                  full((hh, c, c)), full((hh, c, HEAD_DIM)), full((hh, c, HEAD_DIM)),
                  full((hh, HEAD_DIM, HEAD_DIM)), full((1, RET_COLS))],
        out_specs=pl.BlockSpec((1, c, RET_COLS), lambda bi, n: (bi, n, 0)),
        out_shape=jax.ShapeDtypeStruct((b, s, RET_COLS), BF16),
        scratch_shapes=[pltpu.VMEM((hh, HEAD_DIM, HEAD_DIM), F32)],
        compiler_params=_params(("parallel", "arbitrary")),
        name="retention",
    )(ret, ret, ret, ret, decay, xi, zeta, g_chunk, gn_gain.reshape(1, RET_COLS).astype(F32))


def _out_proj_kernel(h_ref, a_ref, r_ref, w_ref, g_ref, h_out_ref, hn_ref):
    h = h_ref[...] + _dot(a_ref[...], w_ref[:NSA_Q_COLS]) + _dot(r_ref[...], w_ref[NSA_Q_COLS:])
    h_out_ref[...] = h
    hn_ref[...] = _rms(h, g_ref[...]).astype(BF16)


def _out_proj(h, a, r, w, g):
    t, d = h.shape
    tm = min(ROW_TILE, t)
    row = lambda n: pl.BlockSpec((tm, n), lambda i: (i, 0))
    return pl.pallas_call(
        _out_proj_kernel,
        grid=(t // tm,),
        in_specs=[row(d), row(NSA_Q_COLS), row(RET_COLS),
                  pl.BlockSpec((MIX_WIDTH, d), lambda i: (0, 0)),
                  pl.BlockSpec((1, d), lambda i: (0, 0))],
        out_specs=[row(d), row(d)],
        out_shape=[jax.ShapeDtypeStruct((t, d), F32), jax.ShapeDtypeStruct((t, d), BF16)],
        compiler_params=_params(("parallel",)),
        name="out_proj",
    )(h, a, r, w.astype(BF16), g.reshape(1, d))


def _swiglu_chunk(x, wg, wu, wd):
    hid = jax.nn.silu(_dot(x, wg)) * _dot(x, wu)
    return _dot(hid.astype(BF16), wd)


def _ffn_kernel(x_ref, wg_ref, wu_ref, wd_ref, o_ref):
    @pl.when(pl.program_id(1) == 0)
    def _():
        o_ref[...] = jnp.zeros_like(o_ref)

    o_ref[...] += _swiglu_chunk(x_ref[...], wg_ref[...], wu_ref[...], wd_ref[...])


def _dense_ffn(hn, wg, wu, wd):
    t, d = hn.shape
    tm = min(FFN_ROW_TILE, t)
    tf = FFN_COL_TILE
    return pl.pallas_call(
        _ffn_kernel,
        grid=(t // tm, D_FF // tf),
        in_specs=[pl.BlockSpec((tm, d), lambda i, f: (i, 0)),
                  pl.BlockSpec((d, tf), lambda i, f: (0, f)),
                  pl.BlockSpec((d, tf), lambda i, f: (0, f)),
                  pl.BlockSpec((tf, d), lambda i, f: (f, 0))],
        out_specs=pl.BlockSpec((tm, d), lambda i, f: (i, 0)),
        out_shape=jax.ShapeDtypeStruct((t, d), F32),
        compiler_params=_params(("parallel", "arbitrary")),
        name="dense_ffn",
    )(hn, wg.astype(BF16), wu.astype(BF16), wd.astype(BF16))


def _lane_column(table, lane_index):
    lane = lax.broadcasted_iota(jnp.int32, table.shape, 1)
    col = jnp.sum(jnp.where(lane == lane_index, table, 0.0), axis=-1, keepdims=True)
    return jnp.broadcast_to(col, table.shape)


def _moe_kernel(x_ref, router_ref, wg_ref, wu_ref, wd_ref, o_ref,
                xs_ref, y_ref, slot_ref, gate_ref, slot_row_ref, slot_e_ref, gate_e_ref, count_ref):
    e = pl.program_id(1)
    f = pl.program_id(2)
    tm, d = x_ref.shape
    sub = MOE_GROUP_ROWS

    @pl.when((e == 0) & (f == 0))
    def _route():
        o_ref[...] = jnp.zeros_like(o_ref)
        logits = _dot(x_ref[...], router_ref[...])
        lane = lax.broadcasted_iota(jnp.int32, logits.shape, 1)
        logits = jnp.where(lane < N_EXPERTS, logits, -jnp.inf)
        v1 = jnp.max(logits, axis=-1, keepdims=True)
        i1 = jnp.min(jnp.where(logits == v1, lane, V7X_LANES), axis=-1, keepdims=True)
        rest = jnp.where(lane == i1, -jnp.inf, logits)
        v2 = jnp.max(rest, axis=-1, keepdims=True)
        i2 = jnp.min(jnp.where(rest == v2, lane, V7X_LANES), axis=-1, keepdims=True)
        e2 = jnp.exp(v2 - v1)
        inv = 1.0 / (1.0 + e2)
        gate_ref[...] = jnp.where(lane == i1, inv, 0.0) + jnp.where(lane == i2, e2 * inv, 0.0)
        routed = jnp.where((lane == i1) | (lane == i2), 1.0, 0.0)
        c = MOE_RANK_CHUNK
        before = (lax.broadcasted_iota(jnp.int32, (c, c), 1) < lax.broadcasted_iota(jnp.int32, (c, c), 0))
        before = jnp.where(before, 1.0, 0.0).astype(BF16)
        offset = jnp.zeros((1, V7X_LANES), F32)
        for j in range(tm // c):
            part = routed[j * c:(j + 1) * c]
            rank = _dot(before, part.astype(BF16)) + offset
            slot_ref[j * c:(j + 1) * c, :] = jnp.where(part > 0.0, rank, -1.0)
            offset = offset + jnp.sum(part, axis=0, keepdims=True)
        for ee in range(N_EXPERTS):
            count_ref[ee] = offset[0, ee].astype(jnp.int32)
        slot_row_ref[...] = slot_ref[...].T[:N_EXPERTS]

    n_groups = (count_ref[e] + sub - 1) // sub

    @pl.when(f == 0)
    def _gather():
        slot_e_ref[...] = _lane_column(slot_ref[...], e)
        gate_e_ref[...] = _lane_column(gate_ref[...], e)
        slot_row = slot_row_ref[pl.ds(e, 1), :]

        def body(s, _):
            r0 = pl.multiple_of(s * sub, sub)
            want = (r0 + lax.broadcasted_iota(jnp.int32, (sub, 1), 0)).astype(F32)
            onehot = jnp.where(slot_row == want, 1.0, 0.0).astype(BF16)
            xs_ref[pl.ds(r0, sub), :] = _dot(onehot, x_ref[...]).astype(BF16)
            y_ref[pl.ds(r0, sub), :] = jnp.zeros((sub, d), F32)
            return 0

        lax.fori_loop(0, n_groups, body, 0)

    def expert(s, _):
        r0 = pl.multiple_of(s * sub, sub)
        y_ref[pl.ds(r0, sub), :] += _swiglu_chunk(xs_ref[pl.ds(r0, sub), :], wg_ref[0], wu_ref[0], wd_ref[0])
        return 0

    lax.fori_loop(0, n_groups, expert, 0)

    @pl.when(f == pl.num_programs(2) - 1)
    def _scatter():
        tc = MOE_SCATTER_ROWS

        def body(s, _):
            r0 = pl.multiple_of(s * sub, sub)
            y = y_ref[pl.ds(r0, sub), :].astype(BF16)
            want = (r0 + lax.broadcasted_iota(jnp.int32, (1, sub), 1)).astype(F32)
            for j in range(tm // tc):
                rows = slice(j * tc, (j + 1) * tc)
                slot = jnp.concatenate([slot_e_ref[rows, :]] * (sub // V7X_LANES), axis=1)
                onehot = jnp.where(slot == want, 1.0, 0.0).astype(BF16)
                weight = jnp.concatenate([gate_e_ref[rows, :]] * (d // V7X_LANES), axis=1)
                o_ref[rows, :] += weight * _dot(onehot, y)
            return 0

        lax.fori_loop(0, n_groups, body, 0)


def _moe_ffn(hn, router, wg, wu, wd):
    t, d = hn.shape
    tm = min(MOE_ROW_TILE, t)
    tf = FFN_COL_TILE
    router = jnp.pad(router, ((0, 0), (0, V7X_LANES - N_EXPERTS))).astype(BF16)
    once = pl.Buffered(1)
    return pl.pallas_call(
        _moe_kernel,
        grid=(t // tm, N_EXPERTS, D_FF // tf),
        in_specs=[pl.BlockSpec((tm, d), lambda i, e, f: (i, 0), pipeline_mode=once),
                  pl.BlockSpec((d, V7X_LANES), lambda i, e, f: (0, 0), pipeline_mode=once),
                  pl.BlockSpec((1, d, tf), lambda i, e, f: (e, 0, f)),
                  pl.BlockSpec((1, d, tf), lambda i, e, f: (e, 0, f)),
                  pl.BlockSpec((1, tf, d), lambda i, e, f: (e, f, 0))],
        out_specs=pl.BlockSpec((tm, d), lambda i, e, f: (i, 0)),
        out_shape=jax.ShapeDtypeStruct((t, d), F32),
        scratch_shapes=[pltpu.VMEM((tm, d), BF16), pltpu.VMEM((tm, d), F32),
                        pltpu.VMEM((tm, V7X_LANES), F32), pltpu.VMEM((tm, V7X_LANES), F32),
                        pltpu.VMEM((N_EXPERTS, tm), F32),
                        pltpu.VMEM((tm, V7X_LANES), F32), pltpu.VMEM((tm, V7X_LANES), F32),
                        pltpu.SMEM((N_EXPERTS,), jnp.int32)],
        compiler_params=_params(("parallel", "arbitrary", "arbitrary"), vmem=MOE_VMEM_LIMIT),
        name="moe_ffn",
    )(hn, router, wg.astype(BF16), wu.astype(BF16), wd.astype(BF16))


def _ple_kernel(h_ref, f_ref, p_ref, g_ref, proj_ref, gate_ref, gf_ref, o_ref, *, final_norm):
    h = h_ref[...] + f_ref[...]
    emb = _dot(p_ref[...].astype(BF16), proj_ref[...])
    sig = jax.nn.sigmoid(_dot(_rms(h, g_ref[...]).astype(BF16), gate_ref[...]))
    h = h + emb * sig
    if final_norm:
        h = _rms(h, gf_ref[...])
    o_ref[...] = h


def _ple(h, f, p, g, proj, gate, g_final, final_norm):
    t, d = h.shape
    tm = min(ROW_TILE, t)
    vec = pl.BlockSpec((1, d), lambda i: (0, 0))
    return pl.pallas_call(
        functools.partial(_ple_kernel, final_norm=final_norm),
        grid=(t // tm,),
        in_specs=[pl.BlockSpec((tm, d), lambda i: (i, 0)),
                  pl.BlockSpec((tm, d), lambda i: (i, 0)),
                  pl.BlockSpec((tm, PLE_DIM), lambda i: (i, 0)),
                  vec,
                  pl.BlockSpec((PLE_DIM, d), lambda i: (0, 0)),
                  pl.BlockSpec((d, d), lambda i: (0, 0)),
                  vec],
        out_specs=pl.BlockSpec((tm, d), lambda i: (i, 0)),
        out_shape=jax.ShapeDtypeStruct((t, d), F32),
        compiler_params=_params(("parallel",)),
        name="ple",
    )(h, f, p, g.reshape(1, d), proj.astype(BF16), gate.astype(BF16), g_final.reshape(1, d))


def kernel(x, p, w_in, w_out, g_mix, g_ffn, g_ple, g_final, cmp_pos, cmp_w1, cmp_w2, ret_gn,
           ffn_gate, ffn_up, ffn_down, moe_router, moe_gate, moe_up, moe_down, ple_proj, ple_gate):
    b, s, d = x.shape
    depth = w_in.shape[0]
    t = b * s
    h = x
    for i in range(depth):
        qn, kcv, ks, kv3, gl, ret = _in_proj(h.reshape(b, s, d), g_mix[i], _arrange_w_in(w_in[i]))
        kv_cmp = _compress(kcv, cmp_pos[i], cmp_w1[i], cmp_w2[i])
        a = _nsa_attention(qn, kv_cmp, ks, kv3, gl)
        r = _retention(ret, ret_gn[i])
        h, hn = _out_proj(h.reshape(t, d), a.reshape(t, NSA_Q_COLS), r.reshape(t, RET_COLS), w_out[i], g_ffn[i])
        if i % 2 == 0:
            f = _dense_ffn(hn, ffn_gate[i // 2], ffn_up[i // 2], ffn_down[i // 2])
        else:
            f = _moe_ffn(hn, moe_router[i // 2], moe_gate[i // 2], moe_up[i // 2], moe_down[i // 2])
        h = _ple(h, f, p[i].reshape(t, PLE_DIM), g_ple[i], ple_proj[i], ple_gate[i], g_final, i == depth - 1)
    return h.reshape(b, s, d)
```

```python
import functools

import jax
import jax.numpy as jnp
from jax import lax
from jax.experimental import pallas as pl
from jax.experimental.pallas import tpu as pltpu

F32 = jnp.float32
BF16 = jnp.bfloat16

D_MODEL = 1024
HEAD_DIM = 64
NSA_HEADS = 8
NSA_KV_HEADS = 2
NSA_GROUP = NSA_HEADS // NSA_KV_HEADS
RET_HEADS = 8
CMP_BLOCK = 32
CMP_STRIDE = 16
CMP_HIDDEN = 256
SEL_BLOCK = 64
N_SELECT = 16
N_LOCAL_SEL = 2
WINDOW = 512
Q_BLOCK = 128
RET_CHUNK = 128
D_FF = 3584
N_EXPERTS = 8
TOP_K = 2
PLE_DIM = 256
EPS = 1e-6
NEG_INF = -1e30
BIG = 1e9

NSA_Q_COLS = NSA_HEADS * HEAD_DIM
NSA_KV_COLS = NSA_KV_HEADS * HEAD_DIM
NSA_GATE_COLS = 3 * NSA_HEADS
RET_COLS = RET_HEADS * HEAD_DIM
MIX_WIDTH = NSA_Q_COLS + RET_COLS

V7X_LANES = 128
V7X_VMEM_BYTES = 64 * 1024 * 1024
VMEM_LIMIT = V7X_VMEM_BYTES * 3 // 4

ROW_TILE = 512
SEL_KV_TILE = 256
FFN_ROW_TILE = 512
FFN_COL_TILE = 512
MOE_COL_TILE = 896
MOE_ROW_TILE = 2048
MOE_GROUP_ROWS = 256
MOE_RANK_CHUNK = 256
MOE_SCATTER_ROWS = 512
MOE_VMEM_LIMIT = V7X_VMEM_BYTES * 7 // 8

POS_SHIFT = 6
POS_SPLIT = 1 << POS_SHIFT
SEL_SHIFT = SEL_BLOCK.bit_length() - 1
MAX_SEL_BLOCKS = V7X_LANES


def _params(sem, vmem=VMEM_LIMIT):
    return pltpu.CompilerParams(dimension_semantics=sem, vmem_limit_bytes=vmem)


def _dot(a, b):
    return jnp.dot(a, b, preferred_element_type=F32)


def _dot_nt(a, b):
    return lax.dot_general(a, b, (((1,), (1,)), ((), ())), preferred_element_type=F32)


def _dot_tn(a, b):
    return lax.dot_general(a, b, (((0,), (0,)), ((), ())), preferred_element_type=F32)


def _rms(x, g):
    return x * lax.rsqrt(jnp.mean(x * x, axis=-1, keepdims=True) + EPS) * g


def _lane_features(shape, first, second):
    lane = lax.broadcasted_iota(jnp.int32, shape, 1)
    return jnp.where(lane == 0, first, jnp.where(lane == 1, second, 0))


def _key_features(pos, width):
    return _lane_features((pos.shape[0], width), pos >> POS_SHIFT, pos & (POS_SPLIT - 1))


def _ones_features(n, width):
    return _lane_features((n, width), 1, 0)


def _int_to_bf16(x):
    return x.astype(F32).astype(BF16)


def _in_proj_kernel(h_ref, g_ref, w_ref, qn_ref, kcv_ref, ks_ref, kv3_ref, gl_ref, ret_ref):
    tm = h_ref.shape[1]
    xn = _rms(h_ref[0], g_ref[...]).astype(BF16)
    pair = 2 * HEAD_DIM
    pos = pl.program_id(1) * tm + lax.broadcasted_iota(jnp.int32, (tm, 1), 0)
    kfeat = _int_to_bf16(_key_features(pos, HEAD_DIM))
    vfeat = _int_to_bf16(_ones_features(tm, HEAD_DIM))
    lane = lax.broadcasted_iota(jnp.int32, (tm, MAX_SEL_BLOCKS), 1)
    block_onehot = jnp.where(lane == (pos >> SEL_SHIFT), 1.0, 0.0).astype(BF16)

    for j in range(NSA_HEADS // 2):
        z = (_dot(xn, w_ref[:, j * pair:(j + 1) * pair]) * HEAD_DIM ** -0.5).astype(BF16)
        for k in range(2):
            slope = 2.0 ** -(2 * j + k + 1)
            qfeat = _lane_features((tm, HEAD_DIM), POS_SPLIT * slope, slope).astype(BF16)
            qn_ref[0, 2 * j + k] = jnp.concatenate([z[:, k * HEAD_DIM:(k + 1) * HEAD_DIM], qfeat], axis=1)
    base = NSA_Q_COLS
    for a in range(2):
        kcv_ref[a, 0] = _dot(xn, w_ref[:, base + a * pair:base + (a + 1) * pair]).astype(BF16)
    base += 2 * pair
    for a in range(4):
        z = _dot(xn, w_ref[:, base + a * pair:base + (a + 1) * pair]).astype(BF16)
        for g in range(NSA_KV_HEADS):
            zg = z[:, g * HEAD_DIM:(g + 1) * HEAD_DIM]
            if a == 0:
                ks_ref[0, g] = jnp.concatenate([zg, kfeat, block_onehot], axis=1)
            else:
                kv3_ref[a - 1, 0, g] = jnp.concatenate([zg, kfeat if a == 2 else vfeat], axis=1)
    base += 4 * pair
    for a in range(4):
        ret_ref[0, :, a * RET_COLS:(a + 1) * RET_COLS] = _dot(
            xn, w_ref[:, base + a * RET_COLS:base + (a + 1) * RET_COLS]).astype(BF16)
    base += 4 * RET_COLS
    gl_ref[0] = _dot(xn, w_ref[:, base:base + NSA_KV_HEADS * V7X_LANES])


def _arrange_w_in(w):
    q_end = NSA_Q_COLS
    kv_end = q_end + 6 * NSA_KV_COLS
    gl_end = kv_end + NSA_GATE_COLS
    gl = w[:, kv_end:gl_end].reshape(D_MODEL, 3, NSA_KV_HEADS, NSA_GROUP)
    gl = jnp.transpose(gl, (0, 2, 1, 3)).reshape(D_MODEL, NSA_KV_HEADS, 3 * NSA_GROUP)
    gl = jnp.pad(gl, ((0, 0), (0, 0), (0, V7X_LANES - 3 * NSA_GROUP)))
    gl = gl.reshape(D_MODEL, NSA_KV_HEADS * V7X_LANES)
    return jnp.concatenate([w[:, :kv_end], w[:, gl_end:], gl], axis=1).astype(BF16)


def _in_proj(h, g, w):
    b, s, d = h.shape
    assert s // SEL_BLOCK <= MAX_SEL_BLOCKS
    tm = min(ROW_TILE, s)
    ncols = w.shape[1]
    aug = 2 * HEAD_DIM
    return pl.pallas_call(
        _in_proj_kernel,
        grid=(b, s // tm),
        in_specs=[
            pl.BlockSpec((1, tm, d), lambda bi, si: (bi, si, 0)),
            pl.BlockSpec((1, d), lambda bi, si: (0, 0)),
            pl.BlockSpec((d, ncols), lambda bi, si: (0, 0)),
        ],
        out_specs=[
            pl.BlockSpec((1, NSA_HEADS, tm, aug), lambda bi, si: (bi, 0, si, 0)),
            pl.BlockSpec((2, 1, tm, NSA_KV_COLS), lambda bi, si: (0, bi, si, 0)),
            pl.BlockSpec((1, NSA_KV_HEADS, tm, aug + MAX_SEL_BLOCKS), lambda bi, si: (bi, 0, si, 0)),
            pl.BlockSpec((3, 1, NSA_KV_HEADS, tm, aug), lambda bi, si: (0, bi, 0, si, 0)),
            pl.BlockSpec((1, tm, NSA_KV_HEADS * V7X_LANES), lambda bi, si: (bi, si, 0)),
            pl.BlockSpec((1, tm, 4 * RET_COLS), lambda bi, si: (bi, si, 0)),
        ],
        out_shape=[
            jax.ShapeDtypeStruct((b, NSA_HEADS, s, aug), BF16),
            jax.ShapeDtypeStruct((2, b, s, NSA_KV_COLS), BF16),
            jax.ShapeDtypeStruct((b, NSA_KV_HEADS, s, aug + MAX_SEL_BLOCKS), BF16),
            jax.ShapeDtypeStruct((3, b, NSA_KV_HEADS, s, aug), BF16),
            jax.ShapeDtypeStruct((b, s, NSA_KV_HEADS * V7X_LANES), F32),
            jax.ShapeDtypeStruct((b, s, 4 * RET_COLS), BF16),
        ],
        compiler_params=_params(("parallel", "parallel")),
        name="in_proj",
    )(h, g.reshape(1, d), w)


def _compress_kernel(x_ref, wbig_ref, pos_ref, w1_ref, w2_ref, o_ref):
    nch = x_ref.shape[2]
    u = _dot(x_ref[0, 0], wbig_ref[0])
    hid0 = _dot(pos_ref[0], w1_ref[0])[0:1]
    w2 = w2_ref[0]
    cmp_end = lax.broadcasted_iota(jnp.int32, (nch, 1), 0) * CMP_STRIDE + (CMP_BLOCK - 1)
    feat = _int_to_bf16(jnp.where(pl.program_id(0) == 0, _key_features(cmp_end, HEAD_DIM),
                                  _ones_features(nch, HEAD_DIM)))
    for g in range(NSA_KV_HEADS):
        c0 = g * 2 * CMP_HIDDEN
        first = u[:, c0:c0 + CMP_HIDDEN]
        second = u[:, c0 + CMP_HIDDEN:c0 + 2 * CMP_HIDDEN]
        hid = first + pltpu.roll(second, nch - 1, 0) + hid0
        out = _dot(jax.nn.gelu(hid).astype(BF16), w2).astype(BF16)
        o_ref[0, 0, g] = jnp.concatenate([out, feat], axis=1)


def _arrange_cmp_w1(w1):
    r = CMP_BLOCK // CMP_STRIDE
    w1r = w1.reshape(2, r, CMP_STRIDE, HEAD_DIM, CMP_HIDDEN)
    eye = jnp.eye(NSA_KV_HEADS, dtype=w1.dtype)
    big = jnp.einsum('krcdh,gf->kcgdfrh', w1r, eye)
    return big.reshape(2, CMP_STRIDE * NSA_KV_COLS, NSA_KV_HEADS * r * CMP_HIDDEN).astype(BF16)


def _compress(kcv, cmp_pos, cmp_w1, cmp_w2):
    _, b, s, _ = kcv.shape
    nch = s // CMP_STRIDE
    x = kcv.reshape(2, b, nch, CMP_STRIDE * NSA_KV_COLS)
    wbig = _arrange_cmp_w1(cmp_w1)
    pos = jnp.broadcast_to(cmp_pos.reshape(2, 1, CMP_BLOCK * HEAD_DIM), (2, 8, CMP_BLOCK * HEAD_DIM)).astype(BF16)
    w1 = cmp_w1.reshape(2, CMP_BLOCK * HEAD_DIM, CMP_HIDDEN).astype(BF16)
    w2 = cmp_w2.astype(BF16)
    kdim = CMP_STRIDE * NSA_KV_COLS
    return pl.pallas_call(
        _compress_kernel,
        grid=(2, b),
        in_specs=[
            pl.BlockSpec((1, 1, nch, kdim), lambda a, bi: (a, bi, 0, 0)),
            pl.BlockSpec((1, kdim, wbig.shape[2]), lambda a, bi: (a, 0, 0)),
            pl.BlockSpec((1, 8, CMP_BLOCK * HEAD_DIM), lambda a, bi: (a, 0, 0)),
            pl.BlockSpec((1, CMP_BLOCK * HEAD_DIM, CMP_HIDDEN), lambda a, bi: (a, 0, 0)),
            pl.BlockSpec((1, CMP_HIDDEN, HEAD_DIM), lambda a, bi: (a, 0, 0)),
        ],
        out_specs=pl.BlockSpec((1, 1, NSA_KV_HEADS, nch, 2 * HEAD_DIM), lambda a, bi: (a, bi, 0, 0, 0)),
        out_shape=jax.ShapeDtypeStruct((2, b, NSA_KV_HEADS, nch, 2 * HEAD_DIM), BF16),
        compiler_params=_params(("parallel", "parallel")),
        name="compress",
    )(x, wbig, pos, w1, w2)


def _normalise(acc):
    return acc[:, :HEAD_DIM] * (1.0 / jnp.maximum(acc[:, HEAD_DIM:HEAD_DIM + 1], 1e-30))


def _attn_kernel(q_ref, kc_ref, vc_ref, ks_ref, vs_ref, kw_ref, vw_ref, gl_ref, ovt_ref, o_ref, tiles_ref,
                 *, n_pick, tk):
    rows = NSA_GROUP * Q_BLOCK
    t0 = pl.program_id(2) * Q_BLOCK
    q = q_ref[0].reshape(rows, 2 * HEAD_DIM)
    tq = t0 + (lax.broadcasted_iota(jnp.int32, (rows, 1), 0) & (Q_BLOCK - 1))

    nc = kc_ref.shape[3]
    cmp_end = lax.broadcasted_iota(jnp.int32, (1, nc), 1) * CMP_STRIDE + (CMP_BLOCK - 1)
    valid_c = cmp_end <= tq
    logit_c = jnp.where(valid_c, _dot_nt(q, kc_ref[0, 0, 0]), NEG_INF)
    e_c = jnp.where(valid_c, jnp.exp(logit_c - jnp.max(logit_c, axis=-1, keepdims=True)), 0.0)
    p_c = e_c * (1.0 / jnp.maximum(jnp.sum(e_c, axis=-1, keepdims=True), 1e-30))
    o_c = _dot(p_c.astype(BF16), vc_ref[0, 0, 0])[:, :HEAD_DIM]

    p_sum = p_c[0:Q_BLOCK]
    for r in range(1, NSA_GROUP):
        p_sum = p_sum + p_c[r * Q_BLOCK:(r + 1) * Q_BLOCK]
    p_hi = p_sum.astype(BF16)
    p_lo = (p_sum - p_hi.astype(F32)).astype(BF16)
    imp = _dot_nt(ovt_ref[...], p_hi) + _dot_nt(ovt_ref[...], p_lo)
    shape = (MAX_SEL_BLOCKS, Q_BLOCK)
    blk = lax.broadcasted_iota(jnp.int32, shape, 0)
    back = ((t0 + lax.broadcasted_iota(jnp.int32, shape, 1)) >> SEL_SHIFT) - blk
    forced = (blk == 0) | ((back >= 0) & (back < N_LOCAL_SEL))
    score = jnp.where(forced, BIG, jnp.where(back >= 0, imp, -BIG))

    def pick(_, carry):
        work, sel = carry
        m = jnp.max(work, axis=0, keepdims=True)
        first = jnp.min(jnp.where(work == m, blk, MAX_SEL_BLOCKS), axis=0, keepdims=True)
        hit = blk == first
        return jnp.where(hit, -jnp.inf, work), jnp.where(hit, 1.0, sel)

    _, sel_t = lax.fori_loop(0, n_pick, pick, (score, jnp.zeros(shape, F32)))
    sel_bias = ((sel_t.T - 1.0) * -NEG_INF).astype(BF16)
    q_sel = jnp.concatenate([q, jnp.concatenate([sel_bias] * NSA_GROUP, axis=0)], axis=1)

    n_full = t0 // tk
    blocks_per_tile = tk // SEL_BLOCK
    block_used = jnp.max(sel_t, axis=1, keepdims=True)
    n_visit = jnp.int32(0)
    for j in range(tiles_ref.shape[0]):
        used = jnp.max(block_used[j * blocks_per_tile:(j + 1) * blocks_per_tile]) > 0.0
        tiles_ref[n_visit] = j
        n_visit = n_visit + (used & (j < n_full)).astype(jnp.int32)

    def sweep_tile(j, carry, causal):
        m_i, acc = carry
        k0 = pl.multiple_of(j * tk, tk)
        s = _dot_nt(q_sel, ks_ref[0, 0, pl.ds(k0, tk), :])
        if causal:
            s = jnp.where(k0 + lax.broadcasted_iota(jnp.int32, (1, tk), 1) <= tq, s, NEG_INF)
        m_new = jnp.maximum(m_i, jnp.max(s, axis=-1, keepdims=True))
        p = jnp.exp(s - m_new)
        acc = jnp.exp(m_i - m_new) * acc + _dot(p.astype(BF16), vs_ref[0, 0, 0, pl.ds(k0, tk), :])
        return m_new, acc

    init = (jnp.full((rows, 1), NEG_INF, F32), jnp.zeros((rows, 2 * HEAD_DIM), F32))
    carry = lax.fori_loop(0, n_visit, lambda i, c: sweep_tile(tiles_ref[i], c, causal=False), init)
    _, acc_s = sweep_tile(n_full, carry, causal=True)
    o_s = _normalise(acc_s)

    band = WINDOW + Q_BLOCK
    w0 = pl.multiple_of(jnp.maximum(t0 - WINDOW, 0), Q_BLOCK)
    kpos = w0 + lax.broadcasted_iota(jnp.int32, (1, band), 1)
    valid_w = (kpos <= tq) & (kpos > tq - WINDOW)
    logit_w = jnp.where(valid_w, _dot_nt(q, kw_ref[0, 0, 0, pl.ds(w0, band), :]), NEG_INF)
    e_w = jnp.exp(logit_w - jnp.max(logit_w, axis=-1, keepdims=True))
    o_w = _normalise(_dot(e_w.astype(BF16), vw_ref[0, 0, 0, pl.ds(w0, band), :]))

    gate = jax.nn.sigmoid(gl_ref[0])
    outs = []
    for r in range(NSA_GROUP):
        sl = slice(r * Q_BLOCK, (r + 1) * Q_BLOCK)
        outs.append(gate[:, r:r + 1] * o_c[sl]
                    + gate[:, NSA_GROUP + r:NSA_GROUP + r + 1] * o_s[sl]
                    + gate[:, 2 * NSA_GROUP + r:2 * NSA_GROUP + r + 1] * o_w[sl])
    o_ref[0] = jnp.concatenate(outs, axis=1).astype(o_ref.dtype)


def _nsa_attention(qn, kv_cmp, ks, kv3, gl):
    b, _, s, _ = qn.shape
    nc = kv_cmp.shape[3]
    nsel = s // SEL_BLOCK
    n_pick = min(N_SELECT, nsel)
    tk = min(SEL_KV_TILE, s)
    aug = 2 * HEAD_DIM
    rows = NSA_GROUP * Q_BLOCK
    cmp_start = jnp.arange(nc) * CMP_STRIDE
    sel_start = jnp.arange(nsel) * SEL_BLOCK
    overlap = (jnp.minimum(cmp_start[:, None] + CMP_BLOCK, sel_start[None, :] + SEL_BLOCK)
               > jnp.maximum(cmp_start[:, None], sel_start[None, :]))
    overlap_t = jnp.pad(overlap.T.astype(BF16), ((0, MAX_SEL_BLOCKS - nsel), (0, 0)))
    kv_spec = lambda a: pl.BlockSpec((1, 1, 1, s, aug), lambda bi, g, i, a=a: (a, bi, g, 0, 0))
    cmp_spec = lambda a: pl.BlockSpec((1, 1, 1, nc, aug), lambda bi, g, i, a=a: (a, bi, g, 0, 0))
    return pl.pallas_call(
        functools.partial(_attn_kernel, n_pick=n_pick, tk=tk),
        grid=(b, NSA_KV_HEADS, s // Q_BLOCK),
        in_specs=[
            pl.BlockSpec((1, NSA_GROUP, Q_BLOCK, aug), lambda bi, g, i: (bi, g, i, 0)),
            cmp_spec(0), cmp_spec(1),
            pl.BlockSpec((1, 1, s, aug + MAX_SEL_BLOCKS), lambda bi, g, i: (bi, g, 0, 0)),
            kv_spec(0), kv_spec(1), kv_spec(2),
            pl.BlockSpec((1, Q_BLOCK, V7X_LANES), lambda bi, g, i: (bi, i, g)),
            pl.BlockSpec((MAX_SEL_BLOCKS, nc), lambda bi, g, i: (0, 0)),
        ],
        out_specs=pl.BlockSpec((1, Q_BLOCK, NSA_GROUP * HEAD_DIM), lambda bi, g, i: (bi, i, g)),
        out_shape=jax.ShapeDtypeStruct((b, s, NSA_Q_COLS), BF16),
        scratch_shapes=[pltpu.SMEM((s // tk,), jnp.int32)],
        compiler_params=_params(("parallel", "parallel", "arbitrary")),
        name="nsa_attention",
    )(qn, kv_cmp, kv_cmp, ks, kv3, kv3, kv3, gl, overlap_t)


def _retention_kernel(q_ref, k_ref, v_ref, g_ref, decay_ref, xi_ref, zeta_ref, gch_ref, gn_ref, o_ref, state_ref):
    @pl.when(pl.program_id(1) == 0)
    def _():
        state_ref[...] = jnp.zeros_like(state_ref)

    scale = jnp.asarray(HEAD_DIM ** -0.5, BF16)
    outs = []
    for h in range(RET_HEADS):
        cols = slice(h * HEAD_DIM, (h + 1) * HEAD_DIM)
        q = q_ref[0, :, cols]
        k = k_ref[0, :, cols] * scale
        v = v_ref[0, :, cols]
        state = state_ref[h]
        inner = _dot_nt(q, k) * decay_ref[h]
        o = _dot(inner.astype(BF16), v) + _dot(q, state.astype(BF16)) * xi_ref[h]
        kz = (k.astype(F32) * zeta_ref[h]).astype(BF16)
        state_ref[h] = gch_ref[h] * state + _dot_tn(kz, v)
        mu = jnp.mean(o, axis=-1, keepdims=True)
        var = jnp.mean(jnp.square(o - mu), axis=-1, keepdims=True)
        outs.append((o - mu) * lax.rsqrt(var + EPS))
    o = jnp.concatenate(outs, axis=1) * gn_ref[...]
    o_ref[0] = (jax.nn.silu(g_ref[0].astype(F32)) * o).astype(o_ref.dtype)


def _retention(ret, gn_gain):
    b, s, _ = ret.shape
    c = RET_CHUNK
    hh = RET_HEADS
    log_gamma = jnp.log1p(-jnp.exp2(-5.0 - jnp.arange(hh, dtype=F32)))
    pos = jnp.arange(c, dtype=F32)
    diff = pos[:, None] - pos[None, :]
    decay = jnp.where(diff >= 0, jnp.exp(jnp.maximum(diff, 0.0)[None] * log_gamma[:, None, None]), 0.0)
    xi = jnp.exp((pos + 1.0)[None] * log_gamma[:, None])[..., None]
    zeta = jnp.exp((c - 1.0 - pos)[None] * log_gamma[:, None])[..., None]
    g_chunk = jnp.exp(c * log_gamma)[:, None, None]
    xi = jnp.broadcast_to(xi, (hh, c, HEAD_DIM))
    zeta = jnp.broadcast_to(zeta, (hh, c, HEAD_DIM))
    g_chunk = jnp.broadcast_to(g_chunk, (hh, HEAD_DIM, HEAD_DIM))
    part = lambda a: pl.BlockSpec((1, c, RET_COLS), lambda bi, n, a=a: (bi, n, a))
    full = lambda shape: pl.BlockSpec(shape, lambda bi, n: (0,) * len(shape))
    return pl.pallas_call(
        _retention_kernel,
        grid=(b, s // c),
        in_specs=[part(0), part(1), part(2), part(3),
                  full((hh, c, c)), full((hh, c, HEAD_DIM)), full((hh, c, HEAD_DIM)),
                  full((hh, HEAD_DIM, HEAD_DIM)), full((1, RET_COLS))],
        out_specs=pl.BlockSpec((1, c, RET_COLS), lambda bi, n: (bi, n, 0)),
        out_shape=jax.ShapeDtypeStruct((b, s, RET_COLS), BF16),
        scratch_shapes=[pltpu.VMEM((hh, HEAD_DIM, HEAD_DIM), F32)],
        compiler_params=_params(("parallel", "arbitrary")),
        name="retention",
    )(ret, ret, ret, ret, decay, xi, zeta, g_chunk, gn_gain.reshape(1, RET_COLS).astype(F32))


def _out_proj_kernel(h_ref, a_ref, r_ref, w_ref, g_ref, h_out_ref, hn_ref):
    h = h_ref[...] + _dot(a_ref[...], w_ref[:NSA_Q_COLS]) + _dot(r_ref[...], w_ref[NSA_Q_COLS:])
    h_out_ref[...] = h
    hn_ref[...] = _rms(h, g_ref[...]).astype(BF16)


def _out_proj(h, a, r, w, g):
    t, d = h.shape
    tm = min(ROW_TILE, t)
    row = lambda n: pl.BlockSpec((tm, n), lambda i: (i, 0))
    return pl.pallas_call(
        _out_proj_kernel,
        grid=(t // tm,),
        in_specs=[row(d), row(NSA_Q_COLS), row(RET_COLS),
                  pl.BlockSpec((MIX_WIDTH, d), lambda i: (0, 0)),
                  pl.BlockSpec((1, d), lambda i: (0, 0))],
        out_specs=[row(d), row(d)],
        out_shape=[jax.ShapeDtypeStruct((t, d), F32), jax.ShapeDtypeStruct((t, d), BF16)],
        compiler_params=_params(("parallel",)),
        name="out_proj",
    )(h, a, r, w.astype(BF16), g.reshape(1, d))


def _swiglu_chunk(x, wg, wu, wd):
    hid = jax.nn.silu(_dot(x, wg)) * _dot(x, wu)
    return _dot(hid.astype(BF16), wd)


def _ffn_kernel(x_ref, wg_ref, wu_ref, wd_ref, o_ref):
    x = x_ref[...]
    acc = None
    for f in range(D_FF // FFN_COL_TILE):
        cols = slice(f * FFN_COL_TILE, (f + 1) * FFN_COL_TILE)
        part = _swiglu_chunk(x, wg_ref[:, cols], wu_ref[:, cols], wd_ref[cols, :])
        acc = part if acc is None else acc + part
    o_ref[...] = acc


def _dense_ffn(hn, wg, wu, wd):
    t, d = hn.shape
    tm = min(FFN_ROW_TILE, t)
    once = pl.Buffered(1)
    return pl.pallas_call(
        _ffn_kernel,
        grid=(t // tm,),
        in_specs=[pl.BlockSpec((tm, d), lambda i: (i, 0)),
                  pl.BlockSpec((d, D_FF), lambda i: (0, 0), pipeline_mode=once),
                  pl.BlockSpec((d, D_FF), lambda i: (0, 0), pipeline_mode=once),
                  pl.BlockSpec((D_FF, d), lambda i: (0, 0), pipeline_mode=once)],
        out_specs=pl.BlockSpec((tm, d), lambda i: (i, 0)),
        out_shape=jax.ShapeDtypeStruct((t, d), F32),
        compiler_params=_params(("parallel",)),
        name="dense_ffn",
    )(hn, wg.astype(BF16), wu.astype(BF16), wd.astype(BF16))


def _lane_column(table, lane_index):
    lane = lax.broadcasted_iota(jnp.int32, table.shape, 1)
    col = jnp.sum(jnp.where(lane == lane_index, table, 0.0), axis=-1, keepdims=True)
    return jnp.broadcast_to(col, table.shape)


def _moe_kernel(x_ref, router_ref, wg_ref, wu_ref, wd_ref, o_ref,
                xs_ref, y_ref, slot_ref, gate_ref, slot_row_ref, slot_e_ref, gate_e_ref, count_ref):
    e = pl.program_id(1)
    f = pl.program_id(2)
    tm, d = x_ref.shape
    sub = MOE_GROUP_ROWS

    @pl.when((e == 0) & (f == 0))
    def _route():
        o_ref[...] = jnp.zeros_like(o_ref)
        logits = _dot(x_ref[...], router_ref[...])
        lane = lax.broadcasted_iota(jnp.int32, logits.shape, 1)
        logits = jnp.where(lane < N_EXPERTS, logits, -jnp.inf)
        v1 = jnp.max(logits, axis=-1, keepdims=True)
        i1 = jnp.min(jnp.where(logits == v1, lane, V7X_LANES), axis=-1, keepdims=True)
        rest = jnp.where(lane == i1, -jnp.inf, logits)
        v2 = jnp.max(rest, axis=-1, keepdims=True)
        i2 = jnp.min(jnp.where(rest == v2, lane, V7X_LANES), axis=-1, keepdims=True)
        e2 = jnp.exp(v2 - v1)
        inv = 1.0 / (1.0 + e2)
        gate_ref[...] = jnp.where(lane == i1, inv, 0.0) + jnp.where(lane == i2, e2 * inv, 0.0)
        routed = jnp.where((lane == i1) | (lane == i2), 1.0, 0.0)
        c = MOE_RANK_CHUNK
        before = (lax.broadcasted_iota(jnp.int32, (c, c), 1) < lax.broadcasted_iota(jnp.int32, (c, c), 0))
        before = jnp.where(before, 1.0, 0.0).astype(BF16)
        offset = jnp.zeros((1, V7X_LANES), F32)
        for j in range(tm // c):
            part = routed[j * c:(j + 1) * c]
            rank = _dot(before, part.astype(BF16)) + offset
            slot_ref[j * c:(j + 1) * c, :] = jnp.where(part > 0.0, rank, -1.0)
            offset = offset + jnp.sum(part, axis=0, keepdims=True)
        for ee in range(N_EXPERTS):
            count_ref[ee] = offset[0, ee].astype(jnp.int32)
        slot_row_ref[...] = slot_ref[...].T[:N_EXPERTS]

    n_groups = (count_ref[e] + sub - 1) // sub

    @pl.when(f == 0)
    def _gather():
        slot_e_ref[...] = _lane_column(slot_ref[...], e)
        gate_e_ref[...] = _lane_column(gate_ref[...], e)
        slot_row = slot_row_ref[pl.ds(e, 1), :]

        def body(s, _):
            r0 = pl.multiple_of(s * sub, sub)
            want = (r0 + lax.broadcasted_iota(jnp.int32, (sub, 1), 0)).astype(F32)
            onehot = jnp.where(slot_row == want, 1.0, 0.0).astype(BF16)
            xs_ref[pl.ds(r0, sub), :] = _dot(onehot, x_ref[...]).astype(BF16)
            y_ref[pl.ds(r0, sub), :] = jnp.zeros((sub, d), F32)
            return 0

        lax.fori_loop(0, n_groups, body, 0)

    def expert(s, _):
        r0 = pl.multiple_of(s * sub, sub)
        y_ref[pl.ds(r0, sub), :] += _swiglu_chunk(xs_ref[pl.ds(r0, sub), :], wg_ref[0, 0], wu_ref[0, 0], wd_ref[0])
        return 0

    lax.fori_loop(0, n_groups, expert, 0)

    @pl.when(f == pl.num_programs(2) - 1)
    def _scatter():
        tc = MOE_SCATTER_ROWS

        def body(s, _):
            r0 = pl.multiple_of(s * sub, sub)
            y = y_ref[pl.ds(r0, sub), :].astype(BF16)
            want = (r0 + lax.broadcasted_iota(jnp.int32, (1, sub), 1)).astype(F32)
            for j in range(tm // tc):
                rows = slice(j * tc, (j + 1) * tc)
                slot = jnp.concatenate([slot_e_ref[rows, :]] * (sub // V7X_LANES), axis=1)
                onehot = jnp.where(slot == want, 1.0, 0.0).astype(BF16)
                weight = jnp.concatenate([gate_e_ref[rows, :]] * (d // V7X_LANES), axis=1)
                o_ref[rows, :] += weight * _dot(onehot, y)
            return 0

        lax.fori_loop(0, n_groups, body, 0)


def _moe_ffn(hn, router, wg, wu, wd):
    t, d = hn.shape
    tm = min(MOE_ROW_TILE, t)
    tf = MOE_COL_TILE
    nf = D_FF // tf
    router = jnp.pad(router, ((0, 0), (0, V7X_LANES - N_EXPERTS))).astype(BF16)
    chunked = lambda w: jnp.transpose(w.astype(BF16).reshape(N_EXPERTS, d, nf, tf), (0, 2, 1, 3))
    once = pl.Buffered(1)
    return pl.pallas_call(
        _moe_kernel,
        grid=(t // tm, N_EXPERTS, nf),
        in_specs=[pl.BlockSpec((tm, d), lambda i, e, f: (i, 0), pipeline_mode=once),
                  pl.BlockSpec((d, V7X_LANES), lambda i, e, f: (0, 0), pipeline_mode=once),
                  pl.BlockSpec((1, 1, d, tf), lambda i, e, f: (e, f, 0, 0)),
                  pl.BlockSpec((1, 1, d, tf), lambda i, e, f: (e, f, 0, 0)),
                  pl.BlockSpec((1, tf, d), lambda i, e, f: (e, f, 0))],
        out_specs=pl.BlockSpec((tm, d), lambda i, e, f: (i, 0)),
        out_shape=jax.ShapeDtypeStruct((t, d), F32),
        scratch_shapes=[pltpu.VMEM((tm, d), BF16), pltpu.VMEM((tm, d), F32),
                        pltpu.VMEM((tm, V7X_LANES), F32), pltpu.VMEM((tm, V7X_LANES), F32),
                        pltpu.VMEM((N_EXPERTS, tm), F32),
                        pltpu.VMEM((tm, V7X_LANES), F32), pltpu.VMEM((tm, V7X_LANES), F32),
                        pltpu.SMEM((N_EXPERTS,), jnp.int32)],
        compiler_params=_params(("parallel", "arbitrary", "arbitrary"), vmem=MOE_VMEM_LIMIT),
        name="moe_ffn",
    )(hn, router, chunked(wg), chunked(wu), wd.astype(BF16))


def _ple_kernel(h_ref, f_ref, p_ref, g_ref, proj_ref, gate_ref, gf_ref, o_ref, *, final_norm):
    h = h_ref[...] + f_ref[...]
    emb = _dot(p_ref[...].astype(BF16), proj_ref[...])
    sig = jax.nn.sigmoid(_dot(_rms(h, g_ref[...]).astype(BF16), gate_ref[...]))
    h = h + emb * sig
    if final_norm:
        h = _rms(h, gf_ref[...])
    o_ref[...] = h


def _ple(h, f, p, g, proj, gate, g_final, final_norm):
    t, d = h.shape
    tm = min(ROW_TILE, t)
    vec = pl.BlockSpec((1, d), lambda i: (0, 0))
    return pl.pallas_call(
        functools.partial(_ple_kernel, final_norm=final_norm),
        grid=(t // tm,),
        in_specs=[pl.BlockSpec((tm, d), lambda i: (i, 0)),
                  pl.BlockSpec((tm, d), lambda i: (i, 0)),
                  pl.BlockSpec((tm, PLE_DIM), lambda i: (i, 0)),
                  vec,
                  pl.BlockSpec((PLE_DIM, d), lambda i: (0, 0)),
                  pl.BlockSpec((d, d), lambda i: (0, 0)),
                  vec],
        out_specs=pl.BlockSpec((tm, d), lambda i: (i, 0)),
        out_shape=jax.ShapeDtypeStruct((t, d), F32),
        compiler_params=_params(("parallel",)),
        name="ple",
    )(h, f, p, g.reshape(1, d), proj.astype(BF16), gate.astype(BF16), g_final.reshape(1, d))


def kernel(x, p, w_in, w_out, g_mix, g_ffn, g_ple, g_final, cmp_pos, cmp_w1, cmp_w2, ret_gn,
           ffn_gate, ffn_up, ffn_down, moe_router, moe_gate, moe_up, moe_down, ple_proj, ple_gate):
    b, s, d = x.shape
    depth = w_in.shape[0]
    t = b * s
    h = x
    for i in range(depth):
        qn, kcv, ks, kv3, gl, ret = _in_proj(h.reshape(b, s, d), g_mix[i], _arrange_w_in(w_in[i]))
        kv_cmp = _compress(kcv, cmp_pos[i], cmp_w1[i], cmp_w2[i])
        a = _nsa_attention(qn, kv_cmp, ks, kv3, gl)
        r = _retention(ret, ret_gn[i])
        h, hn = _out_proj(h.reshape(t, d), a.reshape(t, NSA_Q_COLS), r.reshape(t, RET_COLS), w_out[i], g_ffn[i])
        if i % 2 == 0:
            f = _dense_ffn(hn, ffn_gate[i // 2], ffn_up[i // 2], ffn_down[i // 2])
        else:
            f = _moe_ffn(hn, moe_router[i // 2], moe_gate[i // 2], moe_up[i // 2], moe_down[i // 2])
        h = _ple(h, f, p[i].reshape(t, PLE_DIM), g_ple[i], ple_proj[i], ple_gate[i], g_final, i == depth - 1)
    return h.reshape(b, s, d)
```

```python
import functools
import math

import jax
import jax.numpy as jnp
from jax import lax
from jax.experimental import pallas as pl
from jax.experimental.pallas import tpu as pltpu

F32 = jnp.float32
BF16 = jnp.bfloat16

D_MODEL = 1024
HEAD_DIM = 64
NSA_HEADS = 8
NSA_KV_HEADS = 2
NSA_GROUP = NSA_HEADS // NSA_KV_HEADS
RET_HEADS = 8
CMP_BLOCK = 32
CMP_STRIDE = 16
CMP_HIDDEN = 256
SEL_BLOCK = 64
N_SELECT = 16
N_LOCAL_SEL = 2
WINDOW = 512
Q_BLOCK = 128
RET_CHUNK = 128
D_FF = 3584
N_EXPERTS = 8
TOP_K = 2
PLE_DIM = 256
EPS = 1e-6
NEG_INF = -1e30
BIG = 1e9

NSA_Q_COLS = NSA_HEADS * HEAD_DIM
NSA_KV_COLS = NSA_KV_HEADS * HEAD_DIM
NSA_GATE_COLS = 3 * NSA_HEADS
RET_COLS = RET_HEADS * HEAD_DIM
MIX_WIDTH = NSA_Q_COLS + RET_COLS

V7X_LANES = 128
V7X_VMEM_BYTES = 64 * 1024 * 1024
VMEM_LIMIT = V7X_VMEM_BYTES * 3 // 4

ROW_TILE = 512
ATTN_Q_TILE = 256
SEL_KV_TILE = 256
FFN_ROW_TILE = 512
FFN_COL_TILE = 512
MOE_COL_TILE = 896
MOE_ROW_TILE = 2048
MOE_GROUP_ROWS = 256
MOE_RANK_CHUNK = 256
MOE_SCATTER_ROWS = 512
MOE_VMEM_LIMIT = V7X_VMEM_BYTES * 7 // 8

POS_SHIFT = 6
POS_SPLIT = 1 << POS_SHIFT
SEL_SHIFT = SEL_BLOCK.bit_length() - 1
MAX_SEL_BLOCKS = V7X_LANES


def _params(sem, vmem=VMEM_LIMIT):
    return pltpu.CompilerParams(dimension_semantics=sem, vmem_limit_bytes=vmem)


def _dot(a, b):
    return jnp.dot(a, b, preferred_element_type=F32)


def _dot_nt(a, b):
    return lax.dot_general(a, b, (((1,), (1,)), ((), ())), preferred_element_type=F32)


def _dot_tn(a, b):
    return lax.dot_general(a, b, (((0,), (0,)), ((), ())), preferred_element_type=F32)


def _rms(x, g):
    return x * lax.rsqrt(jnp.mean(x * x, axis=-1, keepdims=True) + EPS) * g


def _lane_features(shape, first, second):
    lane = lax.broadcasted_iota(jnp.int32, shape, 1)
    return jnp.where(lane == 0, first, jnp.where(lane == 1, second, 0))


def _key_features(pos, width):
    return _lane_features((pos.shape[0], width), pos >> POS_SHIFT, pos & (POS_SPLIT - 1))


def _ones_features(n, width):
    return _lane_features((n, width), 1, 0)


def _int_to_bf16(x):
    return x.astype(F32).astype(BF16)


def _in_proj_kernel(h_ref, g_ref, w_ref, qn_ref, kcv_ref, ks_ref, kv3_ref, gl_ref, ret_ref):
    tm = h_ref.shape[1]
    xn = _rms(h_ref[0], g_ref[...]).astype(BF16)
    pair = 2 * HEAD_DIM
    pos = pl.program_id(1) * tm + lax.broadcasted_iota(jnp.int32, (tm, 1), 0)
    kfeat = _int_to_bf16(_key_features(pos, HEAD_DIM))
    vfeat = _int_to_bf16(_ones_features(tm, HEAD_DIM))
    lane = lax.broadcasted_iota(jnp.int32, (tm, MAX_SEL_BLOCKS), 1)
    block_onehot = jnp.where(lane == (pos >> SEL_SHIFT), 1.0, 0.0).astype(BF16)

    for j in range(NSA_HEADS // 2):
        z = (_dot(xn, w_ref[:, j * pair:(j + 1) * pair]) * HEAD_DIM ** -0.5).astype(BF16)
        for k in range(2):
            slope = 2.0 ** -(2 * j + k + 1)
            qfeat = _lane_features((tm, HEAD_DIM), POS_SPLIT * slope, slope).astype(BF16)
            qn_ref[0, 2 * j + k] = jnp.concatenate([z[:, k * HEAD_DIM:(k + 1) * HEAD_DIM], qfeat], axis=1)
    base = NSA_Q_COLS
    for a in range(2):
        kcv_ref[a, 0] = _dot(xn, w_ref[:, base + a * pair:base + (a + 1) * pair]).astype(BF16)
    base += 2 * pair
    for a in range(4):
        z = _dot(xn, w_ref[:, base + a * pair:base + (a + 1) * pair]).astype(BF16)
        for g in range(NSA_KV_HEADS):
            zg = z[:, g * HEAD_DIM:(g + 1) * HEAD_DIM]
            if a == 0:
                ks_ref[0, g] = jnp.concatenate([zg, kfeat, block_onehot], axis=1)
            else:
                kv3_ref[a - 1, 0, g] = jnp.concatenate([zg, kfeat if a == 2 else vfeat], axis=1)
    base += 4 * pair
    for a in range(4):
        ret_ref[0, :, a * RET_COLS:(a + 1) * RET_COLS] = _dot(
            xn, w_ref[:, base + a * RET_COLS:base + (a + 1) * RET_COLS]).astype(BF16)
    base += 4 * RET_COLS
    gl_ref[0] = _dot(xn, w_ref[:, base:base + NSA_KV_HEADS * V7X_LANES])


def _arrange_w_in(w):
    q_end = NSA_Q_COLS
    kv_end = q_end + 6 * NSA_KV_COLS
    gl_end = kv_end + NSA_GATE_COLS
    gl = w[:, kv_end:gl_end].reshape(D_MODEL, 3, NSA_KV_HEADS, NSA_GROUP)
    gl = jnp.transpose(gl, (0, 2, 1, 3)).reshape(D_MODEL, NSA_KV_HEADS, 3 * NSA_GROUP)
    gl = jnp.pad(gl, ((0, 0), (0, 0), (0, V7X_LANES - 3 * NSA_GROUP)))
    gl = gl.reshape(D_MODEL, NSA_KV_HEADS * V7X_LANES)
    return jnp.concatenate([w[:, :kv_end], w[:, gl_end:], gl], axis=1).astype(BF16)


def _in_proj(h, g, w):
    b, s, d = h.shape
    assert s // SEL_BLOCK <= MAX_SEL_BLOCKS
    tm = min(ROW_TILE, s)
    ncols = w.shape[1]
    aug = 2 * HEAD_DIM
    return pl.pallas_call(
        _in_proj_kernel,
        grid=(b, s // tm),
        in_specs=[
            pl.BlockSpec((1, tm, d), lambda bi, si: (bi, si, 0)),
            pl.BlockSpec((1, d), lambda bi, si: (0, 0)),
            pl.BlockSpec((d, ncols), lambda bi, si: (0, 0)),
        ],
        out_specs=[
            pl.BlockSpec((1, NSA_HEADS, tm, aug), lambda bi, si: (bi, 0, si, 0)),
            pl.BlockSpec((2, 1, tm, NSA_KV_COLS), lambda bi, si: (0, bi, si, 0)),
            pl.BlockSpec((1, NSA_KV_HEADS, tm, aug + MAX_SEL_BLOCKS), lambda bi, si: (bi, 0, si, 0)),
            pl.BlockSpec((3, 1, NSA_KV_HEADS, tm, aug), lambda bi, si: (0, bi, 0, si, 0)),
            pl.BlockSpec((1, tm, NSA_KV_HEADS * V7X_LANES), lambda bi, si: (bi, si, 0)),
            pl.BlockSpec((1, tm, 4 * RET_COLS), lambda bi, si: (bi, si, 0)),
        ],
        out_shape=[
            jax.ShapeDtypeStruct((b, NSA_HEADS, s, aug), BF16),
            jax.ShapeDtypeStruct((2, b, s, NSA_KV_COLS), BF16),
            jax.ShapeDtypeStruct((b, NSA_KV_HEADS, s, aug + MAX_SEL_BLOCKS), BF16),
            jax.ShapeDtypeStruct((3, b, NSA_KV_HEADS, s, aug), BF16),
            jax.ShapeDtypeStruct((b, s, NSA_KV_HEADS * V7X_LANES), F32),
            jax.ShapeDtypeStruct((b, s, 4 * RET_COLS), BF16),
        ],
        compiler_params=_params(("parallel", "parallel")),
        name="in_proj",
    )(h, g.reshape(1, d), w)


def _compress_kernel(x_ref, wbig_ref, pos_ref, w1_ref, w2_ref, o_ref):
    nch = x_ref.shape[2]
    u = _dot(x_ref[0, 0], wbig_ref[0])
    hid0 = _dot(pos_ref[0], w1_ref[0])[0:1]
    w2 = w2_ref[0]
    cmp_end = lax.broadcasted_iota(jnp.int32, (nch, 1), 0) * CMP_STRIDE + (CMP_BLOCK - 1)
    feat = _int_to_bf16(jnp.where(pl.program_id(0) == 0, _key_features(cmp_end, HEAD_DIM),
                                  _ones_features(nch, HEAD_DIM)))
    for g in range(NSA_KV_HEADS):
        c0 = g * 2 * CMP_HIDDEN
        first = u[:, c0:c0 + CMP_HIDDEN]
        second = u[:, c0 + CMP_HIDDEN:c0 + 2 * CMP_HIDDEN]
        hid = first + pltpu.roll(second, nch - 1, 0) + hid0
        out = _dot(jax.nn.gelu(hid).astype(BF16), w2).astype(BF16)
        o_ref[0, 0, g] = jnp.concatenate([out, feat], axis=1)


def _arrange_cmp_w1(w1):
    r = CMP_BLOCK // CMP_STRIDE
    w1r = w1.reshape(2, r, CMP_STRIDE, HEAD_DIM, CMP_HIDDEN)
    eye = jnp.eye(NSA_KV_HEADS, dtype=w1.dtype)
    big = jnp.einsum('krcdh,gf->kcgdfrh', w1r, eye)
    return big.reshape(2, CMP_STRIDE * NSA_KV_COLS, NSA_KV_HEADS * r * CMP_HIDDEN).astype(BF16)


def _compress(kcv, cmp_pos, cmp_w1, cmp_w2):
    _, b, s, _ = kcv.shape
    nch = s // CMP_STRIDE
    x = kcv.reshape(2, b, nch, CMP_STRIDE * NSA_KV_COLS)
    wbig = _arrange_cmp_w1(cmp_w1)
    pos = jnp.broadcast_to(cmp_pos.reshape(2, 1, CMP_BLOCK * HEAD_DIM), (2, 8, CMP_BLOCK * HEAD_DIM)).astype(BF16)
    w1 = cmp_w1.reshape(2, CMP_BLOCK * HEAD_DIM, CMP_HIDDEN).astype(BF16)
    w2 = cmp_w2.astype(BF16)
    kdim = CMP_STRIDE * NSA_KV_COLS
    return pl.pallas_call(
        _compress_kernel,
        grid=(2, b),
        in_specs=[
            pl.BlockSpec((1, 1, nch, kdim), lambda a, bi: (a, bi, 0, 0)),
            pl.BlockSpec((1, kdim, wbig.shape[2]), lambda a, bi: (a, 0, 0)),
            pl.BlockSpec((1, 8, CMP_BLOCK * HEAD_DIM), lambda a, bi: (a, 0, 0)),
            pl.BlockSpec((1, CMP_BLOCK * HEAD_DIM, CMP_HIDDEN), lambda a, bi: (a, 0, 0)),
            pl.BlockSpec((1, CMP_HIDDEN, HEAD_DIM), lambda a, bi: (a, 0, 0)),
        ],
        out_specs=pl.BlockSpec((1, 1, NSA_KV_HEADS, nch, 2 * HEAD_DIM), lambda a, bi: (a, bi, 0, 0, 0)),
        out_shape=jax.ShapeDtypeStruct((2, b, NSA_KV_HEADS, nch, 2 * HEAD_DIM), BF16),
        compiler_params=_params(("parallel", "parallel")),
        name="compress",
    )(x, wbig, pos, w1, w2)


def _normalise(acc):
    return acc[:, :HEAD_DIM] * (1.0 / jnp.maximum(acc[:, HEAD_DIM:HEAD_DIM + 1], 1e-30))


def _attn_kernel(q_ref, kc_ref, vc_ref, ks_ref, vs_ref, kw_ref, vw_ref, gl_ref, ovt_ref, o_ref, tiles_ref,
                 *, n_pick, tk):
    qt = q_ref.shape[2]
    rows = NSA_GROUP * qt
    t0 = pl.program_id(2) * qt
    q = q_ref[0].reshape(rows, 2 * HEAD_DIM)
    tq = t0 + (lax.broadcasted_iota(jnp.int32, (rows, 1), 0) & (qt - 1))

    nc = kc_ref.shape[3]
    cmp_end = lax.broadcasted_iota(jnp.int32, (1, nc), 1) * CMP_STRIDE + (CMP_BLOCK - 1)
    valid_c = cmp_end <= tq
    logit_c = jnp.where(valid_c, _dot_nt(q, kc_ref[0, 0, 0]), NEG_INF)
    e_c = jnp.where(valid_c, jnp.exp(logit_c - jnp.max(logit_c, axis=-1, keepdims=True)), 0.0)
    p_c = e_c * (1.0 / jnp.maximum(jnp.sum(e_c, axis=-1, keepdims=True), 1e-30))
    o_c = _dot(p_c.astype(BF16), vc_ref[0, 0, 0])[:, :HEAD_DIM]

    p_sum = p_c[0:qt]
    for r in range(1, NSA_GROUP):
        p_sum = p_sum + p_c[r * qt:(r + 1) * qt]
    p_hi = p_sum.astype(BF16)
    p_lo = (p_sum - p_hi.astype(F32)).astype(BF16)
    imp = _dot_nt(ovt_ref[...], p_hi) + _dot_nt(ovt_ref[...], p_lo)
    shape = (MAX_SEL_BLOCKS, qt)
    blk = lax.broadcasted_iota(jnp.int32, shape, 0)
    back = ((t0 + lax.broadcasted_iota(jnp.int32, shape, 1)) >> SEL_SHIFT) - blk
    forced = (blk == 0) | ((back >= 0) & (back < N_LOCAL_SEL))
    score = jnp.where(forced, BIG, jnp.where(back >= 0, imp, -BIG))

    def pick(_, carry):
        work, sel = carry
        m = jnp.max(work, axis=0, keepdims=True)
        first = jnp.min(jnp.where(work == m, blk, MAX_SEL_BLOCKS), axis=0, keepdims=True)
        hit = blk == first
        return jnp.where(hit, -jnp.inf, work), jnp.where(hit, 1.0, sel)

    _, sel_t = lax.fori_loop(0, n_pick, pick, (score, jnp.zeros(shape, F32)))
    sel_bias = ((sel_t.T - 1.0) * -NEG_INF).astype(BF16)
    q_sel = jnp.concatenate([q, jnp.concatenate([sel_bias] * NSA_GROUP, axis=0)], axis=1)

    n_own = max(1, qt // tk)
    n_full = t0 // tk
    blocks_per_tile = tk // SEL_BLOCK
    block_used = jnp.max(sel_t, axis=1, keepdims=True)
    n_visit = jnp.int32(0)
    for j in range(tiles_ref.shape[0]):
        used = jnp.max(block_used[j * blocks_per_tile:(j + 1) * blocks_per_tile]) > 0.0
        tiles_ref[n_visit] = j
        n_visit = n_visit + (used & (j < n_full)).astype(jnp.int32)

    def sweep_tile(j, carry, causal):
        m_i, acc = carry
        k0 = pl.multiple_of(j * tk, tk)
        s = _dot_nt(q_sel, ks_ref[0, 0, pl.ds(k0, tk), :])
        if causal:
            s = jnp.where(k0 + lax.broadcasted_iota(jnp.int32, (1, tk), 1) <= tq, s, NEG_INF)
        m_new = jnp.maximum(m_i, jnp.max(s, axis=-1, keepdims=True))
        p = jnp.exp(s - m_new)
        acc = jnp.exp(m_i - m_new) * acc + _dot(p.astype(BF16), vs_ref[0, 0, 0, pl.ds(k0, tk), :])
        return m_new, acc

    init = (jnp.full((rows, 1), NEG_INF, F32), jnp.zeros((rows, 2 * HEAD_DIM), F32))
    carry = lax.fori_loop(0, n_visit, lambda i, c: sweep_tile(tiles_ref[i], c, causal=False), init)
    for j in range(n_own):
        carry = sweep_tile(n_full + j, carry, causal=True)
    o_s = _normalise(carry[1])

    band = WINDOW + qt
    w0 = pl.multiple_of(jnp.maximum(t0 - WINDOW, 0), math.gcd(qt, WINDOW))
    kpos = w0 + lax.broadcasted_iota(jnp.int32, (1, band), 1)
    valid_w = (kpos <= tq) & (kpos > tq - WINDOW)
    logit_w = jnp.where(valid_w, _dot_nt(q, kw_ref[0, 0, 0, pl.ds(w0, band), :]), NEG_INF)
    e_w = jnp.exp(logit_w - jnp.max(logit_w, axis=-1, keepdims=True))
    o_w = _normalise(_dot(e_w.astype(BF16), vw_ref[0, 0, 0, pl.ds(w0, band), :]))

    gate = jax.nn.sigmoid(gl_ref[0])
    outs = []
    for r in range(NSA_GROUP):
        sl = slice(r * qt, (r + 1) * qt)
        outs.append(gate[:, r:r + 1] * o_c[sl]
                    + gate[:, NSA_GROUP + r:NSA_GROUP + r + 1] * o_s[sl]
                    + gate[:, 2 * NSA_GROUP + r:2 * NSA_GROUP + r + 1] * o_w[sl])
    o_ref[0] = jnp.concatenate(outs, axis=1).astype(o_ref.dtype)


def _nsa_attention(qn, kv_cmp, ks, kv3, gl):
    b, _, s, _ = qn.shape
    nc = kv_cmp.shape[3]
    nsel = s // SEL_BLOCK
    n_pick = min(N_SELECT, nsel)
    tk = min(SEL_KV_TILE, s)
    aug = 2 * HEAD_DIM
    qt = min(ATTN_Q_TILE, s)
    assert qt % tk == 0 or tk % qt == 0
    cmp_start = jnp.arange(nc) * CMP_STRIDE
    sel_start = jnp.arange(nsel) * SEL_BLOCK
    overlap = (jnp.minimum(cmp_start[:, None] + CMP_BLOCK, sel_start[None, :] + SEL_BLOCK)
               > jnp.maximum(cmp_start[:, None], sel_start[None, :]))
    overlap_t = jnp.pad(overlap.T.astype(BF16), ((0, MAX_SEL_BLOCKS - nsel), (0, 0)))
    kv_spec = lambda a: pl.BlockSpec((1, 1, 1, s, aug), lambda bi, g, i, a=a: (a, bi, g, 0, 0))
    cmp_spec = lambda a: pl.BlockSpec((1, 1, 1, nc, aug), lambda bi, g, i, a=a: (a, bi, g, 0, 0))
    return pl.pallas_call(
        functools.partial(_attn_kernel, n_pick=n_pick, tk=tk),
        grid=(b, NSA_KV_HEADS, s // qt),
        in_specs=[
            pl.BlockSpec((1, NSA_GROUP, qt, aug), lambda bi, g, i: (bi, g, i, 0)),
            cmp_spec(0), cmp_spec(1),
            pl.BlockSpec((1, 1, s, aug + MAX_SEL_BLOCKS), lambda bi, g, i: (bi, g, 0, 0)),
            kv_spec(0), kv_spec(1), kv_spec(2),
            pl.BlockSpec((1, qt, V7X_LANES), lambda bi, g, i: (bi, i, g)),
            pl.BlockSpec((MAX_SEL_BLOCKS, nc), lambda bi, g, i: (0, 0)),
        ],
        out_specs=pl.BlockSpec((1, qt, NSA_GROUP * HEAD_DIM), lambda bi, g, i: (bi, i, g)),
        out_shape=jax.ShapeDtypeStruct((b, s, NSA_Q_COLS), BF16),
        scratch_shapes=[pltpu.SMEM((s // tk,), jnp.int32)],
        compiler_params=_params(("parallel", "parallel", "arbitrary")),
        name="nsa_attention",
    )(qn, kv_cmp, kv_cmp, ks, kv3, kv3, kv3, gl, overlap_t)


def _retention_kernel(q_ref, k_ref, v_ref, g_ref, decay_ref, xi_ref, zeta_ref, gch_ref, gn_ref, o_ref, state_ref):
    @pl.when(pl.program_id(1) == 0)
    def _():
        state_ref[...] = jnp.zeros_like(state_ref)

    scale = jnp.asarray(HEAD_DIM ** -0.5, BF16)
    outs = []
    for h in range(RET_HEADS):
        cols = slice(h * HEAD_DIM, (h + 1) * HEAD_DIM)
        q = q_ref[0, :, cols]
        k = k_ref[0, :, cols] * scale
        v = v_ref[0, :, cols]
        state = state_ref[h]
        inner = _dot_nt(q, k) * decay_ref[h]
        o = _dot(inner.astype(BF16), v) + _dot(q, state.astype(BF16)) * xi_ref[h]
        kz = (k.astype(F32) * zeta_ref[h]).astype(BF16)
        state_ref[h] = gch_ref[h] * state + _dot_tn(kz, v)
        mu = jnp.mean(o, axis=-1, keepdims=True)
        var = jnp.mean(jnp.square(o - mu), axis=-1, keepdims=True)
        outs.append((o - mu) * lax.rsqrt(var + EPS))
    o = jnp.concatenate(outs, axis=1) * gn_ref[...]
    o_ref[0] = (jax.nn.silu(g_ref[0].astype(F32)) * o).astype(o_ref.dtype)


def _retention(ret, gn_gain):
    b, s, _ = ret.shape
    c = RET_CHUNK
    hh = RET_HEADS
    log_gamma = jnp.log1p(-jnp.exp2(-5.0 - jnp.arange(hh, dtype=F32)))
    pos = jnp.arange(c, dtype=F32)
    diff = pos[:, None] - pos[None, :]
    decay = jnp.where(diff >= 0, jnp.exp(jnp.maximum(diff, 0.0)[None] * log_gamma[:, None, None]), 0.0)
    xi = jnp.exp((pos + 1.0)[None] * log_gamma[:, None])[..., None]
    zeta = jnp.exp((c - 1.0 - pos)[None] * log_gamma[:, None])[..., None]
    g_chunk = jnp.exp(c * log_gamma)[:, None, None]
    xi = jnp.broadcast_to(xi, (hh, c, HEAD_DIM))
    zeta = jnp.broadcast_to(zeta, (hh, c, HEAD_DIM))
    g_chunk = jnp.broadcast_to(g_chunk, (hh, HEAD_DIM, HEAD_DIM))
    part = lambda a: pl.BlockSpec((1, c, RET_COLS), lambda bi, n, a=a: (bi, n, a))
    full = lambda shape: pl.BlockSpec(shape, lambda bi, n: (0,) * len(shape))
    return pl.pallas_call(
        _retention_kernel,
        grid=(b, s // c),
        in_specs=[part(0), part(1), part(2), part(3),
                  full((hh, c, c)), full((hh, c, HEAD_DIM)), full((hh, c, HEAD_DIM)),
                  full((hh, HEAD_DIM, HEAD_DIM)), full((1, RET_COLS))],
        out_specs=pl.BlockSpec((1, c, RET_COLS), lambda bi, n: (bi, n, 0)),
        out_shape=jax.ShapeDtypeStruct((b, s, RET_COLS), BF16),
        scratch_shapes=[pltpu.VMEM((hh, HEAD_DIM, HEAD_DIM), F32)],
        compiler_params=_params(("parallel", "arbitrary")),
        name="retention",
    )(ret, ret, ret, ret, decay, xi, zeta, g_chunk, gn_gain.reshape(1, RET_COLS).astype(F32))


def _out_proj_kernel(h_ref, a_ref, r_ref, w_ref, g_ref, h_out_ref, hn_ref):
    h = h_ref[...] + _dot(a_ref[...], w_ref[:NSA_Q_COLS]) + _dot(r_ref[...], w_ref[NSA_Q_COLS:])
    h_out_ref[...] = h
    hn_ref[...] = _rms(h, g_ref[...]).astype(BF16)


def _out_proj(h, a, r, w, g):
    t, d = h.shape
    tm = min(ROW_TILE, t)
    row = lambda n: pl.BlockSpec((tm, n), lambda i: (i, 0))
    return pl.pallas_call(
        _out_proj_kernel,
        grid=(t // tm,),
        in_specs=[row(d), row(NSA_Q_COLS), row(RET_COLS),
                  pl.BlockSpec((MIX_WIDTH, d), lambda i: (0, 0)),
                  pl.BlockSpec((1, d), lambda i: (0, 0))],
        out_specs=[row(d), row(d)],
        out_shape=[jax.ShapeDtypeStruct((t, d), F32), jax.ShapeDtypeStruct((t, d), BF16)],
        compiler_params=_params(("parallel",)),
        name="out_proj",
    )(h, a, r, w.astype(BF16), g.reshape(1, d))


def _swiglu_chunk(x, wg, wu, wd):
    hid = jax.nn.silu(_dot(x, wg)) * _dot(x, wu)
    return _dot(hid.astype(BF16), wd)


def _ffn_kernel(x_ref, wg_ref, wu_ref, wd_ref, o_ref):
    x = x_ref[...]
    acc = None
    for f in range(D_FF // FFN_COL_TILE):
        cols = slice(f * FFN_COL_TILE, (f + 1) * FFN_COL_TILE)
        part = _swiglu_chunk(x, wg_ref[:, cols], wu_ref[:, cols], wd_ref[cols, :])
        acc = part if acc is None else acc + part
    o_ref[...] = acc


def _dense_ffn(hn, wg, wu, wd):
    t, d = hn.shape
    tm = min(FFN_ROW_TILE, t)
    once = pl.Buffered(1)
    return pl.pallas_call(
        _ffn_kernel,
        grid=(t // tm,),
        in_specs=[pl.BlockSpec((tm, d), lambda i: (i, 0)),
                  pl.BlockSpec((d, D_FF), lambda i: (0, 0), pipeline_mode=once),
                  pl.BlockSpec((d, D_FF), lambda i: (0, 0), pipeline_mode=once),
                  pl.BlockSpec((D_FF, d), lambda i: (0, 0), pipeline_mode=once)],
        out_specs=pl.BlockSpec((tm, d), lambda i: (i, 0)),
        out_shape=jax.ShapeDtypeStruct((t, d), F32),
        compiler_params=_params(("parallel",)),
        name="dense_ffn",
    )(hn, wg.astype(BF16), wu.astype(BF16), wd.astype(BF16))


def _lane_column(table, lane_index):
    lane = lax.broadcasted_iota(jnp.int32, table.shape, 1)
    col = jnp.sum(jnp.where(lane == lane_index, table, 0.0), axis=-1, keepdims=True)
    return jnp.broadcast_to(col, table.shape)


def _moe_kernel(x_ref, router_ref, wg_ref, wu_ref, wd_ref, o_ref,
                xs_ref, y_ref, slot_ref, gate_ref, slot_row_ref, slot_e_ref, gate_e_ref, count_ref):
    e = pl.program_id(1)
    f = pl.program_id(2)
    tm, d = x_ref.shape
    sub = MOE_GROUP_ROWS

    @pl.when((e == 0) & (f == 0))
    def _route():
        o_ref[...] = jnp.zeros_like(o_ref)
        logits = _dot(x_ref[...], router_ref[...])
        lane = lax.broadcasted_iota(jnp.int32, logits.shape, 1)
        logits = jnp.where(lane < N_EXPERTS, logits, -jnp.inf)
        v1 = jnp.max(logits, axis=-1, keepdims=True)
        i1 = jnp.min(jnp.where(logits == v1, lane, V7X_LANES), axis=-1, keepdims=True)
        rest = jnp.where(lane == i1, -jnp.inf, logits)
        v2 = jnp.max(rest, axis=-1, keepdims=True)
        i2 = jnp.min(jnp.where(rest == v2, lane, V7X_LANES), axis=-1, keepdims=True)
        e2 = jnp.exp(v2 - v1)
        inv = 1.0 / (1.0 + e2)
        gate_ref[...] = jnp.where(lane == i1, inv, 0.0) + jnp.where(lane == i2, e2 * inv, 0.0)
        routed = jnp.where((lane == i1) | (lane == i2), 1.0, 0.0)
        c = MOE_RANK_CHUNK
        before = (lax.broadcasted_iota(jnp.int32, (c, c), 1) < lax.broadcasted_iota(jnp.int32, (c, c), 0))
        before = jnp.where(before, 1.0, 0.0).astype(BF16)
        offset = jnp.zeros((1, V7X_LANES), F32)
        for j in range(tm // c):
            part = routed[j * c:(j + 1) * c]
            rank = _dot(before, part.astype(BF16)) + offset
            slot_ref[j * c:(j + 1) * c, :] = jnp.where(part > 0.0, rank, -1.0)
            offset = offset + jnp.sum(part, axis=0, keepdims=True)
        for ee in range(N_EXPERTS):
            count_ref[ee] = offset[0, ee].astype(jnp.int32)
        slot_row_ref[...] = slot_ref[...].T[:N_EXPERTS]

    n_groups = (count_ref[e] + sub - 1) // sub

    @pl.when(f == 0)
    def _gather():
        slot_e_ref[...] = _lane_column(slot_ref[...], e)
        gate_e_ref[...] = _lane_column(gate_ref[...], e)
        slot_row = slot_row_ref[pl.ds(e, 1), :]

        def body(s, _):
            r0 = pl.multiple_of(s * sub, sub)
            want = (r0 + lax.broadcasted_iota(jnp.int32, (sub, 1), 0)).astype(F32)
            onehot = jnp.where(slot_row == want, 1.0, 0.0).astype(BF16)
            xs_ref[pl.ds(r0, sub), :] = _dot(onehot, x_ref[...]).astype(BF16)
            y_ref[pl.ds(r0, sub), :] = jnp.zeros((sub, d), F32)
            return 0

        lax.fori_loop(0, n_groups, body, 0)

    def expert(s, _):
        r0 = pl.multiple_of(s * sub, sub)
        y_ref[pl.ds(r0, sub), :] += _swiglu_chunk(xs_ref[pl.ds(r0, sub), :], wg_ref[0, 0], wu_ref[0, 0], wd_ref[0])
        return 0

    lax.fori_loop(0, n_groups, expert, 0)

    @pl.when(f == pl.num_programs(2) - 1)
    def _scatter():
        tc = MOE_SCATTER_ROWS

        def body(s, _):
            r0 = pl.multiple_of(s * sub, sub)
            y = y_ref[pl.ds(r0, sub), :].astype(BF16)
            want = (r0 + lax.broadcasted_iota(jnp.int32, (1, sub), 1)).astype(F32)
            for j in range(tm // tc):
                rows = slice(j * tc, (j + 1) * tc)
                slot = jnp.concatenate([slot_e_ref[rows, :]] * (sub // V7X_LANES), axis=1)
                onehot = jnp.where(slot == want, 1.0, 0.0).astype(BF16)
                weight = jnp.concatenate([gate_e_ref[rows, :]] * (d // V7X_LANES), axis=1)
                o_ref[rows, :] += weight * _dot(onehot, y)
            return 0

        lax.fori_loop(0, n_groups, body, 0)


def _moe_ffn(hn, router, wg, wu, wd):
    t, d = hn.shape
    tm = min(MOE_ROW_TILE, t)
    tf = MOE_COL_TILE
    nf = D_FF // tf
    router = jnp.pad(router, ((0, 0), (0, V7X_LANES - N_EXPERTS))).astype(BF16)
    chunked = lambda w: jnp.transpose(w.astype(BF16).reshape(N_EXPERTS, d, nf, tf), (0, 2, 1, 3))
    once = pl.Buffered(1)
    return pl.pallas_call(
        _moe_kernel,
        grid=(t // tm, N_EXPERTS, nf),
        in_specs=[pl.BlockSpec((tm, d), lambda i, e, f: (i, 0), pipeline_mode=once),
                  pl.BlockSpec((d, V7X_LANES), lambda i, e, f: (0, 0), pipeline_mode=once),
                  pl.BlockSpec((1, 1, d, tf), lambda i, e, f: (e, f, 0, 0)),
                  pl.BlockSpec((1, 1, d, tf), lambda i, e, f: (e, f, 0, 0)),
                  pl.BlockSpec((1, tf, d), lambda i, e, f: (e, f, 0))],
        out_specs=pl.BlockSpec((tm, d), lambda i, e, f: (i, 0)),
        out_shape=jax.ShapeDtypeStruct((t, d), F32),
        scratch_shapes=[pltpu.VMEM((tm, d), BF16), pltpu.VMEM((tm, d), F32),
                        pltpu.VMEM((tm, V7X_LANES), F32), pltpu.VMEM((tm, V7X_LANES), F32),
                        pltpu.VMEM((N_EXPERTS, tm), F32),
                        pltpu.VMEM((tm, V7X_LANES), F32), pltpu.VMEM((tm, V7X_LANES), F32),
                        pltpu.SMEM((N_EXPERTS,), jnp.int32)],
        compiler_params=_params(("parallel", "arbitrary", "arbitrary"), vmem=MOE_VMEM_LIMIT),
        name="moe_ffn",
    )(hn, router, chunked(wg), chunked(wu), wd.astype(BF16))


def _ple_kernel(h_ref, f_ref, p_ref, g_ref, proj_ref, gate_ref, gf_ref, o_ref, *, final_norm):
    h = h_ref[...] + f_ref[...]
    emb = _dot(p_ref[...].astype(BF16), proj_ref[...])
    sig = jax.nn.sigmoid(_dot(_rms(h, g_ref[...]).astype(BF16), gate_ref[...]))
    h = h + emb * sig
    if final_norm:
        h = _rms(h, gf_ref[...])
    o_ref[...] = h


def _ple(h, f, p, g, proj, gate, g_final, final_norm):
    t, d = h.shape
    tm = min(ROW_TILE, t)
    vec = pl.BlockSpec((1, d), lambda i: (0, 0))
    return pl.pallas_call(
        functools.partial(_ple_kernel, final_norm=final_norm),
        grid=(t // tm,),
        in_specs=[pl.BlockSpec((tm, d), lambda i: (i, 0)),
                  pl.BlockSpec((tm, d), lambda i: (i, 0)),
                  pl.BlockSpec((tm, PLE_DIM), lambda i: (i, 0)),
                  vec,
                  pl.BlockSpec((PLE_DIM, d), lambda i: (0, 0)),
                  pl.BlockSpec((d, d), lambda i: (0, 0)),
                  vec],
        out_specs=pl.BlockSpec((tm, d), lambda i: (i, 0)),
        out_shape=jax.ShapeDtypeStruct((t, d), F32),
        compiler_params=_params(("parallel",)),
        name="ple",
    )(h, f, p, g.reshape(1, d), proj.astype(BF16), gate.astype(BF16), g_final.reshape(1, d))


def kernel(x, p, w_in, w_out, g_mix, g_ffn, g_ple, g_final, cmp_pos, cmp_w1, cmp_w2, ret_gn,
           ffn_gate, ffn_up, ffn_down, moe_router, moe_gate, moe_up, moe_down, ple_proj, ple_gate):
    b, s, d = x.shape
    depth = w_in.shape[0]
    t = b * s
    h = x
    for i in range(depth):
        qn, kcv, ks, kv3, gl, ret = _in_proj(h.reshape(b, s, d), g_mix[i], _arrange_w_in(w_in[i]))
        kv_cmp = _compress(kcv, cmp_pos[i], cmp_w1[i], cmp_w2[i])
        a = _nsa_attention(qn, kv_cmp, ks, kv3, gl)
        r = _retention(ret, ret_gn[i])
        h, hn = _out_proj(h.reshape(t, d), a.reshape(t, NSA_Q_COLS), r.reshape(t, RET_COLS), w_out[i], g_ffn[i])
        if i % 2 == 0:
            f = _dense_ffn(hn, ffn_gate[i // 2], ffn_up[i // 2], ffn_down[i // 2])
        else:
            f = _moe_ffn(hn, moe_router[i // 2], moe_gate[i // 2], moe_up[i // 2], moe_down[i // 2])
        h = _ple(h, f, p[i].reshape(t, PLE_DIM), g_ple[i], ple_proj[i], ple_gate[i], g_final, i == depth - 1)
    return h.reshape(b, s, d)
```

```python
import functools
import math

import jax
import jax.numpy as jnp
from jax import lax
from jax.experimental import pallas as pl
from jax.experimental.pallas import tpu as pltpu

F32 = jnp.float32
BF16 = jnp.bfloat16

D_MODEL = 1024
HEAD_DIM = 64
NSA_HEADS = 8
NSA_KV_HEADS = 2
NSA_GROUP = NSA_HEADS // NSA_KV_HEADS
RET_HEADS = 8
CMP_BLOCK = 32
CMP_STRIDE = 16
CMP_HIDDEN = 256
SEL_BLOCK = 64
N_SELECT = 16
N_LOCAL_SEL = 2
WINDOW = 512
Q_BLOCK = 128
RET_CHUNK = 128
D_FF = 3584
N_EXPERTS = 8
TOP_K = 2
PLE_DIM = 256
EPS = 1e-6
NEG_INF = -1e30
BIG = 1e9

NSA_Q_COLS = NSA_HEADS * HEAD_DIM
NSA_KV_COLS = NSA_KV_HEADS * HEAD_DIM
NSA_GATE_COLS = 3 * NSA_HEADS
RET_COLS = RET_HEADS * HEAD_DIM
MIX_WIDTH = NSA_Q_COLS + RET_COLS

V7X_LANES = 128
V7X_VMEM_BYTES = 64 * 1024 * 1024
VMEM_LIMIT = V7X_VMEM_BYTES * 3 // 4

ROW_TILE = 512
ATTN_Q_TILE = 256
SEL_KV_TILE = 256
RET_BATCH = 4
FFN_ROW_TILE = 512
FFN_COL_TILE = 512
MOE_COL_TILE = 896
MOE_ROW_TILE = 2048
MOE_GROUP_ROWS = 256
MOE_TAIL_ROWS = 64
MOE_RANK_CHUNK = 256
MOE_SCATTER_ROWS = 512
MOE_VMEM_LIMIT = V7X_VMEM_BYTES * 7 // 8

POS_SHIFT = 6
POS_SPLIT = 1 << POS_SHIFT
SEL_SHIFT = SEL_BLOCK.bit_length() - 1
MAX_SEL_BLOCKS = V7X_LANES


def _params(sem, vmem=VMEM_LIMIT):
    return pltpu.CompilerParams(dimension_semantics=sem, vmem_limit_bytes=vmem)


def _dot(a, b):
    return jnp.dot(a, b, preferred_element_type=F32)


def _dot_nt(a, b):
    return lax.dot_general(a, b, (((1,), (1,)), ((), ())), preferred_element_type=F32)


def _dot_tn(a, b):
    return lax.dot_general(a, b, (((0,), (0,)), ((), ())), preferred_element_type=F32)


def _rms(x, g):
    return x * lax.rsqrt(jnp.mean(x * x, axis=-1, keepdims=True) + EPS) * g


def _lane_features(shape, first, second):
    lane = lax.broadcasted_iota(jnp.int32, shape, 1)
    return jnp.where(lane == 0, first, jnp.where(lane == 1, second, 0))


def _key_features(pos, width):
    return _lane_features((pos.shape[0], width), pos >> POS_SHIFT, pos & (POS_SPLIT - 1))


def _ones_features(n, width):
    return _lane_features((n, width), 1, 0)


def _int_to_bf16(x):
    return x.astype(F32).astype(BF16)


def _in_proj_kernel(h_ref, g_ref, w_ref, qn_ref, kcv_ref, ks_ref, kv3_ref, gl_ref, ret_ref):
    tm = h_ref.shape[1]
    xn = _rms(h_ref[0], g_ref[...]).astype(BF16)
    pair = 2 * HEAD_DIM
    pos = pl.program_id(1) * tm + lax.broadcasted_iota(jnp.int32, (tm, 1), 0)
    kfeat = _int_to_bf16(_key_features(pos, HEAD_DIM))
    vfeat = _int_to_bf16(_ones_features(tm, HEAD_DIM))
    lane = lax.broadcasted_iota(jnp.int32, (tm, MAX_SEL_BLOCKS), 1)
    block_onehot = jnp.where(lane == (pos >> SEL_SHIFT), 1.0, 0.0).astype(BF16)

    for j in range(NSA_HEADS // 2):
        z = (_dot(xn, w_ref[:, j * pair:(j + 1) * pair]) * HEAD_DIM ** -0.5).astype(BF16)
        for k in range(2):
            slope = 2.0 ** -(2 * j + k + 1)
            qfeat = _lane_features((tm, HEAD_DIM), POS_SPLIT * slope, slope).astype(BF16)
            qn_ref[0, 2 * j + k] = jnp.concatenate([z[:, k * HEAD_DIM:(k + 1) * HEAD_DIM], qfeat], axis=1)
    base = NSA_Q_COLS
    for a in range(2):
        kcv_ref[a, 0] = _dot(xn, w_ref[:, base + a * pair:base + (a + 1) * pair]).astype(BF16)
    base += 2 * pair
    for a in range(4):
        z = _dot(xn, w_ref[:, base + a * pair:base + (a + 1) * pair]).astype(BF16)
        for g in range(NSA_KV_HEADS):
            zg = z[:, g * HEAD_DIM:(g + 1) * HEAD_DIM]
            if a == 0:
                ks_ref[0, g] = jnp.concatenate([zg, kfeat, block_onehot], axis=1)
            else:
                kv3_ref[a - 1, 0, g] = jnp.concatenate([zg, kfeat if a == 2 else vfeat], axis=1)
    base += 4 * pair
    for a in range(4):
        ret_ref[0, :, a * RET_COLS:(a + 1) * RET_COLS] = _dot(
            xn, w_ref[:, base + a * RET_COLS:base + (a + 1) * RET_COLS]).astype(BF16)
    base += 4 * RET_COLS
    gl_ref[0] = _dot(xn, w_ref[:, base:base + NSA_KV_HEADS * V7X_LANES])


def _arrange_w_in(w):
    q_end = NSA_Q_COLS
    kv_end = q_end + 6 * NSA_KV_COLS
    gl_end = kv_end + NSA_GATE_COLS
    gl = w[:, kv_end:gl_end].reshape(D_MODEL, 3, NSA_KV_HEADS, NSA_GROUP)
    gl = jnp.transpose(gl, (0, 2, 1, 3)).reshape(D_MODEL, NSA_KV_HEADS, 3 * NSA_GROUP)
    gl = jnp.pad(gl, ((0, 0), (0, 0), (0, V7X_LANES - 3 * NSA_GROUP)))
    gl = gl.reshape(D_MODEL, NSA_KV_HEADS * V7X_LANES)
    return jnp.concatenate([w[:, :kv_end], w[:, gl_end:], gl], axis=1).astype(BF16)


def _in_proj(h, g, w):
    b, s, d = h.shape
    assert s // SEL_BLOCK <= MAX_SEL_BLOCKS
    tm = min(ROW_TILE, s)
    ncols = w.shape[1]
    aug = 2 * HEAD_DIM
    return pl.pallas_call(
        _in_proj_kernel,
        grid=(b, s // tm),
        in_specs=[
            pl.BlockSpec((1, tm, d), lambda bi, si: (bi, si, 0)),
            pl.BlockSpec((1, d), lambda bi, si: (0, 0)),
            pl.BlockSpec((d, ncols), lambda bi, si: (0, 0)),
        ],
        out_specs=[
            pl.BlockSpec((1, NSA_HEADS, tm, aug), lambda bi, si: (bi, 0, si, 0)),
            pl.BlockSpec((2, 1, tm, NSA_KV_COLS), lambda bi, si: (0, bi, si, 0)),
            pl.BlockSpec((1, NSA_KV_HEADS, tm, aug + MAX_SEL_BLOCKS), lambda bi, si: (bi, 0, si, 0)),
            pl.BlockSpec((3, 1, NSA_KV_HEADS, tm, aug), lambda bi, si: (0, bi, 0, si, 0)),
            pl.BlockSpec((1, tm, NSA_KV_HEADS * V7X_LANES), lambda bi, si: (bi, si, 0)),
            pl.BlockSpec((1, tm, 4 * RET_COLS), lambda bi, si: (bi, si, 0)),
        ],
        out_shape=[
            jax.ShapeDtypeStruct((b, NSA_HEADS, s, aug), BF16),
            jax.ShapeDtypeStruct((2, b, s, NSA_KV_COLS), BF16),
            jax.ShapeDtypeStruct((b, NSA_KV_HEADS, s, aug + MAX_SEL_BLOCKS), BF16),
            jax.ShapeDtypeStruct((3, b, NSA_KV_HEADS, s, aug), BF16),
            jax.ShapeDtypeStruct((b, s, NSA_KV_HEADS * V7X_LANES), F32),
            jax.ShapeDtypeStruct((b, s, 4 * RET_COLS), BF16),
        ],
        compiler_params=_params(("parallel", "parallel")),
        name="in_proj",
    )(h, g.reshape(1, d), w)


def _compress_kernel(x_ref, wbig_ref, pos_ref, w1_ref, w2_ref, o_ref):
    nch = x_ref.shape[2]
    u = _dot(x_ref[0, 0], wbig_ref[0])
    hid0 = _dot(pos_ref[0], w1_ref[0])[0:1]
    w2 = w2_ref[0]
    cmp_end = lax.broadcasted_iota(jnp.int32, (nch, 1), 0) * CMP_STRIDE + (CMP_BLOCK - 1)
    feat = _int_to_bf16(jnp.where(pl.program_id(0) == 0, _key_features(cmp_end, HEAD_DIM),
                                  _ones_features(nch, HEAD_DIM)))
    for g in range(NSA_KV_HEADS):
        c0 = g * 2 * CMP_HIDDEN
        first = u[:, c0:c0 + CMP_HIDDEN]
        second = u[:, c0 + CMP_HIDDEN:c0 + 2 * CMP_HIDDEN]
        hid = first + pltpu.roll(second, nch - 1, 0) + hid0
        out = _dot(jax.nn.gelu(hid).astype(BF16), w2).astype(BF16)
        o_ref[0, 0, g] = jnp.concatenate([out, feat], axis=1)


def _arrange_cmp_w1(w1):
    r = CMP_BLOCK // CMP_STRIDE
    w1r = w1.reshape(2, r, CMP_STRIDE, HEAD_DIM, CMP_HIDDEN)
    eye = jnp.eye(NSA_KV_HEADS, dtype=w1.dtype)
    big = jnp.einsum('krcdh,gf->kcgdfrh', w1r, eye)
    return big.reshape(2, CMP_STRIDE * NSA_KV_COLS, NSA_KV_HEADS * r * CMP_HIDDEN).astype(BF16)


def _compress(kcv, cmp_pos, cmp_w1, cmp_w2):
    _, b, s, _ = kcv.shape
    nch = s // CMP_STRIDE
    x = kcv.reshape(2, b, nch, CMP_STRIDE * NSA_KV_COLS)
    wbig = _arrange_cmp_w1(cmp_w1)
    pos = jnp.broadcast_to(cmp_pos.reshape(2, 1, CMP_BLOCK * HEAD_DIM), (2, 8, CMP_BLOCK * HEAD_DIM)).astype(BF16)
    w1 = cmp_w1.reshape(2, CMP_BLOCK * HEAD_DIM, CMP_HIDDEN).astype(BF16)
    w2 = cmp_w2.astype(BF16)
    kdim = CMP_STRIDE * NSA_KV_COLS
    return pl.pallas_call(
        _compress_kernel,
        grid=(2, b),
        in_specs=[
            pl.BlockSpec((1, 1, nch, kdim), lambda a, bi: (a, bi, 0, 0)),
            pl.BlockSpec((1, kdim, wbig.shape[2]), lambda a, bi: (a, 0, 0)),
            pl.BlockSpec((1, 8, CMP_BLOCK * HEAD_DIM), lambda a, bi: (a, 0, 0)),
            pl.BlockSpec((1, CMP_BLOCK * HEAD_DIM, CMP_HIDDEN), lambda a, bi: (a, 0, 0)),
            pl.BlockSpec((1, CMP_HIDDEN, HEAD_DIM), lambda a, bi: (a, 0, 0)),
        ],
        out_specs=pl.BlockSpec((1, 1, NSA_KV_HEADS, nch, 2 * HEAD_DIM), lambda a, bi: (a, bi, 0, 0, 0)),
        out_shape=jax.ShapeDtypeStruct((2, b, NSA_KV_HEADS, nch, 2 * HEAD_DIM), BF16),
        compiler_params=_params(("parallel", "parallel")),
        name="compress",
    )(x, wbig, pos, w1, w2)


def _normalise(acc):
    return acc[:, :HEAD_DIM] * (1.0 / jnp.maximum(acc[:, HEAD_DIM:HEAD_DIM + 1], 1e-30))


def _attn_kernel(q_ref, kc_ref, vc_ref, ks_ref, vs_ref, kw_ref, vw_ref, gl_ref, ovt_ref, o_ref, tiles_ref,
                 *, n_pick, tk):
    qt = q_ref.shape[2]
    rows = NSA_GROUP * qt
    t0 = pl.program_id(2) * qt
    q = q_ref[0].reshape(rows, 2 * HEAD_DIM)
    tq = t0 + (lax.broadcasted_iota(jnp.int32, (rows, 1), 0) & (qt - 1))

    nc = kc_ref.shape[3]
    cmp_end = lax.broadcasted_iota(jnp.int32, (1, nc), 1) * CMP_STRIDE + (CMP_BLOCK - 1)
    valid_c = cmp_end <= tq
    logit_c = jnp.where(valid_c, _dot_nt(q, kc_ref[0, 0, 0]), NEG_INF)
    e_c = jnp.where(valid_c, jnp.exp(logit_c - jnp.max(logit_c, axis=-1, keepdims=True)), 0.0)
    p_c = e_c * (1.0 / jnp.maximum(jnp.sum(e_c, axis=-1, keepdims=True), 1e-30))
    o_c = _dot(p_c.astype(BF16), vc_ref[0, 0, 0])[:, :HEAD_DIM]

    p_sum = p_c[0:qt]
    for r in range(1, NSA_GROUP):
        p_sum = p_sum + p_c[r * qt:(r + 1) * qt]
    p_hi = p_sum.astype(BF16)
    p_lo = (p_sum - p_hi.astype(F32)).astype(BF16)
    imp = _dot_nt(ovt_ref[...], p_hi) + _dot_nt(ovt_ref[...], p_lo)
    shape = (MAX_SEL_BLOCKS, qt)
    blk = lax.broadcasted_iota(jnp.int32, shape, 0)
    back = ((t0 + lax.broadcasted_iota(jnp.int32, shape, 1)) >> SEL_SHIFT) - blk
    forced = (blk == 0) | ((back >= 0) & (back < N_LOCAL_SEL))
    score = jnp.where(forced, BIG, jnp.where(back >= 0, imp, -BIG))

    def pick(_, carry):
        work, sel = carry
        m = jnp.max(work, axis=0, keepdims=True)
        first = jnp.min(jnp.where(work == m, blk, MAX_SEL_BLOCKS), axis=0, keepdims=True)
        hit = blk == first
        return jnp.where(hit, -jnp.inf, work), jnp.where(hit, 1.0, sel)

    _, sel_t = lax.fori_loop(0, n_pick, pick, (score, jnp.zeros(shape, F32)))
    sel_bias = ((sel_t.T - 1.0) * -NEG_INF).astype(BF16)
    q_sel = jnp.concatenate([q, jnp.concatenate([sel_bias] * NSA_GROUP, axis=0)], axis=1)

    n_own = max(1, qt // tk)
    n_full = t0 // tk
    blocks_per_tile = tk // SEL_BLOCK
    block_used = jnp.max(sel_t, axis=1, keepdims=True)
    n_visit = jnp.int32(0)
    for j in range(tiles_ref.shape[0]):
        used = jnp.max(block_used[j * blocks_per_tile:(j + 1) * blocks_per_tile]) > 0.0
        tiles_ref[n_visit] = j
        n_visit = n_visit + (used & (j < n_full)).astype(jnp.int32)

    def sweep_tile(j, carry, causal):
        m_i, acc = carry
        k0 = pl.multiple_of(j * tk, tk)
        s = _dot_nt(q_sel, ks_ref[0, 0, pl.ds(k0, tk), :])
        if causal:
            s = jnp.where(k0 + lax.broadcasted_iota(jnp.int32, (1, tk), 1) <= tq, s, NEG_INF)
        m_new = jnp.maximum(m_i, jnp.max(s, axis=-1, keepdims=True))
        p = jnp.exp(s - m_new)
        acc = jnp.exp(m_i - m_new) * acc + _dot(p.astype(BF16), vs_ref[0, 0, 0, pl.ds(k0, tk), :])
        return m_new, acc

    init = (jnp.full((rows, 1), NEG_INF, F32), jnp.zeros((rows, 2 * HEAD_DIM), F32))
    carry = lax.fori_loop(0, n_visit, lambda i, c: sweep_tile(tiles_ref[i], c, causal=False), init)
    for j in range(n_own):
        carry = sweep_tile(n_full + j, carry, causal=True)
    o_s = _normalise(carry[1])

    band = WINDOW + qt
    w0 = pl.multiple_of(jnp.maximum(t0 - WINDOW, 0), math.gcd(qt, WINDOW))
    kpos = w0 + lax.broadcasted_iota(jnp.int32, (1, band), 1)
    valid_w = (kpos <= tq) & (kpos > tq - WINDOW)
    logit_w = jnp.where(valid_w, _dot_nt(q, kw_ref[0, 0, 0, pl.ds(w0, band), :]), NEG_INF)
    e_w = jnp.exp(logit_w - jnp.max(logit_w, axis=-1, keepdims=True))
    o_w = _normalise(_dot(e_w.astype(BF16), vw_ref[0, 0, 0, pl.ds(w0, band), :]))

    gate = jax.nn.sigmoid(gl_ref[0])
    outs = []
    for r in range(NSA_GROUP):
        sl = slice(r * qt, (r + 1) * qt)
        outs.append(gate[:, r:r + 1] * o_c[sl]
                    + gate[:, NSA_GROUP + r:NSA_GROUP + r + 1] * o_s[sl]
                    + gate[:, 2 * NSA_GROUP + r:2 * NSA_GROUP + r + 1] * o_w[sl])
    o_ref[0] = jnp.concatenate(outs, axis=1).astype(o_ref.dtype)


def _nsa_attention(qn, kv_cmp, ks, kv3, gl):
    b, _, s, _ = qn.shape
    nc = kv_cmp.shape[3]
    nsel = s // SEL_BLOCK
    n_pick = min(N_SELECT, nsel)
    tk = min(SEL_KV_TILE, s)
    aug = 2 * HEAD_DIM
    qt = min(ATTN_Q_TILE, s)
    assert qt % tk == 0 or tk % qt == 0
    cmp_start = jnp.arange(nc) * CMP_STRIDE
    sel_start = jnp.arange(nsel) * SEL_BLOCK
    overlap = (jnp.minimum(cmp_start[:, None] + CMP_BLOCK, sel_start[None, :] + SEL_BLOCK)
               > jnp.maximum(cmp_start[:, None], sel_start[None, :]))
    overlap_t = jnp.pad(overlap.T.astype(BF16), ((0, MAX_SEL_BLOCKS - nsel), (0, 0)))
    kv_spec = lambda a: pl.BlockSpec((1, 1, 1, s, aug), lambda bi, g, i, a=a: (a, bi, g, 0, 0))
    cmp_spec = lambda a: pl.BlockSpec((1, 1, 1, nc, aug), lambda bi, g, i, a=a: (a, bi, g, 0, 0))
    return pl.pallas_call(
        functools.partial(_attn_kernel, n_pick=n_pick, tk=tk),
        grid=(b, NSA_KV_HEADS, s // qt),
        in_specs=[
            pl.BlockSpec((1, NSA_GROUP, qt, aug), lambda bi, g, i: (bi, g, i, 0)),
            cmp_spec(0), cmp_spec(1),
            pl.BlockSpec((1, 1, s, aug + MAX_SEL_BLOCKS), lambda bi, g, i: (bi, g, 0, 0)),
            kv_spec(0), kv_spec(1), kv_spec(2),
            pl.BlockSpec((1, qt, V7X_LANES), lambda bi, g, i: (bi, i, g)),
            pl.BlockSpec((MAX_SEL_BLOCKS, nc), lambda bi, g, i: (0, 0)),
        ],
        out_specs=pl.BlockSpec((1, qt, NSA_GROUP * HEAD_DIM), lambda bi, g, i: (bi, i, g)),
        out_shape=jax.ShapeDtypeStruct((b, s, NSA_Q_COLS), BF16),
        scratch_shapes=[pltpu.SMEM((s // tk,), jnp.int32)],
        compiler_params=_params(("parallel", "parallel", "arbitrary")),
        name="nsa_attention",
    )(qn, kv_cmp, kv_cmp, ks, kv3, kv3, kv3, gl, overlap_t)


def _retention_kernel(q_ref, k_ref, v_ref, g_ref, decay_ref, xi_ref, zeta_ref, gch_ref, gn_ref, o_ref, state_ref):
    @pl.when(pl.program_id(1) == 0)
    def _():
        state_ref[...] = jnp.zeros_like(state_ref)

    scale = jnp.asarray(HEAD_DIM ** -0.5, BF16)
    for i in range(q_ref.shape[0]):
        outs = []
        for h in range(RET_HEADS):
            cols = slice(h * HEAD_DIM, (h + 1) * HEAD_DIM)
            q = q_ref[i, :, cols]
            k = k_ref[i, :, cols] * scale
            v = v_ref[i, :, cols]
            state = state_ref[i, h]
            inner = _dot_nt(q, k) * decay_ref[h]
            o = _dot(inner.astype(BF16), v) + _dot(q, state.astype(BF16)) * xi_ref[h]
            kz = (k.astype(F32) * zeta_ref[h]).astype(BF16)
            state_ref[i, h] = gch_ref[h] * state + _dot_tn(kz, v)
            mu = jnp.mean(o, axis=-1, keepdims=True)
            var = jnp.mean(jnp.square(o - mu), axis=-1, keepdims=True)
            outs.append((o - mu) * lax.rsqrt(var + EPS))
        o = jnp.concatenate(outs, axis=1) * gn_ref[...]
        o_ref[i] = (jax.nn.silu(g_ref[i].astype(F32)) * o).astype(o_ref.dtype)


def _retention(ret, gn_gain):
    b, s, _ = ret.shape
    c = RET_CHUNK
    hh = RET_HEADS
    log_gamma = jnp.log1p(-jnp.exp2(-5.0 - jnp.arange(hh, dtype=F32)))
    pos = jnp.arange(c, dtype=F32)
    diff = pos[:, None] - pos[None, :]
    decay = jnp.where(diff >= 0, jnp.exp(jnp.maximum(diff, 0.0)[None] * log_gamma[:, None, None]), 0.0)
    xi = jnp.exp((pos + 1.0)[None] * log_gamma[:, None])[..., None]
    zeta = jnp.exp((c - 1.0 - pos)[None] * log_gamma[:, None])[..., None]
    g_chunk = jnp.exp(c * log_gamma)[:, None, None]
    xi = jnp.broadcast_to(xi, (hh, c, HEAD_DIM))
    zeta = jnp.broadcast_to(zeta, (hh, c, HEAD_DIM))
    g_chunk = jnp.broadcast_to(g_chunk, (hh, HEAD_DIM, HEAD_DIM))
    nb = math.gcd(RET_BATCH, b)
    part = lambda a: pl.BlockSpec((nb, c, RET_COLS), lambda bi, n, a=a: (bi, n, a))
    full = lambda shape: pl.BlockSpec(shape, lambda bi, n: (0,) * len(shape))
    return pl.pallas_call(
        _retention_kernel,
        grid=(b // nb, s // c),
        in_specs=[part(0), part(1), part(2), part(3),
                  full((hh, c, c)), full((hh, c, HEAD_DIM)), full((hh, c, HEAD_DIM)),
                  full((hh, HEAD_DIM, HEAD_DIM)), full((1, RET_COLS))],
        out_specs=pl.BlockSpec((nb, c, RET_COLS), lambda bi, n: (bi, n, 0)),
        out_shape=jax.ShapeDtypeStruct((b, s, RET_COLS), BF16),
        scratch_shapes=[pltpu.VMEM((nb, hh, HEAD_DIM, HEAD_DIM), F32)],
        compiler_params=_params(("parallel", "arbitrary")),
        name="retention",
    )(ret, ret, ret, ret, decay, xi, zeta, g_chunk, gn_gain.reshape(1, RET_COLS).astype(F32))


def _out_proj_kernel(h_ref, a_ref, r_ref, w_ref, g_ref, h_out_ref, hn_ref):
    h = h_ref[...] + _dot(a_ref[...], w_ref[:NSA_Q_COLS]) + _dot(r_ref[...], w_ref[NSA_Q_COLS:])
    h_out_ref[...] = h
    hn_ref[...] = _rms(h, g_ref[...]).astype(BF16)


def _out_proj(h, a, r, w, g):
    t, d = h.shape
    tm = min(ROW_TILE, t)
    row = lambda n: pl.BlockSpec((tm, n), lambda i: (i, 0))
    return pl.pallas_call(
        _out_proj_kernel,
        grid=(t // tm,),
        in_specs=[row(d), row(NSA_Q_COLS), row(RET_COLS),
                  pl.BlockSpec((MIX_WIDTH, d), lambda i: (0, 0)),
                  pl.BlockSpec((1, d), lambda i: (0, 0))],
        out_specs=[row(d), row(d)],
        out_shape=[jax.ShapeDtypeStruct((t, d), F32), jax.ShapeDtypeStruct((t, d), BF16)],
        compiler_params=_params(("parallel",)),
        name="out_proj",
    )(h, a, r, w.astype(BF16), g.reshape(1, d))


def _swiglu_chunk(x, wg, wu, wd):
    hid = jax.nn.silu(_dot(x, wg)) * _dot(x, wu)
    return _dot(hid.astype(BF16), wd)


def _ffn_kernel(x_ref, wg_ref, wu_ref, wd_ref, o_ref):
    x = x_ref[...]
    acc = None
    for f in range(D_FF // FFN_COL_TILE):
        cols = slice(f * FFN_COL_TILE, (f + 1) * FFN_COL_TILE)
        part = _swiglu_chunk(x, wg_ref[:, cols], wu_ref[:, cols], wd_ref[cols, :])
        acc = part if acc is None else acc + part
    o_ref[...] = acc


def _dense_ffn(hn, wg, wu, wd):
    t, d = hn.shape
    tm = min(FFN_ROW_TILE, t)
    once = pl.Buffered(1)
    return pl.pallas_call(
        _ffn_kernel,
        grid=(t // tm,),
        in_specs=[pl.BlockSpec((tm, d), lambda i: (i, 0)),
                  pl.BlockSpec((d, D_FF), lambda i: (0, 0), pipeline_mode=once),
                  pl.BlockSpec((d, D_FF), lambda i: (0, 0), pipeline_mode=once),
                  pl.BlockSpec((D_FF, d), lambda i: (0, 0), pipeline_mode=once)],
        out_specs=pl.BlockSpec((tm, d), lambda i: (i, 0)),
        out_shape=jax.ShapeDtypeStruct((t, d), F32),
        compiler_params=_params(("parallel",)),
        name="dense_ffn",
    )(hn, wg.astype(BF16), wu.astype(BF16), wd.astype(BF16))


def _lane_column(table, lane_index):
    lane = lax.broadcasted_iota(jnp.int32, table.shape, 1)
    col = jnp.sum(jnp.where(lane == lane_index, table, 0.0), axis=-1, keepdims=True)
    return jnp.broadcast_to(col, table.shape)


def _moe_kernel(x_ref, router_ref, wg_ref, wu_ref, wd_ref, o_ref,
                xs_ref, y_ref, slot_ref, gate_ref, slot_row_ref, slot_e_ref, gate_e_ref, count_ref):
    e = pl.program_id(1)
    f = pl.program_id(2)
    tm, d = x_ref.shape
    sub = MOE_GROUP_ROWS

    @pl.when((e == 0) & (f == 0))
    def _route():
        o_ref[...] = jnp.zeros_like(o_ref)
        logits = _dot(x_ref[...], router_ref[...])
        lane = lax.broadcasted_iota(jnp.int32, logits.shape, 1)
        logits = jnp.where(lane < N_EXPERTS, logits, -jnp.inf)
        v1 = jnp.max(logits, axis=-1, keepdims=True)
        i1 = jnp.min(jnp.where(logits == v1, lane, V7X_LANES), axis=-1, keepdims=True)
        rest = jnp.where(lane == i1, -jnp.inf, logits)
        v2 = jnp.max(rest, axis=-1, keepdims=True)
        i2 = jnp.min(jnp.where(rest == v2, lane, V7X_LANES), axis=-1, keepdims=True)
        e2 = jnp.exp(v2 - v1)
        inv = 1.0 / (1.0 + e2)
        gate_ref[...] = jnp.where(lane == i1, inv, 0.0) + jnp.where(lane == i2, e2 * inv, 0.0)
        routed = jnp.where((lane == i1) | (lane == i2), 1.0, 0.0)
        c = MOE_RANK_CHUNK
        before = (lax.broadcasted_iota(jnp.int32, (c, c), 1) < lax.broadcasted_iota(jnp.int32, (c, c), 0))
        before = jnp.where(before, 1.0, 0.0).astype(BF16)
        offset = jnp.zeros((1, V7X_LANES), F32)
        for j in range(tm // c):
            part = routed[j * c:(j + 1) * c]
            rank = _dot(before, part.astype(BF16)) + offset
            slot_ref[j * c:(j + 1) * c, :] = jnp.where(part > 0.0, rank, -1.0)
            offset = offset + jnp.sum(part, axis=0, keepdims=True)
        for ee in range(N_EXPERTS):
            count_ref[ee] = offset[0, ee].astype(jnp.int32)
        slot_row_ref[...] = slot_ref[...].T[:N_EXPERTS]

    n_groups = (count_ref[e] + sub - 1) // sub

    @pl.when(f == 0)
    def _gather():
        slot_e_ref[...] = _lane_column(slot_ref[...], e)
        gate_e_ref[...] = _lane_column(gate_ref[...], e)
        slot_row = slot_row_ref[pl.ds(e, 1), :]

        def body(s, _):
            r0 = pl.multiple_of(s * sub, sub)
            want = (r0 + lax.broadcasted_iota(jnp.int32, (sub, 1), 0)).astype(F32)
            onehot = jnp.where(slot_row == want, 1.0, 0.0).astype(BF16)
            xs_ref[pl.ds(r0, sub), :] = _dot(onehot, x_ref[...]).astype(BF16)
            y_ref[pl.ds(r0, sub), :] = jnp.zeros((sub, d), F32)
            return 0

        lax.fori_loop(0, n_groups, body, 0)

    def expert(i, _, base, n):
        r0 = pl.multiple_of(base + i * n, n)
        y_ref[pl.ds(r0, n), :] += _swiglu_chunk(xs_ref[pl.ds(r0, n), :], wg_ref[0, 0], wu_ref[0, 0], wd_ref[0])
        return 0

    tail = MOE_TAIL_ROWS
    n_whole = count_ref[e] // sub
    n_tail = (count_ref[e] - n_whole * sub + tail - 1) // tail
    lax.fori_loop(0, n_whole, functools.partial(expert, base=0, n=sub), 0)
    lax.fori_loop(0, n_tail, functools.partial(expert, base=n_whole * sub, n=tail), 0)

    @pl.when(f == pl.num_programs(2) - 1)
    def _scatter():
        tc = MOE_SCATTER_ROWS

        def body(s, _):
            r0 = pl.multiple_of(s * sub, sub)
            y = y_ref[pl.ds(r0, sub), :].astype(BF16)
            want = (r0 + lax.broadcasted_iota(jnp.int32, (1, sub), 1)).astype(F32)
            for j in range(tm // tc):
                rows = slice(j * tc, (j + 1) * tc)
                slot = jnp.concatenate([slot_e_ref[rows, :]] * (sub // V7X_LANES), axis=1)
                onehot = jnp.where(slot == want, 1.0, 0.0).astype(BF16)
                weight = jnp.concatenate([gate_e_ref[rows, :]] * (d // V7X_LANES), axis=1)
                o_ref[rows, :] += weight * _dot(onehot, y)
            return 0

        lax.fori_loop(0, n_groups, body, 0)


def _moe_ffn(hn, router, wg, wu, wd):
    t, d = hn.shape
    tm = min(MOE_ROW_TILE, t)
    tf = MOE_COL_TILE
    nf = D_FF // tf
    router = jnp.pad(router, ((0, 0), (0, V7X_LANES - N_EXPERTS))).astype(BF16)
    chunked = lambda w: jnp.transpose(w.astype(BF16).reshape(N_EXPERTS, d, nf, tf), (0, 2, 1, 3))
    once = pl.Buffered(1)
    return pl.pallas_call(
        _moe_kernel,
        grid=(t // tm, N_EXPERTS, nf),
        in_specs=[pl.BlockSpec((tm, d), lambda i, e, f: (i, 0), pipeline_mode=once),
                  pl.BlockSpec((d, V7X_LANES), lambda i, e, f: (0, 0), pipeline_mode=once),
                  pl.BlockSpec((1, 1, d, tf), lambda i, e, f: (e, f, 0, 0)),
                  pl.BlockSpec((1, 1, d, tf), lambda i, e, f: (e, f, 0, 0)),
                  pl.BlockSpec((1, tf, d), lambda i, e, f: (e, f, 0))],
        out_specs=pl.BlockSpec((tm, d), lambda i, e, f: (i, 0)),
        out_shape=jax.ShapeDtypeStruct((t, d), F32),
        scratch_shapes=[pltpu.VMEM((tm, d), BF16), pltpu.VMEM((tm, d), F32),
                        pltpu.VMEM((tm, V7X_LANES), F32), pltpu.VMEM((tm, V7X_LANES), F32),
                        pltpu.VMEM((N_EXPERTS, tm), F32),
                        pltpu.VMEM((tm, V7X_LANES), F32), pltpu.VMEM((tm, V7X_LANES), F32),
                        pltpu.SMEM((N_EXPERTS,), jnp.int32)],
        compiler_params=_params(("parallel", "arbitrary", "arbitrary"), vmem=MOE_VMEM_LIMIT),
        name="moe_ffn",
    )(hn, router, chunked(wg), chunked(wu), wd.astype(BF16))


def _ple_kernel(h_ref, f_ref, p_ref, g_ref, proj_ref, gate_ref, gf_ref, o_ref, *, final_norm):
    h = h_ref[...] + f_ref[...]
    emb = _dot(p_ref[...].astype(BF16), proj_ref[...])
    sig = jax.nn.sigmoid(_dot(_rms(h, g_ref[...]).astype(BF16), gate_ref[...]))
    h = h + emb * sig
    if final_norm:
        h = _rms(h, gf_ref[...])
    o_ref[...] = h


def _ple(h, f, p, g, proj, gate, g_final, final_norm):
    t, d = h.shape
    tm = min(ROW_TILE, t)
    vec = pl.BlockSpec((1, d), lambda i: (0, 0))
    return pl.pallas_call(
        functools.partial(_ple_kernel, final_norm=final_norm),
        grid=(t // tm,),
        in_specs=[pl.BlockSpec((tm, d), lambda i: (i, 0)),
                  pl.BlockSpec((tm, d), lambda i: (i, 0)),
                  pl.BlockSpec((tm, PLE_DIM), lambda i: (i, 0)),
                  vec,
                  pl.BlockSpec((PLE_DIM, d), lambda i: (0, 0)),
                  pl.BlockSpec((d, d), lambda i: (0, 0)),
                  vec],
        out_specs=pl.BlockSpec((tm, d), lambda i: (i, 0)),
        out_shape=jax.ShapeDtypeStruct((t, d), F32),
        compiler_params=_params(("parallel",)),
        name="ple",
    )(h, f, p, g.reshape(1, d), proj.astype(BF16), gate.astype(BF16), g_final.reshape(1, d))


def kernel(x, p, w_in, w_out, g_mix, g_ffn, g_ple, g_final, cmp_pos, cmp_w1, cmp_w2, ret_gn,
           ffn_gate, ffn_up, ffn_down, moe_router, moe_gate, moe_up, moe_down, ple_proj, ple_gate):
    b, s, d = x.shape
    depth = w_in.shape[0]
    t = b * s
    h = x
    for i in range(depth):
        qn, kcv, ks, kv3, gl, ret = _in_proj(h.reshape(b, s, d), g_mix[i], _arrange_w_in(w_in[i]))
        kv_cmp = _compress(kcv, cmp_pos[i], cmp_w1[i], cmp_w2[i])
        a = _nsa_attention(qn, kv_cmp, ks, kv3, gl)
        r = _retention(ret, ret_gn[i])
        h, hn = _out_proj(h.reshape(t, d), a.reshape(t, NSA_Q_COLS), r.reshape(t, RET_COLS), w_out[i], g_ffn[i])
        if i % 2 == 0:
            f = _dense_ffn(hn, ffn_gate[i // 2], ffn_up[i // 2], ffn_down[i // 2])
        else:
            f = _moe_ffn(hn, moe_router[i // 2], moe_gate[i // 2], moe_up[i // 2], moe_down[i // 2])
        h = _ple(h, f, p[i].reshape(t, PLE_DIM), g_ple[i], ple_proj[i], ple_gate[i], g_final, i == depth - 1)
    return h.reshape(b, s, d)
```

```python
import functools
import math

import jax
import jax.numpy as jnp
from jax import lax
from jax.experimental import pallas as pl
from jax.experimental.pallas import tpu as pltpu

F32 = jnp.float32
BF16 = jnp.bfloat16

D_MODEL = 1024
HEAD_DIM = 64
NSA_HEADS = 8
NSA_KV_HEADS = 2
NSA_GROUP = NSA_HEADS // NSA_KV_HEADS
RET_HEADS = 8
CMP_BLOCK = 32
CMP_STRIDE = 16
CMP_HIDDEN = 256
SEL_BLOCK = 64
N_SELECT = 16
N_LOCAL_SEL = 2
WINDOW = 512
Q_BLOCK = 128
RET_CHUNK = 128
D_FF = 3584
N_EXPERTS = 8
TOP_K = 2
PLE_DIM = 256
EPS = 1e-6
NEG_INF = -1e30
BIG = 1e9

NSA_Q_COLS = NSA_HEADS * HEAD_DIM
NSA_KV_COLS = NSA_KV_HEADS * HEAD_DIM
NSA_GATE_COLS = 3 * NSA_HEADS
RET_COLS = RET_HEADS * HEAD_DIM
MIX_WIDTH = NSA_Q_COLS + RET_COLS

V7X_LANES = 128
V7X_VMEM_BYTES = 64 * 1024 * 1024
VMEM_LIMIT = V7X_VMEM_BYTES * 3 // 4

ROW_TILE = 512
ATTN_Q_TILE = 256
SEL_KV_TILE = 256
RET_BATCH = 4
FFN_ROW_TILE = 512
FFN_COL_TILE = 512
MOE_COL_TILE = 896
MOE_ROW_TILE = 2048
MOE_GROUP_ROWS = 256
MOE_TAIL_ROWS = 64
MOE_RANK_CHUNK = 256
MOE_SCATTER_ROWS = 512
MOE_VMEM_LIMIT = V7X_VMEM_BYTES * 7 // 8

POS_SHIFT = 6
POS_SPLIT = 1 << POS_SHIFT
SEL_SHIFT = SEL_BLOCK.bit_length() - 1
MAX_SEL_BLOCKS = V7X_LANES


def _params(sem, vmem=VMEM_LIMIT):
    return pltpu.CompilerParams(dimension_semantics=sem, vmem_limit_bytes=vmem)


def _dot(a, b):
    return jnp.dot(a, b, preferred_element_type=F32)


def _dot_nt(a, b):
    return lax.dot_general(a, b, (((1,), (1,)), ((), ())), preferred_element_type=F32)


def _dot_tn(a, b):
    return lax.dot_general(a, b, (((0,), (0,)), ((), ())), preferred_element_type=F32)


def _rms(x, g):
    return x * lax.rsqrt(jnp.mean(x * x, axis=-1, keepdims=True) + EPS) * g


def _lane_features(shape, first, second):
    lane = lax.broadcasted_iota(jnp.int32, shape, 1)
    return jnp.where(lane == 0, first, jnp.where(lane == 1, second, 0))


def _key_features(pos, width):
    return _lane_features((pos.shape[0], width), pos >> POS_SHIFT, pos & (POS_SPLIT - 1))


def _ones_features(n, width):
    return _lane_features((n, width), 1, 0)


def _int_to_bf16(x):
    return x.astype(F32).astype(BF16)


def _in_proj_kernel(h_ref, g_ref, w_ref, qn_ref, kcv_ref, ks_ref, kv3_ref, gl_ref, ret_ref):
    tm = h_ref.shape[1]
    xn = _rms(h_ref[0], g_ref[...]).astype(BF16)
    pair = 2 * HEAD_DIM
    pos = pl.program_id(1) * tm + lax.broadcasted_iota(jnp.int32, (tm, 1), 0)
    kfeat = _int_to_bf16(_key_features(pos, HEAD_DIM))
    vfeat = _int_to_bf16(_ones_features(tm, HEAD_DIM))
    lane = lax.broadcasted_iota(jnp.int32, (tm, MAX_SEL_BLOCKS), 1)
    block_onehot = jnp.where(lane == (pos >> SEL_SHIFT), 1.0, 0.0).astype(BF16)

    for j in range(NSA_HEADS // 2):
        z = (_dot(xn, w_ref[:, j * pair:(j + 1) * pair]) * HEAD_DIM ** -0.5).astype(BF16)
        for k in range(2):
            slope = 2.0 ** -(2 * j + k + 1)
            qfeat = _lane_features((tm, HEAD_DIM), POS_SPLIT * slope, slope).astype(BF16)
            qn_ref[0, 2 * j + k] = jnp.concatenate([z[:, k * HEAD_DIM:(k + 1) * HEAD_DIM], qfeat], axis=1)
    base = NSA_Q_COLS
    for a in range(2):
        kcv_ref[a, 0] = _dot(xn, w_ref[:, base + a * pair:base + (a + 1) * pair]).astype(BF16)
    base += 2 * pair
    for a in range(4):
        z = _dot(xn, w_ref[:, base + a * pair:base + (a + 1) * pair]).astype(BF16)
        for g in range(NSA_KV_HEADS):
            zg = z[:, g * HEAD_DIM:(g + 1) * HEAD_DIM]
            if a == 0:
                ks_ref[0, g] = jnp.concatenate([zg, kfeat, block_onehot], axis=1)
            else:
                kv3_ref[a - 1, 0, g] = jnp.concatenate([zg, kfeat if a == 2 else vfeat], axis=1)
    base += 4 * pair
    for a in range(4):
        ret_ref[0, :, a * RET_COLS:(a + 1) * RET_COLS] = _dot(
            xn, w_ref[:, base + a * RET_COLS:base + (a + 1) * RET_COLS]).astype(BF16)
    base += 4 * RET_COLS
    gl_ref[0] = _dot(xn, w_ref[:, base:base + NSA_KV_HEADS * V7X_LANES])


def _arrange_w_in(w):
    q_end = NSA_Q_COLS
    kv_end = q_end + 6 * NSA_KV_COLS
    gl_end = kv_end + NSA_GATE_COLS
    gl = w[:, kv_end:gl_end].reshape(D_MODEL, 3, NSA_KV_HEADS, NSA_GROUP)
    gl = jnp.transpose(gl, (0, 2, 1, 3)).reshape(D_MODEL, NSA_KV_HEADS, 3 * NSA_GROUP)
    gl = jnp.pad(gl, ((0, 0), (0, 0), (0, V7X_LANES - 3 * NSA_GROUP)))
    gl = gl.reshape(D_MODEL, NSA_KV_HEADS * V7X_LANES)
    return jnp.concatenate([w[:, :kv_end], w[:, gl_end:], gl], axis=1).astype(BF16)


def _in_proj(h, g, w):
    b, s, d = h.shape
    assert s // SEL_BLOCK <= MAX_SEL_BLOCKS
    tm = min(ROW_TILE, s)
    ncols = w.shape[1]
    aug = 2 * HEAD_DIM
    return pl.pallas_call(
        _in_proj_kernel,
        grid=(b, s // tm),
        in_specs=[
            pl.BlockSpec((1, tm, d), lambda bi, si: (bi, si, 0)),
            pl.BlockSpec((1, d), lambda bi, si: (0, 0)),
            pl.BlockSpec((d, ncols), lambda bi, si: (0, 0)),
        ],
        out_specs=[
            pl.BlockSpec((1, NSA_HEADS, tm, aug), lambda bi, si: (bi, 0, si, 0)),
            pl.BlockSpec((2, 1, tm, NSA_KV_COLS), lambda bi, si: (0, bi, si, 0)),
            pl.BlockSpec((1, NSA_KV_HEADS, tm, aug + MAX_SEL_BLOCKS), lambda bi, si: (bi, 0, si, 0)),
            pl.BlockSpec((3, 1, NSA_KV_HEADS, tm, aug), lambda bi, si: (0, bi, 0, si, 0)),
            pl.BlockSpec((1, tm, NSA_KV_HEADS * V7X_LANES), lambda bi, si: (bi, si, 0)),
            pl.BlockSpec((1, tm, 4 * RET_COLS), lambda bi, si: (bi, si, 0)),
        ],
        out_shape=[
            jax.ShapeDtypeStruct((b, NSA_HEADS, s, aug), BF16),
            jax.ShapeDtypeStruct((2, b, s, NSA_KV_COLS), BF16),
            jax.ShapeDtypeStruct((b, NSA_KV_HEADS, s, aug + MAX_SEL_BLOCKS), BF16),
            jax.ShapeDtypeStruct((3, b, NSA_KV_HEADS, s, aug), BF16),
            jax.ShapeDtypeStruct((b, s, NSA_KV_HEADS * V7X_LANES), F32),
            jax.ShapeDtypeStruct((b, s, 4 * RET_COLS), BF16),
        ],
        compiler_params=_params(("parallel", "parallel")),
        name="in_proj",
    )(h, g.reshape(1, d), w)


def _compress_kernel(x_ref, wbig_ref, pos_ref, w1_ref, w2_ref, o_ref):
    nch = x_ref.shape[2]
    u = _dot(x_ref[0, 0], wbig_ref[0])
    hid0 = _dot(pos_ref[0], w1_ref[0])[0:1]
    w2 = w2_ref[0]
    cmp_end = lax.broadcasted_iota(jnp.int32, (nch, 1), 0) * CMP_STRIDE + (CMP_BLOCK - 1)
    feat = _int_to_bf16(jnp.where(pl.program_id(0) == 0, _key_features(cmp_end, HEAD_DIM),
                                  _ones_features(nch, HEAD_DIM)))
    for g in range(NSA_KV_HEADS):
        c0 = g * 2 * CMP_HIDDEN
        first = u[:, c0:c0 + CMP_HIDDEN]
        second = u[:, c0 + CMP_HIDDEN:c0 + 2 * CMP_HIDDEN]
        hid = first + pltpu.roll(second, nch - 1, 0) + hid0
        out = _dot(jax.nn.gelu(hid).astype(BF16), w2).astype(BF16)
        o_ref[0, 0, g] = jnp.concatenate([out, feat], axis=1)


def _arrange_cmp_w1(w1):
    r = CMP_BLOCK // CMP_STRIDE
    w1r = w1.reshape(2, r, CMP_STRIDE, HEAD_DIM, CMP_HIDDEN)
    eye = jnp.eye(NSA_KV_HEADS, dtype=w1.dtype)
    big = jnp.einsum('krcdh,gf->kcgdfrh', w1r, eye)
    return big.reshape(2, CMP_STRIDE * NSA_KV_COLS, NSA_KV_HEADS * r * CMP_HIDDEN).astype(BF16)


def _compress(kcv, cmp_pos, cmp_w1, cmp_w2):
    _, b, s, _ = kcv.shape
    nch = s // CMP_STRIDE
    x = kcv.reshape(2, b, nch, CMP_STRIDE * NSA_KV_COLS)
    wbig = _arrange_cmp_w1(cmp_w1)
    pos = jnp.broadcast_to(cmp_pos.reshape(2, 1, CMP_BLOCK * HEAD_DIM), (2, 8, CMP_BLOCK * HEAD_DIM)).astype(BF16)
    w1 = cmp_w1.reshape(2, CMP_BLOCK * HEAD_DIM, CMP_HIDDEN).astype(BF16)
    w2 = cmp_w2.astype(BF16)
    kdim = CMP_STRIDE * NSA_KV_COLS
    return pl.pallas_call(
        _compress_kernel,
        grid=(2, b),
        in_specs=[
            pl.BlockSpec((1, 1, nch, kdim), lambda a, bi: (a, bi, 0, 0)),
            pl.BlockSpec((1, kdim, wbig.shape[2]), lambda a, bi: (a, 0, 0)),
            pl.BlockSpec((1, 8, CMP_BLOCK * HEAD_DIM), lambda a, bi: (a, 0, 0)),
            pl.BlockSpec((1, CMP_BLOCK * HEAD_DIM, CMP_HIDDEN), lambda a, bi: (a, 0, 0)),
            pl.BlockSpec((1, CMP_HIDDEN, HEAD_DIM), lambda a, bi: (a, 0, 0)),
        ],
        out_specs=pl.BlockSpec((1, 1, NSA_KV_HEADS, nch, 2 * HEAD_DIM), lambda a, bi: (a, bi, 0, 0, 0)),
        out_shape=jax.ShapeDtypeStruct((2, b, NSA_KV_HEADS, nch, 2 * HEAD_DIM), BF16),
        compiler_params=_params(("parallel", "parallel")),
        name="compress",
    )(x, wbig, pos, w1, w2)


def _normalise_t(acc):
    return acc[:HEAD_DIM] * (1.0 / jnp.maximum(acc[HEAD_DIM:HEAD_DIM + 1], 1e-30))


def _attn_kernel(q_ref, kc_ref, vc_ref, ks_ref, vs_ref, kw_ref, vw_ref, gl_ref, ovt_ref, o_ref, tiles_ref,
                 *, n_pick, tk):
    qt = q_ref.shape[2]
    cols = NSA_GROUP * qt
    t0 = pl.program_id(2) * qt
    q_t = q_ref[0].reshape(cols, 2 * HEAD_DIM).T
    tq = t0 + (lax.broadcasted_iota(jnp.int32, (1, cols), 1) & (qt - 1))

    nc = kc_ref.shape[3]
    cmp_end = lax.broadcasted_iota(jnp.int32, (nc, 1), 0) * CMP_STRIDE + (CMP_BLOCK - 1)
    valid_c = cmp_end <= tq
    logit_c = jnp.where(valid_c, _dot(kc_ref[0, 0, 0], q_t), NEG_INF)
    e_c = jnp.where(valid_c, jnp.exp(logit_c - jnp.max(logit_c, axis=0, keepdims=True)), 0.0)
    p_c = e_c * (1.0 / jnp.maximum(jnp.sum(e_c, axis=0, keepdims=True), 1e-30))
    o_c = _dot(vc_ref[0, 0, 0].T, p_c.astype(BF16))[:HEAD_DIM]

    p_sum = p_c[:, 0:qt]
    for r in range(1, NSA_GROUP):
        p_sum = p_sum + p_c[:, r * qt:(r + 1) * qt]
    p_hi = p_sum.astype(BF16)
    p_lo = (p_sum - p_hi.astype(F32)).astype(BF16)
    imp = _dot(ovt_ref[...], p_hi) + _dot(ovt_ref[...], p_lo)
    shape = (MAX_SEL_BLOCKS, qt)
    blk = lax.broadcasted_iota(jnp.int32, shape, 0)
    back = ((t0 + lax.broadcasted_iota(jnp.int32, shape, 1)) >> SEL_SHIFT) - blk
    forced = (blk == 0) | ((back >= 0) & (back < N_LOCAL_SEL))
    score = jnp.where(forced, BIG, jnp.where(back >= 0, imp, -BIG))

    def pick(_, work):
        m = jnp.max(work, axis=0, keepdims=True)
        first = jnp.min(jnp.where(work == m, blk, MAX_SEL_BLOCKS), axis=0, keepdims=True)
        return jnp.where(blk == first, -jnp.inf, work)

    sel = jnp.where(lax.fori_loop(0, n_pick, pick, score) == -jnp.inf, 1.0, 0.0)
    sel_bias = ((sel - 1.0) * -NEG_INF).astype(BF16)
    q_sel = jnp.concatenate([q_t, jnp.concatenate([sel_bias] * NSA_GROUP, axis=1)], axis=0)

    n_own = max(1, qt // tk)
    n_full = t0 // tk
    blocks_per_tile = tk // SEL_BLOCK
    block_used = jnp.max(sel, axis=1, keepdims=True)
    n_visit = jnp.int32(0)
    for j in range(tiles_ref.shape[0]):
        used = jnp.max(block_used[j * blocks_per_tile:(j + 1) * blocks_per_tile]) > 0.0
        tiles_ref[n_visit] = j
        n_visit = n_visit + (used & (j < n_full)).astype(jnp.int32)

    def sweep_tile(j, carry, causal):
        m_i, acc = carry
        k0 = pl.multiple_of(j * tk, tk)
        s = _dot(ks_ref[0, 0, pl.ds(k0, tk), :], q_sel)
        if causal:
            s = jnp.where(k0 + lax.broadcasted_iota(jnp.int32, (tk, 1), 0) <= tq, s, NEG_INF)
        m_new = jnp.maximum(m_i, jnp.max(s, axis=0, keepdims=True))
        p = jnp.exp(s - m_new)
        acc = jnp.exp(m_i - m_new) * acc + _dot(vs_ref[0, 0, 0, pl.ds(k0, tk), :].T, p.astype(BF16))
        return m_new, acc

    init = (jnp.full((1, cols), NEG_INF, F32), jnp.zeros((2 * HEAD_DIM, cols), F32))
    carry = lax.fori_loop(0, n_visit, lambda i, c: sweep_tile(tiles_ref[i], c, causal=False), init)
    for j in range(n_own):
        carry = sweep_tile(n_full + j, carry, causal=True)
    o_s = _normalise_t(carry[1])

    band = WINDOW + qt
    w0 = pl.multiple_of(jnp.maximum(t0 - WINDOW, 0), math.gcd(qt, WINDOW))
    kpos = w0 + lax.broadcasted_iota(jnp.int32, (band, 1), 0)
    valid_w = (kpos <= tq) & (kpos > tq - WINDOW)
    logit_w = jnp.where(valid_w, _dot(kw_ref[0, 0, 0, pl.ds(w0, band), :], q_t), NEG_INF)
    e_w = jnp.exp(logit_w - jnp.max(logit_w, axis=0, keepdims=True))
    o_w = _normalise_t(_dot(vw_ref[0, 0, 0, pl.ds(w0, band), :].T, e_w.astype(BF16)))

    gate_t = jax.nn.sigmoid(gl_ref[0]).T

    def branch_gate(branch):
        first = branch * NSA_GROUP
        return jnp.concatenate([gate_t[first + r:first + r + 1] for r in range(NSA_GROUP)], axis=1)

    o_t = branch_gate(0) * o_c + branch_gate(1) * o_s + branch_gate(2) * o_w
    o = o_t.T
    o_ref[0] = jnp.concatenate([o[r * qt:(r + 1) * qt] for r in range(NSA_GROUP)], axis=1).astype(o_ref.dtype)


def _nsa_attention(qn, kv_cmp, ks, kv3, gl):
    b, _, s, _ = qn.shape
    nc = kv_cmp.shape[3]
    nsel = s // SEL_BLOCK
    n_pick = min(N_SELECT, nsel)
    tk = min(SEL_KV_TILE, s)
    aug = 2 * HEAD_DIM
    qt = min(ATTN_Q_TILE, s)
    assert qt % tk == 0 or tk % qt == 0
    cmp_start = jnp.arange(nc) * CMP_STRIDE
    sel_start = jnp.arange(nsel) * SEL_BLOCK
    overlap = (jnp.minimum(cmp_start[:, None] + CMP_BLOCK, sel_start[None, :] + SEL_BLOCK)
               > jnp.maximum(cmp_start[:, None], sel_start[None, :]))
    overlap_t = jnp.pad(overlap.T.astype(BF16), ((0, MAX_SEL_BLOCKS - nsel), (0, 0)))
    kv_spec = lambda a: pl.BlockSpec((1, 1, 1, s, aug), lambda bi, g, i, a=a: (a, bi, g, 0, 0))
    cmp_spec = lambda a: pl.BlockSpec((1, 1, 1, nc, aug), lambda bi, g, i, a=a: (a, bi, g, 0, 0))
    return pl.pallas_call(
        functools.partial(_attn_kernel, n_pick=n_pick, tk=tk),
        grid=(b, NSA_KV_HEADS, s // qt),
        in_specs=[
            pl.BlockSpec((1, NSA_GROUP, qt, aug), lambda bi, g, i: (bi, g, i, 0)),
            cmp_spec(0), cmp_spec(1),
            pl.BlockSpec((1, 1, s, aug + MAX_SEL_BLOCKS), lambda bi, g, i: (bi, g, 0, 0)),
            kv_spec(0), kv_spec(1), kv_spec(2),
            pl.BlockSpec((1, qt, V7X_LANES), lambda bi, g, i: (bi, i, g)),
            pl.BlockSpec((MAX_SEL_BLOCKS, nc), lambda bi, g, i: (0, 0)),
        ],
        out_specs=pl.BlockSpec((1, qt, NSA_GROUP * HEAD_DIM), lambda bi, g, i: (bi, i, g)),
        out_shape=jax.ShapeDtypeStruct((b, s, NSA_Q_COLS), BF16),
        scratch_shapes=[pltpu.SMEM((s // tk,), jnp.int32)],
        compiler_params=_params(("parallel", "parallel", "arbitrary")),
        name="nsa_attention",
    )(qn, kv_cmp, kv_cmp, ks, kv3, kv3, kv3, gl, overlap_t)


def _retention_kernel(q_ref, k_ref, v_ref, g_ref, decay_ref, xi_ref, zeta_ref, gch_ref, gn_ref, o_ref, state_ref):
    @pl.when(pl.program_id(1) == 0)
    def _():
        state_ref[...] = jnp.zeros_like(state_ref)

    scale = jnp.asarray(HEAD_DIM ** -0.5, BF16)
    for i in range(q_ref.shape[0]):
        outs = []
        for h in range(RET_HEADS):
            cols = slice(h * HEAD_DIM, (h + 1) * HEAD_DIM)
            q = q_ref[i, :, cols]
            k = k_ref[i, :, cols] * scale
            v = v_ref[i, :, cols]
            state = state_ref[i, h]
            inner = _dot_nt(q, k) * decay_ref[h]
            o = _dot(inner.astype(BF16), v) + _dot(q, state.astype(BF16)) * xi_ref[h]
            kz = (k.astype(F32) * zeta_ref[h]).astype(BF16)
            state_ref[i, h] = gch_ref[h] * state + _dot_tn(kz, v)
            mu = jnp.mean(o, axis=-1, keepdims=True)
            var = jnp.mean(jnp.square(o - mu), axis=-1, keepdims=True)
            outs.append((o - mu) * lax.rsqrt(var + EPS))
        o = jnp.concatenate(outs, axis=1) * gn_ref[...]
        o_ref[i] = (jax.nn.silu(g_ref[i].astype(F32)) * o).astype(o_ref.dtype)


def _retention(ret, gn_gain):
    b, s, _ = ret.shape
    c = RET_CHUNK
    hh = RET_HEADS
    log_gamma = jnp.log1p(-jnp.exp2(-5.0 - jnp.arange(hh, dtype=F32)))
    pos = jnp.arange(c, dtype=F32)
    diff = pos[:, None] - pos[None, :]
    decay = jnp.where(diff >= 0, jnp.exp(jnp.maximum(diff, 0.0)[None] * log_gamma[:, None, None]), 0.0)
    xi = jnp.exp((pos + 1.0)[None] * log_gamma[:, None])[..., None]
    zeta = jnp.exp((c - 1.0 - pos)[None] * log_gamma[:, None])[..., None]
    g_chunk = jnp.exp(c * log_gamma)[:, None, None]
    xi = jnp.broadcast_to(xi, (hh, c, HEAD_DIM))
    zeta = jnp.broadcast_to(zeta, (hh, c, HEAD_DIM))
    g_chunk = jnp.broadcast_to(g_chunk, (hh, HEAD_DIM, HEAD_DIM))
    nb = math.gcd(RET_BATCH, b)
    part = lambda a: pl.BlockSpec((nb, c, RET_COLS), lambda bi, n, a=a: (bi, n, a))
    full = lambda shape: pl.BlockSpec(shape, lambda bi, n: (0,) * len(shape))
    return pl.pallas_call(
        _retention_kernel,
        grid=(b // nb, s // c),
        in_specs=[part(0), part(1), part(2), part(3),
                  full((hh, c, c)), full((hh, c, HEAD_DIM)), full((hh, c, HEAD_DIM)),
                  full((hh, HEAD_DIM, HEAD_DIM)), full((1, RET_COLS))],
        out_specs=pl.BlockSpec((nb, c, RET_COLS), lambda bi, n: (bi, n, 0)),
        out_shape=jax.ShapeDtypeStruct((b, s, RET_COLS), BF16),
        scratch_shapes=[pltpu.VMEM((nb, hh, HEAD_DIM, HEAD_DIM), F32)],
        compiler_params=_params(("parallel", "arbitrary")),
        name="retention",
    )(ret, ret, ret, ret, decay, xi, zeta, g_chunk, gn_gain.reshape(1, RET_COLS).astype(F32))


def _out_proj_kernel(h_ref, a_ref, r_ref, w_ref, g_ref, h_out_ref, hn_ref):
    h = h_ref[...] + _dot(a_ref[...], w_ref[:NSA_Q_COLS]) + _dot(r_ref[...], w_ref[NSA_Q_COLS:])
    h_out_ref[...] = h
    hn_ref[...] = _rms(h, g_ref[...]).astype(BF16)


def _out_proj(h, a, r, w, g):
    t, d = h.shape
    tm = min(ROW_TILE, t)
    row = lambda n: pl.BlockSpec((tm, n), lambda i: (i, 0))
    return pl.pallas_call(
        _out_proj_kernel,
        grid=(t // tm,),
        in_specs=[row(d), row(NSA_Q_COLS), row(RET_COLS),
                  pl.BlockSpec((MIX_WIDTH, d), lambda i: (0, 0)),
                  pl.BlockSpec((1, d), lambda i: (0, 0))],
        out_specs=[row(d), row(d)],
        out_shape=[jax.ShapeDtypeStruct((t, d), F32), jax.ShapeDtypeStruct((t, d), BF16)],
        compiler_params=_params(("parallel",)),
        name="out_proj",
    )(h, a, r, w.astype(BF16), g.reshape(1, d))


def _swiglu_chunk(x, wg, wu, wd):
    hid = jax.nn.silu(_dot(x, wg)) * _dot(x, wu)
    return _dot(hid.astype(BF16), wd)


def _ffn_kernel(x_ref, wg_ref, wu_ref, wd_ref, o_ref):
    x = x_ref[...]
    acc = None
    for f in range(D_FF // FFN_COL_TILE):
        cols = slice(f * FFN_COL_TILE, (f + 1) * FFN_COL_TILE)
        part = _swiglu_chunk(x, wg_ref[:, cols], wu_ref[:, cols], wd_ref[cols, :])
        acc = part if acc is None else acc + part
    o_ref[...] = acc


def _dense_ffn(hn, wg, wu, wd):
    t, d = hn.shape
    tm = min(FFN_ROW_TILE, t)
    once = pl.Buffered(1)
    return pl.pallas_call(
        _ffn_kernel,
        grid=(t // tm,),
        in_specs=[pl.BlockSpec((tm, d), lambda i: (i, 0)),
                  pl.BlockSpec((d, D_FF), lambda i: (0, 0), pipeline_mode=once),
                  pl.BlockSpec((d, D_FF), lambda i: (0, 0), pipeline_mode=once),
                  pl.BlockSpec((D_FF, d), lambda i: (0, 0), pipeline_mode=once)],
        out_specs=pl.BlockSpec((tm, d), lambda i: (i, 0)),
        out_shape=jax.ShapeDtypeStruct((t, d), F32),
        compiler_params=_params(("parallel",)),
        name="dense_ffn",
    )(hn, wg.astype(BF16), wu.astype(BF16), wd.astype(BF16))


def _lane_column(table, lane_index):
    lane = lax.broadcasted_iota(jnp.int32, table.shape, 1)
    col = jnp.sum(jnp.where(lane == lane_index, table, 0.0), axis=-1, keepdims=True)
    return jnp.broadcast_to(col, table.shape)


def _moe_kernel(x_ref, router_ref, wg_ref, wu_ref, wd_ref, o_ref,
                xs_ref, y_ref, slot_ref, gate_ref, slot_row_ref, slot_e_ref, gate_e_ref, count_ref):
    e = pl.program_id(1)
    f = pl.program_id(2)
    tm, d = x_ref.shape
    sub = MOE_GROUP_ROWS

    @pl.when((e == 0) & (f == 0))
    def _route():
        o_ref[...] = jnp.zeros_like(o_ref)
        logits = _dot(x_ref[...], router_ref[...])
        lane = lax.broadcasted_iota(jnp.int32, logits.shape, 1)
        logits = jnp.where(lane < N_EXPERTS, logits, -jnp.inf)
        v1 = jnp.max(logits, axis=-1, keepdims=True)
        i1 = jnp.min(jnp.where(logits == v1, lane, V7X_LANES), axis=-1, keepdims=True)
        rest = jnp.where(lane == i1, -jnp.inf, logits)
        v2 = jnp.max(rest, axis=-1, keepdims=True)
        i2 = jnp.min(jnp.where(rest == v2, lane, V7X_LANES), axis=-1, keepdims=True)
        e2 = jnp.exp(v2 - v1)
        inv = 1.0 / (1.0 + e2)
        gate_ref[...] = jnp.where(lane == i1, inv, 0.0) + jnp.where(lane == i2, e2 * inv, 0.0)
        routed = jnp.where((lane == i1) | (lane == i2), 1.0, 0.0)
        c = MOE_RANK_CHUNK
        before = (lax.broadcasted_iota(jnp.int32, (c, c), 1) < lax.broadcasted_iota(jnp.int32, (c, c), 0))
        before = jnp.where(before, 1.0, 0.0).astype(BF16)
        offset = jnp.zeros((1, V7X_LANES), F32)
        for j in range(tm // c):
            part = routed[j * c:(j + 1) * c]
            rank = _dot(before, part.astype(BF16)) + offset
            slot_ref[j * c:(j + 1) * c, :] = jnp.where(part > 0.0, rank, -1.0)
            offset = offset + jnp.sum(part, axis=0, keepdims=True)
        for ee in range(N_EXPERTS):
            count_ref[ee] = offset[0, ee].astype(jnp.int32)
        slot_row_ref[...] = slot_ref[...].T[:N_EXPERTS]

    n_groups = (count_ref[e] + sub - 1) // sub

    @pl.when(f == 0)
    def _gather():
        slot_e_ref[...] = _lane_column(slot_ref[...], e)
        gate_e_ref[...] = _lane_column(gate_ref[...], e)
        slot_row = slot_row_ref[pl.ds(e, 1), :]

        def body(s, _):
            r0 = pl.multiple_of(s * sub, sub)
            want = (r0 + lax.broadcasted_iota(jnp.int32, (sub, 1), 0)).astype(F32)
            onehot = jnp.where(slot_row == want, 1.0, 0.0).astype(BF16)
            xs_ref[pl.ds(r0, sub), :] = _dot(onehot, x_ref[...]).astype(BF16)
            y_ref[pl.ds(r0, sub), :] = jnp.zeros((sub, d), F32)
            return 0

        lax.fori_loop(0, n_groups, body, 0)

    def expert(i, _, base, n):
        r0 = pl.multiple_of(base + i * n, n)
        y_ref[pl.ds(r0, n), :] += _swiglu_chunk(xs_ref[pl.ds(r0, n), :], wg_ref[0, 0], wu_ref[0, 0], wd_ref[0])
        return 0

    tail = MOE_TAIL_ROWS
    n_filled = count_ref[e] // sub
    rest = count_ref[e] - n_filled * sub
    short = rest <= tail
    n_whole = jnp.where(short, n_filled, n_filled + 1)
    n_tail = jnp.where(short & (rest > 0), 1, 0)
    lax.fori_loop(0, n_whole, functools.partial(expert, base=0, n=sub), 0)
    lax.fori_loop(0, n_tail, functools.partial(expert, base=n_filled * sub, n=tail), 0)

    @pl.when(f == pl.num_programs(2) - 1)
    def _scatter():
        tc = MOE_SCATTER_ROWS

        def body(s, _):
            r0 = pl.multiple_of(s * sub, sub)
            y = y_ref[pl.ds(r0, sub), :].astype(BF16)
            want = (r0 + lax.broadcasted_iota(jnp.int32, (1, sub), 1)).astype(F32)
            for j in range(tm // tc):
                rows = slice(j * tc, (j + 1) * tc)
                slot = jnp.concatenate([slot_e_ref[rows, :]] * (sub // V7X_LANES), axis=1)
                onehot = jnp.where(slot == want, 1.0, 0.0).astype(BF16)
                weight = jnp.concatenate([gate_e_ref[rows, :]] * (d // V7X_LANES), axis=1)
                o_ref[rows, :] += weight * _dot(onehot, y)
            return 0

        lax.fori_loop(0, n_groups, body, 0)


def _moe_ffn(hn, router, wg, wu, wd):
    t, d = hn.shape
    tm = min(MOE_ROW_TILE, t)
    tf = MOE_COL_TILE
    nf = D_FF // tf
    router = jnp.pad(router, ((0, 0), (0, V7X_LANES - N_EXPERTS))).astype(BF16)
    chunked = lambda w: jnp.transpose(w.astype(BF16).reshape(N_EXPERTS, d, nf, tf), (0, 2, 1, 3))
    once = pl.Buffered(1)
    return pl.pallas_call(
        _moe_kernel,
        grid=(t // tm, N_EXPERTS, nf),
        in_specs=[pl.BlockSpec((tm, d), lambda i, e, f: (i, 0), pipeline_mode=once),
                  pl.BlockSpec((d, V7X_LANES), lambda i, e, f: (0, 0), pipeline_mode=once),
                  pl.BlockSpec((1, 1, d, tf), lambda i, e, f: (e, f, 0, 0)),
                  pl.BlockSpec((1, 1, d, tf), lambda i, e, f: (e, f, 0, 0)),
                  pl.BlockSpec((1, tf, d), lambda i, e, f: (e, f, 0))],
        out_specs=pl.BlockSpec((tm, d), lambda i, e, f: (i, 0)),
        out_shape=jax.ShapeDtypeStruct((t, d), F32),
        scratch_shapes=[pltpu.VMEM((tm, d), BF16), pltpu.VMEM((tm, d), F32),
                        pltpu.VMEM((tm, V7X_LANES), F32), pltpu.VMEM((tm, V7X_LANES), F32),
                        pltpu.VMEM((N_EXPERTS, tm), F32),
                        pltpu.VMEM((tm, V7X_LANES), F32), pltpu.VMEM((tm, V7X_LANES), F32),
                        pltpu.SMEM((N_EXPERTS,), jnp.int32)],
        compiler_params=_params(("parallel", "arbitrary", "arbitrary"), vmem=MOE_VMEM_LIMIT),
        name="moe_ffn",
    )(hn, router, chunked(wg), chunked(wu), wd.astype(BF16))


def _ple_kernel(h_ref, f_ref, p_ref, g_ref, proj_ref, gate_ref, gf_ref, o_ref, *, final_norm):
    h = h_ref[...] + f_ref[...]
    emb = _dot(p_ref[...].astype(BF16), proj_ref[...])
    sig = jax.nn.sigmoid(_dot(_rms(h, g_ref[...]).astype(BF16), gate_ref[...]))
    h = h + emb * sig
    if final_norm:
        h = _rms(h, gf_ref[...])
    o_ref[...] = h


def _ple(h, f, p, g, proj, gate, g_final, final_norm):
    t, d = h.shape
    tm = min(ROW_TILE, t)
    vec = pl.BlockSpec((1, d), lambda i: (0, 0))
    return pl.pallas_call(
        functools.partial(_ple_kernel, final_norm=final_norm),
        grid=(t // tm,),
        in_specs=[pl.BlockSpec((tm, d), lambda i: (i, 0)),
                  pl.BlockSpec((tm, d), lambda i: (i, 0)),
                  pl.BlockSpec((tm, PLE_DIM), lambda i: (i, 0)),
                  vec,
                  pl.BlockSpec((PLE_DIM, d), lambda i: (0, 0)),
                  pl.BlockSpec((d, d), lambda i: (0, 0)),
                  vec],
        out_specs=pl.BlockSpec((tm, d), lambda i: (i, 0)),
        out_shape=jax.ShapeDtypeStruct((t, d), F32),
        compiler_params=_params(("parallel",)),
        name="ple",
    )(h, f, p, g.reshape(1, d), proj.astype(BF16), gate.astype(BF16), g_final.reshape(1, d))


def kernel(x, p, w_in, w_out, g_mix, g_ffn, g_ple, g_final, cmp_pos, cmp_w1, cmp_w2, ret_gn,
           ffn_gate, ffn_up, ffn_down, moe_router, moe_gate, moe_up, moe_down, ple_proj, ple_gate):
    b, s, d = x.shape
    depth = w_in.shape[0]
    t = b * s
    h = x
    for i in range(depth):
        qn, kcv, ks, kv3, gl, ret = _in_proj(h.reshape(b, s, d), g_mix[i], _arrange_w_in(w_in[i]))
        kv_cmp = _compress(kcv, cmp_pos[i], cmp_w1[i], cmp_w2[i])
        a = _nsa_attention(qn, kv_cmp, ks, kv3, gl)
        r = _retention(ret, ret_gn[i])
        h, hn = _out_proj(h.reshape(t, d), a.reshape(t, NSA_Q_COLS), r.reshape(t, RET_COLS), w_out[i], g_ffn[i])
        if i % 2 == 0:
            f = _dense_ffn(hn, ffn_gate[i // 2], ffn_up[i // 2], ffn_down[i // 2])
        else:
            f = _moe_ffn(hn, moe_router[i // 2], moe_gate[i // 2], moe_up[i // 2], moe_down[i // 2])
        h = _ple(h, f, p[i].reshape(t, PLE_DIM), g_ple[i], ple_proj[i], ple_gate[i], g_final, i == depth - 1)
    return h.reshape(b, s, d)
```

```python
import functools
import math

import jax
import jax.numpy as jnp
from jax import lax
from jax.experimental import pallas as pl
from jax.experimental.pallas import tpu as pltpu

F32 = jnp.float32
BF16 = jnp.bfloat16

D_MODEL = 1024
HEAD_DIM = 64
NSA_HEADS = 8
NSA_KV_HEADS = 2
NSA_GROUP = NSA_HEADS // NSA_KV_HEADS
RET_HEADS = 8
CMP_BLOCK = 32
CMP_STRIDE = 16
CMP_HIDDEN = 256
SEL_BLOCK = 64
N_SELECT = 16
N_LOCAL_SEL = 2
WINDOW = 512
Q_BLOCK = 128
RET_CHUNK = 128
D_FF = 3584
N_EXPERTS = 8
TOP_K = 2
PLE_DIM = 256
EPS = 1e-6
NEG_INF = -1e30
BIG = 1e9

NSA_Q_COLS = NSA_HEADS * HEAD_DIM
NSA_KV_COLS = NSA_KV_HEADS * HEAD_DIM
NSA_GATE_COLS = 3 * NSA_HEADS
RET_COLS = RET_HEADS * HEAD_DIM
MIX_WIDTH = NSA_Q_COLS + RET_COLS

V7X_LANES = 128
V7X_VMEM_BYTES = 64 * 1024 * 1024
VMEM_LIMIT = V7X_VMEM_BYTES * 3 // 4

ROW_TILE = 512
ATTN_Q_TILE = 256
SEL_KV_TILE = 256
RET_BATCH = 4
FFN_ROW_TILE = 512
FFN_COL_TILE = 512
MOE_COL_TILE = 896
MOE_ROW_TILE = 2048
MOE_GROUP_ROWS = 256
MOE_TAIL_ROWS = 64
MOE_RANK_CHUNK = 256
MOE_SCATTER_ROWS = 512
MOE_VMEM_LIMIT = V7X_VMEM_BYTES * 7 // 8

POS_SHIFT = 6
POS_SPLIT = 1 << POS_SHIFT
SEL_SHIFT = SEL_BLOCK.bit_length() - 1
MAX_SEL_BLOCKS = V7X_LANES


def _params(sem, vmem=VMEM_LIMIT):
    return pltpu.CompilerParams(dimension_semantics=sem, vmem_limit_bytes=vmem)


def _dot(a, b):
    return jnp.dot(a, b, preferred_element_type=F32)


def _dot_nt(a, b):
    return lax.dot_general(a, b, (((1,), (1,)), ((), ())), preferred_element_type=F32)


def _dot_tn(a, b):
    return lax.dot_general(a, b, (((0,), (0,)), ((), ())), preferred_element_type=F32)


def _rms(x, g):
    return x * lax.rsqrt(jnp.mean(x * x, axis=-1, keepdims=True) + EPS) * g


def _lane_features(shape, first, second):
    lane = lax.broadcasted_iota(jnp.int32, shape, 1)
    return jnp.where(lane == 0, first, jnp.where(lane == 1, second, 0))


def _key_features(pos, width):
    return _lane_features((pos.shape[0], width), pos >> POS_SHIFT, pos & (POS_SPLIT - 1))


def _ones_features(n, width):
    return _lane_features((n, width), 1, 0)


def _int_to_bf16(x):
    return x.astype(F32).astype(BF16)


def _in_proj_kernel(h_ref, g_ref, w_ref, qn_ref, kcv_ref, ks_ref, kv3_ref, gl_ref, ret_ref):
    tm = h_ref.shape[1]
    xn = _rms(h_ref[0], g_ref[...]).astype(BF16)
    pair = 2 * HEAD_DIM
    pos = pl.program_id(1) * tm + lax.broadcasted_iota(jnp.int32, (tm, 1), 0)
    kfeat = _int_to_bf16(_key_features(pos, HEAD_DIM))
    vfeat = _int_to_bf16(_ones_features(tm, HEAD_DIM))
    lane = lax.broadcasted_iota(jnp.int32, (tm, MAX_SEL_BLOCKS), 1)
    block_onehot = jnp.where(lane == (pos >> SEL_SHIFT), 1.0, 0.0).astype(BF16)

    for j in range(NSA_HEADS // 2):
        z = (_dot(xn, w_ref[:, j * pair:(j + 1) * pair]) * HEAD_DIM ** -0.5).astype(BF16)
        for k in range(2):
            slope = 2.0 ** -(2 * j + k + 1)
            qfeat = _lane_features((tm, HEAD_DIM), POS_SPLIT * slope, slope).astype(BF16)
            qn_ref[0, 2 * j + k] = jnp.concatenate([z[:, k * HEAD_DIM:(k + 1) * HEAD_DIM], qfeat], axis=1)
    base = NSA_Q_COLS
    for a in range(2):
        kcv_ref[a, 0] = _dot(xn, w_ref[:, base + a * pair:base + (a + 1) * pair]).astype(BF16)
    base += 2 * pair
    for a in range(4):
        z = _dot(xn, w_ref[:, base + a * pair:base + (a + 1) * pair]).astype(BF16)
        for g in range(NSA_KV_HEADS):
            zg = z[:, g * HEAD_DIM:(g + 1) * HEAD_DIM]
            if a == 0:
                ks_ref[0, g] = jnp.concatenate([zg, kfeat, block_onehot], axis=1)
            else:
                kv3_ref[a - 1, 0, g] = jnp.concatenate([zg, kfeat if a == 2 else vfeat], axis=1)
    base += 4 * pair
    for a in range(4):
        ret_ref[0, :, a * RET_COLS:(a + 1) * RET_COLS] = _dot(
            xn, w_ref[:, base + a * RET_COLS:base + (a + 1) * RET_COLS]).astype(BF16)
    base += 4 * RET_COLS
    gl_ref[0] = _dot(xn, w_ref[:, base:base + NSA_KV_HEADS * V7X_LANES])


def _arrange_w_in(w):
    q_end = NSA_Q_COLS
    kv_end = q_end + 6 * NSA_KV_COLS
    gl_end = kv_end + NSA_GATE_COLS
    gl = w[:, kv_end:gl_end].reshape(D_MODEL, 3, NSA_KV_HEADS, NSA_GROUP)
    gl = jnp.transpose(gl, (0, 2, 1, 3)).reshape(D_MODEL, NSA_KV_HEADS, 3 * NSA_GROUP)
    gl = jnp.pad(gl, ((0, 0), (0, 0), (0, V7X_LANES - 3 * NSA_GROUP)))
    gl = gl.reshape(D_MODEL, NSA_KV_HEADS * V7X_LANES)
    return jnp.concatenate([w[:, :kv_end], w[:, gl_end:], gl], axis=1).astype(BF16)


def _in_proj(h, g, w):
    b, s, d = h.shape
    assert s // SEL_BLOCK <= MAX_SEL_BLOCKS
    tm = min(ROW_TILE, s)
    ncols = w.shape[1]
    aug = 2 * HEAD_DIM
    return pl.pallas_call(
        _in_proj_kernel,
        grid=(b, s // tm),
        in_specs=[
            pl.BlockSpec((1, tm, d), lambda bi, si: (bi, si, 0)),
            pl.BlockSpec((1, d), lambda bi, si: (0, 0)),
            pl.BlockSpec((d, ncols), lambda bi, si: (0, 0)),
        ],
        out_specs=[
            pl.BlockSpec((1, NSA_HEADS, tm, aug), lambda bi, si: (bi, 0, si, 0)),
            pl.BlockSpec((2, 1, tm, NSA_KV_COLS), lambda bi, si: (0, bi, si, 0)),
            pl.BlockSpec((1, NSA_KV_HEADS, tm, aug + MAX_SEL_BLOCKS), lambda bi, si: (bi, 0, si, 0)),
            pl.BlockSpec((3, 1, NSA_KV_HEADS, tm, aug), lambda bi, si: (0, bi, 0, si, 0)),
            pl.BlockSpec((1, tm, NSA_KV_HEADS * V7X_LANES), lambda bi, si: (bi, si, 0)),
            pl.BlockSpec((1, tm, 4 * RET_COLS), lambda bi, si: (bi, si, 0)),
        ],
        out_shape=[
            jax.ShapeDtypeStruct((b, NSA_HEADS, s, aug), BF16),
            jax.ShapeDtypeStruct((2, b, s, NSA_KV_COLS), BF16),
            jax.ShapeDtypeStruct((b, NSA_KV_HEADS, s, aug + MAX_SEL_BLOCKS), BF16),
            jax.ShapeDtypeStruct((3, b, NSA_KV_HEADS, s, aug), BF16),
            jax.ShapeDtypeStruct((b, s, NSA_KV_HEADS * V7X_LANES), F32),
            jax.ShapeDtypeStruct((b, s, 4 * RET_COLS), BF16),
        ],
        compiler_params=_params(("parallel", "parallel")),
        name="in_proj",
    )(h, g.reshape(1, d), w)


def _compress_kernel(x_ref, wbig_ref, pos_ref, w1_ref, w2_ref, o_ref):
    nch = x_ref.shape[2]
    u = _dot(x_ref[0, 0], wbig_ref[0])
    hid0 = _dot(pos_ref[0], w1_ref[0])[0:1]
    w2 = w2_ref[0]
    cmp_end = lax.broadcasted_iota(jnp.int32, (nch, 1), 0) * CMP_STRIDE + (CMP_BLOCK - 1)
    feat = _int_to_bf16(jnp.where(pl.program_id(0) == 0, _key_features(cmp_end, HEAD_DIM),
                                  _ones_features(nch, HEAD_DIM)))
    for g in range(NSA_KV_HEADS):
        c0 = g * 2 * CMP_HIDDEN
        first = u[:, c0:c0 + CMP_HIDDEN]
        second = u[:, c0 + CMP_HIDDEN:c0 + 2 * CMP_HIDDEN]
        hid = first + pltpu.roll(second, nch - 1, 0) + hid0
        out = _dot(jax.nn.gelu(hid).astype(BF16), w2).astype(BF16)
        o_ref[0, 0, g] = jnp.concatenate([out, feat], axis=1)


def _arrange_cmp_w1(w1):
    r = CMP_BLOCK // CMP_STRIDE
    w1r = w1.reshape(2, r, CMP_STRIDE, HEAD_DIM, CMP_HIDDEN)
    eye = jnp.eye(NSA_KV_HEADS, dtype=w1.dtype)
    big = jnp.einsum('krcdh,gf->kcgdfrh', w1r, eye)
    return big.reshape(2, CMP_STRIDE * NSA_KV_COLS, NSA_KV_HEADS * r * CMP_HIDDEN).astype(BF16)


def _compress(kcv, cmp_pos, cmp_w1, cmp_w2):
    _, b, s, _ = kcv.shape
    nch = s // CMP_STRIDE
    x = kcv.reshape(2, b, nch, CMP_STRIDE * NSA_KV_COLS)
    wbig = _arrange_cmp_w1(cmp_w1)
    pos = jnp.broadcast_to(cmp_pos.reshape(2, 1, CMP_BLOCK * HEAD_DIM), (2, 8, CMP_BLOCK * HEAD_DIM)).astype(BF16)
    w1 = cmp_w1.reshape(2, CMP_BLOCK * HEAD_DIM, CMP_HIDDEN).astype(BF16)
    w2 = cmp_w2.astype(BF16)
    kdim = CMP_STRIDE * NSA_KV_COLS
    return pl.pallas_call(
        _compress_kernel,
        grid=(2, b),
        in_specs=[
            pl.BlockSpec((1, 1, nch, kdim), lambda a, bi: (a, bi, 0, 0)),
            pl.BlockSpec((1, kdim, wbig.shape[2]), lambda a, bi: (a, 0, 0)),
            pl.BlockSpec((1, 8, CMP_BLOCK * HEAD_DIM), lambda a, bi: (a, 0, 0)),
            pl.BlockSpec((1, CMP_BLOCK * HEAD_DIM, CMP_HIDDEN), lambda a, bi: (a, 0, 0)),
            pl.BlockSpec((1, CMP_HIDDEN, HEAD_DIM), lambda a, bi: (a, 0, 0)),
        ],
        out_specs=pl.BlockSpec((1, 1, NSA_KV_HEADS, nch, 2 * HEAD_DIM), lambda a, bi: (a, bi, 0, 0, 0)),
        out_shape=jax.ShapeDtypeStruct((2, b, NSA_KV_HEADS, nch, 2 * HEAD_DIM), BF16),
        compiler_params=_params(("parallel", "parallel")),
        name="compress",
    )(x, wbig, pos, w1, w2)


def _normalise_t(acc):
    return acc[:HEAD_DIM] * (1.0 / jnp.maximum(acc[HEAD_DIM:HEAD_DIM + 1], 1e-30))


def _attn_kernel(q_ref, kc_ref, vc_ref, ks_ref, vs_ref, kw_ref, vw_ref, gl_ref, ovt_ref, o_ref, tiles_ref,
                 *, n_pick, tk):
    qt = q_ref.shape[2]
    cols = NSA_GROUP * qt
    t0 = pl.program_id(2) * qt
    q_t = q_ref[0].reshape(cols, 2 * HEAD_DIM).T
    tq = t0 + (lax.broadcasted_iota(jnp.int32, (1, cols), 1) & (qt - 1))

    nc = kc_ref.shape[3]
    cmp_end = lax.broadcasted_iota(jnp.int32, (nc, 1), 0) * CMP_STRIDE + (CMP_BLOCK - 1)
    valid_c = cmp_end <= tq
    logit_c = jnp.where(valid_c, _dot(kc_ref[0, 0, 0], q_t), NEG_INF)
    e_c = jnp.where(valid_c, jnp.exp(logit_c - jnp.max(logit_c, axis=0, keepdims=True)), 0.0)
    p_c = e_c * (1.0 / jnp.maximum(jnp.sum(e_c, axis=0, keepdims=True), 1e-30))
    o_c = _dot(vc_ref[0, 0, 0].T, p_c.astype(BF16))[:HEAD_DIM]

    p_sum = p_c[:, 0:qt]
    for r in range(1, NSA_GROUP):
        p_sum = p_sum + p_c[:, r * qt:(r + 1) * qt]
    p_hi = p_sum.astype(BF16)
    p_lo = (p_sum - p_hi.astype(F32)).astype(BF16)
    imp = _dot(ovt_ref[...], p_hi) + _dot(ovt_ref[...], p_lo)
    shape = (MAX_SEL_BLOCKS, qt)
    blk = lax.broadcasted_iota(jnp.int32, shape, 0)
    back = ((t0 + lax.broadcasted_iota(jnp.int32, shape, 1)) >> SEL_SHIFT) - blk
    forced = (blk == 0) | ((back >= 0) & (back < N_LOCAL_SEL))
    score = jnp.where(forced, BIG, jnp.where(back >= 0, imp, -BIG))

    def pick(_, work):
        m = jnp.max(work, axis=0, keepdims=True)
        first = jnp.min(jnp.where(work == m, blk, MAX_SEL_BLOCKS), axis=0, keepdims=True)
        return jnp.where(blk == first, -jnp.inf, work)

    sel = jnp.where(lax.fori_loop(0, n_pick, pick, score) == -jnp.inf, 1.0, 0.0)
    sel_bias = ((sel - 1.0) * -NEG_INF).astype(BF16)
    q_sel = jnp.concatenate([q_t, jnp.concatenate([sel_bias] * NSA_GROUP, axis=1)], axis=0)

    n_own = max(1, qt // tk)
    n_full = t0 // tk
    blocks_per_tile = tk // SEL_BLOCK
    block_used = jnp.max(sel, axis=1, keepdims=True)
    n_visit = jnp.int32(0)
    for j in range(tiles_ref.shape[0]):
        used = jnp.max(block_used[j * blocks_per_tile:(j + 1) * blocks_per_tile]) > 0.0
        tiles_ref[n_visit] = j
        n_visit = n_visit + (used & (j < n_full)).astype(jnp.int32)

    def sweep(tiles, carry, causal):
        m_i, acc = carry
        starts = [pl.multiple_of(j * tk, tk) for j in tiles]
        scores = []
        for k0 in starts:
            s = _dot(ks_ref[0, 0, pl.ds(k0, tk), :], q_sel)
            if causal:
                s = jnp.where(k0 + lax.broadcasted_iota(jnp.int32, (tk, 1), 0) <= tq, s, NEG_INF)
            scores.append(s)
        m_new = m_i
        for s in scores:
            m_new = jnp.maximum(m_new, jnp.max(s, axis=0, keepdims=True))
        acc = jnp.exp(m_i - m_new) * acc
        for s, k0 in zip(scores, starts):
            acc = acc + _dot(vs_ref[0, 0, 0, pl.ds(k0, tk), :].T, jnp.exp(s - m_new).astype(BF16))
        return m_new, acc

    odd = n_visit & 1
    init = (jnp.full((1, cols), NEG_INF, F32), jnp.zeros((2 * HEAD_DIM, cols), F32))
    carry = lax.fori_loop(0, odd, lambda i, c: sweep([tiles_ref[i]], c, causal=False), init)
    carry = lax.fori_loop(0, n_visit // 2,
                          lambda i, c: sweep([tiles_ref[odd + 2 * i], tiles_ref[odd + 2 * i + 1]], c, causal=False),
                          carry)
    carry = sweep([n_full + j for j in range(n_own)], carry, causal=True)
    o_s = _normalise_t(carry[1])

    band = WINDOW + qt
    w0 = pl.multiple_of(jnp.maximum(t0 - WINDOW, 0), math.gcd(qt, WINDOW))
    kpos = w0 + lax.broadcasted_iota(jnp.int32, (band, 1), 0)
    valid_w = (kpos <= tq) & (kpos > tq - WINDOW)
    logit_w = jnp.where(valid_w, _dot(kw_ref[0, 0, 0, pl.ds(w0, band), :], q_t), NEG_INF)
    e_w = jnp.exp(logit_w - jnp.max(logit_w, axis=0, keepdims=True))
    o_w = _normalise_t(_dot(vw_ref[0, 0, 0, pl.ds(w0, band), :].T, e_w.astype(BF16)))

    gate_t = jax.nn.sigmoid(gl_ref[0]).T

    def branch_gate(branch):
        first = branch * NSA_GROUP
        return jnp.concatenate([gate_t[first + r:first + r + 1] for r in range(NSA_GROUP)], axis=1)

    o_t = branch_gate(0) * o_c + branch_gate(1) * o_s + branch_gate(2) * o_w
    o = o_t.T
    o_ref[0] = jnp.concatenate([o[r * qt:(r + 1) * qt] for r in range(NSA_GROUP)], axis=1).astype(o_ref.dtype)


def _nsa_attention(qn, kv_cmp, ks, kv3, gl):
    b, _, s, _ = qn.shape
    nc = kv_cmp.shape[3]
    nsel = s // SEL_BLOCK
    n_pick = min(N_SELECT, nsel)
    tk = min(SEL_KV_TILE, s)
    aug = 2 * HEAD_DIM
    qt = min(ATTN_Q_TILE, s)
    assert qt % tk == 0 or tk % qt == 0
    cmp_start = jnp.arange(nc) * CMP_STRIDE
    sel_start = jnp.arange(nsel) * SEL_BLOCK
    overlap = (jnp.minimum(cmp_start[:, None] + CMP_BLOCK, sel_start[None, :] + SEL_BLOCK)
               > jnp.maximum(cmp_start[:, None], sel_start[None, :]))
    overlap_t = jnp.pad(overlap.T.astype(BF16), ((0, MAX_SEL_BLOCKS - nsel), (0, 0)))
    kv_spec = lambda a: pl.BlockSpec((1, 1, 1, s, aug), lambda bi, g, i, a=a: (a, bi, g, 0, 0))
    cmp_spec = lambda a: pl.BlockSpec((1, 1, 1, nc, aug), lambda bi, g, i, a=a: (a, bi, g, 0, 0))
    return pl.pallas_call(
        functools.partial(_attn_kernel, n_pick=n_pick, tk=tk),
        grid=(b, NSA_KV_HEADS, s // qt),
        in_specs=[
            pl.BlockSpec((1, NSA_GROUP, qt, aug), lambda bi, g, i: (bi, g, i, 0)),
            cmp_spec(0), cmp_spec(1),
            pl.BlockSpec((1, 1, s, aug + MAX_SEL_BLOCKS), lambda bi, g, i: (bi, g, 0, 0)),
            kv_spec(0), kv_spec(1), kv_spec(2),
            pl.BlockSpec((1, qt, V7X_LANES), lambda bi, g, i: (bi, i, g)),
            pl.BlockSpec((MAX_SEL_BLOCKS, nc), lambda bi, g, i: (0, 0)),
        ],
        out_specs=pl.BlockSpec((1, qt, NSA_GROUP * HEAD_DIM), lambda bi, g, i: (bi, i, g)),
        out_shape=jax.ShapeDtypeStruct((b, s, NSA_Q_COLS), BF16),
        scratch_shapes=[pltpu.SMEM((s // tk,), jnp.int32)],
        compiler_params=_params(("parallel", "parallel", "arbitrary")),
        name="nsa_attention",
    )(qn, kv_cmp, kv_cmp, ks, kv3, kv3, kv3, gl, overlap_t)


def _retention_kernel(q_ref, k_ref, v_ref, g_ref, decay_ref, xi_ref, zeta_ref, gch_ref, gn_ref, o_ref, state_ref):
    @pl.when(pl.program_id(1) == 0)
    def _():
        state_ref[...] = jnp.zeros_like(state_ref)

    scale = jnp.asarray(HEAD_DIM ** -0.5, BF16)
    for i in range(q_ref.shape[0]):
        outs = []
        for h in range(RET_HEADS):
            cols = slice(h * HEAD_DIM, (h + 1) * HEAD_DIM)
            q = q_ref[i, :, cols]
            k = k_ref[i, :, cols] * scale
            v = v_ref[i, :, cols]
            state = state_ref[i, h]
            inner = _dot_nt(q, k) * decay_ref[h]
            o = _dot(inner.astype(BF16), v) + _dot(q, state.astype(BF16)) * xi_ref[h]
            kz = (k.astype(F32) * zeta_ref[h]).astype(BF16)
            state_ref[i, h] = gch_ref[h] * state + _dot_tn(kz, v)
            mu = jnp.mean(o, axis=-1, keepdims=True)
            var = jnp.mean(jnp.square(o - mu), axis=-1, keepdims=True)
            outs.append((o - mu) * lax.rsqrt(var + EPS))
        o = jnp.concatenate(outs, axis=1) * gn_ref[...]
        o_ref[i] = (jax.nn.silu(g_ref[i].astype(F32)) * o).astype(o_ref.dtype)


def _retention(ret, gn_gain):
    b, s, _ = ret.shape
    c = RET_CHUNK
    hh = RET_HEADS
    log_gamma = jnp.log1p(-jnp.exp2(-5.0 - jnp.arange(hh, dtype=F32)))
    pos = jnp.arange(c, dtype=F32)
    diff = pos[:, None] - pos[None, :]
    decay = jnp.where(diff >= 0, jnp.exp(jnp.maximum(diff, 0.0)[None] * log_gamma[:, None, None]), 0.0)
    xi = jnp.exp((pos + 1.0)[None] * log_gamma[:, None])[..., None]
    zeta = jnp.exp((c - 1.0 - pos)[None] * log_gamma[:, None])[..., None]
    g_chunk = jnp.exp(c * log_gamma)[:, None, None]
    xi = jnp.broadcast_to(xi, (hh, c, HEAD_DIM))
    zeta = jnp.broadcast_to(zeta, (hh, c, HEAD_DIM))
    g_chunk = jnp.broadcast_to(g_chunk, (hh, HEAD_DIM, HEAD_DIM))
    nb = math.gcd(RET_BATCH, b)
    part = lambda a: pl.BlockSpec((nb, c, RET_COLS), lambda bi, n, a=a: (bi, n, a))
    full = lambda shape: pl.BlockSpec(shape, lambda bi, n: (0,) * len(shape))
    return pl.pallas_call(
        _retention_kernel,
        grid=(b // nb, s // c),
        in_specs=[part(0), part(1), part(2), part(3),
                  full((hh, c, c)), full((hh, c, HEAD_DIM)), full((hh, c, HEAD_DIM)),
                  full((hh, HEAD_DIM, HEAD_DIM)), full((1, RET_COLS))],
        out_specs=pl.BlockSpec((nb, c, RET_COLS), lambda bi, n: (bi, n, 0)),
        out_shape=jax.ShapeDtypeStruct((b, s, RET_COLS), BF16),
        scratch_shapes=[pltpu.VMEM((nb, hh, HEAD_DIM, HEAD_DIM), F32)],
        compiler_params=_params(("parallel", "arbitrary")),
        name="retention",
    )(ret, ret, ret, ret, decay, xi, zeta, g_chunk, gn_gain.reshape(1, RET_COLS).astype(F32))


def _out_proj_kernel(h_ref, a_ref, r_ref, w_ref, g_ref, h_out_ref, hn_ref):
    h = h_ref[...] + _dot(a_ref[...], w_ref[:NSA_Q_COLS]) + _dot(r_ref[...], w_ref[NSA_Q_COLS:])
    h_out_ref[...] = h
    hn_ref[...] = _rms(h, g_ref[...]).astype(BF16)


def _out_proj(h, a, r, w, g):
    t, d = h.shape
    tm = min(ROW_TILE, t)
    row = lambda n: pl.BlockSpec((tm, n), lambda i: (i, 0))
    return pl.pallas_call(
        _out_proj_kernel,
        grid=(t // tm,),
        in_specs=[row(d), row(NSA_Q_COLS), row(RET_COLS),
                  pl.BlockSpec((MIX_WIDTH, d), lambda i: (0, 0)),
                  pl.BlockSpec((1, d), lambda i: (0, 0))],
        out_specs=[row(d), row(d)],
        out_shape=[jax.ShapeDtypeStruct((t, d), F32), jax.ShapeDtypeStruct((t, d), BF16)],
        compiler_params=_params(("parallel",)),
        name="out_proj",
    )(h, a, r, w.astype(BF16), g.reshape(1, d))


def _swiglu_chunk(x, wg, wu, wd):
    hid = jax.nn.silu(_dot(x, wg)) * _dot(x, wu)
    return _dot(hid.astype(BF16), wd)


def _ffn_kernel(x_ref, wg_ref, wu_ref, wd_ref, o_ref):
    x = x_ref[...]
    acc = None
    for f in range(D_FF // FFN_COL_TILE):
        cols = slice(f * FFN_COL_TILE, (f + 1) * FFN_COL_TILE)
        part = _swiglu_chunk(x, wg_ref[:, cols], wu_ref[:, cols], wd_ref[cols, :])
        acc = part if acc is None else acc + part
    o_ref[...] = acc


def _dense_ffn(hn, wg, wu, wd):
    t, d = hn.shape
    tm = min(FFN_ROW_TILE, t)
    once = pl.Buffered(1)
    return pl.pallas_call(
        _ffn_kernel,
        grid=(t // tm,),
        in_specs=[pl.BlockSpec((tm, d), lambda i: (i, 0)),
                  pl.BlockSpec((d, D_FF), lambda i: (0, 0), pipeline_mode=once),
                  pl.BlockSpec((d, D_FF), lambda i: (0, 0), pipeline_mode=once),
                  pl.BlockSpec((D_FF, d), lambda i: (0, 0), pipeline_mode=once)],
        out_specs=pl.BlockSpec((tm, d), lambda i: (i, 0)),
        out_shape=jax.ShapeDtypeStruct((t, d), F32),
        compiler_params=_params(("parallel",)),
        name="dense_ffn",
    )(hn, wg.astype(BF16), wu.astype(BF16), wd.astype(BF16))


def _lane_column(table, lane_index):
    lane = lax.broadcasted_iota(jnp.int32, table.shape, 1)
    col = jnp.sum(jnp.where(lane == lane_index, table, 0.0), axis=-1, keepdims=True)
    return jnp.broadcast_to(col, table.shape)


def _moe_kernel(x_ref, router_ref, wg_ref, wu_ref, wd_ref, o_ref,
                xs_ref, y_ref, slot_ref, gate_ref, slot_row_ref, slot_e_ref, gate_e_ref, count_ref):
    e = pl.program_id(1)
    f = pl.program_id(2)
    tm, d = x_ref.shape
    sub = MOE_GROUP_ROWS

    @pl.when((e == 0) & (f == 0))
    def _route():
        o_ref[...] = jnp.zeros_like(o_ref)
        logits = _dot(x_ref[...], router_ref[...])
        lane = lax.broadcasted_iota(jnp.int32, logits.shape, 1)
        logits = jnp.where(lane < N_EXPERTS, logits, -jnp.inf)
        v1 = jnp.max(logits, axis=-1, keepdims=True)
        i1 = jnp.min(jnp.where(logits == v1, lane, V7X_LANES), axis=-1, keepdims=True)
        rest = jnp.where(lane == i1, -jnp.inf, logits)
        v2 = jnp.max(rest, axis=-1, keepdims=True)
        i2 = jnp.min(jnp.where(rest == v2, lane, V7X_LANES), axis=-1, keepdims=True)
        e2 = jnp.exp(v2 - v1)
        inv = 1.0 / (1.0 + e2)
        gate_ref[...] = jnp.where(lane == i1, inv, 0.0) + jnp.where(lane == i2, e2 * inv, 0.0)
        routed = jnp.where((lane == i1) | (lane == i2), 1.0, 0.0)
        c = MOE_RANK_CHUNK
        before = (lax.broadcasted_iota(jnp.int32, (c, c), 1) < lax.broadcasted_iota(jnp.int32, (c, c), 0))
        before = jnp.where(before, 1.0, 0.0).astype(BF16)
        offset = jnp.zeros((1, V7X_LANES), F32)
        for j in range(tm // c):
            part = routed[j * c:(j + 1) * c]
            rank = _dot(before, part.astype(BF16)) + offset
            slot_ref[j * c:(j + 1) * c, :] = jnp.where(part > 0.0, rank, -1.0)
            offset = offset + jnp.sum(part, axis=0, keepdims=True)
        for ee in range(N_EXPERTS):
            count_ref[ee] = offset[0, ee].astype(jnp.int32)
        slot_row_ref[...] = slot_ref[...].T[:N_EXPERTS]

    n_groups = (count_ref[e] + sub - 1) // sub

    @pl.when(f == 0)
    def _gather():
        slot_e_ref[...] = _lane_column(slot_ref[...], e)
        gate_e_ref[...] = _lane_column(gate_ref[...], e)
        slot_row = slot_row_ref[pl.ds(e, 1), :]

        def body(s, _):
            r0 = pl.multiple_of(s * sub, sub)
            want = (r0 + lax.broadcasted_iota(jnp.int32, (sub, 1), 0)).astype(F32)
            onehot = jnp.where(slot_row == want, 1.0, 0.0).astype(BF16)
            xs_ref[pl.ds(r0, sub), :] = _dot(onehot, x_ref[...]).astype(BF16)
            y_ref[pl.ds(r0, sub), :] = jnp.zeros((sub, d), F32)
            return 0

        lax.fori_loop(0, n_groups, body, 0)

    def expert(i, _, base, n):
        r0 = pl.multiple_of(base + i * n, n)
        y_ref[pl.ds(r0, n), :] += _swiglu_chunk(xs_ref[pl.ds(r0, n), :], wg_ref[0, 0], wu_ref[0, 0], wd_ref[0])
        return 0

    tail = MOE_TAIL_ROWS
    n_filled = count_ref[e] // sub
    rest = count_ref[e] - n_filled * sub
    short = rest <= tail
    n_whole = jnp.where(short, n_filled, n_filled + 1)
    n_tail = jnp.where(short & (rest > 0), 1, 0)
    lax.fori_loop(0, n_whole, functools.partial(expert, base=0, n=sub), 0)
    lax.fori_loop(0, n_tail, functools.partial(expert, base=n_filled * sub, n=tail), 0)

    @pl.when(f == pl.num_programs(2) - 1)
    def _scatter():
        tc = MOE_SCATTER_ROWS

        def body(s, _):
            r0 = pl.multiple_of(s * sub, sub)
            y = y_ref[pl.ds(r0, sub), :].astype(BF16)
            want = (r0 + lax.broadcasted_iota(jnp.int32, (1, sub), 1)).astype(F32)
            for j in range(tm // tc):
                rows = slice(j * tc, (j + 1) * tc)
                slot = jnp.concatenate([slot_e_ref[rows, :]] * (sub // V7X_LANES), axis=1)
                onehot = jnp.where(slot == want, 1.0, 0.0).astype(BF16)
                weight = jnp.concatenate([gate_e_ref[rows, :]] * (d // V7X_LANES), axis=1)
                o_ref[rows, :] += weight * _dot(onehot, y)
            return 0

        lax.fori_loop(0, n_groups, body, 0)


def _moe_ffn(hn, router, wg, wu, wd):
    t, d = hn.shape
    tm = min(MOE_ROW_TILE, t)
    tf = MOE_COL_TILE
    nf = D_FF // tf
    router = jnp.pad(router, ((0, 0), (0, V7X_LANES - N_EXPERTS))).astype(BF16)
    chunked = lambda w: jnp.transpose(w.astype(BF16).reshape(N_EXPERTS, d, nf, tf), (0, 2, 1, 3))
    once = pl.Buffered(1)
    return pl.pallas_call(
        _moe_kernel,
        grid=(t // tm, N_EXPERTS, nf),
        in_specs=[pl.BlockSpec((tm, d), lambda i, e, f: (i, 0), pipeline_mode=once),
                  pl.BlockSpec((d, V7X_LANES), lambda i, e, f: (0, 0), pipeline_mode=once),
                  pl.BlockSpec((1, 1, d, tf), lambda i, e, f: (e, f, 0, 0)),
                  pl.BlockSpec((1, 1, d, tf), lambda i, e, f: (e, f, 0, 0)),
                  pl.BlockSpec((1, tf, d), lambda i, e, f: (e, f, 0))],
        out_specs=pl.BlockSpec((tm, d), lambda i, e, f: (i, 0)),
        out_shape=jax.ShapeDtypeStruct((t, d), F32),
        scratch_shapes=[pltpu.VMEM((tm, d), BF16), pltpu.VMEM((tm, d), F32),
                        pltpu.VMEM((tm, V7X_LANES), F32), pltpu.VMEM((tm, V7X_LANES), F32),
                        pltpu.VMEM((N_EXPERTS, tm), F32),
                        pltpu.VMEM((tm, V7X_LANES), F32), pltpu.VMEM((tm, V7X_LANES), F32),
                        pltpu.SMEM((N_EXPERTS,), jnp.int32)],
        compiler_params=_params(("parallel", "arbitrary", "arbitrary"), vmem=MOE_VMEM_LIMIT),
        name="moe_ffn",
    )(hn, router, chunked(wg), chunked(wu), wd.astype(BF16))


def _ple_kernel(h_ref, f_ref, p_ref, g_ref, proj_ref, gate_ref, gf_ref, o_ref, *, final_norm):
    h = h_ref[...] + f_ref[...]
    emb = _dot(p_ref[...].astype(BF16), proj_ref[...])
    sig = jax.nn.sigmoid(_dot(_rms(h, g_ref[...]).astype(BF16), gate_ref[...]))
    h = h + emb * sig
    if final_norm:
        h = _rms(h, gf_ref[...])
    o_ref[...] = h


def _ple(h, f, p, g, proj, gate, g_final, final_norm):
    t, d = h.shape
    tm = min(ROW_TILE, t)
    vec = pl.BlockSpec((1, d), lambda i: (0, 0))
    return pl.pallas_call(
        functools.partial(_ple_kernel, final_norm=final_norm),
        grid=(t // tm,),
        in_specs=[pl.BlockSpec((tm, d), lambda i: (i, 0)),
                  pl.BlockSpec((tm, d), lambda i: (i, 0)),
                  pl.BlockSpec((tm, PLE_DIM), lambda i: (i, 0)),
                  vec,
                  pl.BlockSpec((PLE_DIM, d), lambda i: (0, 0)),
                  pl.BlockSpec((d, d), lambda i: (0, 0)),
                  vec],
        out_specs=pl.BlockSpec((tm, d), lambda i: (i, 0)),
        out_shape=jax.ShapeDtypeStruct((t, d), F32),
        compiler_params=_params(("parallel",)),
        name="ple",
    )(h, f, p, g.reshape(1, d), proj.astype(BF16), gate.astype(BF16), g_final.reshape(1, d))


def kernel(x, p, w_in, w_out, g_mix, g_ffn, g_ple, g_final, cmp_pos, cmp_w1, cmp_w2, ret_gn,
           ffn_gate, ffn_up, ffn_down, moe_router, moe_gate, moe_up, moe_down, ple_proj, ple_gate):
    b, s, d = x.shape
    depth = w_in.shape[0]
    t = b * s
    h = x
    for i in range(depth):
        qn, kcv, ks, kv3, gl, ret = _in_proj(h.reshape(b, s, d), g_mix[i], _arrange_w_in(w_in[i]))
        kv_cmp = _compress(kcv, cmp_pos[i], cmp_w1[i], cmp_w2[i])
        a = _nsa_attention(qn, kv_cmp, ks, kv3, gl)
        r = _retention(ret, ret_gn[i])
        h, hn = _out_proj(h.reshape(t, d), a.reshape(t, NSA_Q_COLS), r.reshape(t, RET_COLS), w_out[i], g_ffn[i])
        if i % 2 == 0:
            f = _dense_ffn(hn, ffn_gate[i // 2], ffn_up[i // 2], ffn_down[i // 2])
        else:
            f = _moe_ffn(hn, moe_router[i // 2], moe_gate[i // 2], moe_up[i // 2], moe_down[i // 2])
        h = _ple(h, f, p[i].reshape(t, PLE_DIM), g_ple[i], ple_proj[i], ple_gate[i], g_final, i == depth - 1)
    return h.reshape(b, s, d)
```

```python
import functools
import math

import jax
import jax.numpy as jnp
from jax import lax
from jax.experimental import pallas as pl
from jax.experimental.pallas import tpu as pltpu

F32 = jnp.float32
BF16 = jnp.bfloat16

D_MODEL = 1024
HEAD_DIM = 64
NSA_HEADS = 8
NSA_KV_HEADS = 2
NSA_GROUP = NSA_HEADS // NSA_KV_HEADS
RET_HEADS = 8
CMP_BLOCK = 32
CMP_STRIDE = 16
CMP_HIDDEN = 256
SEL_BLOCK = 64
N_SELECT = 16
N_LOCAL_SEL = 2
WINDOW = 512
Q_BLOCK = 128
RET_CHUNK = 128
D_FF = 3584
N_EXPERTS = 8
TOP_K = 2
PLE_DIM = 256
EPS = 1e-6
NEG_INF = -1e30
BIG = 1e9

NSA_Q_COLS = NSA_HEADS * HEAD_DIM
NSA_KV_COLS = NSA_KV_HEADS * HEAD_DIM
NSA_GATE_COLS = 3 * NSA_HEADS
RET_COLS = RET_HEADS * HEAD_DIM
MIX_WIDTH = NSA_Q_COLS + RET_COLS

V7X_LANES = 128
V7X_VMEM_BYTES = 64 * 1024 * 1024
VMEM_LIMIT = V7X_VMEM_BYTES * 3 // 4

ROW_TILE = 512
ATTN_Q_TILE = 256
SEL_KV_TILE = 256
RET_BATCH = 4
FFN_ROW_TILE = 512
FFN_COL_TILE = 512
MOE_COL_TILE = 896
MOE_ROW_TILE = 2048
MOE_GROUP_ROWS = 256
MOE_TAIL_ROWS = 64
MOE_RANK_CHUNK = 256
MOE_SCATTER_ROWS = 512
MOE_VMEM_LIMIT = V7X_VMEM_BYTES * 7 // 8

POS_SHIFT = 6
POS_SPLIT = 1 << POS_SHIFT
SEL_SHIFT = SEL_BLOCK.bit_length() - 1
MAX_SEL_BLOCKS = V7X_LANES


def _params(sem, vmem=VMEM_LIMIT):
    return pltpu.CompilerParams(dimension_semantics=sem, vmem_limit_bytes=vmem)


def _dot(a, b):
    return jnp.dot(a, b, preferred_element_type=F32)


def _dot_nt(a, b):
    return lax.dot_general(a, b, (((1,), (1,)), ((), ())), preferred_element_type=F32)


def _dot_tn(a, b):
    return lax.dot_general(a, b, (((0,), (0,)), ((), ())), preferred_element_type=F32)


def _rms(x, g):
    return x * lax.rsqrt(jnp.mean(x * x, axis=-1, keepdims=True) + EPS) * g


def _lane_features(shape, first, second):
    lane = lax.broadcasted_iota(jnp.int32, shape, 1)
    return jnp.where(lane == 0, first, jnp.where(lane == 1, second, 0))


def _key_features(pos, width):
    return _lane_features((pos.shape[0], width), pos >> POS_SHIFT, pos & (POS_SPLIT - 1))


def _ones_features(n, width):
    return _lane_features((n, width), 1, 0)


def _int_to_bf16(x):
    return x.astype(F32).astype(BF16)


def _in_proj_kernel(h_ref, g_ref, w_ref, qn_ref, kcv_ref, ks_ref, kv3_ref, gl_ref, ret_ref):
    tm = h_ref.shape[1]
    xn = _rms(h_ref[0], g_ref[...]).astype(BF16)
    pair = 2 * HEAD_DIM
    pos = pl.program_id(1) * tm + lax.broadcasted_iota(jnp.int32, (tm, 1), 0)
    kfeat = _int_to_bf16(_key_features(pos, HEAD_DIM))
    vfeat = _int_to_bf16(_ones_features(tm, HEAD_DIM))
    lane = lax.broadcasted_iota(jnp.int32, (tm, MAX_SEL_BLOCKS), 1)
    block_onehot = jnp.where(lane == (pos >> SEL_SHIFT), 1.0, 0.0).astype(BF16)

    for j in range(NSA_HEADS // 2):
        z = (_dot(xn, w_ref[:, j * pair:(j + 1) * pair]) * HEAD_DIM ** -0.5).astype(BF16)
        for k in range(2):
            slope = 2.0 ** -(2 * j + k + 1)
            qfeat = _lane_features((tm, HEAD_DIM), POS_SPLIT * slope, slope).astype(BF16)
            qn_ref[0, 2 * j + k] = jnp.concatenate([z[:, k * HEAD_DIM:(k + 1) * HEAD_DIM], qfeat], axis=1)
    base = NSA_Q_COLS
    for a in range(2):
        kcv_ref[a, 0] = _dot(xn, w_ref[:, base + a * pair:base + (a + 1) * pair]).astype(BF16)
    base += 2 * pair
    for a in range(4):
        z = _dot(xn, w_ref[:, base + a * pair:base + (a + 1) * pair]).astype(BF16)
        for g in range(NSA_KV_HEADS):
            zg = z[:, g * HEAD_DIM:(g + 1) * HEAD_DIM]
            if a == 0:
                ks_ref[0, g] = jnp.concatenate([zg, kfeat, block_onehot], axis=1)
            else:
                kv3_ref[a - 1, 0, g] = jnp.concatenate([zg, kfeat if a == 2 else vfeat], axis=1)
    base += 4 * pair
    for a in range(4):
        ret_ref[0, :, a * RET_COLS:(a + 1) * RET_COLS] = _dot(
            xn, w_ref[:, base + a * RET_COLS:base + (a + 1) * RET_COLS]).astype(BF16)
    base += 4 * RET_COLS
    gl_ref[0] = _dot(xn, w_ref[:, base:base + NSA_KV_HEADS * V7X_LANES])


def _arrange_w_in(w):
    q_end = NSA_Q_COLS
    kv_end = q_end + 6 * NSA_KV_COLS
    gl_end = kv_end + NSA_GATE_COLS
    gl = w[:, kv_end:gl_end].reshape(D_MODEL, 3, NSA_KV_HEADS, NSA_GROUP)
    gl = jnp.transpose(gl, (0, 2, 1, 3)).reshape(D_MODEL, NSA_KV_HEADS, 3 * NSA_GROUP)
    gl = jnp.pad(gl, ((0, 0), (0, 0), (0, V7X_LANES - 3 * NSA_GROUP)))
    gl = gl.reshape(D_MODEL, NSA_KV_HEADS * V7X_LANES)
    return jnp.concatenate([w[:, :kv_end], w[:, gl_end:], gl], axis=1).astype(BF16)


def _in_proj(h, g, w):
    b, s, d = h.shape
    assert s // SEL_BLOCK <= MAX_SEL_BLOCKS
    tm = min(ROW_TILE, s)
    ncols = w.shape[1]
    aug = 2 * HEAD_DIM
    return pl.pallas_call(
        _in_proj_kernel,
        grid=(b, s // tm),
        in_specs=[
            pl.BlockSpec((1, tm, d), lambda bi, si: (bi, si, 0)),
            pl.BlockSpec((1, d), lambda bi, si: (0, 0)),
            pl.BlockSpec((d, ncols), lambda bi, si: (0, 0)),
        ],
        out_specs=[
            pl.BlockSpec((1, NSA_HEADS, tm, aug), lambda bi, si: (bi, 0, si, 0)),
            pl.BlockSpec((2, 1, tm, NSA_KV_COLS), lambda bi, si: (0, bi, si, 0)),
            pl.BlockSpec((1, NSA_KV_HEADS, tm, aug + MAX_SEL_BLOCKS), lambda bi, si: (bi, 0, si, 0)),
            pl.BlockSpec((3, 1, NSA_KV_HEADS, tm, aug), lambda bi, si: (0, bi, 0, si, 0)),
            pl.BlockSpec((1, tm, NSA_KV_HEADS * V7X_LANES), lambda bi, si: (bi, si, 0)),
            pl.BlockSpec((1, tm, 4 * RET_COLS), lambda bi, si: (bi, si, 0)),
        ],
        out_shape=[
            jax.ShapeDtypeStruct((b, NSA_HEADS, s, aug), BF16),
            jax.ShapeDtypeStruct((2, b, s, NSA_KV_COLS), BF16),
            jax.ShapeDtypeStruct((b, NSA_KV_HEADS, s, aug + MAX_SEL_BLOCKS), BF16),
            jax.ShapeDtypeStruct((3, b, NSA_KV_HEADS, s, aug), BF16),
            jax.ShapeDtypeStruct((b, s, NSA_KV_HEADS * V7X_LANES), F32),
            jax.ShapeDtypeStruct((b, s, 4 * RET_COLS), BF16),
        ],
        compiler_params=_params(("parallel", "parallel")),
        name="in_proj",
    )(h, g.reshape(1, d), w)


def _compress_kernel(x_ref, wbig_ref, pos_ref, w1_ref, w2_ref, o_ref):
    nch = x_ref.shape[2]
    u = _dot(x_ref[0, 0], wbig_ref[0])
    hid0 = _dot(pos_ref[0], w1_ref[0])[0:1]
    w2 = w2_ref[0]
    cmp_end = lax.broadcasted_iota(jnp.int32, (nch, 1), 0) * CMP_STRIDE + (CMP_BLOCK - 1)
    feat = _int_to_bf16(jnp.where(pl.program_id(0) == 0, _key_features(cmp_end, HEAD_DIM),
                                  _ones_features(nch, HEAD_DIM)))
    for g in range(NSA_KV_HEADS):
        c0 = g * 2 * CMP_HIDDEN
        first = u[:, c0:c0 + CMP_HIDDEN]
        second = u[:, c0 + CMP_HIDDEN:c0 + 2 * CMP_HIDDEN]
        hid = first + pltpu.roll(second, nch - 1, 0) + hid0
        out = _dot(jax.nn.gelu(hid).astype(BF16), w2).astype(BF16)
        o_ref[0, 0, g] = jnp.concatenate([out, feat], axis=1)


def _arrange_cmp_w1(w1):
    r = CMP_BLOCK // CMP_STRIDE
    w1r = w1.reshape(2, r, CMP_STRIDE, HEAD_DIM, CMP_HIDDEN)
    eye = jnp.eye(NSA_KV_HEADS, dtype=w1.dtype)
    big = jnp.einsum('krcdh,gf->kcgdfrh', w1r, eye)
    return big.reshape(2, CMP_STRIDE * NSA_KV_COLS, NSA_KV_HEADS * r * CMP_HIDDEN).astype(BF16)


def _compress(kcv, cmp_pos, cmp_w1, cmp_w2):
    _, b, s, _ = kcv.shape
    nch = s // CMP_STRIDE
    x = kcv.reshape(2, b, nch, CMP_STRIDE * NSA_KV_COLS)
    wbig = _arrange_cmp_w1(cmp_w1)
    pos = jnp.broadcast_to(cmp_pos.reshape(2, 1, CMP_BLOCK * HEAD_DIM), (2, 8, CMP_BLOCK * HEAD_DIM)).astype(BF16)
    w1 = cmp_w1.reshape(2, CMP_BLOCK * HEAD_DIM, CMP_HIDDEN).astype(BF16)
    w2 = cmp_w2.astype(BF16)
    kdim = CMP_STRIDE * NSA_KV_COLS
    return pl.pallas_call(
        _compress_kernel,
        grid=(2, b),
        in_specs=[
            pl.BlockSpec((1, 1, nch, kdim), lambda a, bi: (a, bi, 0, 0)),
            pl.BlockSpec((1, kdim, wbig.shape[2]), lambda a, bi: (a, 0, 0)),
            pl.BlockSpec((1, 8, CMP_BLOCK * HEAD_DIM), lambda a, bi: (a, 0, 0)),
            pl.BlockSpec((1, CMP_BLOCK * HEAD_DIM, CMP_HIDDEN), lambda a, bi: (a, 0, 0)),
            pl.BlockSpec((1, CMP_HIDDEN, HEAD_DIM), lambda a, bi: (a, 0, 0)),
        ],
        out_specs=pl.BlockSpec((1, 1, NSA_KV_HEADS, nch, 2 * HEAD_DIM), lambda a, bi: (a, bi, 0, 0, 0)),
        out_shape=jax.ShapeDtypeStruct((2, b, NSA_KV_HEADS, nch, 2 * HEAD_DIM), BF16),
        compiler_params=_params(("parallel", "parallel")),
        name="compress",
    )(x, wbig, pos, w1, w2)


def _normalise_t(acc):
    return acc[:HEAD_DIM] * (1.0 / jnp.maximum(acc[HEAD_DIM:HEAD_DIM + 1], 1e-30))


def _attn_kernel(q_ref, kc_ref, vc_ref, ks_ref, vs_ref, kw_ref, vw_ref, gl_ref, ovt_ref, o_ref, tiles_ref,
                 *, n_pick, tk):
    qt = q_ref.shape[2]
    cols = NSA_GROUP * qt
    t0 = pl.program_id(2) * qt
    q_t = q_ref[0].reshape(cols, 2 * HEAD_DIM).T
    tq = t0 + (lax.broadcasted_iota(jnp.int32, (1, cols), 1) & (qt - 1))

    nc = kc_ref.shape[3]
    cmp_end = lax.broadcasted_iota(jnp.int32, (nc, 1), 0) * CMP_STRIDE + (CMP_BLOCK - 1)
    valid_c = cmp_end <= tq
    logit_c = jnp.where(valid_c, _dot(kc_ref[0, 0, 0], q_t), NEG_INF)
    e_c = jnp.where(valid_c, jnp.exp(logit_c - jnp.max(logit_c, axis=0, keepdims=True)), 0.0)
    p_c = e_c * (1.0 / jnp.maximum(jnp.sum(e_c, axis=0, keepdims=True), 1e-30))
    o_c = _dot(vc_ref[0, 0, 0].T, p_c.astype(BF16))[:HEAD_DIM]

    p_sum = p_c[:, 0:qt]
    for r in range(1, NSA_GROUP):
        p_sum = p_sum + p_c[:, r * qt:(r + 1) * qt]
    p_hi = p_sum.astype(BF16)
    p_lo = (p_sum - p_hi.astype(F32)).astype(BF16)
    imp = _dot(ovt_ref[...], p_hi) + _dot(ovt_ref[...], p_lo)
    shape = (MAX_SEL_BLOCKS, qt)
    blk = lax.broadcasted_iota(jnp.int32, shape, 0)
    back = ((t0 + lax.broadcasted_iota(jnp.int32, shape, 1)) >> SEL_SHIFT) - blk
    forced = (blk == 0) | ((back >= 0) & (back < N_LOCAL_SEL))
    score = jnp.where(forced, BIG, jnp.where(back >= 0, imp, -BIG))

    def pick(_, work):
        m = jnp.max(work, axis=0, keepdims=True)
        first = jnp.min(jnp.where(work == m, blk, MAX_SEL_BLOCKS), axis=0, keepdims=True)
        return jnp.where(blk == first, -jnp.inf, work)

    sel = jnp.where(lax.fori_loop(0, n_pick, pick, score) == -jnp.inf, 1.0, 0.0)
    sel_bias = ((sel - 1.0) * -NEG_INF).astype(BF16)
    q_sel = jnp.concatenate([q_t, jnp.concatenate([sel_bias] * NSA_GROUP, axis=1)], axis=0)

    n_own = max(1, qt // tk)
    n_full = t0 // tk
    blocks_per_tile = tk // SEL_BLOCK
    block_used = jnp.max(sel, axis=1, keepdims=True)
    n_visit = jnp.int32(0)
    for j in range(tiles_ref.shape[0]):
        used = jnp.max(block_used[j * blocks_per_tile:(j + 1) * blocks_per_tile]) > 0.0
        tiles_ref[n_visit] = j
        n_visit = n_visit + (used & (j < n_full)).astype(jnp.int32)

    def sweep(tiles, carry, causal):
        m_i, acc = carry
        starts = [pl.multiple_of(j * tk, tk) for j in tiles]
        scores = []
        for k0 in starts:
            s = _dot(ks_ref[0, 0, pl.ds(k0, tk), :], q_sel)
            if causal:
                s = jnp.where(k0 + lax.broadcasted_iota(jnp.int32, (tk, 1), 0) <= tq, s, NEG_INF)
            scores.append(s)
        m_new = m_i
        for s in scores:
            m_new = jnp.maximum(m_new, jnp.max(s, axis=0, keepdims=True))
        acc = jnp.exp(m_i - m_new) * acc
        for s, k0 in zip(scores, starts):
            acc = acc + _dot(vs_ref[0, 0, 0, pl.ds(k0, tk), :].T, jnp.exp(s - m_new).astype(BF16))
        return m_new, acc

    odd = n_visit & 1
    init = (jnp.full((1, cols), NEG_INF, F32), jnp.zeros((2 * HEAD_DIM, cols), F32))
    carry = lax.fori_loop(0, odd, lambda i, c: sweep([tiles_ref[i]], c, causal=False), init)
    carry = lax.fori_loop(0, n_visit // 2,
                          lambda i, c: sweep([tiles_ref[odd + 2 * i], tiles_ref[odd + 2 * i + 1]], c, causal=False),
                          carry)
    carry = sweep([n_full + j for j in range(n_own)], carry, causal=True)
    o_s = _normalise_t(carry[1])

    band = WINDOW + qt
    w0 = pl.multiple_of(jnp.maximum(t0 - WINDOW, 0), math.gcd(qt, WINDOW))
    kpos = w0 + lax.broadcasted_iota(jnp.int32, (band, 1), 0)
    valid_w = (kpos <= tq) & (kpos > tq - WINDOW)
    logit_w = jnp.where(valid_w, _dot(kw_ref[0, 0, 0, pl.ds(w0, band), :], q_t), NEG_INF)
    e_w = jnp.exp(logit_w - jnp.max(logit_w, axis=0, keepdims=True))
    o_w = _normalise_t(_dot(vw_ref[0, 0, 0, pl.ds(w0, band), :].T, e_w.astype(BF16)))

    gate_t = jax.nn.sigmoid(gl_ref[0]).T

    def branch_gate(branch):
        first = branch * NSA_GROUP
        return jnp.concatenate([gate_t[first + r:first + r + 1] for r in range(NSA_GROUP)], axis=1)

    o_t = branch_gate(0) * o_c + branch_gate(1) * o_s + branch_gate(2) * o_w
    o = o_t.T
    o_ref[0] = jnp.concatenate([o[r * qt:(r + 1) * qt] for r in range(NSA_GROUP)], axis=1).astype(o_ref.dtype)


def _nsa_attention(qn, kv_cmp, ks, kv3, gl):
    b, _, s, _ = qn.shape
    nc = kv_cmp.shape[3]
    nsel = s // SEL_BLOCK
    n_pick = min(N_SELECT, nsel)
    tk = min(SEL_KV_TILE, s)
    aug = 2 * HEAD_DIM
    qt = min(ATTN_Q_TILE, s)
    assert qt % tk == 0 or tk % qt == 0
    cmp_start = jnp.arange(nc) * CMP_STRIDE
    sel_start = jnp.arange(nsel) * SEL_BLOCK
    overlap = (jnp.minimum(cmp_start[:, None] + CMP_BLOCK, sel_start[None, :] + SEL_BLOCK)
               > jnp.maximum(cmp_start[:, None], sel_start[None, :]))
    overlap_t = jnp.pad(overlap.T.astype(BF16), ((0, MAX_SEL_BLOCKS - nsel), (0, 0)))
    kv_spec = lambda a: pl.BlockSpec((1, 1, 1, s, aug), lambda bi, g, i, a=a: (a, bi, g, 0, 0))
    cmp_spec = lambda a: pl.BlockSpec((1, 1, 1, nc, aug), lambda bi, g, i, a=a: (a, bi, g, 0, 0))
    return pl.pallas_call(
        functools.partial(_attn_kernel, n_pick=n_pick, tk=tk),
        grid=(b, NSA_KV_HEADS, s // qt),
        in_specs=[
            pl.BlockSpec((1, NSA_GROUP, qt, aug), lambda bi, g, i: (bi, g, i, 0)),
            cmp_spec(0), cmp_spec(1),
            pl.BlockSpec((1, 1, s, aug + MAX_SEL_BLOCKS), lambda bi, g, i: (bi, g, 0, 0)),
            kv_spec(0), kv_spec(1), kv_spec(2),
            pl.BlockSpec((1, qt, V7X_LANES), lambda bi, g, i: (bi, i, g)),
            pl.BlockSpec((MAX_SEL_BLOCKS, nc), lambda bi, g, i: (0, 0)),
        ],
        out_specs=pl.BlockSpec((1, qt, NSA_GROUP * HEAD_DIM), lambda bi, g, i: (bi, i, g)),
        out_shape=jax.ShapeDtypeStruct((b, s, NSA_Q_COLS), BF16),
        scratch_shapes=[pltpu.SMEM((s // tk,), jnp.int32)],
        compiler_params=_params(("parallel", "parallel", "arbitrary")),
        name="nsa_attention",
    )(qn, kv_cmp, kv_cmp, ks, kv3, kv3, kv3, gl, overlap_t)


def _retention_kernel(q_ref, k_ref, v_ref, g_ref, decay_ref, xi_ref, zeta_ref, gch_ref, same_ref, gn_ref,
                      o_ref, state_ref):
    @pl.when(pl.program_id(1) == 0)
    def _():
        state_ref[...] = jnp.zeros_like(state_ref)

    pair = 2 * HEAD_DIM
    c = q_ref.shape[1]
    scale = jnp.asarray(HEAD_DIM ** -0.5, BF16)
    first = lax.broadcasted_iota(jnp.int32, (c, pair), 1) < HEAD_DIM
    same = same_ref[...]
    mean_w = (same * (1.0 / HEAD_DIM)).astype(BF16)

    def head_mean(x):
        hi = x.astype(BF16)
        lo = (x - hi.astype(F32)).astype(BF16)
        return _dot(hi, mean_w) + _dot(lo, mean_w)

    def stack(x):
        zero = jnp.zeros_like(x)
        return jnp.concatenate([jnp.where(first, x, zero), jnp.where(first, zero, x)], axis=0)

    for i in range(q_ref.shape[0]):
        outs = []
        for p in range(RET_HEADS // 2):
            cols = slice(p * pair, (p + 1) * pair)
            q = q_ref[i, :, cols]
            k = k_ref[i, :, cols] * scale
            v = v_ref[i, :, cols]
            state = state_ref[i, p]
            inner = _dot_nt(q, stack(k)) * decay_ref[p]
            o = _dot(inner.astype(BF16), stack(v)) + _dot(q, state.astype(BF16)) * xi_ref[p]
            kz = (k.astype(F32) * zeta_ref[p]).astype(BF16)
            state_ref[i, p] = gch_ref[p] * state + same * _dot_tn(kz, v)
            centred = o - head_mean(o)
            outs.append(centred * lax.rsqrt(head_mean(jnp.square(centred)) + EPS))
        o = jnp.concatenate(outs, axis=1) * gn_ref[...]
        o_ref[i] = (jax.nn.silu(g_ref[i].astype(F32)) * o).astype(o_ref.dtype)


def _retention(ret, gn_gain):
    b, s, _ = ret.shape
    c = RET_CHUNK
    hh = RET_HEADS
    pair = 2 * HEAD_DIM
    log_gamma = jnp.log1p(-jnp.exp2(-5.0 - jnp.arange(hh, dtype=F32)))
    pos = jnp.arange(c, dtype=F32)
    diff = pos[:, None] - pos[None, :]
    decay = jnp.where(diff >= 0, jnp.exp(jnp.maximum(diff, 0.0)[None] * log_gamma[:, None, None]), 0.0)
    xi = jnp.exp((pos + 1.0)[None] * log_gamma[:, None])[..., None]
    zeta = jnp.exp((c - 1.0 - pos)[None] * log_gamma[:, None])[..., None]
    g_chunk = jnp.exp(c * log_gamma)[:, None, None]
    side_by_side = lambda t: jnp.concatenate([t[0::2], t[1::2]], axis=-1)
    decay = side_by_side(decay)
    xi = side_by_side(jnp.broadcast_to(xi, (hh, c, HEAD_DIM)))
    zeta = side_by_side(jnp.broadcast_to(zeta, (hh, c, HEAD_DIM)))
    head_of = jnp.arange(pair) // HEAD_DIM
    same = (head_of[:, None] == head_of[None, :]).astype(F32)
    g_lane = side_by_side(jnp.broadcast_to(g_chunk, (hh, 1, HEAD_DIM)))
    g_chunk = same[None] * g_lane
    nb = math.gcd(RET_BATCH, b)
    part = lambda a: pl.BlockSpec((nb, c, RET_COLS), lambda bi, n, a=a: (bi, n, a))
    full = lambda shape: pl.BlockSpec(shape, lambda bi, n: (0,) * len(shape))
    return pl.pallas_call(
        _retention_kernel,
        grid=(b // nb, s // c),
        in_specs=[part(0), part(1), part(2), part(3),
                  full((hh // 2, c, 2 * c)), full((hh // 2, c, pair)), full((hh // 2, c, pair)),
                  full((hh // 2, pair, pair)), full((pair, pair)), full((1, RET_COLS))],
        out_specs=pl.BlockSpec((nb, c, RET_COLS), lambda bi, n: (bi, n, 0)),
        out_shape=jax.ShapeDtypeStruct((b, s, RET_COLS), BF16),
        scratch_shapes=[pltpu.VMEM((nb, hh // 2, pair, pair), F32)],
        compiler_params=_params(("parallel", "arbitrary")),
        name="retention",
    )(ret, ret, ret, ret, decay, xi, zeta, g_chunk, same, gn_gain.reshape(1, RET_COLS).astype(F32))


def _out_proj_kernel(h_ref, a_ref, r_ref, w_ref, g_ref, h_out_ref, hn_ref):
    h = h_ref[...] + _dot(a_ref[...], w_ref[:NSA_Q_COLS]) + _dot(r_ref[...], w_ref[NSA_Q_COLS:])
    h_out_ref[...] = h
    hn_ref[...] = _rms(h, g_ref[...]).astype(BF16)


def _out_proj(h, a, r, w, g):
    t, d = h.shape
    tm = min(ROW_TILE, t)
    row = lambda n: pl.BlockSpec((tm, n), lambda i: (i, 0))
    return pl.pallas_call(
        _out_proj_kernel,
        grid=(t // tm,),
        in_specs=[row(d), row(NSA_Q_COLS), row(RET_COLS),
                  pl.BlockSpec((MIX_WIDTH, d), lambda i: (0, 0)),
                  pl.BlockSpec((1, d), lambda i: (0, 0))],
        out_specs=[row(d), row(d)],
        out_shape=[jax.ShapeDtypeStruct((t, d), F32), jax.ShapeDtypeStruct((t, d), BF16)],
        compiler_params=_params(("parallel",)),
        name="out_proj",
    )(h, a, r, w.astype(BF16), g.reshape(1, d))


def _swiglu_chunk(x, wg, wu, wd):
    hid = jax.nn.silu(_dot(x, wg)) * _dot(x, wu)
    return _dot(hid.astype(BF16), wd)


def _ffn_kernel(x_ref, wg_ref, wu_ref, wd_ref, o_ref):
    x = x_ref[...]
    acc = None
    for f in range(D_FF // FFN_COL_TILE):
        cols = slice(f * FFN_COL_TILE, (f + 1) * FFN_COL_TILE)
        part = _swiglu_chunk(x, wg_ref[:, cols], wu_ref[:, cols], wd_ref[cols, :])
        acc = part if acc is None else acc + part
    o_ref[...] = acc


def _dense_ffn(hn, wg, wu, wd):
    t, d = hn.shape
    tm = min(FFN_ROW_TILE, t)
    once = pl.Buffered(1)
    return pl.pallas_call(
        _ffn_kernel,
        grid=(t // tm,),
        in_specs=[pl.BlockSpec((tm, d), lambda i: (i, 0)),
                  pl.BlockSpec((d, D_FF), lambda i: (0, 0), pipeline_mode=once),
                  pl.BlockSpec((d, D_FF), lambda i: (0, 0), pipeline_mode=once),
                  pl.BlockSpec((D_FF, d), lambda i: (0, 0), pipeline_mode=once)],
        out_specs=pl.BlockSpec((tm, d), lambda i: (i, 0)),
        out_shape=jax.ShapeDtypeStruct((t, d), F32),
        compiler_params=_params(("parallel",)),
        name="dense_ffn",
    )(hn, wg.astype(BF16), wu.astype(BF16), wd.astype(BF16))


def _lane_column(table, lane_index):
    lane = lax.broadcasted_iota(jnp.int32, table.shape, 1)
    col = jnp.sum(jnp.where(lane == lane_index, table, 0.0), axis=-1, keepdims=True)
    return jnp.broadcast_to(col, table.shape)


def _moe_kernel(x_ref, router_ref, wg_ref, wu_ref, wd_ref, o_ref,
                xs_ref, y_ref, slot_ref, gate_ref, slot_row_ref, slot_e_ref, gate_e_ref, count_ref):
    e = pl.program_id(1)
    f = pl.program_id(2)
    tm, d = x_ref.shape
    sub = MOE_GROUP_ROWS

    @pl.when((e == 0) & (f == 0))
    def _route():
        o_ref[...] = jnp.zeros_like(o_ref)
        logits = _dot(x_ref[...], router_ref[...])
        lane = lax.broadcasted_iota(jnp.int32, logits.shape, 1)
        logits = jnp.where(lane < N_EXPERTS, logits, -jnp.inf)
        v1 = jnp.max(logits, axis=-1, keepdims=True)
        i1 = jnp.min(jnp.where(logits == v1, lane, V7X_LANES), axis=-1, keepdims=True)
        rest = jnp.where(lane == i1, -jnp.inf, logits)
        v2 = jnp.max(rest, axis=-1, keepdims=True)
        i2 = jnp.min(jnp.where(rest == v2, lane, V7X_LANES), axis=-1, keepdims=True)
        e2 = jnp.exp(v2 - v1)
        inv = 1.0 / (1.0 + e2)
        gate_ref[...] = jnp.where(lane == i1, inv, 0.0) + jnp.where(lane == i2, e2 * inv, 0.0)
        routed = jnp.where((lane == i1) | (lane == i2), 1.0, 0.0)
        c = MOE_RANK_CHUNK
        before = (lax.broadcasted_iota(jnp.int32, (c, c), 1) < lax.broadcasted_iota(jnp.int32, (c, c), 0))
        before = jnp.where(before, 1.0, 0.0).astype(BF16)
        offset = jnp.zeros((1, V7X_LANES), F32)
        for j in range(tm // c):
            part = routed[j * c:(j + 1) * c]
            rank = _dot(before, part.astype(BF16)) + offset
            slot_ref[j * c:(j + 1) * c, :] = jnp.where(part > 0.0, rank, -1.0)
            offset = offset + jnp.sum(part, axis=0, keepdims=True)
        for ee in range(N_EXPERTS):
            count_ref[ee] = offset[0, ee].astype(jnp.int32)
        slot_row_ref[...] = slot_ref[...].T[:N_EXPERTS]

    n_groups = (count_ref[e] + sub - 1) // sub

    @pl.when(f == 0)
    def _gather():
        slot_e_ref[...] = _lane_column(slot_ref[...], e)
        gate_e_ref[...] = _lane_column(gate_ref[...], e)
        slot_row = slot_row_ref[pl.ds(e, 1), :]

        def body(s, _):
            r0 = pl.multiple_of(s * sub, sub)
            want = (r0 + lax.broadcasted_iota(jnp.int32, (sub, 1), 0)).astype(F32)
            onehot = jnp.where(slot_row == want, 1.0, 0.0).astype(BF16)
            xs_ref[pl.ds(r0, sub), :] = _dot(onehot, x_ref[...]).astype(BF16)
            y_ref[pl.ds(r0, sub), :] = jnp.zeros((sub, d), F32)
            return 0

        lax.fori_loop(0, n_groups, body, 0)

    def expert(i, _, base, n):
        r0 = pl.multiple_of(base + i * n, n)
        y_ref[pl.ds(r0, n), :] += _swiglu_chunk(xs_ref[pl.ds(r0, n), :], wg_ref[0, 0], wu_ref[0, 0], wd_ref[0])
        return 0

    tail = MOE_TAIL_ROWS
    n_filled = count_ref[e] // sub
    rest = count_ref[e] - n_filled * sub
    short = rest <= tail
    n_whole = jnp.where(short, n_filled, n_filled + 1)
    n_tail = jnp.where(short & (rest > 0), 1, 0)
    lax.fori_loop(0, n_whole, functools.partial(expert, base=0, n=sub), 0)
    lax.fori_loop(0, n_tail, functools.partial(expert, base=n_filled * sub, n=tail), 0)

    @pl.when(f == pl.num_programs(2) - 1)
    def _scatter():
        tc = MOE_SCATTER_ROWS

        def body(s, _):
            r0 = pl.multiple_of(s * sub, sub)
            y = y_ref[pl.ds(r0, sub), :].astype(BF16)
            want = (r0 + lax.broadcasted_iota(jnp.int32, (1, sub), 1)).astype(F32)
            for j in range(tm // tc):
                rows = slice(j * tc, (j + 1) * tc)
                slot = jnp.concatenate([slot_e_ref[rows, :]] * (sub // V7X_LANES), axis=1)
                onehot = jnp.where(slot == want, 1.0, 0.0).astype(BF16)
                weight = jnp.concatenate([gate_e_ref[rows, :]] * (d // V7X_LANES), axis=1)
                o_ref[rows, :] += weight * _dot(onehot, y)
            return 0

        lax.fori_loop(0, n_groups, body, 0)


def _moe_ffn(hn, router, wg, wu, wd):
    t, d = hn.shape
    tm = min(MOE_ROW_TILE, t)
    tf = MOE_COL_TILE
    nf = D_FF // tf
    router = jnp.pad(router, ((0, 0), (0, V7X_LANES - N_EXPERTS))).astype(BF16)
    chunked = lambda w: jnp.transpose(w.astype(BF16).reshape(N_EXPERTS, d, nf, tf), (0, 2, 1, 3))
    once = pl.Buffered(1)
    return pl.pallas_call(
        _moe_kernel,
        grid=(t // tm, N_EXPERTS, nf),
        in_specs=[pl.BlockSpec((tm, d), lambda i, e, f: (i, 0), pipeline_mode=once),
                  pl.BlockSpec((d, V7X_LANES), lambda i, e, f: (0, 0), pipeline_mode=once),
                  pl.BlockSpec((1, 1, d, tf), lambda i, e, f: (e, f, 0, 0)),
                  pl.BlockSpec((1, 1, d, tf), lambda i, e, f: (e, f, 0, 0)),
                  pl.BlockSpec((1, tf, d), lambda i, e, f: (e, f, 0))],
        out_specs=pl.BlockSpec((tm, d), lambda i, e, f: (i, 0)),
        out_shape=jax.ShapeDtypeStruct((t, d), F32),
        scratch_shapes=[pltpu.VMEM((tm, d), BF16), pltpu.VMEM((tm, d), F32),
                        pltpu.VMEM((tm, V7X_LANES), F32), pltpu.VMEM((tm, V7X_LANES), F32),
                        pltpu.VMEM((N_EXPERTS, tm), F32),
                        pltpu.VMEM((tm, V7X_LANES), F32), pltpu.VMEM((tm, V7X_LANES), F32),
                        pltpu.SMEM((N_EXPERTS,), jnp.int32)],
        compiler_params=_params(("parallel", "arbitrary", "arbitrary"), vmem=MOE_VMEM_LIMIT),
        name="moe_ffn",
    )(hn, router, chunked(wg), chunked(wu), wd.astype(BF16))


def _ple_kernel(h_ref, f_ref, p_ref, g_ref, proj_ref, gate_ref, gf_ref, o_ref, *, final_norm):
    h = h_ref[...] + f_ref[...]
    emb = _dot(p_ref[...].astype(BF16), proj_ref[...])
    sig = jax.nn.sigmoid(_dot(_rms(h, g_ref[...]).astype(BF16), gate_ref[...]))
    h = h + emb * sig
    if final_norm:
        h = _rms(h, gf_ref[...])
    o_ref[...] = h


def _ple(h, f, p, g, proj, gate, g_final, final_norm):
    t, d = h.shape
    tm = min(ROW_TILE, t)
    vec = pl.BlockSpec((1, d), lambda i: (0, 0))
    return pl.pallas_call(
        functools.partial(_ple_kernel, final_norm=final_norm),
        grid=(t // tm,),
        in_specs=[pl.BlockSpec((tm, d), lambda i: (i, 0)),
                  pl.BlockSpec((tm, d), lambda i: (i, 0)),
                  pl.BlockSpec((tm, PLE_DIM), lambda i: (i, 0)),
                  vec,
                  pl.BlockSpec((PLE_DIM, d), lambda i: (0, 0)),
                  pl.BlockSpec((d, d), lambda i: (0, 0)),
                  vec],
        out_specs=pl.BlockSpec((tm, d), lambda i: (i, 0)),
        out_shape=jax.ShapeDtypeStruct((t, d), F32),
        compiler_params=_params(("parallel",)),
        name="ple",
    )(h, f, p, g.reshape(1, d), proj.astype(BF16), gate.astype(BF16), g_final.reshape(1, d))


def kernel(x, p, w_in, w_out, g_mix, g_ffn, g_ple, g_final, cmp_pos, cmp_w1, cmp_w2, ret_gn,
           ffn_gate, ffn_up, ffn_down, moe_router, moe_gate, moe_up, moe_down, ple_proj, ple_gate):
    b, s, d = x.shape
    depth = w_in.shape[0]
    t = b * s
    h = x
    for i in range(depth):
        qn, kcv, ks, kv3, gl, ret = _in_proj(h.reshape(b, s, d), g_mix[i], _arrange_w_in(w_in[i]))
        kv_cmp = _compress(kcv, cmp_pos[i], cmp_w1[i], cmp_w2[i])
        a = _nsa_attention(qn, kv_cmp, ks, kv3, gl)
        r = _retention(ret, ret_gn[i])
        h, hn = _out_proj(h.reshape(t, d), a.reshape(t, NSA_Q_COLS), r.reshape(t, RET_COLS), w_out[i], g_ffn[i])
        if i % 2 == 0:
            f = _dense_ffn(hn, ffn_gate[i // 2], ffn_up[i // 2], ffn_down[i // 2])
        else:
            f = _moe_ffn(hn, moe_router[i // 2], moe_gate[i // 2], moe_up[i // 2], moe_down[i // 2])
        h = _ple(h, f, p[i].reshape(t, PLE_DIM), g_ple[i], ple_proj[i], ple_gate[i], g_final, i == depth - 1)
    return h.reshape(b, s, d)
```

```python
import functools
import math

import jax
import jax.numpy as jnp
from jax import lax
from jax.experimental import pallas as pl
from jax.experimental.pallas import tpu as pltpu

F32 = jnp.float32
BF16 = jnp.bfloat16

D_MODEL = 1024
HEAD_DIM = 64
NSA_HEADS = 8
NSA_KV_HEADS = 2
NSA_GROUP = NSA_HEADS // NSA_KV_HEADS
RET_HEADS = 8
CMP_BLOCK = 32
CMP_STRIDE = 16
CMP_HIDDEN = 256
SEL_BLOCK = 64
N_SELECT = 16
N_LOCAL_SEL = 2
WINDOW = 512
RET_CHUNK = 128
D_FF = 3584
N_EXPERTS = 8
PLE_DIM = 256
EPS = 1e-6
NEG_INF = -1e30
BIG = 1e9

NSA_Q_COLS = NSA_HEADS * HEAD_DIM
NSA_KV_COLS = NSA_KV_HEADS * HEAD_DIM
NSA_GATE_COLS = 3 * NSA_HEADS
RET_COLS = RET_HEADS * HEAD_DIM
MIX_WIDTH = NSA_Q_COLS + RET_COLS

V7X_LANES = 128
V7X_SUBLANES = 8
V7X_VMEM_BYTES = 64 * 1024 * 1024
VMEM_LIMIT = V7X_VMEM_BYTES * 3 // 4

ROW_TILE = 512
ATTN_Q_TILE = 256
SEL_KV_TILE = 256
RET_BATCH = 4
FFN_ROW_TILE = 512
FFN_COL_TILE = 512
MOE_COL_TILE = 896
MOE_ROW_TILE = 2048
MOE_GROUP_ROWS = 256
MOE_TAIL_ROWS = 64
MOE_RANK_CHUNK = 256
MOE_SCATTER_ROWS = 512
MOE_VMEM_LIMIT = V7X_VMEM_BYTES * 7 // 8

POS_SHIFT = 6
POS_SPLIT = 1 << POS_SHIFT
SEL_SHIFT = SEL_BLOCK.bit_length() - 1
MAX_SEL_BLOCKS = V7X_LANES


def _params(sem, vmem=VMEM_LIMIT):
    return pltpu.CompilerParams(dimension_semantics=sem, vmem_limit_bytes=vmem)


def _dot(a, b):
    return jnp.dot(a, b, preferred_element_type=F32)


def _dot_nt(a, b):
    return lax.dot_general(a, b, (((1,), (1,)), ((), ())), preferred_element_type=F32)


def _dot_tn(a, b):
    return lax.dot_general(a, b, (((0,), (0,)), ((), ())), preferred_element_type=F32)


def _rms(x, g):
    return x * lax.rsqrt(jnp.mean(x * x, axis=-1, keepdims=True) + EPS) * g


def _lane_features(shape, first, second):
    lane = lax.broadcasted_iota(jnp.int32, shape, 1)
    return jnp.where(lane == 0, first, jnp.where(lane == 1, second, 0))


def _key_features(pos, width):
    return _lane_features((pos.shape[0], width), pos >> POS_SHIFT, pos & (POS_SPLIT - 1))


def _ones_features(n, width):
    return _lane_features((n, width), 1, 0)


def _int_to_bf16(x):
    return x.astype(F32).astype(BF16)


def _in_proj_kernel(h_ref, g_ref, w_ref, qn_ref, kcv_ref, ks_ref, kv3_ref, gl_ref, ret_ref):
    tm = h_ref.shape[1]
    xn = _rms(h_ref[0], g_ref[...]).astype(BF16)
    pair = 2 * HEAD_DIM
    pos = pl.program_id(1) * tm + lax.broadcasted_iota(jnp.int32, (tm, 1), 0)
    kfeat = _int_to_bf16(_key_features(pos, HEAD_DIM))
    vfeat = _int_to_bf16(_ones_features(tm, HEAD_DIM))
    lane = lax.broadcasted_iota(jnp.int32, (tm, MAX_SEL_BLOCKS), 1)
    block_onehot = jnp.where(lane == (pos >> SEL_SHIFT), 1.0, 0.0).astype(BF16)

    for j in range(NSA_HEADS // 2):
        z = (_dot(xn, w_ref[:, j * pair:(j + 1) * pair]) * HEAD_DIM ** -0.5).astype(BF16)
        for k in range(2):
            slope = 2.0 ** -(2 * j + k + 1)
            qfeat = _lane_features((tm, HEAD_DIM), POS_SPLIT * slope, slope).astype(BF16)
            qn_ref[0, 2 * j + k] = jnp.concatenate([z[:, k * HEAD_DIM:(k + 1) * HEAD_DIM], qfeat], axis=1)
    base = NSA_Q_COLS
    for a in range(2):
        kcv_ref[a, 0] = _dot(xn, w_ref[:, base + a * pair:base + (a + 1) * pair]).astype(BF16)
    base += 2 * pair
    for a in range(4):
        z = _dot(xn, w_ref[:, base + a * pair:base + (a + 1) * pair]).astype(BF16)
        for g in range(NSA_KV_HEADS):
            zg = z[:, g * HEAD_DIM:(g + 1) * HEAD_DIM]
            if a == 0:
                ks_ref[0, g] = jnp.concatenate([zg, kfeat, block_onehot], axis=1)
            else:
                kv3_ref[a - 1, 0, g] = jnp.concatenate([zg, kfeat if a == 2 else vfeat], axis=1)
    base += 4 * pair
    for a in range(4):
        ret_ref[0, :, a * RET_COLS:(a + 1) * RET_COLS] = _dot(
            xn, w_ref[:, base + a * RET_COLS:base + (a + 1) * RET_COLS]).astype(BF16)
    base += 4 * RET_COLS
    gl_ref[0] = _dot(xn, w_ref[:, base:base + NSA_KV_HEADS * V7X_LANES])


def _arrange_w_in(w):
    q_end = NSA_Q_COLS
    kv_end = q_end + 6 * NSA_KV_COLS
    gl_end = kv_end + NSA_GATE_COLS
    gl = w[:, kv_end:gl_end].reshape(D_MODEL, 3, NSA_KV_HEADS, NSA_GROUP)
    gl = jnp.transpose(gl, (0, 2, 1, 3)).reshape(D_MODEL, NSA_KV_HEADS, 3 * NSA_GROUP)
    gl = jnp.pad(gl, ((0, 0), (0, 0), (0, V7X_LANES - 3 * NSA_GROUP)))
    gl = gl.reshape(D_MODEL, NSA_KV_HEADS * V7X_LANES)
    return jnp.concatenate([w[:, :kv_end], w[:, gl_end:], gl], axis=1).astype(BF16)


def _in_proj(h, g, w):
    b, s, d = h.shape
    assert s // SEL_BLOCK <= MAX_SEL_BLOCKS
    tm = min(ROW_TILE, s)
    ncols = w.shape[1]
    aug = 2 * HEAD_DIM
    return pl.pallas_call(
        _in_proj_kernel,
        grid=(b, s // tm),
        in_specs=[
            pl.BlockSpec((1, tm, d), lambda bi, si: (bi, si, 0)),
            pl.BlockSpec((1, d), lambda bi, si: (0, 0)),
            pl.BlockSpec((d, ncols), lambda bi, si: (0, 0)),
        ],
        out_specs=[
            pl.BlockSpec((1, NSA_HEADS, tm, aug), lambda bi, si: (bi, 0, si, 0)),
            pl.BlockSpec((2, 1, tm, NSA_KV_COLS), lambda bi, si: (0, bi, si, 0)),
            pl.BlockSpec((1, NSA_KV_HEADS, tm, aug + MAX_SEL_BLOCKS), lambda bi, si: (bi, 0, si, 0)),
            pl.BlockSpec((3, 1, NSA_KV_HEADS, tm, aug), lambda bi, si: (0, bi, 0, si, 0)),
            pl.BlockSpec((1, tm, NSA_KV_HEADS * V7X_LANES), lambda bi, si: (bi, si, 0)),
            pl.BlockSpec((1, tm, 4 * RET_COLS), lambda bi, si: (bi, si, 0)),
        ],
        out_shape=[
            jax.ShapeDtypeStruct((b, NSA_HEADS, s, aug), BF16),
            jax.ShapeDtypeStruct((2, b, s, NSA_KV_COLS), BF16),
            jax.ShapeDtypeStruct((b, NSA_KV_HEADS, s, aug + MAX_SEL_BLOCKS), BF16),
            jax.ShapeDtypeStruct((3, b, NSA_KV_HEADS, s, aug), BF16),
            jax.ShapeDtypeStruct((b, s, NSA_KV_HEADS * V7X_LANES), F32),
            jax.ShapeDtypeStruct((b, s, 4 * RET_COLS), BF16),
        ],
        compiler_params=_params(("parallel", "parallel")),
        name="in_proj",
    )(h, g.reshape(1, d), w)


def _compress_kernel(x_ref, wbig_ref, pos_ref, w1_ref, w2_ref, o_ref):
    nch = x_ref.shape[2]
    u = _dot(x_ref[0, 0], wbig_ref[0])
    hid0 = _dot(pos_ref[0], w1_ref[0])[0:1]
    w2 = w2_ref[0]
    cmp_end = lax.broadcasted_iota(jnp.int32, (nch, 1), 0) * CMP_STRIDE + (CMP_BLOCK - 1)
    feat = _int_to_bf16(jnp.where(pl.program_id(0) == 0, _key_features(cmp_end, HEAD_DIM),
                                  _ones_features(nch, HEAD_DIM)))
    for g in range(NSA_KV_HEADS):
        c0 = g * 2 * CMP_HIDDEN
        first = u[:, c0:c0 + CMP_HIDDEN]
        second = u[:, c0 + CMP_HIDDEN:c0 + 2 * CMP_HIDDEN]
        hid = first + pltpu.roll(second, nch - 1, 0) + hid0
        out = _dot(jax.nn.gelu(hid).astype(BF16), w2).astype(BF16)
        o_ref[0, 0, g] = jnp.concatenate([out, feat], axis=1)


def _arrange_cmp_w1(w1):
    r = CMP_BLOCK // CMP_STRIDE
    w1r = w1.reshape(2, r, CMP_STRIDE, HEAD_DIM, CMP_HIDDEN)
    eye = jnp.eye(NSA_KV_HEADS, dtype=w1.dtype)
    big = jnp.einsum('krcdh,gf->kcgdfrh', w1r, eye)
    return big.reshape(2, CMP_STRIDE * NSA_KV_COLS, NSA_KV_HEADS * r * CMP_HIDDEN).astype(BF16)


def _compress(kcv, cmp_pos, cmp_w1, cmp_w2):
    _, b, s, _ = kcv.shape
    nch = s // CMP_STRIDE
    x = kcv.reshape(2, b, nch, CMP_STRIDE * NSA_KV_COLS)
    wbig = _arrange_cmp_w1(cmp_w1)
    pos = jnp.broadcast_to(cmp_pos.reshape(2, 1, CMP_BLOCK * HEAD_DIM),
                           (2, V7X_SUBLANES, CMP_BLOCK * HEAD_DIM)).astype(BF16)
    w1 = cmp_w1.reshape(2, CMP_BLOCK * HEAD_DIM, CMP_HIDDEN).astype(BF16)
    w2 = cmp_w2.astype(BF16)
    kdim = CMP_STRIDE * NSA_KV_COLS
    return pl.pallas_call(
        _compress_kernel,
        grid=(2, b),
        in_specs=[
            pl.BlockSpec((1, 1, nch, kdim), lambda a, bi: (a, bi, 0, 0)),
            pl.BlockSpec((1, kdim, wbig.shape[2]), lambda a, bi: (a, 0, 0)),
            pl.BlockSpec((1, V7X_SUBLANES, CMP_BLOCK * HEAD_DIM), lambda a, bi: (a, 0, 0)),
            pl.BlockSpec((1, CMP_BLOCK * HEAD_DIM, CMP_HIDDEN), lambda a, bi: (a, 0, 0)),
            pl.BlockSpec((1, CMP_HIDDEN, HEAD_DIM), lambda a, bi: (a, 0, 0)),
        ],
        out_specs=pl.BlockSpec((1, 1, NSA_KV_HEADS, nch, 2 * HEAD_DIM), lambda a, bi: (a, bi, 0, 0, 0)),
        out_shape=jax.ShapeDtypeStruct((2, b, NSA_KV_HEADS, nch, 2 * HEAD_DIM), BF16),
        compiler_params=_params(("parallel", "parallel")),
        name="compress",
    )(x, wbig, pos, w1, w2)


def _normalise_t(acc):
    return acc[:HEAD_DIM] * (1.0 / jnp.maximum(acc[HEAD_DIM:HEAD_DIM + 1], 1e-30))


def _attn_kernel(q_ref, kc_ref, vc_ref, ks_ref, vs_ref, kw_ref, vw_ref, gl_ref, ovt_ref, o_ref, tiles_ref,
                 *, n_pick, tk):
    qt = q_ref.shape[2]
    cols = NSA_GROUP * qt
    t0 = pl.program_id(2) * qt
    q_t = q_ref[0].reshape(cols, 2 * HEAD_DIM).T
    tq = t0 + lax.broadcasted_iota(jnp.int32, (1, qt), 1)

    def per_head(valid, masked):
        bias = jnp.where(valid, 0.0, masked)
        return jnp.concatenate([bias] * NSA_GROUP, axis=1)

    nc = kc_ref.shape[3]
    cmp_end = lax.broadcasted_iota(jnp.int32, (nc, 1), 0) * CMP_STRIDE + (CMP_BLOCK - 1)
    logit_c = _dot(kc_ref[0, 0, 0], q_t) + per_head(cmp_end <= tq, -jnp.inf)
    e_c = jnp.exp(logit_c - jnp.maximum(jnp.max(logit_c, axis=0, keepdims=True), NEG_INF))
    p_c = e_c * (1.0 / jnp.maximum(jnp.sum(e_c, axis=0, keepdims=True), 1e-30))
    o_c = _dot(vc_ref[0, 0, 0].T, p_c.astype(BF16))[:HEAD_DIM]

    p_sum = p_c[:, 0:qt]
    for r in range(1, NSA_GROUP):
        p_sum = p_sum + p_c[:, r * qt:(r + 1) * qt]
    p_hi = p_sum.astype(BF16)
    p_lo = (p_sum - p_hi.astype(F32)).astype(BF16)
    imp = _dot(ovt_ref[...], p_hi) + _dot(ovt_ref[...], p_lo)
    shape = (MAX_SEL_BLOCKS, qt)
    blk = lax.broadcasted_iota(jnp.int32, shape, 0)
    back = ((t0 + lax.broadcasted_iota(jnp.int32, shape, 1)) >> SEL_SHIFT) - blk
    forced = (blk == 0) | ((back >= 0) & (back < N_LOCAL_SEL))
    score = jnp.where(forced, BIG, jnp.where(back >= 0, imp, -BIG))

    def pick(_, work):
        m = jnp.max(work, axis=0, keepdims=True)
        first = jnp.min(jnp.where(work == m, blk, MAX_SEL_BLOCKS), axis=0, keepdims=True)
        return jnp.where(blk == first, -jnp.inf, work)

    sel = jnp.where(lax.fori_loop(0, n_pick, pick, score) == -jnp.inf, 1.0, 0.0)
    sel_bias = ((sel - 1.0) * -NEG_INF).astype(BF16)
    q_sel = jnp.concatenate([q_t, jnp.concatenate([sel_bias] * NSA_GROUP, axis=1)], axis=0)

    n_own = max(1, qt // tk)
    n_full = t0 // tk
    blocks_per_tile = tk // SEL_BLOCK
    block_used = jnp.max(sel, axis=1, keepdims=True)
    n_visit = jnp.int32(0)
    for j in range(tiles_ref.shape[0]):
        used = jnp.max(block_used[j * blocks_per_tile:(j + 1) * blocks_per_tile]) > 0.0
        tiles_ref[n_visit] = j
        n_visit = n_visit + (used & (j < n_full)).astype(jnp.int32)

    def sweep(tiles, carry, causal):
        m_i, acc = carry
        starts = [pl.multiple_of(j * tk, tk) for j in tiles]
        scores = []
        for k0 in starts:
            s = _dot(ks_ref[0, 0, pl.ds(k0, tk), :], q_sel)
            if causal:
                s = s + per_head(k0 + lax.broadcasted_iota(jnp.int32, (tk, 1), 0) <= tq, NEG_INF)
            scores.append(s)
        m_new = m_i
        for s in scores:
            m_new = jnp.maximum(m_new, jnp.max(s, axis=0, keepdims=True))
        acc = jnp.exp(m_i - m_new) * acc
        for s, k0 in zip(scores, starts):
            acc = acc + _dot(vs_ref[0, 0, 0, pl.ds(k0, tk), :].T, jnp.exp(s - m_new).astype(BF16))
        return m_new, acc

    odd = n_visit & 1
    init = (jnp.full((1, cols), NEG_INF, F32), jnp.zeros((2 * HEAD_DIM, cols), F32))
    carry = lax.fori_loop(0, odd, lambda i, c: sweep([tiles_ref[i]], c, causal=False), init)
    carry = lax.fori_loop(0, n_visit // 2,
                          lambda i, c: sweep([tiles_ref[odd + 2 * i], tiles_ref[odd + 2 * i + 1]], c, causal=False),
                          carry)
    carry = sweep([n_full + j for j in range(n_own)], carry, causal=True)
    o_s = _normalise_t(carry[1])

    band = WINDOW + qt
    w0 = pl.multiple_of(jnp.maximum(t0 - WINDOW, 0), math.gcd(qt, WINDOW))
    kpos = w0 + lax.broadcasted_iota(jnp.int32, (band, 1), 0)
    valid_w = (kpos <= tq) & (kpos > tq - WINDOW)
    logit_w = _dot(kw_ref[0, 0, 0, pl.ds(w0, band), :], q_t) + per_head(valid_w, NEG_INF)
    e_w = jnp.exp(logit_w - jnp.max(logit_w, axis=0, keepdims=True))
    o_w = _normalise_t(_dot(vw_ref[0, 0, 0, pl.ds(w0, band), :].T, e_w.astype(BF16)))

    gate_t = jax.nn.sigmoid(gl_ref[0]).T

    def branch_gate(branch):
        first = branch * NSA_GROUP
        return jnp.concatenate([gate_t[first + r:first + r + 1] for r in range(NSA_GROUP)], axis=1)

    o_t = branch_gate(0) * o_c + branch_gate(1) * o_s + branch_gate(2) * o_w
    o = o_t.T
    o_ref[0] = jnp.concatenate([o[r * qt:(r + 1) * qt] for r in range(NSA_GROUP)], axis=1).astype(o_ref.dtype)


def _nsa_attention(qn, kv_cmp, ks, kv3, gl):
    b, _, s, _ = qn.shape
    nc = kv_cmp.shape[3]
    nsel = s // SEL_BLOCK
    n_pick = min(N_SELECT, nsel)
    tk = min(SEL_KV_TILE, s)
    aug = 2 * HEAD_DIM
    qt = min(ATTN_Q_TILE, s)
    assert qt % tk == 0 or tk % qt == 0
    cmp_start = jnp.arange(nc) * CMP_STRIDE
    sel_start = jnp.arange(nsel) * SEL_BLOCK
    overlap = (jnp.minimum(cmp_start[:, None] + CMP_BLOCK, sel_start[None, :] + SEL_BLOCK)
               > jnp.maximum(cmp_start[:, None], sel_start[None, :]))
    overlap_t = jnp.pad(overlap.T.astype(BF16), ((0, MAX_SEL_BLOCKS - nsel), (0, 0)))
    kv_spec = lambda a: pl.BlockSpec((1, 1, 1, s, aug), lambda bi, g, i, a=a: (a, bi, g, 0, 0))
    cmp_spec = lambda a: pl.BlockSpec((1, 1, 1, nc, aug), lambda bi, g, i, a=a: (a, bi, g, 0, 0))
    return pl.pallas_call(
        functools.partial(_attn_kernel, n_pick=n_pick, tk=tk),
        grid=(b, NSA_KV_HEADS, s // qt),
        in_specs=[
            pl.BlockSpec((1, NSA_GROUP, qt, aug), lambda bi, g, i: (bi, g, i, 0)),
            cmp_spec(0), cmp_spec(1),
            pl.BlockSpec((1, 1, s, aug + MAX_SEL_BLOCKS), lambda bi, g, i: (bi, g, 0, 0)),
            kv_spec(0), kv_spec(1), kv_spec(2),
            pl.BlockSpec((1, qt, V7X_LANES), lambda bi, g, i: (bi, i, g)),
            pl.BlockSpec((MAX_SEL_BLOCKS, nc), lambda bi, g, i: (0, 0)),
        ],
        out_specs=pl.BlockSpec((1, qt, NSA_GROUP * HEAD_DIM), lambda bi, g, i: (bi, i, g)),
        out_shape=jax.ShapeDtypeStruct((b, s, NSA_Q_COLS), BF16),
        scratch_shapes=[pltpu.SMEM((s // tk,), jnp.int32)],
        compiler_params=_params(("parallel", "parallel", "arbitrary")),
        name="nsa_attention",
    )(qn, kv_cmp, kv_cmp, ks, kv3, kv3, kv3, gl, overlap_t)


def _retention_kernel(q_ref, k_ref, v_ref, g_ref, decay_ref, xi_ref, zeta_ref, gch_ref, same_ref, gn_ref,
                      o_ref, state_ref):
    @pl.when(pl.program_id(1) == 0)
    def _():
        state_ref[...] = jnp.zeros_like(state_ref)

    pair = 2 * HEAD_DIM
    c = q_ref.shape[1]
    scale = jnp.asarray(HEAD_DIM ** -0.5, BF16)
    first = lax.broadcasted_iota(jnp.int32, (c, pair), 1) < HEAD_DIM
    same = same_ref[...]
    mean_w = (same * (1.0 / HEAD_DIM)).astype(BF16)

    def head_mean(x):
        hi = x.astype(BF16)
        lo = (x - hi.astype(F32)).astype(BF16)
        return _dot(hi, mean_w) + _dot(lo, mean_w)

    def stack(x):
        zero = jnp.zeros_like(x)
        return jnp.concatenate([jnp.where(first, x, zero), jnp.where(first, zero, x)], axis=0)

    for i in range(q_ref.shape[0]):
        outs = []
        for p in range(RET_HEADS // 2):
            cols = slice(p * pair, (p + 1) * pair)
            q = q_ref[i, :, cols]
            k = k_ref[i, :, cols] * scale
            v = v_ref[i, :, cols]
            state = state_ref[i, p]
            inner = _dot_nt(q, stack(k)) * decay_ref[p]
            o = _dot(inner.astype(BF16), stack(v)) + _dot(q, state.astype(BF16)) * xi_ref[p]
            kz = (k.astype(F32) * zeta_ref[p]).astype(BF16)
            state_ref[i, p] = gch_ref[p] * state + same * _dot_tn(kz, v)
            centred = o - head_mean(o)
            outs.append(centred * lax.rsqrt(head_mean(jnp.square(centred)) + EPS))
        o = jnp.concatenate(outs, axis=1) * gn_ref[...]
        o_ref[i] = (jax.nn.silu(g_ref[i].astype(F32)) * o).astype(o_ref.dtype)


def _retention(ret, gn_gain):
    b, s, _ = ret.shape
    c = RET_CHUNK
    hh = RET_HEADS
    pair = 2 * HEAD_DIM
    log_gamma = jnp.log1p(-jnp.exp2(-5.0 - jnp.arange(hh, dtype=F32)))
    pos = jnp.arange(c, dtype=F32)
    diff = pos[:, None] - pos[None, :]
    decay = jnp.where(diff >= 0, jnp.exp(jnp.maximum(diff, 0.0)[None] * log_gamma[:, None, None]), 0.0)
    xi = jnp.exp((pos + 1.0)[None] * log_gamma[:, None])[..., None]
    zeta = jnp.exp((c - 1.0 - pos)[None] * log_gamma[:, None])[..., None]
    g_chunk = jnp.exp(c * log_gamma)[:, None, None]
    side_by_side = lambda t: jnp.concatenate([t[0::2], t[1::2]], axis=-1)
    decay = side_by_side(decay)
    xi = side_by_side(jnp.broadcast_to(xi, (hh, c, HEAD_DIM)))
    zeta = side_by_side(jnp.broadcast_to(zeta, (hh, c, HEAD_DIM)))
    head_of = jnp.arange(pair) // HEAD_DIM
    same = (head_of[:, None] == head_of[None, :]).astype(F32)
    g_lane = side_by_side(jnp.broadcast_to(g_chunk, (hh, 1, HEAD_DIM)))
    g_chunk = same[None] * g_lane
    nb = math.gcd(RET_BATCH, b)
    part = lambda a: pl.BlockSpec((nb, c, RET_COLS), lambda bi, n, a=a: (bi, n, a))
    full = lambda shape: pl.BlockSpec(shape, lambda bi, n: (0,) * len(shape))
    return pl.pallas_call(
        _retention_kernel,
        grid=(b // nb, s // c),
        in_specs=[part(0), part(1), part(2), part(3),
                  full((hh // 2, c, 2 * c)), full((hh // 2, c, pair)), full((hh // 2, c, pair)),
                  full((hh // 2, pair, pair)), full((pair, pair)), full((1, RET_COLS))],
        out_specs=pl.BlockSpec((nb, c, RET_COLS), lambda bi, n: (bi, n, 0)),
        out_shape=jax.ShapeDtypeStruct((b, s, RET_COLS), BF16),
        scratch_shapes=[pltpu.VMEM((nb, hh // 2, pair, pair), F32)],
        compiler_params=_params(("parallel", "arbitrary")),
        name="retention",
    )(ret, ret, ret, ret, decay, xi, zeta, g_chunk, same, gn_gain.reshape(1, RET_COLS).astype(F32))


def _out_proj_kernel(h_ref, a_ref, r_ref, w_ref, g_ref, h_out_ref, hn_ref):
    h = h_ref[...] + _dot(a_ref[...], w_ref[:NSA_Q_COLS]) + _dot(r_ref[...], w_ref[NSA_Q_COLS:])
    h_out_ref[...] = h
    hn_ref[...] = _rms(h, g_ref[...]).astype(BF16)


def _out_proj(h, a, r, w, g):
    t, d = h.shape
    tm = min(ROW_TILE, t)
    row = lambda n: pl.BlockSpec((tm, n), lambda i: (i, 0))
    return pl.pallas_call(
        _out_proj_kernel,
        grid=(t // tm,),
        in_specs=[row(d), row(NSA_Q_COLS), row(RET_COLS),
                  pl.BlockSpec((MIX_WIDTH, d), lambda i: (0, 0)),
                  pl.BlockSpec((1, d), lambda i: (0, 0))],
        out_specs=[row(d), row(d)],
        out_shape=[jax.ShapeDtypeStruct((t, d), F32), jax.ShapeDtypeStruct((t, d), BF16)],
        compiler_params=_params(("parallel",)),
        name="out_proj",
    )(h, a, r, w.astype(BF16), g.reshape(1, d))


def _swiglu_chunk(x, wg, wu, wd):
    hid = jax.nn.silu(_dot(x, wg)) * _dot(x, wu)
    return _dot(hid.astype(BF16), wd)


def _ffn_kernel(x_ref, wg_ref, wu_ref, wd_ref, o_ref):
    x = x_ref[...]
    acc = None
    for f in range(D_FF // FFN_COL_TILE):
        cols = slice(f * FFN_COL_TILE, (f + 1) * FFN_COL_TILE)
        part = _swiglu_chunk(x, wg_ref[:, cols], wu_ref[:, cols], wd_ref[cols, :])
        acc = part if acc is None else acc + part
    o_ref[...] = acc


def _dense_ffn(hn, wg, wu, wd):
    t, d = hn.shape
    tm = min(FFN_ROW_TILE, t)
    once = pl.Buffered(1)
    return pl.pallas_call(
        _ffn_kernel,
        grid=(t // tm,),
        in_specs=[pl.BlockSpec((tm, d), lambda i: (i, 0)),
                  pl.BlockSpec((d, D_FF), lambda i: (0, 0), pipeline_mode=once),
                  pl.BlockSpec((d, D_FF), lambda i: (0, 0), pipeline_mode=once),
                  pl.BlockSpec((D_FF, d), lambda i: (0, 0), pipeline_mode=once)],
        out_specs=pl.BlockSpec((tm, d), lambda i: (i, 0)),
        out_shape=jax.ShapeDtypeStruct((t, d), F32),
        compiler_params=_params(("parallel",)),
        name="dense_ffn",
    )(hn, wg.astype(BF16), wu.astype(BF16), wd.astype(BF16))


def _lane_column(table, lane_index):
    lane = lax.broadcasted_iota(jnp.int32, table.shape, 1)
    col = jnp.sum(jnp.where(lane == lane_index, table, 0.0), axis=-1, keepdims=True)
    return jnp.broadcast_to(col, table.shape)


def _moe_kernel(x_ref, router_ref, wg_ref, wu_ref, wd_ref, o_ref,
                xs_ref, y_ref, slot_ref, gate_ref, slot_row_ref, slot_e_ref, gate_e_ref, count_ref):
    e = pl.program_id(1)
    f = pl.program_id(2)
    tm, d = x_ref.shape
    sub = MOE_GROUP_ROWS

    @pl.when((e == 0) & (f == 0))
    def _route():
        o_ref[...] = jnp.zeros_like(o_ref)
        logits = _dot(x_ref[...], router_ref[...])
        lane = lax.broadcasted_iota(jnp.int32, logits.shape, 1)
        logits = jnp.where(lane < N_EXPERTS, logits, -jnp.inf)
        v1 = jnp.max(logits, axis=-1, keepdims=True)
        i1 = jnp.min(jnp.where(logits == v1, lane, V7X_LANES), axis=-1, keepdims=True)
        rest = jnp.where(lane == i1, -jnp.inf, logits)
        v2 = jnp.max(rest, axis=-1, keepdims=True)
        i2 = jnp.min(jnp.where(rest == v2, lane, V7X_LANES), axis=-1, keepdims=True)
        e2 = jnp.exp(v2 - v1)
        inv = 1.0 / (1.0 + e2)
        gate_ref[...] = jnp.where(lane == i1, inv, 0.0) + jnp.where(lane == i2, e2 * inv, 0.0)
        routed = jnp.where((lane == i1) | (lane == i2), 1.0, 0.0)
        c = MOE_RANK_CHUNK
        before = (lax.broadcasted_iota(jnp.int32, (c, c), 1) < lax.broadcasted_iota(jnp.int32, (c, c), 0))
        before = jnp.where(before, 1.0, 0.0).astype(BF16)
        offset = jnp.zeros((1, V7X_LANES), F32)
        for j in range(tm // c):
            part = routed[j * c:(j + 1) * c]
            rank = _dot(before, part.astype(BF16)) + offset
            slot_ref[j * c:(j + 1) * c, :] = jnp.where(part > 0.0, rank, -1.0)
            offset = offset + jnp.sum(part, axis=0, keepdims=True)
        for ee in range(N_EXPERTS):
            count_ref[ee] = offset[0, ee].astype(jnp.int32)
        slot_row_ref[...] = slot_ref[...].T[:N_EXPERTS]

    n_groups = (count_ref[e] + sub - 1) // sub

    @pl.when(f == 0)
    def _gather():
        slot_e_ref[...] = _lane_column(slot_ref[...], e)
        gate_e_ref[...] = _lane_column(gate_ref[...], e)
        slot_row = slot_row_ref[pl.ds(e, 1), :]

        def body(s, _):
            r0 = pl.multiple_of(s * sub, sub)
            want = (r0 + lax.broadcasted_iota(jnp.int32, (sub, 1), 0)).astype(F32)
            onehot = jnp.where(slot_row == want, 1.0, 0.0).astype(BF16)
            xs_ref[pl.ds(r0, sub), :] = _dot(onehot, x_ref[...]).astype(BF16)
            y_ref[pl.ds(r0, sub), :] = jnp.zeros((sub, d), F32)
            return 0

        lax.fori_loop(0, n_groups, body, 0)

    def expert(i, _, base, n):
        r0 = pl.multiple_of(base + i * n, n)
        y_ref[pl.ds(r0, n), :] += _swiglu_chunk(xs_ref[pl.ds(r0, n), :], wg_ref[0, 0], wu_ref[0, 0], wd_ref[0])
        return 0

    tail = MOE_TAIL_ROWS
    n_filled = count_ref[e] // sub
    rest = count_ref[e] - n_filled * sub
    short = rest <= tail
    n_whole = jnp.where(short, n_filled, n_filled + 1)
    n_tail = jnp.where(short & (rest > 0), 1, 0)
    lax.fori_loop(0, n_whole, functools.partial(expert, base=0, n=sub), 0)
    lax.fori_loop(0, n_tail, functools.partial(expert, base=n_filled * sub, n=tail), 0)

    @pl.when(f == pl.num_programs(2) - 1)
    def _scatter():
        tc = MOE_SCATTER_ROWS

        def body(s, _):
            r0 = pl.multiple_of(s * sub, sub)
            y = y_ref[pl.ds(r0, sub), :].astype(BF16)
            want = (r0 + lax.broadcasted_iota(jnp.int32, (1, sub), 1)).astype(F32)
            for j in range(tm // tc):
                rows = slice(j * tc, (j + 1) * tc)
                slot = jnp.concatenate([slot_e_ref[rows, :]] * (sub // V7X_LANES), axis=1)
                onehot = jnp.where(slot == want, 1.0, 0.0).astype(BF16)
                weight = jnp.concatenate([gate_e_ref[rows, :]] * (d // V7X_LANES), axis=1)
                o_ref[rows, :] += weight * _dot(onehot, y)
            return 0

        lax.fori_loop(0, n_groups, body, 0)


def _moe_ffn(hn, router, wg, wu, wd):
    t, d = hn.shape
    tm = min(MOE_ROW_TILE, t)
    tf = MOE_COL_TILE
    nf = D_FF // tf
    router = jnp.pad(router, ((0, 0), (0, V7X_LANES - N_EXPERTS))).astype(BF16)
    chunked = lambda w: jnp.transpose(w.astype(BF16).reshape(N_EXPERTS, d, nf, tf), (0, 2, 1, 3))
    once = pl.Buffered(1)
    return pl.pallas_call(
        _moe_kernel,
        grid=(t // tm, N_EXPERTS, nf),
        in_specs=[pl.BlockSpec((tm, d), lambda i, e, f: (i, 0), pipeline_mode=once),
                  pl.BlockSpec((d, V7X_LANES), lambda i, e, f: (0, 0), pipeline_mode=once),
                  pl.BlockSpec((1, 1, d, tf), lambda i, e, f: (e, f, 0, 0)),
                  pl.BlockSpec((1, 1, d, tf), lambda i, e, f: (e, f, 0, 0)),
                  pl.BlockSpec((1, tf, d), lambda i, e, f: (e, f, 0))],
        out_specs=pl.BlockSpec((tm, d), lambda i, e, f: (i, 0)),
        out_shape=jax.ShapeDtypeStruct((t, d), F32),
        scratch_shapes=[pltpu.VMEM((tm, d), BF16), pltpu.VMEM((tm, d), F32),
                        pltpu.VMEM((tm, V7X_LANES), F32), pltpu.VMEM((tm, V7X_LANES), F32),
                        pltpu.VMEM((N_EXPERTS, tm), F32),
                        pltpu.VMEM((tm, V7X_LANES), F32), pltpu.VMEM((tm, V7X_LANES), F32),
                        pltpu.SMEM((N_EXPERTS,), jnp.int32)],
        compiler_params=_params(("parallel", "arbitrary", "arbitrary"), vmem=MOE_VMEM_LIMIT),
        name="moe_ffn",
    )(hn, router, chunked(wg), chunked(wu), wd.astype(BF16))


def _ple_kernel(h_ref, f_ref, p_ref, g_ref, proj_ref, gate_ref, gf_ref, o_ref, *, final_norm):
    h = h_ref[...] + f_ref[...]
    emb = _dot(p_ref[...].astype(BF16), proj_ref[...])
    sig = jax.nn.sigmoid(_dot(_rms(h, g_ref[...]).astype(BF16), gate_ref[...]))
    h = h + emb * sig
    if final_norm:
        h = _rms(h, gf_ref[...])
    o_ref[...] = h


def _ple(h, f, p, g, proj, gate, g_final, final_norm):
    t, d = h.shape
    tm = min(ROW_TILE, t)
    vec = pl.BlockSpec((1, d), lambda i: (0, 0))
    return pl.pallas_call(
        functools.partial(_ple_kernel, final_norm=final_norm),
        grid=(t // tm,),
        in_specs=[pl.BlockSpec((tm, d), lambda i: (i, 0)),
                  pl.BlockSpec((tm, d), lambda i: (i, 0)),
                  pl.BlockSpec((tm, PLE_DIM), lambda i: (i, 0)),
                  vec,
                  pl.BlockSpec((PLE_DIM, d), lambda i: (0, 0)),
                  pl.BlockSpec((d, d), lambda i: (0, 0)),
                  vec],
        out_specs=pl.BlockSpec((tm, d), lambda i: (i, 0)),
        out_shape=jax.ShapeDtypeStruct((t, d), F32),
        compiler_params=_params(("parallel",)),
        name="ple",
    )(h, f, p, g.reshape(1, d), proj.astype(BF16), gate.astype(BF16), g_final.reshape(1, d))


def kernel(x, p, w_in, w_out, g_mix, g_ffn, g_ple, g_final, cmp_pos, cmp_w1, cmp_w2, ret_gn,
           ffn_gate, ffn_up, ffn_down, moe_router, moe_gate, moe_up, moe_down, ple_proj, ple_gate):
    b, s, d = x.shape
    depth = w_in.shape[0]
    t = b * s
    h = x
    for i in range(depth):
        qn, kcv, ks, kv3, gl, ret = _in_proj(h.reshape(b, s, d), g_mix[i], _arrange_w_in(w_in[i]))
        kv_cmp = _compress(kcv, cmp_pos[i], cmp_w1[i], cmp_w2[i])
        a = _nsa_attention(qn, kv_cmp, ks, kv3, gl)
        r = _retention(ret, ret_gn[i])
        h, hn = _out_proj(h.reshape(t, d), a.reshape(t, NSA_Q_COLS), r.reshape(t, RET_COLS), w_out[i], g_ffn[i])
        if i % 2 == 0:
            f = _dense_ffn(hn, ffn_gate[i // 2], ffn_up[i // 2], ffn_down[i // 2])
        else:
            f = _moe_ffn(hn, moe_router[i // 2], moe_gate[i // 2], moe_up[i // 2], moe_down[i // 2])
        h = _ple(h, f, p[i].reshape(t, PLE_DIM), g_ple[i], ple_proj[i], ple_gate[i], g_final, i == depth - 1)
    return h.reshape(b, s, d)
```

```python
import functools
import math

import jax
import jax.numpy as jnp
from jax import lax
from jax.experimental import pallas as pl
from jax.experimental.pallas import tpu as pltpu

F32 = jnp.float32
BF16 = jnp.bfloat16

D_MODEL = 1024
HEAD_DIM = 64
NSA_HEADS = 8
NSA_KV_HEADS = 2
NSA_GROUP = NSA_HEADS // NSA_KV_HEADS
RET_HEADS = 8
CMP_BLOCK = 32
CMP_STRIDE = 16
CMP_HIDDEN = 256
SEL_BLOCK = 64
N_SELECT = 16
N_LOCAL_SEL = 2
WINDOW = 512
RET_CHUNK = 128
D_FF = 3584
N_EXPERTS = 8
PLE_DIM = 256
EPS = 1e-6
NEG_INF = -1e30
BIG = 1e9

NSA_Q_COLS = NSA_HEADS * HEAD_DIM
NSA_KV_COLS = NSA_KV_HEADS * HEAD_DIM
NSA_GATE_COLS = 3 * NSA_HEADS
RET_COLS = RET_HEADS * HEAD_DIM
MIX_WIDTH = NSA_Q_COLS + RET_COLS

V7X_LANES = 128
V7X_SUBLANES = 8
V7X_VMEM_BYTES = 64 * 1024 * 1024
VMEM_LIMIT = V7X_VMEM_BYTES * 3 // 4

ROW_TILE = 512
STREAM_ROW_TILE = 1024
ATTN_Q_TILE = 256
SEL_KV_TILE = 256
RET_BATCH = 4
FFN_ROW_TILE = 512
FFN_COL_TILE = 512
MOE_COL_TILE = 896
MOE_ROW_TILE = 2048
MOE_GROUP_ROWS = 256
MOE_TAIL_ROWS = 64
MOE_RANK_CHUNK = 256
MOE_SCATTER_ROWS = 512
MOE_VMEM_LIMIT = V7X_VMEM_BYTES * 7 // 8

POS_SHIFT = 6
POS_SPLIT = 1 << POS_SHIFT
SEL_SHIFT = SEL_BLOCK.bit_length() - 1
MAX_SEL_BLOCKS = V7X_LANES


def _params(sem, vmem=VMEM_LIMIT):
    return pltpu.CompilerParams(dimension_semantics=sem, vmem_limit_bytes=vmem)


def _dot(a, b):
    return jnp.dot(a, b, preferred_element_type=F32)


def _dot_nt(a, b):
    return lax.dot_general(a, b, (((1,), (1,)), ((), ())), preferred_element_type=F32)


def _dot_tn(a, b):
    return lax.dot_general(a, b, (((0,), (0,)), ((), ())), preferred_element_type=F32)


def _rms(x, g):
    return x * lax.rsqrt(jnp.mean(x * x, axis=-1, keepdims=True) + EPS) * g


def _lane_features(shape, first, second):
    lane = lax.broadcasted_iota(jnp.int32, shape, 1)
    return jnp.where(lane == 0, first, jnp.where(lane == 1, second, 0))


def _key_features(pos, width):
    return _lane_features((pos.shape[0], width), pos >> POS_SHIFT, pos & (POS_SPLIT - 1))


def _ones_features(n, width):
    return _lane_features((n, width), 1, 0)


def _int_to_bf16(x):
    return x.astype(F32).astype(BF16)


def _in_proj_kernel(h_ref, g_ref, w_ref, qn_ref, kcv_ref, ks_ref, kv3_ref, gl_ref, ret_ref):
    tm = h_ref.shape[1]
    xn = _rms(h_ref[0], g_ref[...]).astype(BF16)
    pair = 2 * HEAD_DIM
    pos = pl.program_id(1) * tm + lax.broadcasted_iota(jnp.int32, (tm, 1), 0)
    kfeat = _int_to_bf16(_key_features(pos, HEAD_DIM))
    vfeat = _int_to_bf16(_ones_features(tm, HEAD_DIM))
    lane = lax.broadcasted_iota(jnp.int32, (tm, MAX_SEL_BLOCKS), 1)
    block_onehot = jnp.where(lane == (pos >> SEL_SHIFT), 1.0, 0.0).astype(BF16)

    for j in range(NSA_HEADS // 2):
        z = (_dot(xn, w_ref[:, j * pair:(j + 1) * pair]) * HEAD_DIM ** -0.5).astype(BF16)
        for k in range(2):
            slope = 2.0 ** -(2 * j + k + 1)
            qfeat = _lane_features((tm, HEAD_DIM), POS_SPLIT * slope, slope).astype(BF16)
            qn_ref[0, 2 * j + k] = jnp.concatenate([z[:, k * HEAD_DIM:(k + 1) * HEAD_DIM], qfeat], axis=1)
    base = NSA_Q_COLS
    for a in range(2):
        kcv_ref[a, 0] = _dot(xn, w_ref[:, base + a * pair:base + (a + 1) * pair]).astype(BF16)
    base += 2 * pair
    for a in range(4):
        z = _dot(xn, w_ref[:, base + a * pair:base + (a + 1) * pair]).astype(BF16)
        for g in range(NSA_KV_HEADS):
            zg = z[:, g * HEAD_DIM:(g + 1) * HEAD_DIM]
            if a == 0:
                ks_ref[0, g] = jnp.concatenate([zg, kfeat, block_onehot], axis=1)
            else:
                kv3_ref[a - 1, 0, g] = jnp.concatenate([zg, kfeat if a == 2 else vfeat], axis=1)
    base += 4 * pair
    for a in range(4):
        ret_ref[0, :, a * RET_COLS:(a + 1) * RET_COLS] = _dot(
            xn, w_ref[:, base + a * RET_COLS:base + (a + 1) * RET_COLS]).astype(BF16)
    base += 4 * RET_COLS
    gl_ref[0] = _dot(xn, w_ref[:, base:base + NSA_KV_HEADS * V7X_LANES])


def _arrange_w_in(w):
    q_end = NSA_Q_COLS
    kv_end = q_end + 6 * NSA_KV_COLS
    gl_end = kv_end + NSA_GATE_COLS
    gl = w[:, kv_end:gl_end].reshape(D_MODEL, 3, NSA_KV_HEADS, NSA_GROUP)
    gl = jnp.transpose(gl, (0, 2, 1, 3)).reshape(D_MODEL, NSA_KV_HEADS, 3 * NSA_GROUP)
    gl = jnp.pad(gl, ((0, 0), (0, 0), (0, V7X_LANES - 3 * NSA_GROUP)))
    gl = gl.reshape(D_MODEL, NSA_KV_HEADS * V7X_LANES)
    return jnp.concatenate([w[:, :kv_end], w[:, gl_end:], gl], axis=1).astype(BF16)


def _in_proj(h, g, w):
    b, s, d = h.shape
    assert s // SEL_BLOCK <= MAX_SEL_BLOCKS
    tm = min(ROW_TILE, s)
    ncols = w.shape[1]
    aug = 2 * HEAD_DIM
    return pl.pallas_call(
        _in_proj_kernel,
        grid=(b, s // tm),
        in_specs=[
            pl.BlockSpec((1, tm, d), lambda bi, si: (bi, si, 0)),
            pl.BlockSpec((1, d), lambda bi, si: (0, 0)),
            pl.BlockSpec((d, ncols), lambda bi, si: (0, 0)),
        ],
        out_specs=[
            pl.BlockSpec((1, NSA_HEADS, tm, aug), lambda bi, si: (bi, 0, si, 0)),
            pl.BlockSpec((2, 1, tm, NSA_KV_COLS), lambda bi, si: (0, bi, si, 0)),
            pl.BlockSpec((1, NSA_KV_HEADS, tm, aug + MAX_SEL_BLOCKS), lambda bi, si: (bi, 0, si, 0)),
            pl.BlockSpec((3, 1, NSA_KV_HEADS, tm, aug), lambda bi, si: (0, bi, 0, si, 0)),
            pl.BlockSpec((1, tm, NSA_KV_HEADS * V7X_LANES), lambda bi, si: (bi, si, 0)),
            pl.BlockSpec((1, tm, 4 * RET_COLS), lambda bi, si: (bi, si, 0)),
        ],
        out_shape=[
            jax.ShapeDtypeStruct((b, NSA_HEADS, s, aug), BF16),
            jax.ShapeDtypeStruct((2, b, s, NSA_KV_COLS), BF16),
            jax.ShapeDtypeStruct((b, NSA_KV_HEADS, s, aug + MAX_SEL_BLOCKS), BF16),
            jax.ShapeDtypeStruct((3, b, NSA_KV_HEADS, s, aug), BF16),
            jax.ShapeDtypeStruct((b, s, NSA_KV_HEADS * V7X_LANES), F32),
            jax.ShapeDtypeStruct((b, s, 4 * RET_COLS), BF16),
        ],
        compiler_params=_params(("parallel", "parallel")),
        name="in_proj",
    )(h, g.reshape(1, d), w)


def _compress_kernel(x_ref, wbig_ref, pos_ref, w1_ref, w2_ref, o_ref):
    nch = x_ref.shape[2]
    u = _dot(x_ref[0, 0], wbig_ref[0])
    hid0 = _dot(pos_ref[0], w1_ref[0])[0:1]
    w2 = w2_ref[0]
    cmp_end = lax.broadcasted_iota(jnp.int32, (nch, 1), 0) * CMP_STRIDE + (CMP_BLOCK - 1)
    feat = _int_to_bf16(jnp.where(pl.program_id(0) == 0, _key_features(cmp_end, HEAD_DIM),
                                  _ones_features(nch, HEAD_DIM)))
    for g in range(NSA_KV_HEADS):
        c0 = g * 2 * CMP_HIDDEN
        first = u[:, c0:c0 + CMP_HIDDEN]
        second = u[:, c0 + CMP_HIDDEN:c0 + 2 * CMP_HIDDEN]
        hid = first + pltpu.roll(second, nch - 1, 0) + hid0
        out = _dot(jax.nn.gelu(hid).astype(BF16), w2).astype(BF16)
        o_ref[0, 0, g] = jnp.concatenate([out, feat], axis=1)


def _arrange_cmp_w1(w1):
    r = CMP_BLOCK // CMP_STRIDE
    w1r = w1.reshape(2, r, CMP_STRIDE, HEAD_DIM, CMP_HIDDEN)
    eye = jnp.eye(NSA_KV_HEADS, dtype=w1.dtype)
    big = jnp.einsum('krcdh,gf->kcgdfrh', w1r, eye)
    return big.reshape(2, CMP_STRIDE * NSA_KV_COLS, NSA_KV_HEADS * r * CMP_HIDDEN).astype(BF16)


def _compress(kcv, cmp_pos, cmp_w1, cmp_w2):
    _, b, s, _ = kcv.shape
    nch = s // CMP_STRIDE
    x = kcv.reshape(2, b, nch, CMP_STRIDE * NSA_KV_COLS)
    wbig = _arrange_cmp_w1(cmp_w1)
    pos = jnp.broadcast_to(cmp_pos.reshape(2, 1, CMP_BLOCK * HEAD_DIM),
                           (2, V7X_SUBLANES, CMP_BLOCK * HEAD_DIM)).astype(BF16)
    w1 = cmp_w1.reshape(2, CMP_BLOCK * HEAD_DIM, CMP_HIDDEN).astype(BF16)
    w2 = cmp_w2.astype(BF16)
    kdim = CMP_STRIDE * NSA_KV_COLS
    return pl.pallas_call(
        _compress_kernel,
        grid=(2, b),
        in_specs=[
            pl.BlockSpec((1, 1, nch, kdim), lambda a, bi: (a, bi, 0, 0)),
            pl.BlockSpec((1, kdim, wbig.shape[2]), lambda a, bi: (a, 0, 0)),
            pl.BlockSpec((1, V7X_SUBLANES, CMP_BLOCK * HEAD_DIM), lambda a, bi: (a, 0, 0)),
            pl.BlockSpec((1, CMP_BLOCK * HEAD_DIM, CMP_HIDDEN), lambda a, bi: (a, 0, 0)),
            pl.BlockSpec((1, CMP_HIDDEN, HEAD_DIM), lambda a, bi: (a, 0, 0)),
        ],
        out_specs=pl.BlockSpec((1, 1, NSA_KV_HEADS, nch, 2 * HEAD_DIM), lambda a, bi: (a, bi, 0, 0, 0)),
        out_shape=jax.ShapeDtypeStruct((2, b, NSA_KV_HEADS, nch, 2 * HEAD_DIM), BF16),
        compiler_params=_params(("parallel", "parallel")),
        name="compress",
    )(x, wbig, pos, w1, w2)


def _normalise_t(acc):
    return acc[:HEAD_DIM] * (1.0 / jnp.maximum(acc[HEAD_DIM:HEAD_DIM + 1], 1e-30))


def _attn_kernel(q_ref, kc_ref, vc_ref, ks_ref, vs_ref, kw_ref, vw_ref, gl_ref, ovt_ref, o_ref, tiles_ref,
                 *, n_pick, tk):
    qt = q_ref.shape[2]
    cols = NSA_GROUP * qt
    t0 = pl.program_id(2) * qt
    q_t = q_ref[0].reshape(cols, 2 * HEAD_DIM).T
    tq = t0 + lax.broadcasted_iota(jnp.int32, (1, qt), 1)

    def per_head(valid, masked):
        bias = jnp.where(valid, 0.0, masked)
        return jnp.concatenate([bias] * NSA_GROUP, axis=1)

    nc = kc_ref.shape[3]
    cmp_end = lax.broadcasted_iota(jnp.int32, (nc, 1), 0) * CMP_STRIDE + (CMP_BLOCK - 1)
    logit_c = _dot(kc_ref[0, 0, 0], q_t) + per_head(cmp_end <= tq, -jnp.inf)
    e_c = jnp.exp(logit_c - jnp.maximum(jnp.max(logit_c, axis=0, keepdims=True), NEG_INF))
    p_c = e_c * (1.0 / jnp.maximum(jnp.sum(e_c, axis=0, keepdims=True), 1e-30))
    o_c = _dot(vc_ref[0, 0, 0].T, p_c.astype(BF16))[:HEAD_DIM]

    p_sum = p_c[:, 0:qt]
    for r in range(1, NSA_GROUP):
        p_sum = p_sum + p_c[:, r * qt:(r + 1) * qt]
    p_hi = p_sum.astype(BF16)
    p_lo = (p_sum - p_hi.astype(F32)).astype(BF16)
    imp = _dot(ovt_ref[...], p_hi) + _dot(ovt_ref[...], p_lo)
    shape = (MAX_SEL_BLOCKS, qt)
    blk = lax.broadcasted_iota(jnp.int32, shape, 0)
    back = ((t0 + lax.broadcasted_iota(jnp.int32, shape, 1)) >> SEL_SHIFT) - blk
    forced = (blk == 0) | ((back >= 0) & (back < N_LOCAL_SEL))
    score = jnp.where(forced, BIG, jnp.where(back >= 0, imp, -BIG))

    def pick(_, work):
        m = jnp.max(work, axis=0, keepdims=True)
        first = jnp.min(jnp.where(work == m, blk, MAX_SEL_BLOCKS), axis=0, keepdims=True)
        return jnp.where(blk == first, -jnp.inf, work)

    sel = jnp.where(lax.fori_loop(0, n_pick, pick, score) == -jnp.inf, 1.0, 0.0)
    sel_bias = ((sel - 1.0) * -NEG_INF).astype(BF16)
    q_sel = jnp.concatenate([q_t, jnp.concatenate([sel_bias] * NSA_GROUP, axis=1)], axis=0)

    n_own = max(1, qt // tk)
    n_full = t0 // tk
    blocks_per_tile = tk // SEL_BLOCK
    block_used = jnp.max(sel, axis=1, keepdims=True)
    n_visit = jnp.int32(0)
    for j in range(tiles_ref.shape[0]):
        used = jnp.max(block_used[j * blocks_per_tile:(j + 1) * blocks_per_tile]) > 0.0
        tiles_ref[n_visit] = j
        n_visit = n_visit + (used & (j < n_full)).astype(jnp.int32)

    def sweep(tiles, carry, causal):
        m_i, acc = carry
        starts = [pl.multiple_of(j * tk, tk) for j in tiles]
        scores = []
        for k0 in starts:
            s = _dot(ks_ref[0, 0, pl.ds(k0, tk), :], q_sel)
            if causal:
                s = s + per_head(k0 + lax.broadcasted_iota(jnp.int32, (tk, 1), 0) <= tq, NEG_INF)
            scores.append(s)
        m_new = m_i
        for s in scores:
            m_new = jnp.maximum(m_new, jnp.max(s, axis=0, keepdims=True))
        acc = jnp.exp(m_i - m_new) * acc
        for s, k0 in zip(scores, starts):
            acc = acc + _dot(vs_ref[0, 0, 0, pl.ds(k0, tk), :].T, jnp.exp(s - m_new).astype(BF16))
        return m_new, acc

    carry = (jnp.full((1, cols), NEG_INF, F32), jnp.zeros((2 * HEAD_DIM, cols), F32))
    done = jnp.int32(0)
    for width in (1, 2, 4):
        steps = n_visit // width if width == 4 else (n_visit // width) & 1
        carry = lax.fori_loop(
            0, steps,
            lambda i, c, width=width, done=done: sweep(
                [tiles_ref[done + width * i + u] for u in range(width)], c, causal=False),
            carry)
        done = done + steps * width
    carry = sweep([n_full + j for j in range(n_own)], carry, causal=True)
    o_s = _normalise_t(carry[1])

    band = WINDOW + qt
    w0 = pl.multiple_of(jnp.maximum(t0 - WINDOW, 0), math.gcd(qt, WINDOW))
    kpos = w0 + lax.broadcasted_iota(jnp.int32, (band, 1), 0)
    valid_w = (kpos <= tq) & (kpos > tq - WINDOW)
    logit_w = _dot(kw_ref[0, 0, 0, pl.ds(w0, band), :], q_t) + per_head(valid_w, NEG_INF)
    e_w = jnp.exp(logit_w - jnp.max(logit_w, axis=0, keepdims=True))
    o_w = _normalise_t(_dot(vw_ref[0, 0, 0, pl.ds(w0, band), :].T, e_w.astype(BF16)))

    gate_t = jax.nn.sigmoid(gl_ref[0]).T

    def branch_gate(branch):
        first = branch * NSA_GROUP
        return jnp.concatenate([gate_t[first + r:first + r + 1] for r in range(NSA_GROUP)], axis=1)

    o_t = branch_gate(0) * o_c + branch_gate(1) * o_s + branch_gate(2) * o_w
    o = o_t.T
    o_ref[0] = jnp.concatenate([o[r * qt:(r + 1) * qt] for r in range(NSA_GROUP)], axis=1).astype(o_ref.dtype)


def _nsa_attention(qn, kv_cmp, ks, kv3, gl):
    b, _, s, _ = qn.shape
    nc = kv_cmp.shape[3]
    nsel = s // SEL_BLOCK
    n_pick = min(N_SELECT, nsel)
    tk = min(SEL_KV_TILE, s)
    aug = 2 * HEAD_DIM
    qt = min(ATTN_Q_TILE, s)
    assert qt % tk == 0 or tk % qt == 0
    cmp_start = jnp.arange(nc) * CMP_STRIDE
    sel_start = jnp.arange(nsel) * SEL_BLOCK
    overlap = (jnp.minimum(cmp_start[:, None] + CMP_BLOCK, sel_start[None, :] + SEL_BLOCK)
               > jnp.maximum(cmp_start[:, None], sel_start[None, :]))
    overlap_t = jnp.pad(overlap.T.astype(BF16), ((0, MAX_SEL_BLOCKS - nsel), (0, 0)))
    kv_spec = lambda a: pl.BlockSpec((1, 1, 1, s, aug), lambda bi, g, i, a=a: (a, bi, g, 0, 0))
    cmp_spec = lambda a: pl.BlockSpec((1, 1, 1, nc, aug), lambda bi, g, i, a=a: (a, bi, g, 0, 0))
    return pl.pallas_call(
        functools.partial(_attn_kernel, n_pick=n_pick, tk=tk),
        grid=(b, NSA_KV_HEADS, s // qt),
        in_specs=[
            pl.BlockSpec((1, NSA_GROUP, qt, aug), lambda bi, g, i: (bi, g, i, 0)),
            cmp_spec(0), cmp_spec(1),
            pl.BlockSpec((1, 1, s, aug + MAX_SEL_BLOCKS), lambda bi, g, i: (bi, g, 0, 0)),
            kv_spec(0), kv_spec(1), kv_spec(2),
            pl.BlockSpec((1, qt, V7X_LANES), lambda bi, g, i: (bi, i, g)),
            pl.BlockSpec((MAX_SEL_BLOCKS, nc), lambda bi, g, i: (0, 0)),
        ],
        out_specs=pl.BlockSpec((1, qt, NSA_GROUP * HEAD_DIM), lambda bi, g, i: (bi, i, g)),
        out_shape=jax.ShapeDtypeStruct((b, s, NSA_Q_COLS), BF16),
        scratch_shapes=[pltpu.SMEM((s // tk,), jnp.int32)],
        compiler_params=_params(("parallel", "parallel", "arbitrary")),
        name="nsa_attention",
    )(qn, kv_cmp, kv_cmp, ks, kv3, kv3, kv3, gl, overlap_t)


def _retention_kernel(q_ref, k_ref, v_ref, g_ref, decay_ref, xi_ref, zeta_ref, gch_ref, same_ref, gn_ref,
                      o_ref, state_ref):
    @pl.when(pl.program_id(1) == 0)
    def _():
        state_ref[...] = jnp.zeros_like(state_ref)

    pair = 2 * HEAD_DIM
    c = q_ref.shape[1]
    scale = jnp.asarray(HEAD_DIM ** -0.5, BF16)
    first = lax.broadcasted_iota(jnp.int32, (c, pair), 1) < HEAD_DIM
    same = same_ref[...]
    mean_w = (same * (1.0 / HEAD_DIM)).astype(BF16)

    def head_mean(x):
        hi = x.astype(BF16)
        lo = (x - hi.astype(F32)).astype(BF16)
        return _dot(hi, mean_w) + _dot(lo, mean_w)

    def stack(x):
        zero = jnp.zeros_like(x)
        return jnp.concatenate([jnp.where(first, x, zero), jnp.where(first, zero, x)], axis=0)

    for i in range(q_ref.shape[0]):
        outs = []
        for p in range(RET_HEADS // 2):
            cols = slice(p * pair, (p + 1) * pair)
            q = q_ref[i, :, cols]
            k = k_ref[i, :, cols] * scale
            v = v_ref[i, :, cols]
            state = state_ref[i, p]
            inner = _dot_nt(q, stack(k)) * decay_ref[p]
            o = _dot(inner.astype(BF16), stack(v)) + _dot(q, state.astype(BF16)) * xi_ref[p]
            kz = (k.astype(F32) * zeta_ref[p]).astype(BF16)
            state_ref[i, p] = gch_ref[p] * state + same * _dot_tn(kz, v)
            centred = o - head_mean(o)
            outs.append(centred * lax.rsqrt(head_mean(jnp.square(centred)) + EPS))
        o = jnp.concatenate(outs, axis=1) * gn_ref[...]
        o_ref[i] = (jax.nn.silu(g_ref[i].astype(F32)) * o).astype(o_ref.dtype)


def _retention(ret, gn_gain):
    b, s, _ = ret.shape
    c = RET_CHUNK
    hh = RET_HEADS
    pair = 2 * HEAD_DIM
    log_gamma = jnp.log1p(-jnp.exp2(-5.0 - jnp.arange(hh, dtype=F32)))
    pos = jnp.arange(c, dtype=F32)
    diff = pos[:, None] - pos[None, :]
    decay = jnp.where(diff >= 0, jnp.exp(jnp.maximum(diff, 0.0)[None] * log_gamma[:, None, None]), 0.0)
    xi = jnp.exp((pos + 1.0)[None] * log_gamma[:, None])[..., None]
    zeta = jnp.exp((c - 1.0 - pos)[None] * log_gamma[:, None])[..., None]
    g_chunk = jnp.exp(c * log_gamma)[:, None, None]
    side_by_side = lambda t: jnp.concatenate([t[0::2], t[1::2]], axis=-1)
    decay = side_by_side(decay)
    xi = side_by_side(jnp.broadcast_to(xi, (hh, c, HEAD_DIM)))
    zeta = side_by_side(jnp.broadcast_to(zeta, (hh, c, HEAD_DIM)))
    head_of = jnp.arange(pair) // HEAD_DIM
    same = (head_of[:, None] == head_of[None, :]).astype(F32)
    g_lane = side_by_side(jnp.broadcast_to(g_chunk, (hh, 1, HEAD_DIM)))
    g_chunk = same[None] * g_lane
    nb = math.gcd(RET_BATCH, b)
    part = lambda a: pl.BlockSpec((nb, c, RET_COLS), lambda bi, n, a=a: (bi, n, a))
    full = lambda shape: pl.BlockSpec(shape, lambda bi, n: (0,) * len(shape))
    return pl.pallas_call(
        _retention_kernel,
        grid=(b // nb, s // c),
        in_specs=[part(0), part(1), part(2), part(3),
                  full((hh // 2, c, 2 * c)), full((hh // 2, c, pair)), full((hh // 2, c, pair)),
                  full((hh // 2, pair, pair)), full((pair, pair)), full((1, RET_COLS))],
        out_specs=pl.BlockSpec((nb, c, RET_COLS), lambda bi, n: (bi, n, 0)),
        out_shape=jax.ShapeDtypeStruct((b, s, RET_COLS), BF16),
        scratch_shapes=[pltpu.VMEM((nb, hh // 2, pair, pair), F32)],
        compiler_params=_params(("parallel", "arbitrary")),
        name="retention",
    )(ret, ret, ret, ret, decay, xi, zeta, g_chunk, same, gn_gain.reshape(1, RET_COLS).astype(F32))


def _out_proj_kernel(h_ref, a_ref, r_ref, w_ref, g_ref, h_out_ref, hn_ref):
    h = h_ref[...] + _dot(a_ref[...], w_ref[:NSA_Q_COLS]) + _dot(r_ref[...], w_ref[NSA_Q_COLS:])
    h_out_ref[...] = h
    hn_ref[...] = _rms(h, g_ref[...]).astype(BF16)


def _out_proj(h, a, r, w, g):
    t, d = h.shape
    tm = min(STREAM_ROW_TILE, t)
    row = lambda n: pl.BlockSpec((tm, n), lambda i: (i, 0))
    return pl.pallas_call(
        _out_proj_kernel,
        grid=(t // tm,),
        in_specs=[row(d), row(NSA_Q_COLS), row(RET_COLS),
                  pl.BlockSpec((MIX_WIDTH, d), lambda i: (0, 0)),
                  pl.BlockSpec((1, d), lambda i: (0, 0))],
        out_specs=[row(d), row(d)],
        out_shape=[jax.ShapeDtypeStruct((t, d), F32), jax.ShapeDtypeStruct((t, d), BF16)],
        compiler_params=_params(("parallel",)),
        name="out_proj",
    )(h, a, r, w.astype(BF16), g.reshape(1, d))


def _swiglu_chunk(x, wg, wu, wd):
    hid = jax.nn.silu(_dot(x, wg)) * _dot(x, wu)
    return _dot(hid.astype(BF16), wd)


def _ffn_kernel(x_ref, wg_ref, wu_ref, wd_ref, o_ref):
    x = x_ref[...]
    acc = None
    for f in range(D_FF // FFN_COL_TILE):
        cols = slice(f * FFN_COL_TILE, (f + 1) * FFN_COL_TILE)
        part = _swiglu_chunk(x, wg_ref[:, cols], wu_ref[:, cols], wd_ref[cols, :])
        acc = part if acc is None else acc + part
    o_ref[...] = acc


def _dense_ffn(hn, wg, wu, wd):
    t, d = hn.shape
    tm = min(FFN_ROW_TILE, t)
    once = pl.Buffered(1)
    return pl.pallas_call(
        _ffn_kernel,
        grid=(t // tm,),
        in_specs=[pl.BlockSpec((tm, d), lambda i: (i, 0)),
                  pl.BlockSpec((d, D_FF), lambda i: (0, 0), pipeline_mode=once),
                  pl.BlockSpec((d, D_FF), lambda i: (0, 0), pipeline_mode=once),
                  pl.BlockSpec((D_FF, d), lambda i: (0, 0), pipeline_mode=once)],
        out_specs=pl.BlockSpec((tm, d), lambda i: (i, 0)),
        out_shape=jax.ShapeDtypeStruct((t, d), F32),
        compiler_params=_params(("parallel",)),
        name="dense_ffn",
    )(hn, wg.astype(BF16), wu.astype(BF16), wd.astype(BF16))


def _lane_column(table, lane_index):
    lane = lax.broadcasted_iota(jnp.int32, table.shape, 1)
    col = jnp.sum(jnp.where(lane == lane_index, table, 0.0), axis=-1, keepdims=True)
    return jnp.broadcast_to(col, table.shape)


def _moe_kernel(x_ref, router_ref, wg_ref, wu_ref, wd_ref, o_ref,
                xs_ref, y_ref, slot_ref, gate_ref, slot_row_ref, slot_e_ref, gate_e_ref, count_ref):
    e = pl.program_id(1)
    f = pl.program_id(2)
    tm, d = x_ref.shape
    sub = MOE_GROUP_ROWS

    @pl.when((e == 0) & (f == 0))
    def _route():
        o_ref[...] = jnp.zeros_like(o_ref)
        logits = _dot(x_ref[...], router_ref[...])
        lane = lax.broadcasted_iota(jnp.int32, logits.shape, 1)
        logits = jnp.where(lane < N_EXPERTS, logits, -jnp.inf)
        v1 = jnp.max(logits, axis=-1, keepdims=True)
        i1 = jnp.min(jnp.where(logits == v1, lane, V7X_LANES), axis=-1, keepdims=True)
        rest = jnp.where(lane == i1, -jnp.inf, logits)
        v2 = jnp.max(rest, axis=-1, keepdims=True)
        i2 = jnp.min(jnp.where(rest == v2, lane, V7X_LANES), axis=-1, keepdims=True)
        e2 = jnp.exp(v2 - v1)
        inv = 1.0 / (1.0 + e2)
        gate_ref[...] = jnp.where(lane == i1, inv, 0.0) + jnp.where(lane == i2, e2 * inv, 0.0)
        routed = jnp.where((lane == i1) | (lane == i2), 1.0, 0.0)
        c = MOE_RANK_CHUNK
        before = (lax.broadcasted_iota(jnp.int32, (c, c), 1) < lax.broadcasted_iota(jnp.int32, (c, c), 0))
        before = jnp.where(before, 1.0, 0.0).astype(BF16)
        offset = jnp.zeros((1, V7X_LANES), F32)
        for j in range(tm // c):
            part = routed[j * c:(j + 1) * c]
            rank = _dot(before, part.astype(BF16)) + offset
            slot_ref[j * c:(j + 1) * c, :] = jnp.where(part > 0.0, rank, -1.0)
            offset = offset + jnp.sum(part, axis=0, keepdims=True)
        for ee in range(N_EXPERTS):
            count_ref[ee] = offset[0, ee].astype(jnp.int32)
        slot_row_ref[...] = slot_ref[...].T[:N_EXPERTS]

    n_groups = (count_ref[e] + sub - 1) // sub

    @pl.when(f == 0)
    def _gather():
        slot_e_ref[...] = _lane_column(slot_ref[...], e)
        gate_e_ref[...] = _lane_column(gate_ref[...], e)
        slot_row = slot_row_ref[pl.ds(e, 1), :]

        def body(s, _):
            r0 = pl.multiple_of(s * sub, sub)
            want = (r0 + lax.broadcasted_iota(jnp.int32, (sub, 1), 0)).astype(F32)
            onehot = jnp.where(slot_row == want, 1.0, 0.0).astype(BF16)
            xs_ref[pl.ds(r0, sub), :] = _dot(onehot, x_ref[...]).astype(BF16)
            y_ref[pl.ds(r0, sub), :] = jnp.zeros((sub, d), F32)
            return 0

        lax.fori_loop(0, n_groups, body, 0)

    def expert(i, _, base, n):
        r0 = pl.multiple_of(base + i * n, n)
        y_ref[pl.ds(r0, n), :] += _swiglu_chunk(xs_ref[pl.ds(r0, n), :], wg_ref[0, 0], wu_ref[0, 0], wd_ref[0])
        return 0

    tail = MOE_TAIL_ROWS
    n_filled = count_ref[e] // sub
    rest = count_ref[e] - n_filled * sub
    short = rest <= tail
    n_whole = jnp.where(short, n_filled, n_filled + 1)
    n_tail = jnp.where(short & (rest > 0), 1, 0)
    lax.fori_loop(0, n_whole, functools.partial(expert, base=0, n=sub), 0)
    lax.fori_loop(0, n_tail, functools.partial(expert, base=n_filled * sub, n=tail), 0)

    @pl.when(f == pl.num_programs(2) - 1)
    def _scatter():
        tc = MOE_SCATTER_ROWS

        def body(s, _):
            r0 = pl.multiple_of(s * sub, sub)
            y = y_ref[pl.ds(r0, sub), :].astype(BF16)
            want = (r0 + lax.broadcasted_iota(jnp.int32, (1, sub), 1)).astype(F32)
            for j in range(tm // tc):
                rows = slice(j * tc, (j + 1) * tc)
                slot = jnp.concatenate([slot_e_ref[rows, :]] * (sub // V7X_LANES), axis=1)
                onehot = jnp.where(slot == want, 1.0, 0.0).astype(BF16)
                weight = jnp.concatenate([gate_e_ref[rows, :]] * (d // V7X_LANES), axis=1)
                o_ref[rows, :] += weight * _dot(onehot, y)
            return 0

        lax.fori_loop(0, n_groups, body, 0)


def _moe_ffn(hn, router, wg, wu, wd):
    t, d = hn.shape
    tm = min(MOE_ROW_TILE, t)
    tf = MOE_COL_TILE
    nf = D_FF // tf
    router = jnp.pad(router, ((0, 0), (0, V7X_LANES - N_EXPERTS))).astype(BF16)
    chunked = lambda w: jnp.transpose(w.astype(BF16).reshape(N_EXPERTS, d, nf, tf), (0, 2, 1, 3))
    once = pl.Buffered(1)
    return pl.pallas_call(
        _moe_kernel,
        grid=(t // tm, N_EXPERTS, nf),
        in_specs=[pl.BlockSpec((tm, d), lambda i, e, f: (i, 0), pipeline_mode=once),
                  pl.BlockSpec((d, V7X_LANES), lambda i, e, f: (0, 0), pipeline_mode=once),
                  pl.BlockSpec((1, 1, d, tf), lambda i, e, f: (e, f, 0, 0)),
                  pl.BlockSpec((1, 1, d, tf), lambda i, e, f: (e, f, 0, 0)),
                  pl.BlockSpec((1, tf, d), lambda i, e, f: (e, f, 0))],
        out_specs=pl.BlockSpec((tm, d), lambda i, e, f: (i, 0)),
        out_shape=jax.ShapeDtypeStruct((t, d), F32),
        scratch_shapes=[pltpu.VMEM((tm, d), BF16), pltpu.VMEM((tm, d), F32),
                        pltpu.VMEM((tm, V7X_LANES), F32), pltpu.VMEM((tm, V7X_LANES), F32),
                        pltpu.VMEM((N_EXPERTS, tm), F32),
                        pltpu.VMEM((tm, V7X_LANES), F32), pltpu.VMEM((tm, V7X_LANES), F32),
                        pltpu.SMEM((N_EXPERTS,), jnp.int32)],
        compiler_params=_params(("parallel", "arbitrary", "arbitrary"), vmem=MOE_VMEM_LIMIT),
        name="moe_ffn",
    )(hn, router, chunked(wg), chunked(wu), wd.astype(BF16))


def _ple_kernel(h_ref, f_ref, p_ref, g_ref, proj_ref, gate_ref, gf_ref, o_ref, *, final_norm):
    h = h_ref[...] + f_ref[...]
    emb = _dot(p_ref[...].astype(BF16), proj_ref[...])
    sig = jax.nn.sigmoid(_dot(_rms(h, g_ref[...]).astype(BF16), gate_ref[...]))
    h = h + emb * sig
    if final_norm:
        h = _rms(h, gf_ref[...])
    o_ref[...] = h


def _ple(h, f, p, g, proj, gate, g_final, final_norm):
    t, d = h.shape
    tm = min(STREAM_ROW_TILE, t)
    vec = pl.BlockSpec((1, d), lambda i: (0, 0))
    return pl.pallas_call(
        functools.partial(_ple_kernel, final_norm=final_norm),
        grid=(t // tm,),
        in_specs=[pl.BlockSpec((tm, d), lambda i: (i, 0)),
                  pl.BlockSpec((tm, d), lambda i: (i, 0)),
                  pl.BlockSpec((tm, PLE_DIM), lambda i: (i, 0)),
                  vec,
                  pl.BlockSpec((PLE_DIM, d), lambda i: (0, 0)),
                  pl.BlockSpec((d, d), lambda i: (0, 0)),
                  vec],
        out_specs=pl.BlockSpec((tm, d), lambda i: (i, 0)),
        out_shape=jax.ShapeDtypeStruct((t, d), F32),
        compiler_params=_params(("parallel",)),
        name="ple",
    )(h, f, p, g.reshape(1, d), proj.astype(BF16), gate.astype(BF16), g_final.reshape(1, d))


def kernel(x, p, w_in, w_out, g_mix, g_ffn, g_ple, g_final, cmp_pos, cmp_w1, cmp_w2, ret_gn,
           ffn_gate, ffn_up, ffn_down, moe_router, moe_gate, moe_up, moe_down, ple_proj, ple_gate):
    b, s, d = x.shape
    depth = w_in.shape[0]
    t = b * s
    h = x
    for i in range(depth):
        qn, kcv, ks, kv3, gl, ret = _in_proj(h.reshape(b, s, d), g_mix[i], _arrange_w_in(w_in[i]))
        kv_cmp = _compress(kcv, cmp_pos[i], cmp_w1[i], cmp_w2[i])
        a = _nsa_attention(qn, kv_cmp, ks, kv3, gl)
        r = _retention(ret, ret_gn[i])
        h, hn = _out_proj(h.reshape(t, d), a.reshape(t, NSA_Q_COLS), r.reshape(t, RET_COLS), w_out[i], g_ffn[i])
        if i % 2 == 0:
            f = _dense_ffn(hn, ffn_gate[i // 2], ffn_up[i // 2], ffn_down[i // 2])
        else:
            f = _moe_ffn(hn, moe_router[i // 2], moe_gate[i // 2], moe_up[i // 2], moe_down[i // 2])
        h = _ple(h, f, p[i].reshape(t, PLE_DIM), g_ple[i], ple_proj[i], ple_gate[i], g_final, i == depth - 1)
    return h.reshape(b, s, d)
```

```python
import functools
import math

import jax
import jax.numpy as jnp
from jax import lax
from jax.experimental import pallas as pl
from jax.experimental.pallas import tpu as pltpu

F32 = jnp.float32
BF16 = jnp.bfloat16

D_MODEL = 1024
HEAD_DIM = 64
NSA_HEADS = 8
NSA_KV_HEADS = 2
NSA_GROUP = NSA_HEADS // NSA_KV_HEADS
RET_HEADS = 8
CMP_BLOCK = 32
CMP_STRIDE = 16
CMP_HIDDEN = 256
SEL_BLOCK = 64
N_SELECT = 16
N_LOCAL_SEL = 2
WINDOW = 512
RET_CHUNK = 128
D_FF = 3584
N_EXPERTS = 8
PLE_DIM = 256
EPS = 1e-6
NEG_INF = -1e30
BIG = 1e9

NSA_Q_COLS = NSA_HEADS * HEAD_DIM
NSA_KV_COLS = NSA_KV_HEADS * HEAD_DIM
NSA_GATE_COLS = 3 * NSA_HEADS
RET_COLS = RET_HEADS * HEAD_DIM
MIX_WIDTH = NSA_Q_COLS + RET_COLS

V7X_LANES = 128
V7X_SUBLANES = 8
V7X_VMEM_BYTES = 64 * 1024 * 1024
VMEM_LIMIT = V7X_VMEM_BYTES * 3 // 4

ROW_TILE = 512
STREAM_ROW_TILE = 1024
ATTN_Q_TILE = 256
SEL_KV_TILE = 256
RET_BATCH = 4
FFN_ROW_TILE = 512
FFN_COL_TILE = 512
MOE_COL_TILE = 896
MOE_ROW_TILE = 2048
MOE_GROUP_ROWS = 256
MOE_TAIL_ROWS = 64
MOE_RANK_CHUNK = 256
MOE_SCATTER_ROWS = 512
MOE_VMEM_LIMIT = V7X_VMEM_BYTES * 7 // 8

POS_SHIFT = 6
POS_SPLIT = 1 << POS_SHIFT
SEL_SHIFT = SEL_BLOCK.bit_length() - 1
MAX_SEL_BLOCKS = V7X_LANES


def _params(sem, vmem=VMEM_LIMIT):
    return pltpu.CompilerParams(dimension_semantics=sem, vmem_limit_bytes=vmem)


def _dot(a, b):
    return jnp.dot(a, b, preferred_element_type=F32)


def _dot_nt(a, b):
    return lax.dot_general(a, b, (((1,), (1,)), ((), ())), preferred_element_type=F32)


def _dot_tn(a, b):
    return lax.dot_general(a, b, (((0,), (0,)), ((), ())), preferred_element_type=F32)


def _rms(x, g):
    return x * lax.rsqrt(jnp.mean(x * x, axis=-1, keepdims=True) + EPS) * g


def _lane_features(shape, first, second):
    lane = lax.broadcasted_iota(jnp.int32, shape, 1)
    return jnp.where(lane == 0, first, jnp.where(lane == 1, second, 0))


def _key_features(pos, width):
    return _lane_features((pos.shape[0], width), pos >> POS_SHIFT, pos & (POS_SPLIT - 1))


def _ones_features(n, width):
    return _lane_features((n, width), 1, 0)


def _int_to_bf16(x):
    return x.astype(F32).astype(BF16)


def _in_proj_kernel(h_ref, g_ref, w_ref, qn_ref, kcv_ref, ks_ref, kv3_ref, gl_ref, ret_ref):
    tm = h_ref.shape[1]
    xn = _rms(h_ref[0], g_ref[...]).astype(BF16)
    pair = 2 * HEAD_DIM
    pos = pl.program_id(1) * tm + lax.broadcasted_iota(jnp.int32, (tm, 1), 0)
    kfeat = _int_to_bf16(_key_features(pos, HEAD_DIM))
    vfeat = _int_to_bf16(_ones_features(tm, HEAD_DIM))
    lane = lax.broadcasted_iota(jnp.int32, (tm, MAX_SEL_BLOCKS), 1)
    block_onehot = jnp.where(lane == (pos >> SEL_SHIFT), 1.0, 0.0).astype(BF16)

    wide = 2 * pair
    per_dot = wide // HEAD_DIM
    for j in range(NSA_Q_COLS // wide):
        z = (_dot(xn, w_ref[:, j * wide:(j + 1) * wide]) * HEAD_DIM ** -0.5).astype(BF16)
        for k in range(per_dot):
            head = per_dot * j + k
            slope = 2.0 ** -(head + 1)
            qfeat = _lane_features((tm, HEAD_DIM), POS_SPLIT * slope, slope).astype(BF16)
            qn_ref[0, head] = jnp.concatenate([z[:, k * HEAD_DIM:(k + 1) * HEAD_DIM], qfeat], axis=1)
    base = NSA_Q_COLS
    z = _dot(xn, w_ref[:, base:base + wide]).astype(BF16)
    for a in range(2):
        kcv_ref[a, 0] = z[:, a * pair:(a + 1) * pair]
    base += wide
    for c in range(2):
        z = _dot(xn, w_ref[:, base + c * wide:base + (c + 1) * wide]).astype(BF16)
        for half in range(2):
            a = 2 * c + half
            for g in range(NSA_KV_HEADS):
                zg = z[:, half * pair + g * HEAD_DIM:half * pair + (g + 1) * HEAD_DIM]
                if a == 0:
                    ks_ref[0, g] = jnp.concatenate([zg, kfeat, block_onehot], axis=1)
                else:
                    kv3_ref[a - 1, 0, g] = jnp.concatenate([zg, kfeat if a == 2 else vfeat], axis=1)
    base += 2 * wide
    for a in range(4):
        ret_ref[0, :, a * RET_COLS:(a + 1) * RET_COLS] = _dot(
            xn, w_ref[:, base + a * RET_COLS:base + (a + 1) * RET_COLS]).astype(BF16)
    base += 4 * RET_COLS
    gl_ref[0] = _dot(xn, w_ref[:, base:base + NSA_KV_HEADS * V7X_LANES])


def _arrange_w_in(w):
    q_end = NSA_Q_COLS
    kv_end = q_end + 6 * NSA_KV_COLS
    gl_end = kv_end + NSA_GATE_COLS
    gl = w[:, kv_end:gl_end].reshape(D_MODEL, 3, NSA_KV_HEADS, NSA_GROUP)
    gl = jnp.transpose(gl, (0, 2, 1, 3)).reshape(D_MODEL, NSA_KV_HEADS, 3 * NSA_GROUP)
    gl = jnp.pad(gl, ((0, 0), (0, 0), (0, V7X_LANES - 3 * NSA_GROUP)))
    gl = gl.reshape(D_MODEL, NSA_KV_HEADS * V7X_LANES)
    return jnp.concatenate([w[:, :kv_end], w[:, gl_end:], gl], axis=1).astype(BF16)


def _in_proj(h, g, w):
    b, s, d = h.shape
    assert s // SEL_BLOCK <= MAX_SEL_BLOCKS
    tm = min(ROW_TILE, s)
    ncols = w.shape[1]
    aug = 2 * HEAD_DIM
    return pl.pallas_call(
        _in_proj_kernel,
        grid=(b, s // tm),
        in_specs=[
            pl.BlockSpec((1, tm, d), lambda bi, si: (bi, si, 0)),
            pl.BlockSpec((1, d), lambda bi, si: (0, 0)),
            pl.BlockSpec((d, ncols), lambda bi, si: (0, 0)),
        ],
        out_specs=[
            pl.BlockSpec((1, NSA_HEADS, tm, aug), lambda bi, si: (bi, 0, si, 0)),
            pl.BlockSpec((2, 1, tm, NSA_KV_COLS), lambda bi, si: (0, bi, si, 0)),
            pl.BlockSpec((1, NSA_KV_HEADS, tm, aug + MAX_SEL_BLOCKS), lambda bi, si: (bi, 0, si, 0)),
            pl.BlockSpec((3, 1, NSA_KV_HEADS, tm, aug), lambda bi, si: (0, bi, 0, si, 0)),
            pl.BlockSpec((1, tm, NSA_KV_HEADS * V7X_LANES), lambda bi, si: (bi, si, 0)),
            pl.BlockSpec((1, tm, 4 * RET_COLS), lambda bi, si: (bi, si, 0)),
        ],
        out_shape=[
            jax.ShapeDtypeStruct((b, NSA_HEADS, s, aug), BF16),
            jax.ShapeDtypeStruct((2, b, s, NSA_KV_COLS), BF16),
            jax.ShapeDtypeStruct((b, NSA_KV_HEADS, s, aug + MAX_SEL_BLOCKS), BF16),
            jax.ShapeDtypeStruct((3, b, NSA_KV_HEADS, s, aug), BF16),
            jax.ShapeDtypeStruct((b, s, NSA_KV_HEADS * V7X_LANES), F32),
            jax.ShapeDtypeStruct((b, s, 4 * RET_COLS), BF16),
        ],
        compiler_params=_params(("parallel", "parallel")),
        name="in_proj",
    )(h, g.reshape(1, d), w)


def _compress_kernel(x_ref, wbig_ref, pos_ref, w1_ref, w2_ref, o_ref):
    nch = x_ref.shape[2]
    u = _dot(x_ref[0, 0], wbig_ref[0])
    hid0 = _dot(pos_ref[0], w1_ref[0])[0:1]
    w2 = w2_ref[0]
    cmp_end = lax.broadcasted_iota(jnp.int32, (nch, 1), 0) * CMP_STRIDE + (CMP_BLOCK - 1)
    feat = _int_to_bf16(jnp.where(pl.program_id(0) == 0, _key_features(cmp_end, HEAD_DIM),
                                  _ones_features(nch, HEAD_DIM)))
    for g in range(NSA_KV_HEADS):
        c0 = g * 2 * CMP_HIDDEN
        first = u[:, c0:c0 + CMP_HIDDEN]
        second = u[:, c0 + CMP_HIDDEN:c0 + 2 * CMP_HIDDEN]
        hid = first + pltpu.roll(second, nch - 1, 0) + hid0
        out = _dot(jax.nn.gelu(hid).astype(BF16), w2).astype(BF16)
        o_ref[0, 0, g] = jnp.concatenate([out, feat], axis=1)


def _arrange_cmp_w1(w1):
    r = CMP_BLOCK // CMP_STRIDE
    w1r = w1.reshape(2, r, CMP_STRIDE, HEAD_DIM, CMP_HIDDEN)
    eye = jnp.eye(NSA_KV_HEADS, dtype=w1.dtype)
    big = jnp.einsum('krcdh,gf->kcgdfrh', w1r, eye)
    return big.reshape(2, CMP_STRIDE * NSA_KV_COLS, NSA_KV_HEADS * r * CMP_HIDDEN).astype(BF16)


def _compress(kcv, cmp_pos, cmp_w1, cmp_w2):
    _, b, s, _ = kcv.shape
    nch = s // CMP_STRIDE
    x = kcv.reshape(2, b, nch, CMP_STRIDE * NSA_KV_COLS)
    wbig = _arrange_cmp_w1(cmp_w1)
    pos = jnp.broadcast_to(cmp_pos.reshape(2, 1, CMP_BLOCK * HEAD_DIM),
                           (2, V7X_SUBLANES, CMP_BLOCK * HEAD_DIM)).astype(BF16)
    w1 = cmp_w1.reshape(2, CMP_BLOCK * HEAD_DIM, CMP_HIDDEN).astype(BF16)
    w2 = cmp_w2.astype(BF16)
    kdim = CMP_STRIDE * NSA_KV_COLS
    return pl.pallas_call(
        _compress_kernel,
        grid=(2, b),
        in_specs=[
            pl.BlockSpec((1, 1, nch, kdim), lambda a, bi: (a, bi, 0, 0)),
            pl.BlockSpec((1, kdim, wbig.shape[2]), lambda a, bi: (a, 0, 0)),
            pl.BlockSpec((1, V7X_SUBLANES, CMP_BLOCK * HEAD_DIM), lambda a, bi: (a, 0, 0)),
            pl.BlockSpec((1, CMP_BLOCK * HEAD_DIM, CMP_HIDDEN), lambda a, bi: (a, 0, 0)),
            pl.BlockSpec((1, CMP_HIDDEN, HEAD_DIM), lambda a, bi: (a, 0, 0)),
        ],
        out_specs=pl.BlockSpec((1, 1, NSA_KV_HEADS, nch, 2 * HEAD_DIM), lambda a, bi: (a, bi, 0, 0, 0)),
        out_shape=jax.ShapeDtypeStruct((2, b, NSA_KV_HEADS, nch, 2 * HEAD_DIM), BF16),
        compiler_params=_params(("parallel", "parallel")),
        name="compress",
    )(x, wbig, pos, w1, w2)


def _normalise_t(acc):
    return acc[:HEAD_DIM] * (1.0 / jnp.maximum(acc[HEAD_DIM:HEAD_DIM + 1], 1e-30))


def _attn_kernel(q_ref, kc_ref, vc_ref, ks_ref, vs_ref, kw_ref, vw_ref, gl_ref, ovt_ref, o_ref, tiles_ref,
                 *, n_pick, tk):
    qt = q_ref.shape[2]
    cols = NSA_GROUP * qt
    t0 = pl.program_id(2) * qt
    q_t = q_ref[0].reshape(cols, 2 * HEAD_DIM).T
    tq = t0 + lax.broadcasted_iota(jnp.int32, (1, qt), 1)

    def per_head(valid, masked):
        bias = jnp.where(valid, 0.0, masked)
        return jnp.concatenate([bias] * NSA_GROUP, axis=1)

    nc = kc_ref.shape[3]
    cmp_end = lax.broadcasted_iota(jnp.int32, (nc, 1), 0) * CMP_STRIDE + (CMP_BLOCK - 1)
    logit_c = _dot(kc_ref[0, 0, 0], q_t) + per_head(cmp_end <= tq, -jnp.inf)
    e_c = jnp.exp(logit_c - jnp.maximum(jnp.max(logit_c, axis=0, keepdims=True), NEG_INF))
    p_c = e_c * (1.0 / jnp.maximum(jnp.sum(e_c, axis=0, keepdims=True), 1e-30))
    o_c = _dot(vc_ref[0, 0, 0].T, p_c.astype(BF16))[:HEAD_DIM]

    p_sum = p_c[:, 0:qt]
    for r in range(1, NSA_GROUP):
        p_sum = p_sum + p_c[:, r * qt:(r + 1) * qt]
    p_hi = p_sum.astype(BF16)
    p_lo = (p_sum - p_hi.astype(F32)).astype(BF16)
    imp = _dot(ovt_ref[...], p_hi) + _dot(ovt_ref[...], p_lo)
    shape = (MAX_SEL_BLOCKS, qt)
    blk = lax.broadcasted_iota(jnp.int32, shape, 0)
    back = ((t0 + lax.broadcasted_iota(jnp.int32, shape, 1)) >> SEL_SHIFT) - blk
    forced = (blk == 0) | ((back >= 0) & (back < N_LOCAL_SEL))
    score = jnp.where(forced, BIG, jnp.where(back >= 0, imp, -BIG))

    def pick(_, work):
        m = jnp.max(work, axis=0, keepdims=True)
        first = jnp.min(jnp.where(work == m, blk, MAX_SEL_BLOCKS), axis=0, keepdims=True)
        return jnp.where(blk == first, -jnp.inf, work)

    sel = jnp.where(lax.fori_loop(0, n_pick, pick, score) == -jnp.inf, 1.0, 0.0)
    sel_bias = ((sel - 1.0) * -NEG_INF).astype(BF16)
    q_sel = jnp.concatenate([q_t, jnp.concatenate([sel_bias] * NSA_GROUP, axis=1)], axis=0)

    n_own = max(1, qt // tk)
    n_full = t0 // tk
    blocks_per_tile = tk // SEL_BLOCK
    block_used = jnp.max(sel, axis=1, keepdims=True)
    n_visit = jnp.int32(0)
    for j in range(tiles_ref.shape[0]):
        used = jnp.max(block_used[j * blocks_per_tile:(j + 1) * blocks_per_tile]) > 0.0
        tiles_ref[n_visit] = j
        n_visit = n_visit + (used & (j < n_full)).astype(jnp.int32)

    def sweep(tiles, carry, causal):
        m_i, acc = carry
        starts = [pl.multiple_of(j * tk, tk) for j in tiles]
        scores = []
        for k0 in starts:
            s = _dot(ks_ref[0, 0, pl.ds(k0, tk), :], q_sel)
            if causal:
                s = s + per_head(k0 + lax.broadcasted_iota(jnp.int32, (tk, 1), 0) <= tq, NEG_INF)
            scores.append(s)
        m_new = m_i
        for s in scores:
            m_new = jnp.maximum(m_new, jnp.max(s, axis=0, keepdims=True))
        acc = jnp.exp(m_i - m_new) * acc
        for s, k0 in zip(scores, starts):
            acc = acc + _dot(vs_ref[0, 0, 0, pl.ds(k0, tk), :].T, jnp.exp(s - m_new).astype(BF16))
        return m_new, acc

    carry = (jnp.full((1, cols), NEG_INF, F32), jnp.zeros((2 * HEAD_DIM, cols), F32))
    done = jnp.int32(0)
    for width in (1, 2, 4):
        steps = n_visit // width if width == 4 else (n_visit // width) & 1
        carry = lax.fori_loop(
            0, steps,
            lambda i, c, width=width, done=done: sweep(
                [tiles_ref[done + width * i + u] for u in range(width)], c, causal=False),
            carry)
        done = done + steps * width
    carry = sweep([n_full + j for j in range(n_own)], carry, causal=True)
    o_s = _normalise_t(carry[1])

    band = WINDOW + qt
    w0 = pl.multiple_of(jnp.maximum(t0 - WINDOW, 0), math.gcd(qt, WINDOW))
    kpos = w0 + lax.broadcasted_iota(jnp.int32, (band, 1), 0)
    valid_w = (kpos <= tq) & (kpos > tq - WINDOW)
    logit_w = _dot(kw_ref[0, 0, 0, pl.ds(w0, band), :], q_t) + per_head(valid_w, NEG_INF)
    e_w = jnp.exp(logit_w - jnp.max(logit_w, axis=0, keepdims=True))
    o_w = _normalise_t(_dot(vw_ref[0, 0, 0, pl.ds(w0, band), :].T, e_w.astype(BF16)))

    gate_t = jax.nn.sigmoid(gl_ref[0]).T

    def branch_gate(branch):
        first = branch * NSA_GROUP
        return jnp.concatenate([gate_t[first + r:first + r + 1] for r in range(NSA_GROUP)], axis=1)

    o_t = branch_gate(0) * o_c + branch_gate(1) * o_s + branch_gate(2) * o_w
    o = o_t.T
    o_ref[0] = jnp.concatenate([o[r * qt:(r + 1) * qt] for r in range(NSA_GROUP)], axis=1).astype(o_ref.dtype)


def _nsa_attention(qn, kv_cmp, ks, kv3, gl):
    b, _, s, _ = qn.shape
    nc = kv_cmp.shape[3]
    nsel = s // SEL_BLOCK
    n_pick = min(N_SELECT, nsel)
    tk = min(SEL_KV_TILE, s)
    aug = 2 * HEAD_DIM
    qt = min(ATTN_Q_TILE, s)
    assert qt % tk == 0 or tk % qt == 0
    cmp_start = jnp.arange(nc) * CMP_STRIDE
    sel_start = jnp.arange(nsel) * SEL_BLOCK
    overlap = (jnp.minimum(cmp_start[:, None] + CMP_BLOCK, sel_start[None, :] + SEL_BLOCK)
               > jnp.maximum(cmp_start[:, None], sel_start[None, :]))
    overlap_t = jnp.pad(overlap.T.astype(BF16), ((0, MAX_SEL_BLOCKS - nsel), (0, 0)))
    kv_spec = lambda a: pl.BlockSpec((1, 1, 1, s, aug), lambda bi, g, i, a=a: (a, bi, g, 0, 0))
    cmp_spec = lambda a: pl.BlockSpec((1, 1, 1, nc, aug), lambda bi, g, i, a=a: (a, bi, g, 0, 0))
    return pl.pallas_call(
        functools.partial(_attn_kernel, n_pick=n_pick, tk=tk),
        grid=(b, NSA_KV_HEADS, s // qt),
        in_specs=[
            pl.BlockSpec((1, NSA_GROUP, qt, aug), lambda bi, g, i: (bi, g, i, 0)),
            cmp_spec(0), cmp_spec(1),
            pl.BlockSpec((1, 1, s, aug + MAX_SEL_BLOCKS), lambda bi, g, i: (bi, g, 0, 0)),
            kv_spec(0), kv_spec(1), kv_spec(2),
            pl.BlockSpec((1, qt, V7X_LANES), lambda bi, g, i: (bi, i, g)),
            pl.BlockSpec((MAX_SEL_BLOCKS, nc), lambda bi, g, i: (0, 0)),
        ],
        out_specs=pl.BlockSpec((1, qt, NSA_GROUP * HEAD_DIM), lambda bi, g, i: (bi, i, g)),
        out_shape=jax.ShapeDtypeStruct((b, s, NSA_Q_COLS), BF16),
        scratch_shapes=[pltpu.SMEM((s // tk,), jnp.int32)],
        compiler_params=_params(("parallel", "parallel", "arbitrary")),
        name="nsa_attention",
    )(qn, kv_cmp, kv_cmp, ks, kv3, kv3, kv3, gl, overlap_t)


def _retention_kernel(q_ref, k_ref, v_ref, g_ref, decay_ref, xi_ref, zeta_ref, gch_ref, same_ref, gn_ref,
                      o_ref, state_ref):
    @pl.when(pl.program_id(1) == 0)
    def _():
        state_ref[...] = jnp.zeros_like(state_ref)

    pair = 2 * HEAD_DIM
    c = q_ref.shape[1]
    scale = jnp.asarray(HEAD_DIM ** -0.5, BF16)
    first = lax.broadcasted_iota(jnp.int32, (c, pair), 1) < HEAD_DIM
    same = same_ref[...]
    mean_w = (same * (1.0 / HEAD_DIM)).astype(BF16)

    def head_mean(x):
        hi = x.astype(BF16)
        lo = (x - hi.astype(F32)).astype(BF16)
        return _dot(hi, mean_w) + _dot(lo, mean_w)

    def stack(x):
        zero = jnp.zeros_like(x)
        return jnp.concatenate([jnp.where(first, x, zero), jnp.where(first, zero, x)], axis=0)

    for i in range(q_ref.shape[0]):
        outs = []
        for p in range(RET_HEADS // 2):
            cols = slice(p * pair, (p + 1) * pair)
            q = q_ref[i, :, cols]
            k = k_ref[i, :, cols] * scale
            v = v_ref[i, :, cols]
            state = state_ref[i, p]
            inner = _dot_nt(q, stack(k)) * decay_ref[p]
            o = _dot(inner.astype(BF16), stack(v)) + _dot(q, state.astype(BF16)) * xi_ref[p]
            kz = (k.astype(F32) * zeta_ref[p]).astype(BF16)
            state_ref[i, p] = gch_ref[p] * state + same * _dot_tn(kz, v)
            centred = o - head_mean(o)
            outs.append(centred * lax.rsqrt(head_mean(jnp.square(centred)) + EPS))
        o = jnp.concatenate(outs, axis=1) * gn_ref[...]
        o_ref[i] = (jax.nn.silu(g_ref[i].astype(F32)) * o).astype(o_ref.dtype)


def _retention(ret, gn_gain):
    b, s, _ = ret.shape
    c = RET_CHUNK
    hh = RET_HEADS
    pair = 2 * HEAD_DIM
    log_gamma = jnp.log1p(-jnp.exp2(-5.0 - jnp.arange(hh, dtype=F32)))
    pos = jnp.arange(c, dtype=F32)
    diff = pos[:, None] - pos[None, :]
    decay = jnp.where(diff >= 0, jnp.exp(jnp.maximum(diff, 0.0)[None] * log_gamma[:, None, None]), 0.0)
    xi = jnp.exp((pos + 1.0)[None] * log_gamma[:, None])[..., None]
    zeta = jnp.exp((c - 1.0 - pos)[None] * log_gamma[:, None])[..., None]
    g_chunk = jnp.exp(c * log_gamma)[:, None, None]
    side_by_side = lambda t: jnp.concatenate([t[0::2], t[1::2]], axis=-1)
    decay = side_by_side(decay)
    xi = side_by_side(jnp.broadcast_to(xi, (hh, c, HEAD_DIM)))
    zeta = side_by_side(jnp.broadcast_to(zeta, (hh, c, HEAD_DIM)))
    head_of = jnp.arange(pair) // HEAD_DIM
    same = (head_of[:, None] == head_of[None, :]).astype(F32)
    g_lane = side_by_side(jnp.broadcast_to(g_chunk, (hh, 1, HEAD_DIM)))
    g_chunk = same[None] * g_lane
    nb = math.gcd(RET_BATCH, b)
    part = lambda a: pl.BlockSpec((nb, c, RET_COLS), lambda bi, n, a=a: (bi, n, a))
    full = lambda shape: pl.BlockSpec(shape, lambda bi, n: (0,) * len(shape))
    return pl.pallas_call(
        _retention_kernel,
        grid=(b // nb, s // c),
        in_specs=[part(0), part(1), part(2), part(3),
                  full((hh // 2, c, 2 * c)), full((hh // 2, c, pair)), full((hh // 2, c, pair)),
                  full((hh // 2, pair, pair)), full((pair, pair)), full((1, RET_COLS))],
        out_specs=pl.BlockSpec((nb, c, RET_COLS), lambda bi, n: (bi, n, 0)),
        out_shape=jax.ShapeDtypeStruct((b, s, RET_COLS), BF16),
        scratch_shapes=[pltpu.VMEM((nb, hh // 2, pair, pair), F32)],
        compiler_params=_params(("parallel", "arbitrary")),
        name="retention",
    )(ret, ret, ret, ret, decay, xi, zeta, g_chunk, same, gn_gain.reshape(1, RET_COLS).astype(F32))


def _out_proj_kernel(h_ref, a_ref, r_ref, w_ref, g_ref, h_out_ref, hn_ref):
    h = h_ref[...] + _dot(a_ref[...], w_ref[:NSA_Q_COLS]) + _dot(r_ref[...], w_ref[NSA_Q_COLS:])
    h_out_ref[...] = h
    hn_ref[...] = _rms(h, g_ref[...]).astype(BF16)


def _out_proj(h, a, r, w, g):
    t, d = h.shape
    tm = min(STREAM_ROW_TILE, t)
    row = lambda n: pl.BlockSpec((tm, n), lambda i: (i, 0))
    return pl.pallas_call(
        _out_proj_kernel,
        grid=(t // tm,),
        in_specs=[row(d), row(NSA_Q_COLS), row(RET_COLS),
                  pl.BlockSpec((MIX_WIDTH, d), lambda i: (0, 0)),
                  pl.BlockSpec((1, d), lambda i: (0, 0))],
        out_specs=[row(d), row(d)],
        out_shape=[jax.ShapeDtypeStruct((t, d), F32), jax.ShapeDtypeStruct((t, d), BF16)],
        compiler_params=_params(("parallel",)),
        name="out_proj",
    )(h, a, r, w.astype(BF16), g.reshape(1, d))


def _swiglu_chunk(x, wg, wu, wd):
    hid = jax.nn.silu(_dot(x, wg)) * _dot(x, wu)
    return _dot(hid.astype(BF16), wd)


def _ffn_kernel(x_ref, wg_ref, wu_ref, wd_ref, o_ref):
    x = x_ref[...]
    acc = None
    for f in range(D_FF // FFN_COL_TILE):
        cols = slice(f * FFN_COL_TILE, (f + 1) * FFN_COL_TILE)
        part = _swiglu_chunk(x, wg_ref[:, cols], wu_ref[:, cols], wd_ref[cols, :])
        acc = part if acc is None else acc + part
    o_ref[...] = acc


def _dense_ffn(hn, wg, wu, wd):
    t, d = hn.shape
    tm = min(FFN_ROW_TILE, t)
    once = pl.Buffered(1)
    return pl.pallas_call(
        _ffn_kernel,
        grid=(t // tm,),
        in_specs=[pl.BlockSpec((tm, d), lambda i: (i, 0)),
                  pl.BlockSpec((d, D_FF), lambda i: (0, 0), pipeline_mode=once),
                  pl.BlockSpec((d, D_FF), lambda i: (0, 0), pipeline_mode=once),
                  pl.BlockSpec((D_FF, d), lambda i: (0, 0), pipeline_mode=once)],
        out_specs=pl.BlockSpec((tm, d), lambda i: (i, 0)),
        out_shape=jax.ShapeDtypeStruct((t, d), F32),
        compiler_params=_params(("parallel",)),
        name="dense_ffn",
    )(hn, wg.astype(BF16), wu.astype(BF16), wd.astype(BF16))


def _lane_column(table, lane_index):
    lane = lax.broadcasted_iota(jnp.int32, table.shape, 1)
    col = jnp.sum(jnp.where(lane == lane_index, table, 0.0), axis=-1, keepdims=True)
    return jnp.broadcast_to(col, table.shape)


def _moe_kernel(x_ref, router_ref, wg_ref, wu_ref, wd_ref, o_ref,
                xs_ref, y_ref, slot_ref, gate_ref, slot_row_ref, slot_e_ref, gate_e_ref, count_ref):
    e = pl.program_id(1)
    f = pl.program_id(2)
    tm, d = x_ref.shape
    sub = MOE_GROUP_ROWS

    @pl.when((e == 0) & (f == 0))
    def _route():
        o_ref[...] = jnp.zeros_like(o_ref)
        logits = _dot(x_ref[...], router_ref[...])
        lane = lax.broadcasted_iota(jnp.int32, logits.shape, 1)
        logits = jnp.where(lane < N_EXPERTS, logits, -jnp.inf)
        v1 = jnp.max(logits, axis=-1, keepdims=True)
        i1 = jnp.min(jnp.where(logits == v1, lane, V7X_LANES), axis=-1, keepdims=True)
        rest = jnp.where(lane == i1, -jnp.inf, logits)
        v2 = jnp.max(rest, axis=-1, keepdims=True)
        i2 = jnp.min(jnp.where(rest == v2, lane, V7X_LANES), axis=-1, keepdims=True)
        e2 = jnp.exp(v2 - v1)
        inv = 1.0 / (1.0 + e2)
        gate_ref[...] = jnp.where(lane == i1, inv, 0.0) + jnp.where(lane == i2, e2 * inv, 0.0)
        routed = jnp.where((lane == i1) | (lane == i2), 1.0, 0.0)
        c = MOE_RANK_CHUNK
        before = (lax.broadcasted_iota(jnp.int32, (c, c), 1) < lax.broadcasted_iota(jnp.int32, (c, c), 0))
        before = jnp.where(before, 1.0, 0.0).astype(BF16)
        offset = jnp.zeros((1, V7X_LANES), F32)
        for j in range(tm // c):
            part = routed[j * c:(j + 1) * c]
            rank = _dot(before, part.astype(BF16)) + offset
            slot_ref[j * c:(j + 1) * c, :] = jnp.where(part > 0.0, rank, -1.0)
            offset = offset + jnp.sum(part, axis=0, keepdims=True)
        for ee in range(N_EXPERTS):
            count_ref[ee] = offset[0, ee].astype(jnp.int32)
        slot_row_ref[...] = slot_ref[...].T[:N_EXPERTS]

    n_groups = (count_ref[e] + sub - 1) // sub

    @pl.when(f == 0)
    def _gather():
        slot_e_ref[...] = _lane_column(slot_ref[...], e)
        gate_e_ref[...] = _lane_column(gate_ref[...], e)
        slot_row = slot_row_ref[pl.ds(e, 1), :]

        def body(s, _):
            r0 = pl.multiple_of(s * sub, sub)
            want = (r0 + lax.broadcasted_iota(jnp.int32, (sub, 1), 0)).astype(F32)
            onehot = jnp.where(slot_row == want, 1.0, 0.0).astype(BF16)
            xs_ref[pl.ds(r0, sub), :] = _dot(onehot, x_ref[...]).astype(BF16)
            y_ref[pl.ds(r0, sub), :] = jnp.zeros((sub, d), F32)
            return 0

        lax.fori_loop(0, n_groups, body, 0)

    def expert(i, _, base, n):
        r0 = pl.multiple_of(base + i * n, n)
        y_ref[pl.ds(r0, n), :] += _swiglu_chunk(xs_ref[pl.ds(r0, n), :], wg_ref[0, 0], wu_ref[0, 0], wd_ref[0])
        return 0

    tail = MOE_TAIL_ROWS
    n_filled = count_ref[e] // sub
    rest = count_ref[e] - n_filled * sub
    short = rest <= tail
    n_whole = jnp.where(short, n_filled, n_filled + 1)
    n_tail = jnp.where(short & (rest > 0), 1, 0)
    lax.fori_loop(0, n_whole, functools.partial(expert, base=0, n=sub), 0)
    lax.fori_loop(0, n_tail, functools.partial(expert, base=n_filled * sub, n=tail), 0)

    @pl.when(f == pl.num_programs(2) - 1)
    def _scatter():
        tc = MOE_SCATTER_ROWS

        def body(s, _):
            r0 = pl.multiple_of(s * sub, sub)
            y = y_ref[pl.ds(r0, sub), :].astype(BF16)
            want = (r0 + lax.broadcasted_iota(jnp.int32, (1, sub), 1)).astype(F32)
            for j in range(tm // tc):
                rows = slice(j * tc, (j + 1) * tc)
                slot = jnp.concatenate([slot_e_ref[rows, :]] * (sub // V7X_LANES), axis=1)
                onehot = jnp.where(slot == want, 1.0, 0.0).astype(BF16)
                weight = jnp.concatenate([gate_e_ref[rows, :]] * (d // V7X_LANES), axis=1)
                o_ref[rows, :] += weight * _dot(onehot, y)
            return 0

        lax.fori_loop(0, n_groups, body, 0)


def _moe_ffn(hn, router, wg, wu, wd):
    t, d = hn.shape
    tm = min(MOE_ROW_TILE, t)
    tf = MOE_COL_TILE
    nf = D_FF // tf
    router = jnp.pad(router, ((0, 0), (0, V7X_LANES - N_EXPERTS))).astype(BF16)
    chunked = lambda w: jnp.transpose(w.astype(BF16).reshape(N_EXPERTS, d, nf, tf), (0, 2, 1, 3))
    once = pl.Buffered(1)
    return pl.pallas_call(
        _moe_kernel,
        grid=(t // tm, N_EXPERTS, nf),
        in_specs=[pl.BlockSpec((tm, d), lambda i, e, f: (i, 0), pipeline_mode=once),
                  pl.BlockSpec((d, V7X_LANES), lambda i, e, f: (0, 0), pipeline_mode=once),
                  pl.BlockSpec((1, 1, d, tf), lambda i, e, f: (e, f, 0, 0)),
                  pl.BlockSpec((1, 1, d, tf), lambda i, e, f: (e, f, 0, 0)),
                  pl.BlockSpec((1, tf, d), lambda i, e, f: (e, f, 0))],
        out_specs=pl.BlockSpec((tm, d), lambda i, e, f: (i, 0)),
        out_shape=jax.ShapeDtypeStruct((t, d), F32),
        scratch_shapes=[pltpu.VMEM((tm, d), BF16), pltpu.VMEM((tm, d), F32),
                        pltpu.VMEM((tm, V7X_LANES), F32), pltpu.VMEM((tm, V7X_LANES), F32),
                        pltpu.VMEM((N_EXPERTS, tm), F32),
                        pltpu.VMEM((tm, V7X_LANES), F32), pltpu.VMEM((tm, V7X_LANES), F32),
                        pltpu.SMEM((N_EXPERTS,), jnp.int32)],
        compiler_params=_params(("parallel", "arbitrary", "arbitrary"), vmem=MOE_VMEM_LIMIT),
        name="moe_ffn",
    )(hn, router, chunked(wg), chunked(wu), wd.astype(BF16))


def _ple_kernel(h_ref, f_ref, p_ref, g_ref, proj_ref, gate_ref, gf_ref, o_ref, *, final_norm):
    h = h_ref[...] + f_ref[...]
    emb = _dot(p_ref[...].astype(BF16), proj_ref[...])
    sig = jax.nn.sigmoid(_dot(_rms(h, g_ref[...]).astype(BF16), gate_ref[...]))
    h = h + emb * sig
    if final_norm:
        h = _rms(h, gf_ref[...])
    o_ref[...] = h


def _ple(h, f, p, g, proj, gate, g_final, final_norm):
    t, d = h.shape
    tm = min(STREAM_ROW_TILE, t)
    vec = pl.BlockSpec((1, d), lambda i: (0, 0))
    return pl.pallas_call(
        functools.partial(_ple_kernel, final_norm=final_norm),
        grid=(t // tm,),
        in_specs=[pl.BlockSpec((tm, d), lambda i: (i, 0)),
                  pl.BlockSpec((tm, d), lambda i: (i, 0)),
                  pl.BlockSpec((tm, PLE_DIM), lambda i: (i, 0)),
                  vec,
                  pl.BlockSpec((PLE_DIM, d), lambda i: (0, 0)),
                  pl.BlockSpec((d, d), lambda i: (0, 0)),
                  vec],
        out_specs=pl.BlockSpec((tm, d), lambda i: (i, 0)),
        out_shape=jax.ShapeDtypeStruct((t, d), F32),
        compiler_params=_params(("parallel",)),
        name="ple",
    )(h, f, p, g.reshape(1, d), proj.astype(BF16), gate.astype(BF16), g_final.reshape(1, d))


def kernel(x, p, w_in, w_out, g_mix, g_ffn, g_ple, g_final, cmp_pos, cmp_w1, cmp_w2, ret_gn,
           ffn_gate, ffn_up, ffn_down, moe_router, moe_gate, moe_up, moe_down, ple_proj, ple_gate):
    b, s, d = x.shape
    depth = w_in.shape[0]
    t = b * s
    h = x
    for i in range(depth):
        qn, kcv, ks, kv3, gl, ret = _in_proj(h.reshape(b, s, d), g_mix[i], _arrange_w_in(w_in[i]))
        kv_cmp = _compress(kcv, cmp_pos[i], cmp_w1[i], cmp_w2[i])
        a = _nsa_attention(qn, kv_cmp, ks, kv3, gl)
        r = _retention(ret, ret_gn[i])
        h, hn = _out_proj(h.reshape(t, d), a.reshape(t, NSA_Q_COLS), r.reshape(t, RET_COLS), w_out[i], g_ffn[i])
        if i % 2 == 0:
            f = _dense_ffn(hn, ffn_gate[i // 2], ffn_up[i // 2], ffn_down[i // 2])
        else:
            f = _moe_ffn(hn, moe_router[i // 2], moe_gate[i // 2], moe_up[i // 2], moe_down[i // 2])
        h = _ple(h, f, p[i].reshape(t, PLE_DIM), g_ple[i], ple_proj[i], ple_gate[i], g_final, i == depth - 1)
    return h.reshape(b, s, d)
```

```python
import functools
import math

import jax
import jax.numpy as jnp
from jax import lax
from jax.experimental import pallas as pl
from jax.experimental.pallas import tpu as pltpu

F32 = jnp.float32
BF16 = jnp.bfloat16

D_MODEL = 1024
HEAD_DIM = 64
NSA_HEADS = 8
NSA_KV_HEADS = 2
NSA_GROUP = NSA_HEADS // NSA_KV_HEADS
RET_HEADS = 8
CMP_BLOCK = 32
CMP_STRIDE = 16
CMP_HIDDEN = 256
SEL_BLOCK = 64
N_SELECT = 16
N_LOCAL_SEL = 2
WINDOW = 512
RET_CHUNK = 128
D_FF = 3584
N_EXPERTS = 8
PLE_DIM = 256
EPS = 1e-6
NEG_INF = -1e30
BIG = 1e9

NSA_Q_COLS = NSA_HEADS * HEAD_DIM
NSA_KV_COLS = NSA_KV_HEADS * HEAD_DIM
NSA_GATE_COLS = 3 * NSA_HEADS
RET_COLS = RET_HEADS * HEAD_DIM
MIX_WIDTH = NSA_Q_COLS + RET_COLS

V7X_LANES = 128
V7X_SUBLANES = 8
V7X_VMEM_BYTES = 64 * 1024 * 1024
VMEM_LIMIT = V7X_VMEM_BYTES * 3 // 4

ROW_TILE = 512
STREAM_ROW_TILE = 1024
ATTN_Q_TILE = 256
SEL_KV_TILE = 256
RET_BATCH = 4
RET_HEAD_GROUP = 4
FFN_ROW_TILE = 512
FFN_COL_TILE = 512
MOE_COL_TILE = 1792
MOE_ROW_TILE = 2048
MOE_GROUP_ROWS = 256
MOE_TAIL_ROWS = 64
MOE_RANK_CHUNK = 256
MOE_SCATTER_ROWS = 512
MOE_VMEM_LIMIT = V7X_VMEM_BYTES * 7 // 8

POS_SHIFT = 6
POS_SPLIT = 1 << POS_SHIFT
SEL_SHIFT = SEL_BLOCK.bit_length() - 1
MAX_SEL_BLOCKS = V7X_LANES


def _params(sem, vmem=VMEM_LIMIT):
    return pltpu.CompilerParams(dimension_semantics=sem, vmem_limit_bytes=vmem)


def _dot(a, b):
    return jnp.dot(a, b, preferred_element_type=F32)


def _dot_nt(a, b):
    return lax.dot_general(a, b, (((1,), (1,)), ((), ())), preferred_element_type=F32)


def _dot_tn(a, b):
    return lax.dot_general(a, b, (((0,), (0,)), ((), ())), preferred_element_type=F32)


def _rms(x, g):
    return x * lax.rsqrt(jnp.mean(x * x, axis=-1, keepdims=True) + EPS) * g


def _lane_features(shape, first, second):
    lane = lax.broadcasted_iota(jnp.int32, shape, 1)
    return jnp.where(lane == 0, first, jnp.where(lane == 1, second, 0))


def _key_features(pos, width):
    return _lane_features((pos.shape[0], width), pos >> POS_SHIFT, pos & (POS_SPLIT - 1))


def _ones_features(n, width):
    return _lane_features((n, width), 1, 0)


def _int_to_bf16(x):
    return x.astype(F32).astype(BF16)


def _in_proj_kernel(h_ref, g_ref, w_ref, qn_ref, kcv_ref, ks_ref, kv3_ref, gl_ref, ret_ref):
    tm = h_ref.shape[1]
    xn = _rms(h_ref[0], g_ref[...]).astype(BF16)
    pair = 2 * HEAD_DIM
    pos = pl.program_id(1) * tm + lax.broadcasted_iota(jnp.int32, (tm, 1), 0)
    kfeat = _int_to_bf16(_key_features(pos, HEAD_DIM))
    vfeat = _int_to_bf16(_ones_features(tm, HEAD_DIM))
    lane = lax.broadcasted_iota(jnp.int32, (tm, MAX_SEL_BLOCKS), 1)
    block_onehot = jnp.where(lane == (pos >> SEL_SHIFT), 1.0, 0.0).astype(BF16)

    wide = 2 * pair
    per_dot = wide // HEAD_DIM
    for j in range(NSA_Q_COLS // wide):
        z = (_dot(xn, w_ref[:, j * wide:(j + 1) * wide]) * HEAD_DIM ** -0.5).astype(BF16)
        for k in range(per_dot):
            head = per_dot * j + k
            slope = 2.0 ** -(head + 1)
            qfeat = _lane_features((tm, HEAD_DIM), POS_SPLIT * slope, slope).astype(BF16)
            qn_ref[0, head] = jnp.concatenate([z[:, k * HEAD_DIM:(k + 1) * HEAD_DIM], qfeat], axis=1)
    base = NSA_Q_COLS
    z = _dot(xn, w_ref[:, base:base + wide]).astype(BF16)
    for a in range(2):
        kcv_ref[a, 0] = z[:, a * pair:(a + 1) * pair]
    base += wide
    for c in range(2):
        z = _dot(xn, w_ref[:, base + c * wide:base + (c + 1) * wide]).astype(BF16)
        for half in range(2):
            a = 2 * c + half
            for g in range(NSA_KV_HEADS):
                zg = z[:, half * pair + g * HEAD_DIM:half * pair + (g + 1) * HEAD_DIM]
                if a == 0:
                    ks_ref[0, g] = jnp.concatenate([zg, kfeat, block_onehot], axis=1)
                else:
                    kv3_ref[a - 1, 0, g] = jnp.concatenate([zg, kfeat if a == 2 else vfeat], axis=1)
    base += 2 * wide
    for a in range(4):
        ret_ref[0, :, a * RET_COLS:(a + 1) * RET_COLS] = _dot(
            xn, w_ref[:, base + a * RET_COLS:base + (a + 1) * RET_COLS]).astype(BF16)
    base += 4 * RET_COLS
    gl_ref[0] = _dot(xn, w_ref[:, base:base + NSA_KV_HEADS * V7X_LANES])


def _arrange_w_in(w):
    q_end = NSA_Q_COLS
    kv_end = q_end + 6 * NSA_KV_COLS
    gl_end = kv_end + NSA_GATE_COLS
    gl = w[:, kv_end:gl_end].reshape(D_MODEL, 3, NSA_KV_HEADS, NSA_GROUP)
    gl = jnp.transpose(gl, (0, 2, 1, 3)).reshape(D_MODEL, NSA_KV_HEADS, 3 * NSA_GROUP)
    gl = jnp.pad(gl, ((0, 0), (0, 0), (0, V7X_LANES - 3 * NSA_GROUP)))
    gl = gl.reshape(D_MODEL, NSA_KV_HEADS * V7X_LANES)
    return jnp.concatenate([w[:, :kv_end], w[:, gl_end:], gl], axis=1).astype(BF16)


def _in_proj(h, g, w):
    b, s, d = h.shape
    assert s // SEL_BLOCK <= MAX_SEL_BLOCKS
    tm = min(ROW_TILE, s)
    ncols = w.shape[1]
    aug = 2 * HEAD_DIM
    return pl.pallas_call(
        _in_proj_kernel,
        grid=(b, s // tm),
        in_specs=[
            pl.BlockSpec((1, tm, d), lambda bi, si: (bi, si, 0)),
            pl.BlockSpec((1, d), lambda bi, si: (0, 0)),
            pl.BlockSpec((d, ncols), lambda bi, si: (0, 0)),
        ],
        out_specs=[
            pl.BlockSpec((1, NSA_HEADS, tm, aug), lambda bi, si: (bi, 0, si, 0)),
            pl.BlockSpec((2, 1, tm, NSA_KV_COLS), lambda bi, si: (0, bi, si, 0)),
            pl.BlockSpec((1, NSA_KV_HEADS, tm, aug + MAX_SEL_BLOCKS), lambda bi, si: (bi, 0, si, 0)),
            pl.BlockSpec((3, 1, NSA_KV_HEADS, tm, aug), lambda bi, si: (0, bi, 0, si, 0)),
            pl.BlockSpec((1, tm, NSA_KV_HEADS * V7X_LANES), lambda bi, si: (bi, si, 0)),
            pl.BlockSpec((1, tm, 4 * RET_COLS), lambda bi, si: (bi, si, 0)),
        ],
        out_shape=[
            jax.ShapeDtypeStruct((b, NSA_HEADS, s, aug), BF16),
            jax.ShapeDtypeStruct((2, b, s, NSA_KV_COLS), BF16),
            jax.ShapeDtypeStruct((b, NSA_KV_HEADS, s, aug + MAX_SEL_BLOCKS), BF16),
            jax.ShapeDtypeStruct((3, b, NSA_KV_HEADS, s, aug), BF16),
            jax.ShapeDtypeStruct((b, s, NSA_KV_HEADS * V7X_LANES), F32),
            jax.ShapeDtypeStruct((b, s, 4 * RET_COLS), BF16),
        ],
        compiler_params=_params(("parallel", "parallel")),
        name="in_proj",
    )(h, g.reshape(1, d), w)


def _compress_kernel(x_ref, wbig_ref, pos_ref, w1_ref, w2_ref, o_ref):
    nch = x_ref.shape[2]
    u = _dot(x_ref[0, 0], wbig_ref[0])
    hid0 = _dot(pos_ref[0], w1_ref[0])[0:1]
    w2 = w2_ref[0]
    cmp_end = lax.broadcasted_iota(jnp.int32, (nch, 1), 0) * CMP_STRIDE + (CMP_BLOCK - 1)
    feat = _int_to_bf16(jnp.where(pl.program_id(0) == 0, _key_features(cmp_end, HEAD_DIM),
                                  _ones_features(nch, HEAD_DIM)))
    for g in range(NSA_KV_HEADS):
        c0 = g * 2 * CMP_HIDDEN
        first = u[:, c0:c0 + CMP_HIDDEN]
        second = u[:, c0 + CMP_HIDDEN:c0 + 2 * CMP_HIDDEN]
        hid = first + pltpu.roll(second, nch - 1, 0) + hid0
        out = _dot(jax.nn.gelu(hid).astype(BF16), w2).astype(BF16)
        o_ref[0, 0, g] = jnp.concatenate([out, feat], axis=1)


def _arrange_cmp_w1(w1):
    r = CMP_BLOCK // CMP_STRIDE
    w1r = w1.reshape(2, r, CMP_STRIDE, HEAD_DIM, CMP_HIDDEN)
    eye = jnp.eye(NSA_KV_HEADS, dtype=w1.dtype)
    big = jnp.einsum('krcdh,gf->kcgdfrh', w1r, eye)
    return big.reshape(2, CMP_STRIDE * NSA_KV_COLS, NSA_KV_HEADS * r * CMP_HIDDEN).astype(BF16)


def _compress(kcv, cmp_pos, cmp_w1, cmp_w2):
    _, b, s, _ = kcv.shape
    nch = s // CMP_STRIDE
    x = kcv.reshape(2, b, nch, CMP_STRIDE * NSA_KV_COLS)
    wbig = _arrange_cmp_w1(cmp_w1)
    pos = jnp.broadcast_to(cmp_pos.reshape(2, 1, CMP_BLOCK * HEAD_DIM),
                           (2, V7X_SUBLANES, CMP_BLOCK * HEAD_DIM)).astype(BF16)
    w1 = cmp_w1.reshape(2, CMP_BLOCK * HEAD_DIM, CMP_HIDDEN).astype(BF16)
    w2 = cmp_w2.astype(BF16)
    kdim = CMP_STRIDE * NSA_KV_COLS
    return pl.pallas_call(
        _compress_kernel,
        grid=(2, b),
        in_specs=[
            pl.BlockSpec((1, 1, nch, kdim), lambda a, bi: (a, bi, 0, 0)),
            pl.BlockSpec((1, kdim, wbig.shape[2]), lambda a, bi: (a, 0, 0)),
            pl.BlockSpec((1, V7X_SUBLANES, CMP_BLOCK * HEAD_DIM), lambda a, bi: (a, 0, 0)),
            pl.BlockSpec((1, CMP_BLOCK * HEAD_DIM, CMP_HIDDEN), lambda a, bi: (a, 0, 0)),
            pl.BlockSpec((1, CMP_HIDDEN, HEAD_DIM), lambda a, bi: (a, 0, 0)),
        ],
        out_specs=pl.BlockSpec((1, 1, NSA_KV_HEADS, nch, 2 * HEAD_DIM), lambda a, bi: (a, bi, 0, 0, 0)),
        out_shape=jax.ShapeDtypeStruct((2, b, NSA_KV_HEADS, nch, 2 * HEAD_DIM), BF16),
        compiler_params=_params(("parallel", "parallel")),
        name="compress",
    )(x, wbig, pos, w1, w2)


def _normalise_t(acc):
    return acc[:HEAD_DIM] * (1.0 / jnp.maximum(acc[HEAD_DIM:HEAD_DIM + 1], 1e-30))


def _attn_kernel(q_ref, kc_ref, vc_ref, ks_ref, vs_ref, kw_ref, vw_ref, gl_ref, ovt_ref, o_ref, tiles_ref,
                 *, n_pick, tk):
    qt = q_ref.shape[2]
    cols = NSA_GROUP * qt
    t0 = pl.program_id(2) * qt
    q_t = q_ref[0].reshape(cols, 2 * HEAD_DIM).T
    tq = t0 + lax.broadcasted_iota(jnp.int32, (1, qt), 1)

    def per_head(valid, masked):
        bias = jnp.where(valid, 0.0, masked)
        return jnp.concatenate([bias] * NSA_GROUP, axis=1)

    nc = kc_ref.shape[3]
    cmp_end = lax.broadcasted_iota(jnp.int32, (nc, 1), 0) * CMP_STRIDE + (CMP_BLOCK - 1)
    logit_c = _dot(kc_ref[0, 0, 0], q_t) + per_head(cmp_end <= tq, -jnp.inf)
    e_c = jnp.exp(logit_c - jnp.maximum(jnp.max(logit_c, axis=0, keepdims=True), NEG_INF))
    p_c = e_c * (1.0 / jnp.maximum(jnp.sum(e_c, axis=0, keepdims=True), 1e-30))
    o_c = _dot(vc_ref[0, 0, 0].T, p_c.astype(BF16))[:HEAD_DIM]

    p_sum = p_c[:, 0:qt]
    for r in range(1, NSA_GROUP):
        p_sum = p_sum + p_c[:, r * qt:(r + 1) * qt]
    p_hi = p_sum.astype(BF16)
    p_lo = (p_sum - p_hi.astype(F32)).astype(BF16)
    imp = _dot(ovt_ref[...], p_hi) + _dot(ovt_ref[...], p_lo)
    shape = (MAX_SEL_BLOCKS, qt)
    blk = lax.broadcasted_iota(jnp.int32, shape, 0)
    back = ((t0 + lax.broadcasted_iota(jnp.int32, shape, 1)) >> SEL_SHIFT) - blk
    forced = (blk == 0) | ((back >= 0) & (back < N_LOCAL_SEL))
    score = jnp.where(forced, BIG, jnp.where(back >= 0, imp, -BIG))

    def pick(_, work):
        m = jnp.max(work, axis=0, keepdims=True)
        first = jnp.min(jnp.where(work == m, blk, MAX_SEL_BLOCKS), axis=0, keepdims=True)
        return jnp.where(blk == first, -jnp.inf, work)

    sel = jnp.where(lax.fori_loop(0, n_pick, pick, score) == -jnp.inf, 1.0, 0.0)
    sel_bias = ((sel - 1.0) * -NEG_INF).astype(BF16)
    q_sel = jnp.concatenate([q_t, jnp.concatenate([sel_bias] * NSA_GROUP, axis=1)], axis=0)

    n_own = max(1, qt // tk)
    n_full = t0 // tk
    blocks_per_tile = tk // SEL_BLOCK
    block_used = jnp.max(sel, axis=1, keepdims=True)
    n_visit = jnp.int32(0)
    for j in range(tiles_ref.shape[0]):
        used = jnp.max(block_used[j * blocks_per_tile:(j + 1) * blocks_per_tile]) > 0.0
        tiles_ref[n_visit] = j
        n_visit = n_visit + (used & (j < n_full)).astype(jnp.int32)

    def sweep(tiles, carry, causal):
        m_i, acc = carry
        starts = [pl.multiple_of(j * tk, tk) for j in tiles]
        scores = []
        for k0 in starts:
            s = _dot(ks_ref[0, 0, pl.ds(k0, tk), :], q_sel)
            if causal:
                s = s + per_head(k0 + lax.broadcasted_iota(jnp.int32, (tk, 1), 0) <= tq, NEG_INF)
            scores.append(s)
        m_new = m_i
        for s in scores:
            m_new = jnp.maximum(m_new, jnp.max(s, axis=0, keepdims=True))
        acc = jnp.exp(m_i - m_new) * acc
        for s, k0 in zip(scores, starts):
            acc = acc + _dot(vs_ref[0, 0, 0, pl.ds(k0, tk), :].T, jnp.exp(s - m_new).astype(BF16))
        return m_new, acc

    carry = (jnp.full((1, cols), NEG_INF, F32), jnp.zeros((2 * HEAD_DIM, cols), F32))
    done = jnp.int32(0)
    for width in (1, 2, 4):
        steps = n_visit // width if width == 4 else (n_visit // width) & 1
        carry = lax.fori_loop(
            0, steps,
            lambda i, c, width=width, done=done: sweep(
                [tiles_ref[done + width * i + u] for u in range(width)], c, causal=False),
            carry)
        done = done + steps * width
    carry = sweep([n_full + j for j in range(n_own)], carry, causal=True)
    o_s = _normalise_t(carry[1])

    band = WINDOW + qt
    w0 = pl.multiple_of(jnp.maximum(t0 - WINDOW, 0), math.gcd(qt, WINDOW))
    kpos = w0 + lax.broadcasted_iota(jnp.int32, (band, 1), 0)
    valid_w = (kpos <= tq) & (kpos > tq - WINDOW)
    logit_w = _dot(kw_ref[0, 0, 0, pl.ds(w0, band), :], q_t) + per_head(valid_w, NEG_INF)
    e_w = jnp.exp(logit_w - jnp.max(logit_w, axis=0, keepdims=True))
    o_w = _normalise_t(_dot(vw_ref[0, 0, 0, pl.ds(w0, band), :].T, e_w.astype(BF16)))

    gate_t = jax.nn.sigmoid(gl_ref[0]).T

    def branch_gate(branch):
        first = branch * NSA_GROUP
        return jnp.concatenate([gate_t[first + r:first + r + 1] for r in range(NSA_GROUP)], axis=1)

    o_t = branch_gate(0) * o_c + branch_gate(1) * o_s + branch_gate(2) * o_w
    o = o_t.T
    o_ref[0] = jnp.concatenate([o[r * qt:(r + 1) * qt] for r in range(NSA_GROUP)], axis=1).astype(o_ref.dtype)


def _nsa_attention(qn, kv_cmp, ks, kv3, gl):
    b, _, s, _ = qn.shape
    nc = kv_cmp.shape[3]
    nsel = s // SEL_BLOCK
    n_pick = min(N_SELECT, nsel)
    tk = min(SEL_KV_TILE, s)
    aug = 2 * HEAD_DIM
    qt = min(ATTN_Q_TILE, s)
    assert qt % tk == 0 or tk % qt == 0
    cmp_start = jnp.arange(nc) * CMP_STRIDE
    sel_start = jnp.arange(nsel) * SEL_BLOCK
    overlap = (jnp.minimum(cmp_start[:, None] + CMP_BLOCK, sel_start[None, :] + SEL_BLOCK)
               > jnp.maximum(cmp_start[:, None], sel_start[None, :]))
    overlap_t = jnp.pad(overlap.T.astype(BF16), ((0, MAX_SEL_BLOCKS - nsel), (0, 0)))
    kv_spec = lambda a: pl.BlockSpec((1, 1, 1, s, aug), lambda bi, g, i, a=a: (a, bi, g, 0, 0))
    cmp_spec = lambda a: pl.BlockSpec((1, 1, 1, nc, aug), lambda bi, g, i, a=a: (a, bi, g, 0, 0))
    return pl.pallas_call(
        functools.partial(_attn_kernel, n_pick=n_pick, tk=tk),
        grid=(b, NSA_KV_HEADS, s // qt),
        in_specs=[
            pl.BlockSpec((1, NSA_GROUP, qt, aug), lambda bi, g, i: (bi, g, i, 0)),
            cmp_spec(0), cmp_spec(1),
            pl.BlockSpec((1, 1, s, aug + MAX_SEL_BLOCKS), lambda bi, g, i: (bi, g, 0, 0)),
            kv_spec(0), kv_spec(1), kv_spec(2),
            pl.BlockSpec((1, qt, V7X_LANES), lambda bi, g, i: (bi, i, g)),
            pl.BlockSpec((MAX_SEL_BLOCKS, nc), lambda bi, g, i: (0, 0)),
        ],
        out_specs=pl.BlockSpec((1, qt, NSA_GROUP * HEAD_DIM), lambda bi, g, i: (bi, i, g)),
        out_shape=jax.ShapeDtypeStruct((b, s, NSA_Q_COLS), BF16),
        scratch_shapes=[pltpu.SMEM((s // tk,), jnp.int32)],
        compiler_params=_params(("parallel", "parallel", "arbitrary")),
        name="nsa_attention",
    )(qn, kv_cmp, kv_cmp, ks, kv3, kv3, kv3, gl, overlap_t)


def _retention_kernel(q_ref, k_ref, v_ref, g_ref, decay_ref, xi_ref, zeta_ref, gch_ref, same_ref, gn_ref,
                      o_ref, state_ref):
    @pl.when(pl.program_id(1) == 0)
    def _():
        state_ref[...] = jnp.zeros_like(state_ref)

    width = same_ref.shape[0]
    heads = width // HEAD_DIM
    c = q_ref.shape[1]
    scale = jnp.asarray(HEAD_DIM ** -0.5, BF16)
    lane_head = lax.broadcasted_iota(jnp.int32, (c, width), 1) // HEAD_DIM
    same = same_ref[...]
    mean_w = (same * (1.0 / HEAD_DIM)).astype(BF16)

    def head_mean(x):
        hi = x.astype(BF16)
        lo = (x - hi.astype(F32)).astype(BF16)
        return _dot(hi, mean_w) + _dot(lo, mean_w)

    def stack(x):
        zero = jnp.zeros_like(x)
        return jnp.concatenate([jnp.where(lane_head == j, x, zero) for j in range(heads)], axis=0)

    for i in range(q_ref.shape[0]):
        outs = []
        for p in range(RET_COLS // width):
            cols = slice(p * width, (p + 1) * width)
            q = q_ref[i, :, cols]
            k = k_ref[i, :, cols] * scale
            v = v_ref[i, :, cols]
            state = state_ref[i, p]
            inner = _dot_nt(q, stack(k)) * decay_ref[p]
            o = _dot(inner.astype(BF16), stack(v)) + _dot(q, state.astype(BF16)) * xi_ref[p]
            kz = (k.astype(F32) * zeta_ref[p]).astype(BF16)
            state_ref[i, p] = gch_ref[p] * state + same * _dot_tn(kz, v)
            centred = o - head_mean(o)
            outs.append(centred * lax.rsqrt(head_mean(jnp.square(centred)) + EPS))
        o = jnp.concatenate(outs, axis=1) * gn_ref[...]
        o_ref[i] = (jax.nn.silu(g_ref[i].astype(F32)) * o).astype(o_ref.dtype)


def _retention(ret, gn_gain):
    b, s, _ = ret.shape
    c = RET_CHUNK
    hh = RET_HEADS
    hg = RET_HEAD_GROUP
    width = hg * HEAD_DIM
    log_gamma = jnp.log1p(-jnp.exp2(-5.0 - jnp.arange(hh, dtype=F32)))
    pos = jnp.arange(c, dtype=F32)
    diff = pos[:, None] - pos[None, :]
    decay = jnp.where(diff >= 0, jnp.exp(jnp.maximum(diff, 0.0)[None] * log_gamma[:, None, None]), 0.0)
    xi = jnp.exp((pos + 1.0)[None] * log_gamma[:, None])[..., None]
    zeta = jnp.exp((c - 1.0 - pos)[None] * log_gamma[:, None])[..., None]
    g_chunk = jnp.exp(c * log_gamma)[:, None, None]
    side_by_side = lambda t: jnp.concatenate([t[j::hg] for j in range(hg)], axis=-1)
    decay = side_by_side(decay)
    xi = side_by_side(jnp.broadcast_to(xi, (hh, c, HEAD_DIM)))
    zeta = side_by_side(jnp.broadcast_to(zeta, (hh, c, HEAD_DIM)))
    head_of = jnp.arange(width) // HEAD_DIM
    same = (head_of[:, None] == head_of[None, :]).astype(F32)
    g_lane = side_by_side(jnp.broadcast_to(g_chunk, (hh, 1, HEAD_DIM)))
    g_chunk = same[None] * g_lane
    nb = math.gcd(RET_BATCH, b)
    part = lambda a: pl.BlockSpec((nb, c, RET_COLS), lambda bi, n, a=a: (bi, n, a))
    full = lambda shape: pl.BlockSpec(shape, lambda bi, n: (0,) * len(shape))
    return pl.pallas_call(
        _retention_kernel,
        grid=(b // nb, s // c),
        in_specs=[part(0), part(1), part(2), part(3),
                  full((hh // hg, c, hg * c)), full((hh // hg, c, width)), full((hh // hg, c, width)),
                  full((hh // hg, width, width)), full((width, width)), full((1, RET_COLS))],
        out_specs=pl.BlockSpec((nb, c, RET_COLS), lambda bi, n: (bi, n, 0)),
        out_shape=jax.ShapeDtypeStruct((b, s, RET_COLS), BF16),
        scratch_shapes=[pltpu.VMEM((nb, hh // hg, width, width), F32)],
        compiler_params=_params(("parallel", "arbitrary")),
        name="retention",
    )(ret, ret, ret, ret, decay, xi, zeta, g_chunk, same, gn_gain.reshape(1, RET_COLS).astype(F32))


def _out_proj_kernel(h_ref, a_ref, r_ref, w_ref, g_ref, h_out_ref, hn_ref):
    h = h_ref[...] + _dot(a_ref[...], w_ref[:NSA_Q_COLS]) + _dot(r_ref[...], w_ref[NSA_Q_COLS:])
    h_out_ref[...] = h
    hn_ref[...] = _rms(h, g_ref[...]).astype(BF16)


def _out_proj(h, a, r, w, g):
    t, d = h.shape
    tm = min(STREAM_ROW_TILE, t)
    row = lambda n: pl.BlockSpec((tm, n), lambda i: (i, 0))
    return pl.pallas_call(
        _out_proj_kernel,
        grid=(t // tm,),
        in_specs=[row(d), row(NSA_Q_COLS), row(RET_COLS),
                  pl.BlockSpec((MIX_WIDTH, d), lambda i: (0, 0)),
                  pl.BlockSpec((1, d), lambda i: (0, 0))],
        out_specs=[row(d), row(d)],
        out_shape=[jax.ShapeDtypeStruct((t, d), F32), jax.ShapeDtypeStruct((t, d), BF16)],
        compiler_params=_params(("parallel",)),
        name="out_proj",
    )(h, a, r, w.astype(BF16), g.reshape(1, d))


def _swiglu_chunk(x, wg, wu, wd):
    hid = jax.nn.silu(_dot(x, wg)) * _dot(x, wu)
    return _dot(hid.astype(BF16), wd)


def _ffn_kernel(x_ref, wg_ref, wu_ref, wd_ref, o_ref):
    x = x_ref[...]
    acc = None
    for f in range(D_FF // FFN_COL_TILE):
        cols = slice(f * FFN_COL_TILE, (f + 1) * FFN_COL_TILE)
        part = _swiglu_chunk(x, wg_ref[:, cols], wu_ref[:, cols], wd_ref[cols, :])
        acc = part if acc is None else acc + part
    o_ref[...] = acc


def _dense_ffn(hn, wg, wu, wd):
    t, d = hn.shape
    tm = min(FFN_ROW_TILE, t)
    once = pl.Buffered(1)
    return pl.pallas_call(
        _ffn_kernel,
        grid=(t // tm,),
        in_specs=[pl.BlockSpec((tm, d), lambda i: (i, 0)),
                  pl.BlockSpec((d, D_FF), lambda i: (0, 0), pipeline_mode=once),
                  pl.BlockSpec((d, D_FF), lambda i: (0, 0), pipeline_mode=once),
                  pl.BlockSpec((D_FF, d), lambda i: (0, 0), pipeline_mode=once)],
        out_specs=pl.BlockSpec((tm, d), lambda i: (i, 0)),
        out_shape=jax.ShapeDtypeStruct((t, d), F32),
        compiler_params=_params(("parallel",)),
        name="dense_ffn",
    )(hn, wg.astype(BF16), wu.astype(BF16), wd.astype(BF16))


def _lane_column(table, lane_index):
    lane = lax.broadcasted_iota(jnp.int32, table.shape, 1)
    col = jnp.sum(jnp.where(lane == lane_index, table, 0.0), axis=-1, keepdims=True)
    return jnp.broadcast_to(col, table.shape)


def _moe_kernel(x_ref, router_ref, wg_ref, wu_ref, wd_ref, o_ref,
                xs_ref, y_ref, slot_ref, gate_ref, slot_row_ref, slot_e_ref, gate_e_ref, count_ref):
    e = pl.program_id(1)
    f = pl.program_id(2)
    tm, d = x_ref.shape
    sub = MOE_GROUP_ROWS

    @pl.when((e == 0) & (f == 0))
    def _route():
        o_ref[...] = jnp.zeros_like(o_ref)
        logits = _dot(x_ref[...], router_ref[...])
        lane = lax.broadcasted_iota(jnp.int32, logits.shape, 1)
        logits = jnp.where(lane < N_EXPERTS, logits, -jnp.inf)
        v1 = jnp.max(logits, axis=-1, keepdims=True)
        i1 = jnp.min(jnp.where(logits == v1, lane, V7X_LANES), axis=-1, keepdims=True)
        rest = jnp.where(lane == i1, -jnp.inf, logits)
        v2 = jnp.max(rest, axis=-1, keepdims=True)
        i2 = jnp.min(jnp.where(rest == v2, lane, V7X_LANES), axis=-1, keepdims=True)
        e2 = jnp.exp(v2 - v1)
        inv = 1.0 / (1.0 + e2)
        gate_ref[...] = jnp.where(lane == i1, inv, 0.0) + jnp.where(lane == i2, e2 * inv, 0.0)
        routed = jnp.where((lane == i1) | (lane == i2), 1.0, 0.0)
        c = MOE_RANK_CHUNK
        before = (lax.broadcasted_iota(jnp.int32, (c, c), 1) < lax.broadcasted_iota(jnp.int32, (c, c), 0))
        before = jnp.where(before, 1.0, 0.0).astype(BF16)
        offset = jnp.zeros((1, V7X_LANES), F32)
        for j in range(tm // c):
            part = routed[j * c:(j + 1) * c]
            rank = _dot(before, part.astype(BF16)) + offset
            slot_ref[j * c:(j + 1) * c, :] = jnp.where(part > 0.0, rank, -1.0)
            offset = offset + jnp.sum(part, axis=0, keepdims=True)
        for ee in range(N_EXPERTS):
            count_ref[ee] = offset[0, ee].astype(jnp.int32)
        slot_row_ref[...] = slot_ref[...].T[:N_EXPERTS]

    n_groups = (count_ref[e] + sub - 1) // sub

    @pl.when(f == 0)
    def _gather():
        slot_e_ref[...] = _lane_column(slot_ref[...], e)
        gate_e_ref[...] = _lane_column(gate_ref[...], e)
        slot_row = slot_row_ref[pl.ds(e, 1), :]

        def body(s, _):
            r0 = pl.multiple_of(s * sub, sub)
            want = (r0 + lax.broadcasted_iota(jnp.int32, (sub, 1), 0)).astype(F32)
            onehot = jnp.where(slot_row == want, 1.0, 0.0).astype(BF16)
            xs_ref[pl.ds(r0, sub), :] = _dot(onehot, x_ref[...]).astype(BF16)
            y_ref[pl.ds(r0, sub), :] = jnp.zeros((sub, d), F32)
            return 0

        lax.fori_loop(0, n_groups, body, 0)

    def expert(i, _, base, n):
        r0 = pl.multiple_of(base + i * n, n)
        y_ref[pl.ds(r0, n), :] += _swiglu_chunk(xs_ref[pl.ds(r0, n), :], wg_ref[0, 0], wu_ref[0, 0], wd_ref[0])
        return 0

    tail = MOE_TAIL_ROWS
    n_filled = count_ref[e] // sub
    rest = count_ref[e] - n_filled * sub
    short = rest <= tail
    n_whole = jnp.where(short, n_filled, n_filled + 1)
    n_tail = jnp.where(short & (rest > 0), 1, 0)
    lax.fori_loop(0, n_whole, functools.partial(expert, base=0, n=sub), 0)
    lax.fori_loop(0, n_tail, functools.partial(expert, base=n_filled * sub, n=tail), 0)

    @pl.when(f == pl.num_programs(2) - 1)
    def _scatter():
        tc = MOE_SCATTER_ROWS

        def body(s, _):
            r0 = pl.multiple_of(s * sub, sub)
            y = y_ref[pl.ds(r0, sub), :].astype(BF16)
            want = (r0 + lax.broadcasted_iota(jnp.int32, (1, sub), 1)).astype(F32)
            for j in range(tm // tc):
                rows = slice(j * tc, (j + 1) * tc)
                slot = jnp.concatenate([slot_e_ref[rows, :]] * (sub // V7X_LANES), axis=1)
                onehot = jnp.where(slot == want, 1.0, 0.0).astype(BF16)
                weight = jnp.concatenate([gate_e_ref[rows, :]] * (d // V7X_LANES), axis=1)
                o_ref[rows, :] += weight * _dot(onehot, y)
            return 0

        lax.fori_loop(0, n_groups, body, 0)


def _moe_ffn(hn, router, wg, wu, wd):
    t, d = hn.shape
    tm = min(MOE_ROW_TILE, t)
    tf = MOE_COL_TILE
    nf = D_FF // tf
    router = jnp.pad(router, ((0, 0), (0, V7X_LANES - N_EXPERTS))).astype(BF16)
    chunked = lambda w: jnp.transpose(w.astype(BF16).reshape(N_EXPERTS, d, nf, tf), (0, 2, 1, 3))
    once = pl.Buffered(1)
    return pl.pallas_call(
        _moe_kernel,
        grid=(t // tm, N_EXPERTS, nf),
        in_specs=[pl.BlockSpec((tm, d), lambda i, e, f: (i, 0), pipeline_mode=once),
                  pl.BlockSpec((d, V7X_LANES), lambda i, e, f: (0, 0), pipeline_mode=once),
                  pl.BlockSpec((1, 1, d, tf), lambda i, e, f: (e, f, 0, 0)),
                  pl.BlockSpec((1, 1, d, tf), lambda i, e, f: (e, f, 0, 0)),
                  pl.BlockSpec((1, tf, d), lambda i, e, f: (e, f, 0))],
        out_specs=pl.BlockSpec((tm, d), lambda i, e, f: (i, 0), pipeline_mode=once),
        out_shape=jax.ShapeDtypeStruct((t, d), F32),
        scratch_shapes=[pltpu.VMEM((tm, d), BF16), pltpu.VMEM((tm, d), F32),
                        pltpu.VMEM((tm, V7X_LANES), F32), pltpu.VMEM((tm, V7X_LANES), F32),
                        pltpu.VMEM((N_EXPERTS, tm), F32),
                        pltpu.VMEM((tm, V7X_LANES), F32), pltpu.VMEM((tm, V7X_LANES), F32),
                        pltpu.SMEM((N_EXPERTS,), jnp.int32)],
        compiler_params=_params(("parallel", "arbitrary", "arbitrary"), vmem=MOE_VMEM_LIMIT),
        name="moe_ffn",
    )(hn, router, chunked(wg), chunked(wu), wd.astype(BF16))


def _ple_kernel(h_ref, f_ref, p_ref, g_ref, proj_ref, gate_ref, gf_ref, o_ref, *, final_norm):
    h = h_ref[...] + f_ref[...]
    emb = _dot(p_ref[...].astype(BF16), proj_ref[...])
    sig = jax.nn.sigmoid(_dot(_rms(h, g_ref[...]).astype(BF16), gate_ref[...]))
    h = h + emb * sig
    if final_norm:
        h = _rms(h, gf_ref[...])
    o_ref[...] = h


def _ple(h, f, p, g, proj, gate, g_final, final_norm):
    t, d = h.shape
    tm = min(STREAM_ROW_TILE, t)
    vec = pl.BlockSpec((1, d), lambda i: (0, 0))
    return pl.pallas_call(
        functools.partial(_ple_kernel, final_norm=final_norm),
        grid=(t // tm,),
        in_specs=[pl.BlockSpec((tm, d), lambda i: (i, 0)),
                  pl.BlockSpec((tm, d), lambda i: (i, 0)),
                  pl.BlockSpec((tm, PLE_DIM), lambda i: (i, 0)),
                  vec,
                  pl.BlockSpec((PLE_DIM, d), lambda i: (0, 0)),
                  pl.BlockSpec((d, d), lambda i: (0, 0)),
                  vec],
        out_specs=pl.BlockSpec((tm, d), lambda i: (i, 0)),
        out_shape=jax.ShapeDtypeStruct((t, d), F32),
        compiler_params=_params(("parallel",)),
        name="ple",
    )(h, f, p, g.reshape(1, d), proj.astype(BF16), gate.astype(BF16), g_final.reshape(1, d))


def kernel(x, p, w_in, w_out, g_mix, g_ffn, g_ple, g_final, cmp_pos, cmp_w1, cmp_w2, ret_gn,
           ffn_gate, ffn_up, ffn_down, moe_router, moe_gate, moe_up, moe_down, ple_proj, ple_gate):
    b, s, d = x.shape
    depth = w_in.shape[0]
    t = b * s
    h = x
    for i in range(depth):
        qn, kcv, ks, kv3, gl, ret = _in_proj(h.reshape(b, s, d), g_mix[i], _arrange_w_in(w_in[i]))
        kv_cmp = _compress(kcv, cmp_pos[i], cmp_w1[i], cmp_w2[i])
        a = _nsa_attention(qn, kv_cmp, ks, kv3, gl)
        r = _retention(ret, ret_gn[i])
        h, hn = _out_proj(h.reshape(t, d), a.reshape(t, NSA_Q_COLS), r.reshape(t, RET_COLS), w_out[i], g_ffn[i])
        if i % 2 == 0:
            f = _dense_ffn(hn, ffn_gate[i // 2], ffn_up[i // 2], ffn_down[i // 2])
        else:
            f = _moe_ffn(hn, moe_router[i // 2], moe_gate[i // 2], moe_up[i // 2], moe_down[i // 2])
        h = _ple(h, f, p[i].reshape(t, PLE_DIM), g_ple[i], ple_proj[i], ple_gate[i], g_final, i == depth - 1)
    return h.reshape(b, s, d)
```

```python
import functools
import math

import jax
import jax.numpy as jnp
from jax import lax
from jax.experimental import pallas as pl
from jax.experimental.pallas import tpu as pltpu

F32 = jnp.float32
BF16 = jnp.bfloat16

D_MODEL = 1024
HEAD_DIM = 64
NSA_HEADS = 8
NSA_KV_HEADS = 2
NSA_GROUP = NSA_HEADS // NSA_KV_HEADS
RET_HEADS = 8
CMP_BLOCK = 32
CMP_STRIDE = 16
CMP_HIDDEN = 256
SEL_BLOCK = 64
N_SELECT = 16
N_LOCAL_SEL = 2
WINDOW = 512
RET_CHUNK = 128
D_FF = 3584
N_EXPERTS = 8
PLE_DIM = 256
EPS = 1e-6
NEG_INF = -1e30
BIG = 1e9

NSA_Q_COLS = NSA_HEADS * HEAD_DIM
NSA_KV_COLS = NSA_KV_HEADS * HEAD_DIM
NSA_GATE_COLS = 3 * NSA_HEADS
RET_COLS = RET_HEADS * HEAD_DIM
MIX_WIDTH = NSA_Q_COLS + RET_COLS

V7X_LANES = 128
V7X_SUBLANES = 8
V7X_VMEM_BYTES = 64 * 1024 * 1024
VMEM_LIMIT = V7X_VMEM_BYTES * 3 // 4

ROW_TILE = 512
STREAM_ROW_TILE = 1024
ATTN_Q_TILE = 256
SEL_KV_TILE = 256
RET_BATCH = 4
RET_HEAD_GROUP = 4
FFN_ROW_TILE = 512
FFN_COL_TILE = 512
MOE_COL_TILE = 1792
MOE_ROW_TILE = 2048
MOE_GROUP_ROWS = 256
MOE_TAIL_ROWS = 64
MOE_RANK_CHUNK = 256
MOE_SCATTER_ROWS = 512
MOE_VMEM_LIMIT = V7X_VMEM_BYTES * 7 // 8

POS_SHIFT = 6
POS_SPLIT = 1 << POS_SHIFT
SEL_SHIFT = SEL_BLOCK.bit_length() - 1
MAX_SEL_BLOCKS = V7X_LANES


def _params(sem, vmem=VMEM_LIMIT):
    return pltpu.CompilerParams(dimension_semantics=sem, vmem_limit_bytes=vmem)


def _dot(a, b):
    return jnp.dot(a, b, preferred_element_type=F32)


def _dot_nt(a, b):
    return lax.dot_general(a, b, (((1,), (1,)), ((), ())), preferred_element_type=F32)


def _dot_tn(a, b):
    return lax.dot_general(a, b, (((0,), (0,)), ((), ())), preferred_element_type=F32)


def _rms(x, g):
    return x * lax.rsqrt(jnp.mean(x * x, axis=-1, keepdims=True) + EPS) * g


def _lane_features(shape, first, second):
    lane = lax.broadcasted_iota(jnp.int32, shape, 1)
    return jnp.where(lane == 0, first, jnp.where(lane == 1, second, 0))


def _key_features(pos, width):
    return _lane_features((pos.shape[0], width), pos >> POS_SHIFT, pos & (POS_SPLIT - 1))


def _ones_features(n, width):
    return _lane_features((n, width), 1, 0)


def _int_to_bf16(x):
    return x.astype(F32).astype(BF16)


def _in_proj_kernel(h_ref, g_ref, w_ref, qn_ref, kcv_ref, ks_ref, kv3_ref, gl_ref, ret_ref):
    tm = h_ref.shape[1]
    xn = _rms(h_ref[0], g_ref[...]).astype(BF16)
    pair = 2 * HEAD_DIM
    pos = pl.program_id(1) * tm + lax.broadcasted_iota(jnp.int32, (tm, 1), 0)
    kfeat = _int_to_bf16(_key_features(pos, HEAD_DIM))
    vfeat = _int_to_bf16(_ones_features(tm, HEAD_DIM))
    lane = lax.broadcasted_iota(jnp.int32, (tm, MAX_SEL_BLOCKS), 1)
    block_onehot = jnp.where(lane == (pos >> SEL_SHIFT), 1.0, 0.0).astype(BF16)

    wide = 2 * pair
    per_dot = wide // HEAD_DIM
    for j in range(NSA_Q_COLS // wide):
        z = (_dot(xn, w_ref[:, j * wide:(j + 1) * wide]) * HEAD_DIM ** -0.5).astype(BF16)
        for k in range(per_dot):
            head = per_dot * j + k
            slope = 2.0 ** -(head + 1)
            qfeat = _lane_features((tm, HEAD_DIM), POS_SPLIT * slope, slope).astype(BF16)
            qn_ref[0, head] = jnp.concatenate([z[:, k * HEAD_DIM:(k + 1) * HEAD_DIM], qfeat], axis=1)
    base = NSA_Q_COLS
    z = _dot(xn, w_ref[:, base:base + wide]).astype(BF16)
    for a in range(2):
        kcv_ref[a, 0] = z[:, a * pair:(a + 1) * pair]
    base += wide
    for c in range(2):
        z = _dot(xn, w_ref[:, base + c * wide:base + (c + 1) * wide]).astype(BF16)
        for half in range(2):
            a = 2 * c + half
            for g in range(NSA_KV_HEADS):
                zg = z[:, half * pair + g * HEAD_DIM:half * pair + (g + 1) * HEAD_DIM]
                if a == 0:
                    ks_ref[0, g] = jnp.concatenate([zg, kfeat, block_onehot], axis=1)
                else:
                    kv3_ref[a - 1, 0, g] = jnp.concatenate([zg, kfeat if a == 2 else vfeat], axis=1)
    base += 2 * wide
    for a in range(4):
        ret_ref[0, :, a * RET_COLS:(a + 1) * RET_COLS] = _dot(
            xn, w_ref[:, base + a * RET_COLS:base + (a + 1) * RET_COLS]).astype(BF16)
    base += 4 * RET_COLS
    gl_ref[0] = _dot(xn, w_ref[:, base:base + NSA_KV_HEADS * V7X_LANES])


def _arrange_w_in(w):
    q_end = NSA_Q_COLS
    kv_end = q_end + 6 * NSA_KV_COLS
    gl_end = kv_end + NSA_GATE_COLS
    gl = w[:, kv_end:gl_end].reshape(D_MODEL, 3, NSA_KV_HEADS, NSA_GROUP)
    gl = jnp.transpose(gl, (0, 2, 1, 3)).reshape(D_MODEL, NSA_KV_HEADS, 3 * NSA_GROUP)
    gl = jnp.pad(gl, ((0, 0), (0, 0), (0, V7X_LANES - 3 * NSA_GROUP)))
    gl = gl.reshape(D_MODEL, NSA_KV_HEADS * V7X_LANES)
    return jnp.concatenate([w[:, :kv_end], w[:, gl_end:], gl], axis=1).astype(BF16)


def _in_proj(h, g, w):
    b, s, d = h.shape
    assert s // SEL_BLOCK <= MAX_SEL_BLOCKS
    tm = min(ROW_TILE, s)
    ncols = w.shape[1]
    aug = 2 * HEAD_DIM
    return pl.pallas_call(
        _in_proj_kernel,
        grid=(b, s // tm),
        in_specs=[
            pl.BlockSpec((1, tm, d), lambda bi, si: (bi, si, 0)),
            pl.BlockSpec((1, d), lambda bi, si: (0, 0)),
            pl.BlockSpec((d, ncols), lambda bi, si: (0, 0)),
        ],
        out_specs=[
            pl.BlockSpec((1, NSA_HEADS, tm, aug), lambda bi, si: (bi, 0, si, 0)),
            pl.BlockSpec((2, 1, tm, NSA_KV_COLS), lambda bi, si: (0, bi, si, 0)),
            pl.BlockSpec((1, NSA_KV_HEADS, tm, aug + MAX_SEL_BLOCKS), lambda bi, si: (bi, 0, si, 0)),
            pl.BlockSpec((3, 1, NSA_KV_HEADS, tm, aug), lambda bi, si: (0, bi, 0, si, 0)),
            pl.BlockSpec((1, tm, NSA_KV_HEADS * V7X_LANES), lambda bi, si: (bi, si, 0)),
            pl.BlockSpec((1, tm, 4 * RET_COLS), lambda bi, si: (bi, si, 0)),
        ],
        out_shape=[
            jax.ShapeDtypeStruct((b, NSA_HEADS, s, aug), BF16),
            jax.ShapeDtypeStruct((2, b, s, NSA_KV_COLS), BF16),
            jax.ShapeDtypeStruct((b, NSA_KV_HEADS, s, aug + MAX_SEL_BLOCKS), BF16),
            jax.ShapeDtypeStruct((3, b, NSA_KV_HEADS, s, aug), BF16),
            jax.ShapeDtypeStruct((b, s, NSA_KV_HEADS * V7X_LANES), F32),
            jax.ShapeDtypeStruct((b, s, 4 * RET_COLS), BF16),
        ],
        compiler_params=_params(("parallel", "parallel")),
        name="in_proj",
    )(h, g.reshape(1, d), w)


def _compress_kernel(x_ref, wbig_ref, pos_ref, w1_ref, w2_ref, o_ref):
    nch = x_ref.shape[2]
    u = _dot(x_ref[0, 0], wbig_ref[0])
    hid0 = _dot(pos_ref[0], w1_ref[0])[0:1]
    w2 = w2_ref[0]
    cmp_end = lax.broadcasted_iota(jnp.int32, (nch, 1), 0) * CMP_STRIDE + (CMP_BLOCK - 1)
    feat = _int_to_bf16(jnp.where(pl.program_id(0) == 0, _key_features(cmp_end, HEAD_DIM),
                                  _ones_features(nch, HEAD_DIM)))
    for g in range(NSA_KV_HEADS):
        c0 = g * 2 * CMP_HIDDEN
        first = u[:, c0:c0 + CMP_HIDDEN]
        second = u[:, c0 + CMP_HIDDEN:c0 + 2 * CMP_HIDDEN]
        hid = first + pltpu.roll(second, nch - 1, 0) + hid0
        out = _dot(jax.nn.gelu(hid).astype(BF16), w2).astype(BF16)
        o_ref[0, 0, g] = jnp.concatenate([out, feat], axis=1)


def _arrange_cmp_w1(w1):
    r = CMP_BLOCK // CMP_STRIDE
    w1r = w1.reshape(2, r, CMP_STRIDE, HEAD_DIM, CMP_HIDDEN)
    eye = jnp.eye(NSA_KV_HEADS, dtype=w1.dtype)
    big = jnp.einsum('krcdh,gf->kcgdfrh', w1r, eye)
    return big.reshape(2, CMP_STRIDE * NSA_KV_COLS, NSA_KV_HEADS * r * CMP_HIDDEN).astype(BF16)


def _compress(kcv, cmp_pos, cmp_w1, cmp_w2):
    _, b, s, _ = kcv.shape
    nch = s // CMP_STRIDE
    x = kcv.reshape(2, b, nch, CMP_STRIDE * NSA_KV_COLS)
    wbig = _arrange_cmp_w1(cmp_w1)
    pos = jnp.broadcast_to(cmp_pos.reshape(2, 1, CMP_BLOCK * HEAD_DIM),
                           (2, V7X_SUBLANES, CMP_BLOCK * HEAD_DIM)).astype(BF16)
    w1 = cmp_w1.reshape(2, CMP_BLOCK * HEAD_DIM, CMP_HIDDEN).astype(BF16)
    w2 = cmp_w2.astype(BF16)
    kdim = CMP_STRIDE * NSA_KV_COLS
    return pl.pallas_call(
        _compress_kernel,
        grid=(2, b),
        in_specs=[
            pl.BlockSpec((1, 1, nch, kdim), lambda a, bi: (a, bi, 0, 0)),
            pl.BlockSpec((1, kdim, wbig.shape[2]), lambda a, bi: (a, 0, 0)),
            pl.BlockSpec((1, V7X_SUBLANES, CMP_BLOCK * HEAD_DIM), lambda a, bi: (a, 0, 0)),
            pl.BlockSpec((1, CMP_BLOCK * HEAD_DIM, CMP_HIDDEN), lambda a, bi: (a, 0, 0)),
            pl.BlockSpec((1, CMP_HIDDEN, HEAD_DIM), lambda a, bi: (a, 0, 0)),
        ],
        out_specs=pl.BlockSpec((1, 1, NSA_KV_HEADS, nch, 2 * HEAD_DIM), lambda a, bi: (a, bi, 0, 0, 0)),
        out_shape=jax.ShapeDtypeStruct((2, b, NSA_KV_HEADS, nch, 2 * HEAD_DIM), BF16),
        compiler_params=_params(("parallel", "parallel")),
        name="compress",
    )(x, wbig, pos, w1, w2)


def _normalise_t(acc):
    return acc[:HEAD_DIM] * (1.0 / jnp.maximum(acc[HEAD_DIM:HEAD_DIM + 1], 1e-30))


def _attn_kernel(q_ref, kc_ref, vc_ref, ks_ref, vs_ref, kw_ref, vw_ref, gl_ref, ovt_ref, o_ref, tiles_ref,
                 sel_ref, *, n_pick, tk):
    qt = q_ref.shape[2]
    cols = NSA_GROUP * qt
    t0 = pl.program_id(2) * qt
    q_t = q_ref[0].reshape(cols, 2 * HEAD_DIM).T
    tq = t0 + lax.broadcasted_iota(jnp.int32, (1, qt), 1)

    def per_head(valid, masked):
        bias = jnp.where(valid, 0.0, masked)
        return jnp.concatenate([bias] * NSA_GROUP, axis=1)

    nc = kc_ref.shape[3]
    cmp_end = lax.broadcasted_iota(jnp.int32, (nc, 1), 0) * CMP_STRIDE + (CMP_BLOCK - 1)
    logit_c = _dot(kc_ref[0, 0, 0], q_t) + per_head(cmp_end <= tq, -jnp.inf)
    e_c = jnp.exp(logit_c - jnp.maximum(jnp.max(logit_c, axis=0, keepdims=True), NEG_INF))
    p_c = e_c * (1.0 / jnp.maximum(jnp.sum(e_c, axis=0, keepdims=True), 1e-30))
    o_c = _dot(vc_ref[0, 0, 0].T, p_c.astype(BF16))[:HEAD_DIM]

    p_sum = p_c[:, 0:qt]
    for r in range(1, NSA_GROUP):
        p_sum = p_sum + p_c[:, r * qt:(r + 1) * qt]
    p_hi = p_sum.astype(BF16)
    p_lo = (p_sum - p_hi.astype(F32)).astype(BF16)
    imp = _dot(ovt_ref[...], p_hi) + _dot(ovt_ref[...], p_lo)
    shape = (MAX_SEL_BLOCKS, qt)
    blk = lax.broadcasted_iota(jnp.int32, shape, 0)
    back = ((t0 + lax.broadcasted_iota(jnp.int32, shape, 1)) >> SEL_SHIFT) - blk
    forced = (blk == 0) | ((back >= 0) & (back < N_LOCAL_SEL))
    score = jnp.where(forced, BIG, jnp.where(back >= 0, imp, -BIG))

    def pick(_, work):
        row = lax.broadcasted_iota(jnp.int32, work.shape, 0)
        m = jnp.max(work, axis=0, keepdims=True)
        first = jnp.min(jnp.where(work == m, row, MAX_SEL_BLOCKS), axis=0, keepdims=True)
        return jnp.where(row == first, -jnp.inf, work)

    n_live = (t0 + qt + SEL_BLOCK - 1) >> SEL_SHIFT
    sizes = [MAX_SEL_BLOCKS // 4, MAX_SEL_BLOCKS // 2, MAX_SEL_BLOCKS]
    for lower, rows in zip([0] + sizes[:-1], sizes):
        @pl.when((n_live > lower) & ((n_live <= rows) | (rows == MAX_SEL_BLOCKS)))
        def _(rows=rows):
            left = lax.fori_loop(0, n_pick, pick, score[:rows])
            sel_ref[...] = jnp.zeros(shape, F32)
            sel_ref[:rows] = jnp.where(left == -jnp.inf, 1.0, 0.0)
    sel = sel_ref[...]
    sel_bias = ((sel - 1.0) * -NEG_INF).astype(BF16)
    q_sel = jnp.concatenate([q_t, jnp.concatenate([sel_bias] * NSA_GROUP, axis=1)], axis=0)

    n_own = max(1, qt // tk)
    n_full = t0 // tk
    blocks_per_tile = tk // SEL_BLOCK
    block_used = jnp.max(sel, axis=1, keepdims=True)
    n_visit = jnp.int32(0)
    for j in range(tiles_ref.shape[0]):
        used = jnp.max(block_used[j * blocks_per_tile:(j + 1) * blocks_per_tile]) > 0.0
        tiles_ref[n_visit] = j
        n_visit = n_visit + (used & (j < n_full)).astype(jnp.int32)

    def sweep(tiles, carry, causal):
        m_i, acc = carry
        starts = [pl.multiple_of(j * tk, tk) for j in tiles]
        scores = []
        for k0 in starts:
            s = _dot(ks_ref[0, 0, pl.ds(k0, tk), :], q_sel)
            if causal:
                s = s + per_head(k0 + lax.broadcasted_iota(jnp.int32, (tk, 1), 0) <= tq, NEG_INF)
            scores.append(s)
        m_new = m_i
        for s in scores:
            m_new = jnp.maximum(m_new, jnp.max(s, axis=0, keepdims=True))
        acc = jnp.exp(m_i - m_new) * acc
        for s, k0 in zip(scores, starts):
            acc = acc + _dot(vs_ref[0, 0, 0, pl.ds(k0, tk), :].T, jnp.exp(s - m_new).astype(BF16))
        return m_new, acc

    carry = (jnp.full((1, cols), NEG_INF, F32), jnp.zeros((2 * HEAD_DIM, cols), F32))
    done = jnp.int32(0)
    for width in (1, 2, 4):
        steps = n_visit // width if width == 4 else (n_visit // width) & 1
        carry = lax.fori_loop(
            0, steps,
            lambda i, c, width=width, done=done: sweep(
                [tiles_ref[done + width * i + u] for u in range(width)], c, causal=False),
            carry)
        done = done + steps * width
    carry = sweep([n_full + j for j in range(n_own)], carry, causal=True)
    o_s = _normalise_t(carry[1])

    band = WINDOW + qt
    w0 = pl.multiple_of(jnp.maximum(t0 - WINDOW, 0), math.gcd(qt, WINDOW))
    kpos = w0 + lax.broadcasted_iota(jnp.int32, (band, 1), 0)
    valid_w = (kpos <= tq) & (kpos > tq - WINDOW)
    logit_w = _dot(kw_ref[0, 0, 0, pl.ds(w0, band), :], q_t) + per_head(valid_w, NEG_INF)
    e_w = jnp.exp(logit_w - jnp.max(logit_w, axis=0, keepdims=True))
    o_w = _normalise_t(_dot(vw_ref[0, 0, 0, pl.ds(w0, band), :].T, e_w.astype(BF16)))

    gate_t = jax.nn.sigmoid(gl_ref[0]).T

    def branch_gate(branch):
        first = branch * NSA_GROUP
        return jnp.concatenate([gate_t[first + r:first + r + 1] for r in range(NSA_GROUP)], axis=1)

    o_t = branch_gate(0) * o_c + branch_gate(1) * o_s + branch_gate(2) * o_w
    o = o_t.T
    o_ref[0] = jnp.concatenate([o[r * qt:(r + 1) * qt] for r in range(NSA_GROUP)], axis=1).astype(o_ref.dtype)


def _nsa_attention(qn, kv_cmp, ks, kv3, gl):
    b, _, s, _ = qn.shape
    nc = kv_cmp.shape[3]
    nsel = s // SEL_BLOCK
    n_pick = min(N_SELECT, nsel)
    tk = min(SEL_KV_TILE, s)
    aug = 2 * HEAD_DIM
    qt = min(ATTN_Q_TILE, s)
    assert qt % tk == 0 or tk % qt == 0
    cmp_start = jnp.arange(nc) * CMP_STRIDE
    sel_start = jnp.arange(nsel) * SEL_BLOCK
    overlap = (jnp.minimum(cmp_start[:, None] + CMP_BLOCK, sel_start[None, :] + SEL_BLOCK)
               > jnp.maximum(cmp_start[:, None], sel_start[None, :]))
    overlap_t = jnp.pad(overlap.T.astype(BF16), ((0, MAX_SEL_BLOCKS - nsel), (0, 0)))
    kv_spec = lambda a: pl.BlockSpec((1, 1, 1, s, aug), lambda bi, g, i, a=a: (a, bi, g, 0, 0))
    cmp_spec = lambda a: pl.BlockSpec((1, 1, 1, nc, aug), lambda bi, g, i, a=a: (a, bi, g, 0, 0))
    return pl.pallas_call(
        functools.partial(_attn_kernel, n_pick=n_pick, tk=tk),
        grid=(b, NSA_KV_HEADS, s // qt),
        in_specs=[
            pl.BlockSpec((1, NSA_GROUP, qt, aug), lambda bi, g, i: (bi, g, i, 0)),
            cmp_spec(0), cmp_spec(1),
            pl.BlockSpec((1, 1, s, aug + MAX_SEL_BLOCKS), lambda bi, g, i: (bi, g, 0, 0)),
            kv_spec(0), kv_spec(1), kv_spec(2),
            pl.BlockSpec((1, qt, V7X_LANES), lambda bi, g, i: (bi, i, g)),
            pl.BlockSpec((MAX_SEL_BLOCKS, nc), lambda bi, g, i: (0, 0)),
        ],
        out_specs=pl.BlockSpec((1, qt, NSA_GROUP * HEAD_DIM), lambda bi, g, i: (bi, i, g)),
        out_shape=jax.ShapeDtypeStruct((b, s, NSA_Q_COLS), BF16),
        scratch_shapes=[pltpu.SMEM((s // tk,), jnp.int32), pltpu.VMEM((MAX_SEL_BLOCKS, qt), F32)],
        compiler_params=_params(("parallel", "parallel", "arbitrary")),
        name="nsa_attention",
    )(qn, kv_cmp, kv_cmp, ks, kv3, kv3, kv3, gl, overlap_t)


def _retention_kernel(q_ref, k_ref, v_ref, g_ref, decay_ref, xi_ref, zeta_ref, gch_ref, same_ref, gn_ref,
                      o_ref, state_ref):
    @pl.when(pl.program_id(1) == 0)
    def _():
        state_ref[...] = jnp.zeros_like(state_ref)

    width = same_ref.shape[0]
    heads = width // HEAD_DIM
    c = q_ref.shape[1]
    scale = jnp.asarray(HEAD_DIM ** -0.5, BF16)
    lane_head = lax.broadcasted_iota(jnp.int32, (c, width), 1) // HEAD_DIM
    same = same_ref[...]
    mean_w = (same * (1.0 / HEAD_DIM)).astype(BF16)

    def head_mean(x):
        hi = x.astype(BF16)
        lo = (x - hi.astype(F32)).astype(BF16)
        return _dot(hi, mean_w) + _dot(lo, mean_w)

    def stack(x):
        zero = jnp.zeros_like(x)
        return jnp.concatenate([jnp.where(lane_head == j, x, zero) for j in range(heads)], axis=0)

    for i in range(q_ref.shape[0]):
        outs = []
        for p in range(RET_COLS // width):
            cols = slice(p * width, (p + 1) * width)
            q = q_ref[i, :, cols]
            k = k_ref[i, :, cols] * scale
            v = v_ref[i, :, cols]
            state = state_ref[i, p]
            inner = _dot_nt(q, stack(k)) * decay_ref[p]
            o = _dot(inner.astype(BF16), stack(v)) + _dot(q, state.astype(BF16)) * xi_ref[p]
            kz = (k.astype(F32) * zeta_ref[p]).astype(BF16)
            state_ref[i, p] = gch_ref[p] * state + same * _dot_tn(kz, v)
            centred = o - head_mean(o)
            outs.append(centred * lax.rsqrt(head_mean(jnp.square(centred)) + EPS))
        o = jnp.concatenate(outs, axis=1) * gn_ref[...]
        o_ref[i] = (jax.nn.silu(g_ref[i].astype(F32)) * o).astype(o_ref.dtype)


def _retention(ret, gn_gain):
    b, s, _ = ret.shape
    c = RET_CHUNK
    hh = RET_HEADS
    hg = RET_HEAD_GROUP
    width = hg * HEAD_DIM
    log_gamma = jnp.log1p(-jnp.exp2(-5.0 - jnp.arange(hh, dtype=F32)))
    pos = jnp.arange(c, dtype=F32)
    diff = pos[:, None] - pos[None, :]
    decay = jnp.where(diff >= 0, jnp.exp(jnp.maximum(diff, 0.0)[None] * log_gamma[:, None, None]), 0.0)
    xi = jnp.exp((pos + 1.0)[None] * log_gamma[:, None])[..., None]
    zeta = jnp.exp((c - 1.0 - pos)[None] * log_gamma[:, None])[..., None]
    g_chunk = jnp.exp(c * log_gamma)[:, None, None]
    side_by_side = lambda t: jnp.concatenate([t[j::hg] for j in range(hg)], axis=-1)
    decay = side_by_side(decay)
    xi = side_by_side(jnp.broadcast_to(xi, (hh, c, HEAD_DIM)))
    zeta = side_by_side(jnp.broadcast_to(zeta, (hh, c, HEAD_DIM)))
    head_of = jnp.arange(width) // HEAD_DIM
    same = (head_of[:, None] == head_of[None, :]).astype(F32)
    g_lane = side_by_side(jnp.broadcast_to(g_chunk, (hh, 1, HEAD_DIM)))
    g_chunk = same[None] * g_lane
    nb = math.gcd(RET_BATCH, b)
    part = lambda a: pl.BlockSpec((nb, c, RET_COLS), lambda bi, n, a=a: (bi, n, a))
    full = lambda shape: pl.BlockSpec(shape, lambda bi, n: (0,) * len(shape))
    return pl.pallas_call(
        _retention_kernel,
        grid=(b // nb, s // c),
        in_specs=[part(0), part(1), part(2), part(3),
                  full((hh // hg, c, hg * c)), full((hh // hg, c, width)), full((hh // hg, c, width)),
                  full((hh // hg, width, width)), full((width, width)), full((1, RET_COLS))],
        out_specs=pl.BlockSpec((nb, c, RET_COLS), lambda bi, n: (bi, n, 0)),
        out_shape=jax.ShapeDtypeStruct((b, s, RET_COLS), BF16),
        scratch_shapes=[pltpu.VMEM((nb, hh // hg, width, width), F32)],
        compiler_params=_params(("parallel", "arbitrary")),
        name="retention",
    )(ret, ret, ret, ret, decay, xi, zeta, g_chunk, same, gn_gain.reshape(1, RET_COLS).astype(F32))


def _out_proj_kernel(h_ref, a_ref, r_ref, w_ref, g_ref, h_out_ref, hn_ref):
    h = h_ref[...] + _dot(a_ref[...], w_ref[:NSA_Q_COLS]) + _dot(r_ref[...], w_ref[NSA_Q_COLS:])
    h_out_ref[...] = h
    hn_ref[...] = _rms(h, g_ref[...]).astype(BF16)


def _out_proj(h, a, r, w, g):
    t, d = h.shape
    tm = min(STREAM_ROW_TILE, t)
    row = lambda n: pl.BlockSpec((tm, n), lambda i: (i, 0))
    return pl.pallas_call(
        _out_proj_kernel,
        grid=(t // tm,),
        in_specs=[row(d), row(NSA_Q_COLS), row(RET_COLS),
                  pl.BlockSpec((MIX_WIDTH, d), lambda i: (0, 0)),
                  pl.BlockSpec((1, d), lambda i: (0, 0))],
        out_specs=[row(d), row(d)],
        out_shape=[jax.ShapeDtypeStruct((t, d), F32), jax.ShapeDtypeStruct((t, d), BF16)],
        compiler_params=_params(("parallel",)),
        name="out_proj",
    )(h, a, r, w.astype(BF16), g.reshape(1, d))


def _swiglu_chunk(x, wg, wu, wd):
    hid = jax.nn.silu(_dot(x, wg)) * _dot(x, wu)
    return _dot(hid.astype(BF16), wd)


def _ffn_kernel(x_ref, wg_ref, wu_ref, wd_ref, o_ref):
    x = x_ref[...]
    acc = None
    for f in range(D_FF // FFN_COL_TILE):
        cols = slice(f * FFN_COL_TILE, (f + 1) * FFN_COL_TILE)
        part = _swiglu_chunk(x, wg_ref[:, cols], wu_ref[:, cols], wd_ref[cols, :])
        acc = part if acc is None else acc + part
    o_ref[...] = acc


def _dense_ffn(hn, wg, wu, wd):
    t, d = hn.shape
    tm = min(FFN_ROW_TILE, t)
    once = pl.Buffered(1)
    return pl.pallas_call(
        _ffn_kernel,
        grid=(t // tm,),
        in_specs=[pl.BlockSpec((tm, d), lambda i: (i, 0)),
                  pl.BlockSpec((d, D_FF), lambda i: (0, 0), pipeline_mode=once),
                  pl.BlockSpec((d, D_FF), lambda i: (0, 0), pipeline_mode=once),
                  pl.BlockSpec((D_FF, d), lambda i: (0, 0), pipeline_mode=once)],
        out_specs=pl.BlockSpec((tm, d), lambda i: (i, 0)),
        out_shape=jax.ShapeDtypeStruct((t, d), F32),
        compiler_params=_params(("parallel",)),
        name="dense_ffn",
    )(hn, wg.astype(BF16), wu.astype(BF16), wd.astype(BF16))


def _lane_column(table, lane_index):
    lane = lax.broadcasted_iota(jnp.int32, table.shape, 1)
    col = jnp.sum(jnp.where(lane == lane_index, table, 0.0), axis=-1, keepdims=True)
    return jnp.broadcast_to(col, table.shape)


def _moe_kernel(x_ref, router_ref, wg_ref, wu_ref, wd_ref, o_ref,
                xs_ref, y_ref, slot_ref, gate_ref, slot_row_ref, slot_e_ref, gate_e_ref, count_ref):
    e = pl.program_id(1)
    f = pl.program_id(2)
    tm, d = x_ref.shape
    sub = MOE_GROUP_ROWS

    @pl.when((e == 0) & (f == 0))
    def _route():
        o_ref[...] = jnp.zeros_like(o_ref)
        logits = _dot(x_ref[...], router_ref[...])
        lane = lax.broadcasted_iota(jnp.int32, logits.shape, 1)
        logits = jnp.where(lane < N_EXPERTS, logits, -jnp.inf)
        v1 = jnp.max(logits, axis=-1, keepdims=True)
        i1 = jnp.min(jnp.where(logits == v1, lane, V7X_LANES), axis=-1, keepdims=True)
        rest = jnp.where(lane == i1, -jnp.inf, logits)
        v2 = jnp.max(rest, axis=-1, keepdims=True)
        i2 = jnp.min(jnp.where(rest == v2, lane, V7X_LANES), axis=-1, keepdims=True)
        e2 = jnp.exp(v2 - v1)
        inv = 1.0 / (1.0 + e2)
        gate_ref[...] = jnp.where(lane == i1, inv, 0.0) + jnp.where(lane == i2, e2 * inv, 0.0)
        routed = jnp.where((lane == i1) | (lane == i2), 1.0, 0.0)
        c = MOE_RANK_CHUNK
        before = (lax.broadcasted_iota(jnp.int32, (c, c), 1) < lax.broadcasted_iota(jnp.int32, (c, c), 0))
        before = jnp.where(before, 1.0, 0.0).astype(BF16)
        offset = jnp.zeros((1, V7X_LANES), F32)
        for j in range(tm // c):
            part = routed[j * c:(j + 1) * c]
            rank = _dot(before, part.astype(BF16)) + offset
            slot_ref[j * c:(j + 1) * c, :] = jnp.where(part > 0.0, rank, -1.0)
            offset = offset + jnp.sum(part, axis=0, keepdims=True)
        for ee in range(N_EXPERTS):
            count_ref[ee] = offset[0, ee].astype(jnp.int32)
        slot_row_ref[...] = slot_ref[...].T[:N_EXPERTS]

    n_groups = (count_ref[e] + sub - 1) // sub

    @pl.when(f == 0)
    def _gather():
        slot_e_ref[...] = _lane_column(slot_ref[...], e)
        gate_e_ref[...] = _lane_column(gate_ref[...], e)
        slot_row = slot_row_ref[pl.ds(e, 1), :]

        def body(s, _):
            r0 = pl.multiple_of(s * sub, sub)
            want = (r0 + lax.broadcasted_iota(jnp.int32, (sub, 1), 0)).astype(F32)
            onehot = jnp.where(slot_row == want, 1.0, 0.0).astype(BF16)
            xs_ref[pl.ds(r0, sub), :] = _dot(onehot, x_ref[...]).astype(BF16)
            y_ref[pl.ds(r0, sub), :] = jnp.zeros((sub, d), F32)
            return 0

        lax.fori_loop(0, n_groups, body, 0)

    def expert(i, _, base, n):
        r0 = pl.multiple_of(base + i * n, n)
        y_ref[pl.ds(r0, n), :] += _swiglu_chunk(xs_ref[pl.ds(r0, n), :], wg_ref[0], wu_ref[0], wd_ref[0])
        return 0

    tail = MOE_TAIL_ROWS
    n_filled = count_ref[e] // sub
    rest = count_ref[e] - n_filled * sub
    short = rest <= tail
    n_whole = jnp.where(short, n_filled, n_filled + 1)
    n_tail = jnp.where(short & (rest > 0), 1, 0)
    lax.fori_loop(0, n_whole, functools.partial(expert, base=0, n=sub), 0)
    lax.fori_loop(0, n_tail, functools.partial(expert, base=n_filled * sub, n=tail), 0)

    @pl.when(f == pl.num_programs(2) - 1)
    def _scatter():
        tc = MOE_SCATTER_ROWS

        def body(s, _):
            r0 = pl.multiple_of(s * sub, sub)
            y = y_ref[pl.ds(r0, sub), :].astype(BF16)
            want = (r0 + lax.broadcasted_iota(jnp.int32, (1, sub), 1)).astype(F32)
            for j in range(tm // tc):
                rows = slice(j * tc, (j + 1) * tc)
                slot = jnp.concatenate([slot_e_ref[rows, :]] * (sub // V7X_LANES), axis=1)
                onehot = jnp.where(slot == want, 1.0, 0.0).astype(BF16)
                weight = jnp.concatenate([gate_e_ref[rows, :]] * (d // V7X_LANES), axis=1)
                o_ref[rows, :] += weight * _dot(onehot, y)
            return 0

        lax.fori_loop(0, n_groups, body, 0)


def _moe_ffn(hn, router, wg, wu, wd):
    t, d = hn.shape
    tm = min(MOE_ROW_TILE, t)
    tf = MOE_COL_TILE
    nf = D_FF // tf
    router = jnp.pad(router, ((0, 0), (0, V7X_LANES - N_EXPERTS))).astype(BF16)
    once = pl.Buffered(1)
    return pl.pallas_call(
        _moe_kernel,
        grid=(t // tm, N_EXPERTS, nf),
        in_specs=[pl.BlockSpec((tm, d), lambda i, e, f: (i, 0), pipeline_mode=once),
                  pl.BlockSpec((d, V7X_LANES), lambda i, e, f: (0, 0), pipeline_mode=once),
                  pl.BlockSpec((1, d, tf), lambda i, e, f: (e, 0, f)),
                  pl.BlockSpec((1, d, tf), lambda i, e, f: (e, 0, f)),
                  pl.BlockSpec((1, tf, d), lambda i, e, f: (e, f, 0))],
        out_specs=pl.BlockSpec((tm, d), lambda i, e, f: (i, 0), pipeline_mode=once),
        out_shape=jax.ShapeDtypeStruct((t, d), F32),
        scratch_shapes=[pltpu.VMEM((tm, d), BF16), pltpu.VMEM((tm, d), F32),
                        pltpu.VMEM((tm, V7X_LANES), F32), pltpu.VMEM((tm, V7X_LANES), F32),
                        pltpu.VMEM((N_EXPERTS, tm), F32),
                        pltpu.VMEM((tm, V7X_LANES), F32), pltpu.VMEM((tm, V7X_LANES), F32),
                        pltpu.SMEM((N_EXPERTS,), jnp.int32)],
        compiler_params=_params(("parallel", "arbitrary", "arbitrary"), vmem=MOE_VMEM_LIMIT),
        name="moe_ffn",
    )(hn, router, wg.astype(BF16), wu.astype(BF16), wd.astype(BF16))


def _ple_kernel(h_ref, f_ref, p_ref, g_ref, proj_ref, gate_ref, gf_ref, o_ref, *, final_norm):
    h = h_ref[...] + f_ref[...]
    emb = _dot(p_ref[...].astype(BF16), proj_ref[...])
    sig = jax.nn.sigmoid(_dot(_rms(h, g_ref[...]).astype(BF16), gate_ref[...]))
    h = h + emb * sig
    if final_norm:
        h = _rms(h, gf_ref[...])
    o_ref[...] = h


def _ple(h, f, p, g, proj, gate, g_final, final_norm):
    t, d = h.shape
    tm = min(STREAM_ROW_TILE, t)
    vec = pl.BlockSpec((1, d), lambda i: (0, 0))
    return pl.pallas_call(
        functools.partial(_ple_kernel, final_norm=final_norm),
        grid=(t // tm,),
        in_specs=[pl.BlockSpec((tm, d), lambda i: (i, 0)),
                  pl.BlockSpec((tm, d), lambda i: (i, 0)),
                  pl.BlockSpec((tm, PLE_DIM), lambda i: (i, 0)),
                  vec,
                  pl.BlockSpec((PLE_DIM, d), lambda i: (0, 0)),
                  pl.BlockSpec((d, d), lambda i: (0, 0)),
                  vec],
        out_specs=pl.BlockSpec((tm, d), lambda i: (i, 0)),
        out_shape=jax.ShapeDtypeStruct((t, d), F32),
        compiler_params=_params(("parallel",)),
        name="ple",
    )(h, f, p, g.reshape(1, d), proj.astype(BF16), gate.astype(BF16), g_final.reshape(1, d))


def kernel(x, p, w_in, w_out, g_mix, g_ffn, g_ple, g_final, cmp_pos, cmp_w1, cmp_w2, ret_gn,
           ffn_gate, ffn_up, ffn_down, moe_router, moe_gate, moe_up, moe_down, ple_proj, ple_gate):
    b, s, d = x.shape
    depth = w_in.shape[0]
    t = b * s
    h = x
    for i in range(depth):
        qn, kcv, ks, kv3, gl, ret = _in_proj(h.reshape(b, s, d), g_mix[i], _arrange_w_in(w_in[i]))
        kv_cmp = _compress(kcv, cmp_pos[i], cmp_w1[i], cmp_w2[i])
        a = _nsa_attention(qn, kv_cmp, ks, kv3, gl)
        r = _retention(ret, ret_gn[i])
        h, hn = _out_proj(h.reshape(t, d), a.reshape(t, NSA_Q_COLS), r.reshape(t, RET_COLS), w_out[i], g_ffn[i])
        if i % 2 == 0:
            f = _dense_ffn(hn, ffn_gate[i // 2], ffn_up[i // 2], ffn_down[i // 2])
        else:
            f = _moe_ffn(hn, moe_router[i // 2], moe_gate[i // 2], moe_up[i // 2], moe_down[i // 2])
        h = _ple(h, f, p[i].reshape(t, PLE_DIM), g_ple[i], ple_proj[i], ple_gate[i], g_final, i == depth - 1)
    return h.reshape(b, s, d)
```

```python
import functools
import math

import jax
import jax.numpy as jnp
from jax import lax
from jax.experimental import pallas as pl
from jax.experimental.pallas import tpu as pltpu

F32 = jnp.float32
BF16 = jnp.bfloat16

D_MODEL = 1024
HEAD_DIM = 64
NSA_HEADS = 8
NSA_KV_HEADS = 2
NSA_GROUP = NSA_HEADS // NSA_KV_HEADS
RET_HEADS = 8
CMP_BLOCK = 32
CMP_STRIDE = 16
CMP_HIDDEN = 256
SEL_BLOCK = 64
N_SELECT = 16
N_LOCAL_SEL = 2
WINDOW = 512
RET_CHUNK = 128
D_FF = 3584
N_EXPERTS = 8
PLE_DIM = 256
EPS = 1e-6
NEG_INF = -1e30
BIG = 1e9

NSA_Q_COLS = NSA_HEADS * HEAD_DIM
NSA_KV_COLS = NSA_KV_HEADS * HEAD_DIM
NSA_GATE_COLS = 3 * NSA_HEADS
RET_COLS = RET_HEADS * HEAD_DIM
MIX_WIDTH = NSA_Q_COLS + RET_COLS

V7X_LANES = 128
V7X_SUBLANES = 8
V7X_VMEM_BYTES = 64 * 1024 * 1024
VMEM_LIMIT = V7X_VMEM_BYTES * 3 // 4

ROW_TILE = 512
STREAM_ROW_TILE = 1024
ATTN_Q_TILE = 256
SEL_KV_TILE = 256
RET_BATCH = 4
RET_HEAD_GROUP = 4
FFN_ROW_TILE = 512
FFN_COL_TILE = 512
MOE_COL_TILE = 1792
MOE_ROW_TILE = 2048
MOE_GROUP_ROWS = 256
MOE_TAIL_ROWS = 64
MOE_RANK_CHUNK = 256
MOE_SCATTER_ROWS = 512
MOE_VMEM_LIMIT = V7X_VMEM_BYTES * 7 // 8

POS_SHIFT = 6
POS_SPLIT = 1 << POS_SHIFT
SEL_SHIFT = SEL_BLOCK.bit_length() - 1
MAX_SEL_BLOCKS = V7X_LANES


def _params(sem, vmem=VMEM_LIMIT):
    return pltpu.CompilerParams(dimension_semantics=sem, vmem_limit_bytes=vmem)


def _dot(a, b):
    return jnp.dot(a, b, preferred_element_type=F32)


def _dot_nt(a, b):
    return lax.dot_general(a, b, (((1,), (1,)), ((), ())), preferred_element_type=F32)


def _dot_tn(a, b):
    return lax.dot_general(a, b, (((0,), (0,)), ((), ())), preferred_element_type=F32)


def _rms(x, g):
    return x * lax.rsqrt(jnp.mean(x * x, axis=-1, keepdims=True) + EPS) * g


def _lane_features(shape, first, second):
    lane = lax.broadcasted_iota(jnp.int32, shape, 1)
    return jnp.where(lane == 0, first, jnp.where(lane == 1, second, 0))


def _key_features(pos, width):
    return _lane_features((pos.shape[0], width), pos >> POS_SHIFT, pos & (POS_SPLIT - 1))


def _ones_features(n, width):
    return _lane_features((n, width), 1, 0)


def _int_to_bf16(x):
    return x.astype(F32).astype(BF16)


def _in_proj_kernel(h_ref, g_ref, w_ref, qn_ref, kcv_ref, ks_ref, kv3_ref, gl_ref, ret_ref):
    tm = h_ref.shape[1]
    xn = _rms(h_ref[0], g_ref[...]).astype(BF16)
    pair = 2 * HEAD_DIM
    pos = pl.program_id(1) * tm + lax.broadcasted_iota(jnp.int32, (tm, 1), 0)
    kfeat = _int_to_bf16(_key_features(pos, HEAD_DIM))
    vfeat = _int_to_bf16(_ones_features(tm, HEAD_DIM))
    lane = lax.broadcasted_iota(jnp.int32, (tm, MAX_SEL_BLOCKS), 1)
    block_onehot = jnp.where(lane == (pos >> SEL_SHIFT), 1.0, 0.0).astype(BF16)

    wide = 2 * pair
    per_dot = wide // HEAD_DIM
    for j in range(NSA_Q_COLS // wide):
        z = (_dot(xn, w_ref[:, j * wide:(j + 1) * wide]) * HEAD_DIM ** -0.5).astype(BF16)
        for k in range(per_dot):
            head = per_dot * j + k
            slope = 2.0 ** -(head + 1)
            qfeat = _lane_features((tm, HEAD_DIM), POS_SPLIT * slope, slope).astype(BF16)
            qn_ref[0, head] = jnp.concatenate([z[:, k * HEAD_DIM:(k + 1) * HEAD_DIM], qfeat], axis=1)
    base = NSA_Q_COLS
    z = _dot(xn, w_ref[:, base:base + wide]).astype(BF16)
    for a in range(2):
        kcv_ref[a, 0] = z[:, a * pair:(a + 1) * pair]
    base += wide
    for c in range(2):
        z = _dot(xn, w_ref[:, base + c * wide:base + (c + 1) * wide]).astype(BF16)
        for half in range(2):
            a = 2 * c + half
            for g in range(NSA_KV_HEADS):
                zg = z[:, half * pair + g * HEAD_DIM:half * pair + (g + 1) * HEAD_DIM]
                if a == 0:
                    ks_ref[0, g] = jnp.concatenate([zg, kfeat, block_onehot], axis=1)
                else:
                    kv3_ref[a - 1, 0, g] = jnp.concatenate([zg, kfeat if a == 2 else vfeat], axis=1)
    base += 2 * wide
    for a in range(4):
        ret_ref[0, :, a * RET_COLS:(a + 1) * RET_COLS] = _dot(
            xn, w_ref[:, base + a * RET_COLS:base + (a + 1) * RET_COLS]).astype(BF16)
    base += 4 * RET_COLS
    gl_ref[0] = _dot(xn, w_ref[:, base:base + NSA_KV_HEADS * V7X_LANES])


def _arrange_w_in(w):
    q_end = NSA_Q_COLS
    kv_end = q_end + 6 * NSA_KV_COLS
    gl_end = kv_end + NSA_GATE_COLS
    gl = w[:, kv_end:gl_end].reshape(D_MODEL, 3, NSA_KV_HEADS, NSA_GROUP)
    gl = jnp.transpose(gl, (0, 2, 1, 3)).reshape(D_MODEL, NSA_KV_HEADS, 3 * NSA_GROUP)
    gl = jnp.pad(gl, ((0, 0), (0, 0), (0, V7X_LANES - 3 * NSA_GROUP)))
    gl = gl.reshape(D_MODEL, NSA_KV_HEADS * V7X_LANES)
    return jnp.concatenate([w[:, :kv_end], w[:, gl_end:], gl], axis=1).astype(BF16)


def _in_proj(h, g, w):
    b, s, d = h.shape
    assert s // SEL_BLOCK <= MAX_SEL_BLOCKS
    tm = min(ROW_TILE, s)
    ncols = w.shape[1]
    aug = 2 * HEAD_DIM
    return pl.pallas_call(
        _in_proj_kernel,
        grid=(b, s // tm),
        in_specs=[
            pl.BlockSpec((1, tm, d), lambda bi, si: (bi, si, 0)),
            pl.BlockSpec((1, d), lambda bi, si: (0, 0)),
            pl.BlockSpec((d, ncols), lambda bi, si: (0, 0)),
        ],
        out_specs=[
            pl.BlockSpec((1, NSA_HEADS, tm, aug), lambda bi, si: (bi, 0, si, 0)),
            pl.BlockSpec((2, 1, tm, NSA_KV_COLS), lambda bi, si: (0, bi, si, 0)),
            pl.BlockSpec((1, NSA_KV_HEADS, tm, aug + MAX_SEL_BLOCKS), lambda bi, si: (bi, 0, si, 0)),
            pl.BlockSpec((3, 1, NSA_KV_HEADS, tm, aug), lambda bi, si: (0, bi, 0, si, 0)),
            pl.BlockSpec((1, tm, NSA_KV_HEADS * V7X_LANES), lambda bi, si: (bi, si, 0)),
            pl.BlockSpec((1, tm, 4 * RET_COLS), lambda bi, si: (bi, si, 0)),
        ],
        out_shape=[
            jax.ShapeDtypeStruct((b, NSA_HEADS, s, aug), BF16),
            jax.ShapeDtypeStruct((2, b, s, NSA_KV_COLS), BF16),
            jax.ShapeDtypeStruct((b, NSA_KV_HEADS, s, aug + MAX_SEL_BLOCKS), BF16),
            jax.ShapeDtypeStruct((3, b, NSA_KV_HEADS, s, aug), BF16),
            jax.ShapeDtypeStruct((b, s, NSA_KV_HEADS * V7X_LANES), F32),
            jax.ShapeDtypeStruct((b, s, 4 * RET_COLS), BF16),
        ],
        compiler_params=_params(("parallel", "parallel")),
        name="in_proj",
    )(h, g.reshape(1, d), w)


def _compress_kernel(x_ref, wbig_ref, pos_ref, w1_ref, w2_ref, o_ref):
    nch = x_ref.shape[2]
    u = _dot(x_ref[0, 0], wbig_ref[0])
    hid0 = _dot(pos_ref[0], w1_ref[0])[0:1]
    w2 = w2_ref[0]
    cmp_end = lax.broadcasted_iota(jnp.int32, (nch, 1), 0) * CMP_STRIDE + (CMP_BLOCK - 1)
    feat = _int_to_bf16(jnp.where(pl.program_id(0) == 0, _key_features(cmp_end, HEAD_DIM),
                                  _ones_features(nch, HEAD_DIM)))
    for g in range(NSA_KV_HEADS):
        c0 = g * 2 * CMP_HIDDEN
        first = u[:, c0:c0 + CMP_HIDDEN]
        second = u[:, c0 + CMP_HIDDEN:c0 + 2 * CMP_HIDDEN]
        hid = first + pltpu.roll(second, nch - 1, 0) + hid0
        out = _dot(jax.nn.gelu(hid).astype(BF16), w2).astype(BF16)
        o_ref[0, 0, g] = jnp.concatenate([out, feat], axis=1)


def _arrange_cmp_w1(w1):
    r = CMP_BLOCK // CMP_STRIDE
    w1r = w1.reshape(2, r, CMP_STRIDE, HEAD_DIM, CMP_HIDDEN)
    eye = jnp.eye(NSA_KV_HEADS, dtype=w1.dtype)
    big = jnp.einsum('krcdh,gf->kcgdfrh', w1r, eye)
    return big.reshape(2, CMP_STRIDE * NSA_KV_COLS, NSA_KV_HEADS * r * CMP_HIDDEN).astype(BF16)


def _compress(kcv, cmp_pos, cmp_w1, cmp_w2):
    _, b, s, _ = kcv.shape
    nch = s // CMP_STRIDE
    x = kcv.reshape(2, b, nch, CMP_STRIDE * NSA_KV_COLS)
    wbig = _arrange_cmp_w1(cmp_w1)
    pos = jnp.broadcast_to(cmp_pos.reshape(2, 1, CMP_BLOCK * HEAD_DIM),
                           (2, V7X_SUBLANES, CMP_BLOCK * HEAD_DIM)).astype(BF16)
    w1 = cmp_w1.reshape(2, CMP_BLOCK * HEAD_DIM, CMP_HIDDEN).astype(BF16)
    w2 = cmp_w2.astype(BF16)
    kdim = CMP_STRIDE * NSA_KV_COLS
    return pl.pallas_call(
        _compress_kernel,
        grid=(2, b),
        in_specs=[
            pl.BlockSpec((1, 1, nch, kdim), lambda a, bi: (a, bi, 0, 0)),
            pl.BlockSpec((1, kdim, wbig.shape[2]), lambda a, bi: (a, 0, 0)),
            pl.BlockSpec((1, V7X_SUBLANES, CMP_BLOCK * HEAD_DIM), lambda a, bi: (a, 0, 0)),
            pl.BlockSpec((1, CMP_BLOCK * HEAD_DIM, CMP_HIDDEN), lambda a, bi: (a, 0, 0)),
            pl.BlockSpec((1, CMP_HIDDEN, HEAD_DIM), lambda a, bi: (a, 0, 0)),
        ],
        out_specs=pl.BlockSpec((1, 1, NSA_KV_HEADS, nch, 2 * HEAD_DIM), lambda a, bi: (a, bi, 0, 0, 0)),
        out_shape=jax.ShapeDtypeStruct((2, b, NSA_KV_HEADS, nch, 2 * HEAD_DIM), BF16),
        compiler_params=_params(("parallel", "parallel")),
        name="compress",
    )(x, wbig, pos, w1, w2)


def _normalise_t(acc):
    return acc[:HEAD_DIM] * (1.0 / jnp.maximum(acc[HEAD_DIM:HEAD_DIM + 1], 1e-30))


def _attn_kernel(q_ref, kc_ref, vc_ref, ks_ref, vs_ref, kw_ref, vw_ref, gl_ref, ovt_ref, o_ref, tiles_ref,
                 sel_ref, oc_ref, imp_ref, *, n_pick, tk):
    qt = q_ref.shape[2]
    cols = NSA_GROUP * qt
    t0 = pl.program_id(2) * qt
    q_t = q_ref[0].reshape(cols, 2 * HEAD_DIM).T
    tq = t0 + lax.broadcasted_iota(jnp.int32, (1, qt), 1)

    def per_head(valid, masked):
        bias = jnp.where(valid, 0.0, masked)
        return jnp.concatenate([bias] * NSA_GROUP, axis=1)

    nc = kc_ref.shape[3]

    def compressed(rows):
        cmp_end = lax.broadcasted_iota(jnp.int32, (rows, 1), 0) * CMP_STRIDE + (CMP_BLOCK - 1)
        logit = _dot(kc_ref[0, 0, 0, :rows], q_t) + per_head(cmp_end <= tq, -jnp.inf)
        e = jnp.exp(logit - jnp.maximum(jnp.max(logit, axis=0, keepdims=True), NEG_INF))
        p = e * (1.0 / jnp.maximum(jnp.sum(e, axis=0, keepdims=True), 1e-30))
        oc_ref[...] = _dot(vc_ref[0, 0, 0, :rows].T, p.astype(BF16))[:HEAD_DIM]
        p_sum = p[:, 0:qt]
        for r in range(1, NSA_GROUP):
            p_sum = p_sum + p[:, r * qt:(r + 1) * qt]
        p_hi = p_sum.astype(BF16)
        p_lo = (p_sum - p_hi.astype(F32)).astype(BF16)
        imp_ref[...] = _dot(ovt_ref[:, :rows], p_hi) + _dot(ovt_ref[:, :rows], p_lo)

    n_cmp_live = (t0 + qt - CMP_BLOCK + CMP_STRIDE) // CMP_STRIDE
    cmp_sizes = [nc // 4, nc // 2, nc]
    for lower, rows in zip([None] + cmp_sizes[:-1], cmp_sizes):
        above = True if lower is None else n_cmp_live > lower
        below = True if rows == nc else n_cmp_live <= rows
        pl.when(jnp.logical_and(above, below))(functools.partial(compressed, rows))
    o_c = oc_ref[...]
    imp = imp_ref[...]

    shape = (MAX_SEL_BLOCKS, qt)
    blk = lax.broadcasted_iota(jnp.int32, shape, 0)
    back = ((t0 + lax.broadcasted_iota(jnp.int32, shape, 1)) >> SEL_SHIFT) - blk
    forced = (blk == 0) | ((back >= 0) & (back < N_LOCAL_SEL))
    score = jnp.where(forced, BIG, jnp.where(back >= 0, imp, -BIG))

    def pick(_, work):
        row = lax.broadcasted_iota(jnp.int32, work.shape, 0)
        m = jnp.max(work, axis=0, keepdims=True)
        first = jnp.min(jnp.where(work == m, row, MAX_SEL_BLOCKS), axis=0, keepdims=True)
        return jnp.where(row == first, -jnp.inf, work)

    n_live = (t0 + qt + SEL_BLOCK - 1) >> SEL_SHIFT
    sizes = [MAX_SEL_BLOCKS // 4, MAX_SEL_BLOCKS // 2, MAX_SEL_BLOCKS]
    for lower, rows in zip([0] + sizes[:-1], sizes):
        @pl.when((n_live > lower) & ((n_live <= rows) | (rows == MAX_SEL_BLOCKS)))
        def _(rows=rows):
            left = lax.fori_loop(0, n_pick, pick, score[:rows])
            sel_ref[...] = jnp.zeros(shape, F32)
            sel_ref[:rows] = jnp.where(left == -jnp.inf, 1.0, 0.0)
    sel = sel_ref[...]
    sel_bias = ((sel - 1.0) * -NEG_INF).astype(BF16)
    q_sel = jnp.concatenate([q_t, jnp.concatenate([sel_bias] * NSA_GROUP, axis=1)], axis=0)

    n_own = max(1, qt // tk)
    n_full = t0 // tk
    blocks_per_tile = tk // SEL_BLOCK
    block_used = jnp.max(sel, axis=1, keepdims=True)
    n_visit = jnp.int32(0)
    for j in range(tiles_ref.shape[0]):
        used = jnp.max(block_used[j * blocks_per_tile:(j + 1) * blocks_per_tile]) > 0.0
        tiles_ref[n_visit] = j
        n_visit = n_visit + (used & (j < n_full)).astype(jnp.int32)

    def sweep(tiles, carry, causal):
        m_i, acc = carry
        starts = [pl.multiple_of(j * tk, tk) for j in tiles]
        scores = []
        for k0 in starts:
            s = _dot(ks_ref[0, 0, pl.ds(k0, tk), :], q_sel)
            if causal:
                s = s + per_head(k0 + lax.broadcasted_iota(jnp.int32, (tk, 1), 0) <= tq, NEG_INF)
            scores.append(s)
        m_new = m_i
        for s in scores:
            m_new = jnp.maximum(m_new, jnp.max(s, axis=0, keepdims=True))
        acc = jnp.exp(m_i - m_new) * acc
        for s, k0 in zip(scores, starts):
            acc = acc + _dot(vs_ref[0, 0, 0, pl.ds(k0, tk), :].T, jnp.exp(s - m_new).astype(BF16))
        return m_new, acc

    carry = (jnp.full((1, cols), NEG_INF, F32), jnp.zeros((2 * HEAD_DIM, cols), F32))
    done = jnp.int32(0)
    for width in (1, 2, 4):
        steps = n_visit // width if width == 4 else (n_visit // width) & 1
        carry = lax.fori_loop(
            0, steps,
            lambda i, c, width=width, done=done: sweep(
                [tiles_ref[done + width * i + u] for u in range(width)], c, causal=False),
            carry)
        done = done + steps * width
    carry = sweep([n_full + j for j in range(n_own)], carry, causal=True)
    o_s = _normalise_t(carry[1])

    band = WINDOW + qt
    w0 = pl.multiple_of(jnp.maximum(t0 - WINDOW, 0), math.gcd(qt, WINDOW))
    kpos = w0 + lax.broadcasted_iota(jnp.int32, (band, 1), 0)
    valid_w = (kpos <= tq) & (kpos > tq - WINDOW)
    logit_w = _dot(kw_ref[0, 0, 0, pl.ds(w0, band), :], q_t) + per_head(valid_w, NEG_INF)
    e_w = jnp.exp(logit_w - jnp.max(logit_w, axis=0, keepdims=True))
    o_w = _normalise_t(_dot(vw_ref[0, 0, 0, pl.ds(w0, band), :].T, e_w.astype(BF16)))

    gate_t = jax.nn.sigmoid(gl_ref[0]).T

    def branch_gate(branch):
        first = branch * NSA_GROUP
        return jnp.concatenate([gate_t[first + r:first + r + 1] for r in range(NSA_GROUP)], axis=1)

    o_t = branch_gate(0) * o_c + branch_gate(1) * o_s + branch_gate(2) * o_w
    o = o_t.T
    o_ref[0] = jnp.concatenate([o[r * qt:(r + 1) * qt] for r in range(NSA_GROUP)], axis=1).astype(o_ref.dtype)


def _nsa_attention(qn, kv_cmp, ks, kv3, gl):
    b, _, s, _ = qn.shape
    nc = kv_cmp.shape[3]
    nsel = s // SEL_BLOCK
    n_pick = min(N_SELECT, nsel)
    tk = min(SEL_KV_TILE, s)
    aug = 2 * HEAD_DIM
    qt = min(ATTN_Q_TILE, s)
    assert qt % tk == 0 or tk % qt == 0
    cmp_start = jnp.arange(nc) * CMP_STRIDE
    sel_start = jnp.arange(nsel) * SEL_BLOCK
    overlap = (jnp.minimum(cmp_start[:, None] + CMP_BLOCK, sel_start[None, :] + SEL_BLOCK)
               > jnp.maximum(cmp_start[:, None], sel_start[None, :]))
    overlap_t = jnp.pad(overlap.T.astype(BF16), ((0, MAX_SEL_BLOCKS - nsel), (0, 0)))
    kv_spec = lambda a: pl.BlockSpec((1, 1, 1, s, aug), lambda bi, g, i, a=a: (a, bi, g, 0, 0))
    cmp_spec = lambda a: pl.BlockSpec((1, 1, 1, nc, aug), lambda bi, g, i, a=a: (a, bi, g, 0, 0))
    return pl.pallas_call(
        functools.partial(_attn_kernel, n_pick=n_pick, tk=tk),
        grid=(b, NSA_KV_HEADS, s // qt),
        in_specs=[
            pl.BlockSpec((1, NSA_GROUP, qt, aug), lambda bi, g, i: (bi, g, i, 0)),
            cmp_spec(0), cmp_spec(1),
            pl.BlockSpec((1, 1, s, aug + MAX_SEL_BLOCKS), lambda bi, g, i: (bi, g, 0, 0)),
            kv_spec(0), kv_spec(1), kv_spec(2),
            pl.BlockSpec((1, qt, V7X_LANES), lambda bi, g, i: (bi, i, g)),
            pl.BlockSpec((MAX_SEL_BLOCKS, nc), lambda bi, g, i: (0, 0)),
        ],
        out_specs=pl.BlockSpec((1, qt, NSA_GROUP * HEAD_DIM), lambda bi, g, i: (bi, i, g)),
        out_shape=jax.ShapeDtypeStruct((b, s, NSA_Q_COLS), BF16),
        scratch_shapes=[pltpu.SMEM((s // tk,), jnp.int32), pltpu.VMEM((MAX_SEL_BLOCKS, qt), F32),
                        pltpu.VMEM((HEAD_DIM, NSA_GROUP * qt), F32), pltpu.VMEM((MAX_SEL_BLOCKS, qt), F32)],
        compiler_params=_params(("parallel", "parallel", "arbitrary")),
        name="nsa_attention",
    )(qn, kv_cmp, kv_cmp, ks, kv3, kv3, kv3, gl, overlap_t)


def _retention_kernel(q_ref, k_ref, v_ref, g_ref, decay_ref, xi_ref, zeta_ref, gch_ref, same_ref, gn_ref,
                      o_ref, state_ref):
    @pl.when(pl.program_id(1) == 0)
    def _():
        state_ref[...] = jnp.zeros_like(state_ref)

    width = same_ref.shape[0]
    heads = width // HEAD_DIM
    c = q_ref.shape[1]
    scale = jnp.asarray(HEAD_DIM ** -0.5, BF16)
    lane_head = lax.broadcasted_iota(jnp.int32, (c, width), 1) // HEAD_DIM
    same = same_ref[...]
    mean_w = (same * (1.0 / HEAD_DIM)).astype(BF16)

    def head_mean(x):
        hi = x.astype(BF16)
        lo = (x - hi.astype(F32)).astype(BF16)
        return _dot(hi, mean_w) + _dot(lo, mean_w)

    def stack(x):
        zero = jnp.zeros_like(x)
        return jnp.concatenate([jnp.where(lane_head == j, x, zero) for j in range(heads)], axis=0)

    for i in range(q_ref.shape[0]):
        outs = []
        for p in range(RET_COLS // width):
            cols = slice(p * width, (p + 1) * width)
            q = q_ref[i, :, cols]
            k = k_ref[i, :, cols] * scale
            v = v_ref[i, :, cols]
            state = state_ref[i, p]
            inner = _dot_nt(q, stack(k)) * decay_ref[p]
            o = _dot(inner.astype(BF16), stack(v)) + _dot(q, state.astype(BF16)) * xi_ref[p]
            kz = (k.astype(F32) * zeta_ref[p]).astype(BF16)
            state_ref[i, p] = gch_ref[p] * state + same * _dot_tn(kz, v)
            centred = o - head_mean(o)
            outs.append(centred * lax.rsqrt(head_mean(jnp.square(centred)) + EPS))
        o = jnp.concatenate(outs, axis=1) * gn_ref[...]
        o_ref[i] = (jax.nn.silu(g_ref[i].astype(F32)) * o).astype(o_ref.dtype)


def _retention(ret, gn_gain):
    b, s, _ = ret.shape
    c = RET_CHUNK
    hh = RET_HEADS
    hg = RET_HEAD_GROUP
    width = hg * HEAD_DIM
    log_gamma = jnp.log1p(-jnp.exp2(-5.0 - jnp.arange(hh, dtype=F32)))
    pos = jnp.arange(c, dtype=F32)
    diff = pos[:, None] - pos[None, :]
    decay = jnp.where(diff >= 0, jnp.exp(jnp.maximum(diff, 0.0)[None] * log_gamma[:, None, None]), 0.0)
    xi = jnp.exp((pos + 1.0)[None] * log_gamma[:, None])[..., None]
    zeta = jnp.exp((c - 1.0 - pos)[None] * log_gamma[:, None])[..., None]
    g_chunk = jnp.exp(c * log_gamma)[:, None, None]
    side_by_side = lambda t: jnp.concatenate([t[j::hg] for j in range(hg)], axis=-1)
    decay = side_by_side(decay)
    xi = side_by_side(jnp.broadcast_to(xi, (hh, c, HEAD_DIM)))
    zeta = side_by_side(jnp.broadcast_to(zeta, (hh, c, HEAD_DIM)))
    head_of = jnp.arange(width) // HEAD_DIM
    same = (head_of[:, None] == head_of[None, :]).astype(F32)
    g_lane = side_by_side(jnp.broadcast_to(g_chunk, (hh, 1, HEAD_DIM)))
    g_chunk = same[None] * g_lane
    nb = math.gcd(RET_BATCH, b)
    part = lambda a: pl.BlockSpec((nb, c, RET_COLS), lambda bi, n, a=a: (bi, n, a))
    full = lambda shape: pl.BlockSpec(shape, lambda bi, n: (0,) * len(shape))
    return pl.pallas_call(
        _retention_kernel,
        grid=(b // nb, s // c),
        in_specs=[part(0), part(1), part(2), part(3),
                  full((hh // hg, c, hg * c)), full((hh // hg, c, width)), full((hh // hg, c, width)),
                  full((hh // hg, width, width)), full((width, width)), full((1, RET_COLS))],
        out_specs=pl.BlockSpec((nb, c, RET_COLS), lambda bi, n: (bi, n, 0)),
        out_shape=jax.ShapeDtypeStruct((b, s, RET_COLS), BF16),
        scratch_shapes=[pltpu.VMEM((nb, hh // hg, width, width), F32)],
        compiler_params=_params(("parallel", "arbitrary")),
        name="retention",
    )(ret, ret, ret, ret, decay, xi, zeta, g_chunk, same, gn_gain.reshape(1, RET_COLS).astype(F32))


def _out_proj_kernel(h_ref, a_ref, r_ref, w_ref, g_ref, h_out_ref, hn_ref):
    h = h_ref[...] + _dot(a_ref[...], w_ref[:NSA_Q_COLS]) + _dot(r_ref[...], w_ref[NSA_Q_COLS:])
    h_out_ref[...] = h
    hn_ref[...] = _rms(h, g_ref[...]).astype(BF16)


def _out_proj(h, a, r, w, g):
    t, d = h.shape
    tm = min(STREAM_ROW_TILE, t)
    row = lambda n: pl.BlockSpec((tm, n), lambda i: (i, 0))
    return pl.pallas_call(
        _out_proj_kernel,
        grid=(t // tm,),
        in_specs=[row(d), row(NSA_Q_COLS), row(RET_COLS),
                  pl.BlockSpec((MIX_WIDTH, d), lambda i: (0, 0)),
                  pl.BlockSpec((1, d), lambda i: (0, 0))],
        out_specs=[row(d), row(d)],
        out_shape=[jax.ShapeDtypeStruct((t, d), F32), jax.ShapeDtypeStruct((t, d), BF16)],
        compiler_params=_params(("parallel",)),
        name="out_proj",
    )(h, a, r, w.astype(BF16), g.reshape(1, d))


def _swiglu_chunk(x, wg, wu, wd):
    hid = jax.nn.silu(_dot(x, wg)) * _dot(x, wu)
    return _dot(hid.astype(BF16), wd)


def _ffn_kernel(x_ref, wg_ref, wu_ref, wd_ref, o_ref):
    x = x_ref[...]
    acc = None
    for f in range(D_FF // FFN_COL_TILE):
        cols = slice(f * FFN_COL_TILE, (f + 1) * FFN_COL_TILE)
        part = _swiglu_chunk(x, wg_ref[:, cols], wu_ref[:, cols], wd_ref[cols, :])
        acc = part if acc is None else acc + part
    o_ref[...] = acc


def _dense_ffn(hn, wg, wu, wd):
    t, d = hn.shape
    tm = min(FFN_ROW_TILE, t)
    once = pl.Buffered(1)
    return pl.pallas_call(
        _ffn_kernel,
        grid=(t // tm,),
        in_specs=[pl.BlockSpec((tm, d), lambda i: (i, 0)),
                  pl.BlockSpec((d, D_FF), lambda i: (0, 0), pipeline_mode=once),
                  pl.BlockSpec((d, D_FF), lambda i: (0, 0), pipeline_mode=once),
                  pl.BlockSpec((D_FF, d), lambda i: (0, 0), pipeline_mode=once)],
        out_specs=pl.BlockSpec((tm, d), lambda i: (i, 0)),
        out_shape=jax.ShapeDtypeStruct((t, d), F32),
        compiler_params=_params(("parallel",)),
        name="dense_ffn",
    )(hn, wg.astype(BF16), wu.astype(BF16), wd.astype(BF16))


def _lane_column(table, lane_index):
    lane = lax.broadcasted_iota(jnp.int32, table.shape, 1)
    col = jnp.sum(jnp.where(lane == lane_index, table, 0.0), axis=-1, keepdims=True)
    return jnp.broadcast_to(col, table.shape)


def _moe_kernel(x_ref, router_ref, wg_ref, wu_ref, wd_ref, o_ref,
                xs_ref, y_ref, slot_ref, gate_ref, slot_row_ref, slot_e_ref, gate_e_ref, count_ref):
    e = pl.program_id(1)
    f = pl.program_id(2)
    tm, d = x_ref.shape
    sub = MOE_GROUP_ROWS

    @pl.when((e == 0) & (f == 0))
    def _route():
        o_ref[...] = jnp.zeros_like(o_ref)
        logits = _dot(x_ref[...], router_ref[...])
        lane = lax.broadcasted_iota(jnp.int32, logits.shape, 1)
        logits = jnp.where(lane < N_EXPERTS, logits, -jnp.inf)
        v1 = jnp.max(logits, axis=-1, keepdims=True)
        i1 = jnp.min(jnp.where(logits == v1, lane, V7X_LANES), axis=-1, keepdims=True)
        rest = jnp.where(lane == i1, -jnp.inf, logits)
        v2 = jnp.max(rest, axis=-1, keepdims=True)
        i2 = jnp.min(jnp.where(rest == v2, lane, V7X_LANES), axis=-1, keepdims=True)
        e2 = jnp.exp(v2 - v1)
        inv = 1.0 / (1.0 + e2)
        gate_ref[...] = jnp.where(lane == i1, inv, 0.0) + jnp.where(lane == i2, e2 * inv, 0.0)
        routed = jnp.where((lane == i1) | (lane == i2), 1.0, 0.0)
        c = MOE_RANK_CHUNK
        before = (lax.broadcasted_iota(jnp.int32, (c, c), 1) < lax.broadcasted_iota(jnp.int32, (c, c), 0))
        before = jnp.where(before, 1.0, 0.0).astype(BF16)
        offset = jnp.zeros((1, V7X_LANES), F32)
        for j in range(tm // c):
            part = routed[j * c:(j + 1) * c]
            rank = _dot(before, part.astype(BF16)) + offset
            slot_ref[j * c:(j + 1) * c, :] = jnp.where(part > 0.0, rank, -1.0)
            offset = offset + jnp.sum(part, axis=0, keepdims=True)
        for ee in range(N_EXPERTS):
            count_ref[ee] = offset[0, ee].astype(jnp.int32)
        slot_row_ref[...] = slot_ref[...].T[:N_EXPERTS]

    n_groups = (count_ref[e] + sub - 1) // sub

    @pl.when(f == 0)
    def _gather():
        slot_e_ref[...] = _lane_column(slot_ref[...], e)
        gate_e_ref[...] = _lane_column(gate_ref[...], e)
        slot_row = slot_row_ref[pl.ds(e, 1), :]

        def body(s, _):
            r0 = pl.multiple_of(s * sub, sub)
            want = (r0 + lax.broadcasted_iota(jnp.int32, (sub, 1), 0)).astype(F32)
            onehot = jnp.where(slot_row == want, 1.0, 0.0).astype(BF16)
            xs_ref[pl.ds(r0, sub), :] = _dot(onehot, x_ref[...]).astype(BF16)
            y_ref[pl.ds(r0, sub), :] = jnp.zeros((sub, d), F32)
            return 0

        lax.fori_loop(0, n_groups, body, 0)

    def expert(i, _, base, n):
        r0 = pl.multiple_of(base + i * n, n)
        y_ref[pl.ds(r0, n), :] += _swiglu_chunk(xs_ref[pl.ds(r0, n), :], wg_ref[0], wu_ref[0], wd_ref[0])
        return 0

    tail = MOE_TAIL_ROWS
    n_filled = count_ref[e] // sub
    rest = count_ref[e] - n_filled * sub
    short = rest <= tail
    n_whole = jnp.where(short, n_filled, n_filled + 1)
    n_tail = jnp.where(short & (rest > 0), 1, 0)
    lax.fori_loop(0, n_whole, functools.partial(expert, base=0, n=sub), 0)
    lax.fori_loop(0, n_tail, functools.partial(expert, base=n_filled * sub, n=tail), 0)

    @pl.when(f == pl.num_programs(2) - 1)
    def _scatter():
        tc = MOE_SCATTER_ROWS

        def body(s, _):
            r0 = pl.multiple_of(s * sub, sub)
            y = y_ref[pl.ds(r0, sub), :].astype(BF16)
            want = (r0 + lax.broadcasted_iota(jnp.int32, (1, sub), 1)).astype(F32)
            for j in range(tm // tc):
                rows = slice(j * tc, (j + 1) * tc)
                slot = jnp.concatenate([slot_e_ref[rows, :]] * (sub // V7X_LANES), axis=1)
                onehot = jnp.where(slot == want, 1.0, 0.0).astype(BF16)
                weight = jnp.concatenate([gate_e_ref[rows, :]] * (d // V7X_LANES), axis=1)
                o_ref[rows, :] += weight * _dot(onehot, y)
            return 0

        lax.fori_loop(0, n_groups, body, 0)


def _moe_ffn(hn, router, wg, wu, wd):
    t, d = hn.shape
    tm = min(MOE_ROW_TILE, t)
    tf = MOE_COL_TILE
    nf = D_FF // tf
    router = jnp.pad(router, ((0, 0), (0, V7X_LANES - N_EXPERTS))).astype(BF16)
    once = pl.Buffered(1)
    return pl.pallas_call(
        _moe_kernel,
        grid=(t // tm, N_EXPERTS, nf),
        in_specs=[pl.BlockSpec((tm, d), lambda i, e, f: (i, 0), pipeline_mode=once),
                  pl.BlockSpec((d, V7X_LANES), lambda i, e, f: (0, 0), pipeline_mode=once),
                  pl.BlockSpec((1, d, tf), lambda i, e, f: (e, 0, f)),
                  pl.BlockSpec((1, d, tf), lambda i, e, f: (e, 0, f)),
                  pl.BlockSpec((1, tf, d), lambda i, e, f: (e, f, 0))],
        out_specs=pl.BlockSpec((tm, d), lambda i, e, f: (i, 0), pipeline_mode=once),
        out_shape=jax.ShapeDtypeStruct((t, d), F32),
        scratch_shapes=[pltpu.VMEM((tm, d), BF16), pltpu.VMEM((tm, d), F32),
                        pltpu.VMEM((tm, V7X_LANES), F32), pltpu.VMEM((tm, V7X_LANES), F32),
                        pltpu.VMEM((N_EXPERTS, tm), F32),
                        pltpu.VMEM((tm, V7X_LANES), F32), pltpu.VMEM((tm, V7X_LANES), F32),
                        pltpu.SMEM((N_EXPERTS,), jnp.int32)],
        compiler_params=_params(("parallel", "arbitrary", "arbitrary"), vmem=MOE_VMEM_LIMIT),
        name="moe_ffn",
    )(hn, router, wg.astype(BF16), wu.astype(BF16), wd.astype(BF16))


def _ple_kernel(h_ref, f_ref, p_ref, g_ref, proj_ref, gate_ref, gf_ref, o_ref, *, final_norm):
    h = h_ref[...] + f_ref[...]
    emb = _dot(p_ref[...].astype(BF16), proj_ref[...])
    sig = jax.nn.sigmoid(_dot(_rms(h, g_ref[...]).astype(BF16), gate_ref[...]))
    h = h + emb * sig
    if final_norm:
        h = _rms(h, gf_ref[...])
    o_ref[...] = h


def _ple(h, f, p, g, proj, gate, g_final, final_norm):
    t, d = h.shape
    tm = min(STREAM_ROW_TILE, t)
    vec = pl.BlockSpec((1, d), lambda i: (0, 0))
    return pl.pallas_call(
        functools.partial(_ple_kernel, final_norm=final_norm),
        grid=(t // tm,),
        in_specs=[pl.BlockSpec((tm, d), lambda i: (i, 0)),
                  pl.BlockSpec((tm, d), lambda i: (i, 0)),
                  pl.BlockSpec((tm, PLE_DIM), lambda i: (i, 0)),
                  vec,
                  pl.BlockSpec((PLE_DIM, d), lambda i: (0, 0)),
                  pl.BlockSpec((d, d), lambda i: (0, 0)),
                  vec],
        out_specs=pl.BlockSpec((tm, d), lambda i: (i, 0)),
        out_shape=jax.ShapeDtypeStruct((t, d), F32),
        compiler_params=_params(("parallel",)),
        name="ple",
    )(h, f, p, g.reshape(1, d), proj.astype(BF16), gate.astype(BF16), g_final.reshape(1, d))


def kernel(x, p, w_in, w_out, g_mix, g_ffn, g_ple, g_final, cmp_pos, cmp_w1, cmp_w2, ret_gn,
           ffn_gate, ffn_up, ffn_down, moe_router, moe_gate, moe_up, moe_down, ple_proj, ple_gate):
    b, s, d = x.shape
    depth = w_in.shape[0]
    t = b * s
    h = x
    for i in range(depth):
        qn, kcv, ks, kv3, gl, ret = _in_proj(h.reshape(b, s, d), g_mix[i], _arrange_w_in(w_in[i]))
        kv_cmp = _compress(kcv, cmp_pos[i], cmp_w1[i], cmp_w2[i])
        a = _nsa_attention(qn, kv_cmp, ks, kv3, gl)
        r = _retention(ret, ret_gn[i])
        h, hn = _out_proj(h.reshape(t, d), a.reshape(t, NSA_Q_COLS), r.reshape(t, RET_COLS), w_out[i], g_ffn[i])
        if i % 2 == 0:
            f = _dense_ffn(hn, ffn_gate[i // 2], ffn_up[i // 2], ffn_down[i // 2])
        else:
            f = _moe_ffn(hn, moe_router[i // 2], moe_gate[i // 2], moe_up[i // 2], moe_down[i // 2])
        h = _ple(h, f, p[i].reshape(t, PLE_DIM), g_ple[i], ple_proj[i], ple_gate[i], g_final, i == depth - 1)
    return h.reshape(b, s, d)
```

```python
import functools
import math

import jax
import jax.numpy as jnp
from jax import lax
from jax.experimental import pallas as pl
from jax.experimental.pallas import tpu as pltpu

F32 = jnp.float32
BF16 = jnp.bfloat16

D_MODEL = 1024
HEAD_DIM = 64
NSA_HEADS = 8
NSA_KV_HEADS = 2
NSA_GROUP = NSA_HEADS // NSA_KV_HEADS
RET_HEADS = 8
CMP_BLOCK = 32
CMP_STRIDE = 16
CMP_HIDDEN = 256
SEL_BLOCK = 64
N_SELECT = 16
N_LOCAL_SEL = 2
WINDOW = 512
RET_CHUNK = 128
D_FF = 3584
N_EXPERTS = 8
PLE_DIM = 256
EPS = 1e-6
NEG_INF = -1e30
BIG = 1e9

NSA_Q_COLS = NSA_HEADS * HEAD_DIM
NSA_KV_COLS = NSA_KV_HEADS * HEAD_DIM
NSA_GATE_COLS = 3 * NSA_HEADS
RET_COLS = RET_HEADS * HEAD_DIM
MIX_WIDTH = NSA_Q_COLS + RET_COLS

V7X_LANES = 128
V7X_SUBLANES = 8
V7X_VMEM_BYTES = 64 * 1024 * 1024
VMEM_LIMIT = V7X_VMEM_BYTES * 3 // 4

ROW_TILE = 512
STREAM_ROW_TILE = 1024
ATTN_Q_TILE = 256
SEL_KV_TILE = 256
RET_BATCH = 4
RET_HEAD_GROUP = 4
FFN_ROW_TILE = 512
FFN_COL_TILE = 512
MOE_COL_TILE = 1792
MOE_ROW_TILE = 2048
MOE_GROUP_ROWS = 256
MOE_TAIL_ROWS = 64
MOE_RANK_CHUNK = 256
MOE_SCATTER_ROWS = 512
MOE_VMEM_LIMIT = V7X_VMEM_BYTES * 7 // 8

POS_SHIFT = 6
POS_SPLIT = 1 << POS_SHIFT
SEL_SHIFT = SEL_BLOCK.bit_length() - 1
MAX_SEL_BLOCKS = V7X_LANES


def _params(sem, vmem=VMEM_LIMIT):
    return pltpu.CompilerParams(dimension_semantics=sem, vmem_limit_bytes=vmem)


def _dot(a, b):
    return jnp.dot(a, b, preferred_element_type=F32)


def _dot_nt(a, b):
    return lax.dot_general(a, b, (((1,), (1,)), ((), ())), preferred_element_type=F32)


def _dot_tn(a, b):
    return lax.dot_general(a, b, (((0,), (0,)), ((), ())), preferred_element_type=F32)


def _rms(x, g):
    return x * lax.rsqrt(jnp.mean(x * x, axis=-1, keepdims=True) + EPS) * g


def _lane_features(shape, first, second):
    lane = lax.broadcasted_iota(jnp.int32, shape, 1)
    return jnp.where(lane == 0, first, jnp.where(lane == 1, second, 0))


def _key_features(pos, width):
    return _lane_features((pos.shape[0], width), pos >> POS_SHIFT, pos & (POS_SPLIT - 1))


def _ones_features(n, width):
    return _lane_features((n, width), 1, 0)


def _int_to_bf16(x):
    return x.astype(F32).astype(BF16)


def _in_proj_kernel(h_ref, g_ref, w_ref, qn_ref, kcv_ref, ks_ref, kv3_ref, gl_ref, ret_ref):
    tm = h_ref.shape[1]
    xn = _rms(h_ref[0], g_ref[...]).astype(BF16)
    pair = 2 * HEAD_DIM
    pos = pl.program_id(1) * tm + lax.broadcasted_iota(jnp.int32, (tm, 1), 0)
    kfeat = _int_to_bf16(_key_features(pos, HEAD_DIM))
    vfeat = _int_to_bf16(_ones_features(tm, HEAD_DIM))
    lane = lax.broadcasted_iota(jnp.int32, (tm, MAX_SEL_BLOCKS), 1)
    block_onehot = jnp.where(lane == (pos >> SEL_SHIFT), 1.0, 0.0).astype(BF16)

    wide = 2 * pair
    per_dot = wide // HEAD_DIM
    for j in range(NSA_Q_COLS // wide):
        z = (_dot(xn, w_ref[:, j * wide:(j + 1) * wide]) * HEAD_DIM ** -0.5).astype(BF16)
        for k in range(per_dot):
            head = per_dot * j + k
            slope = 2.0 ** -(head + 1)
            qfeat = _lane_features((tm, HEAD_DIM), POS_SPLIT * slope, slope).astype(BF16)
            qn_ref[0, head] = jnp.concatenate([z[:, k * HEAD_DIM:(k + 1) * HEAD_DIM], qfeat], axis=1)
    base = NSA_Q_COLS
    z = _dot(xn, w_ref[:, base:base + wide]).astype(BF16)
    for a in range(2):
        kcv_ref[a, 0] = z[:, a * pair:(a + 1) * pair]
    base += wide
    for c in range(2):
        z = _dot(xn, w_ref[:, base + c * wide:base + (c + 1) * wide]).astype(BF16)
        for half in range(2):
            a = 2 * c + half
            for g in range(NSA_KV_HEADS):
                zg = z[:, half * pair + g * HEAD_DIM:half * pair + (g + 1) * HEAD_DIM]
                if a == 0:
                    ks_ref[0, g] = jnp.concatenate([zg, kfeat, block_onehot], axis=1)
                else:
                    kv3_ref[a - 1, 0, g] = jnp.concatenate([zg, kfeat if a == 2 else vfeat], axis=1)
    base += 2 * wide
    for a in range(4):
        ret_ref[0, :, a * RET_COLS:(a + 1) * RET_COLS] = _dot(
            xn, w_ref[:, base + a * RET_COLS:base + (a + 1) * RET_COLS]).astype(BF16)
    base += 4 * RET_COLS
    gl_ref[0] = _dot(xn, w_ref[:, base:base + NSA_KV_HEADS * V7X_LANES])


def _arrange_w_in(w):
    q_end = NSA_Q_COLS
    kv_end = q_end + 6 * NSA_KV_COLS
    gl_end = kv_end + NSA_GATE_COLS
    gl = w[:, kv_end:gl_end].reshape(D_MODEL, 3, NSA_KV_HEADS, NSA_GROUP)
    gl = jnp.transpose(gl, (0, 2, 1, 3)).reshape(D_MODEL, NSA_KV_HEADS, 3 * NSA_GROUP)
    gl = jnp.pad(gl, ((0, 0), (0, 0), (0, V7X_LANES - 3 * NSA_GROUP)))
    gl = gl.reshape(D_MODEL, NSA_KV_HEADS * V7X_LANES)
    return jnp.concatenate([w[:, :kv_end], w[:, gl_end:], gl], axis=1).astype(BF16)


def _in_proj(h, g, w):
    b, s, d = h.shape
    assert s // SEL_BLOCK <= MAX_SEL_BLOCKS
    tm = min(ROW_TILE, s)
    ncols = w.shape[1]
    aug = 2 * HEAD_DIM
    return pl.pallas_call(
        _in_proj_kernel,
        grid=(b, s // tm),
        in_specs=[
            pl.BlockSpec((1, tm, d), lambda bi, si: (bi, si, 0)),
            pl.BlockSpec((1, d), lambda bi, si: (0, 0)),
            pl.BlockSpec((d, ncols), lambda bi, si: (0, 0)),
        ],
        out_specs=[
            pl.BlockSpec((1, NSA_HEADS, tm, aug), lambda bi, si: (bi, 0, si, 0)),
            pl.BlockSpec((2, 1, tm, NSA_KV_COLS), lambda bi, si: (0, bi, si, 0)),
            pl.BlockSpec((1, NSA_KV_HEADS, tm, aug + MAX_SEL_BLOCKS), lambda bi, si: (bi, 0, si, 0)),
            pl.BlockSpec((3, 1, NSA_KV_HEADS, tm, aug), lambda bi, si: (0, bi, 0, si, 0)),
            pl.BlockSpec((1, tm, NSA_KV_HEADS * V7X_LANES), lambda bi, si: (bi, si, 0)),
            pl.BlockSpec((1, tm, 4 * RET_COLS), lambda bi, si: (bi, si, 0)),
        ],
        out_shape=[
            jax.ShapeDtypeStruct((b, NSA_HEADS, s, aug), BF16),
            jax.ShapeDtypeStruct((2, b, s, NSA_KV_COLS), BF16),
            jax.ShapeDtypeStruct((b, NSA_KV_HEADS, s, aug + MAX_SEL_BLOCKS), BF16),
            jax.ShapeDtypeStruct((3, b, NSA_KV_HEADS, s, aug), BF16),
            jax.ShapeDtypeStruct((b, s, NSA_KV_HEADS * V7X_LANES), F32),
            jax.ShapeDtypeStruct((b, s, 4 * RET_COLS), BF16),
        ],
        compiler_params=_params(("parallel", "parallel")),
        name="in_proj",
    )(h, g.reshape(1, d), w)


def _compress_kernel(x_ref, wbig_ref, pos_ref, w1_ref, w2_ref, o_ref):
    nch = x_ref.shape[2]
    u = _dot(x_ref[0, 0], wbig_ref[0])
    hid0 = _dot(pos_ref[0], w1_ref[0])[0:1]
    w2 = w2_ref[0]
    cmp_end = lax.broadcasted_iota(jnp.int32, (nch, 1), 0) * CMP_STRIDE + (CMP_BLOCK - 1)
    feat = _int_to_bf16(jnp.where(pl.program_id(0) == 0, _key_features(cmp_end, HEAD_DIM),
                                  _ones_features(nch, HEAD_DIM)))
    for g in range(NSA_KV_HEADS):
        c0 = g * 2 * CMP_HIDDEN
        first = u[:, c0:c0 + CMP_HIDDEN]
        second = u[:, c0 + CMP_HIDDEN:c0 + 2 * CMP_HIDDEN]
        hid = first + pltpu.roll(second, nch - 1, 0) + hid0
        out = _dot(jax.nn.gelu(hid).astype(BF16), w2).astype(BF16)
        o_ref[0, 0, g] = jnp.concatenate([out, feat], axis=1)


def _arrange_cmp_w1(w1):
    r = CMP_BLOCK // CMP_STRIDE
    w1r = w1.reshape(2, r, CMP_STRIDE, HEAD_DIM, CMP_HIDDEN)
    eye = jnp.eye(NSA_KV_HEADS, dtype=w1.dtype)
    big = jnp.einsum('krcdh,gf->kcgdfrh', w1r, eye)
    return big.reshape(2, CMP_STRIDE * NSA_KV_COLS, NSA_KV_HEADS * r * CMP_HIDDEN).astype(BF16)


def _compress(kcv, cmp_pos, cmp_w1, cmp_w2):
    _, b, s, _ = kcv.shape
    nch = s // CMP_STRIDE
    x = kcv.reshape(2, b, nch, CMP_STRIDE * NSA_KV_COLS)
    wbig = _arrange_cmp_w1(cmp_w1)
    pos = jnp.broadcast_to(cmp_pos.reshape(2, 1, CMP_BLOCK * HEAD_DIM),
                           (2, V7X_SUBLANES, CMP_BLOCK * HEAD_DIM)).astype(BF16)
    w1 = cmp_w1.reshape(2, CMP_BLOCK * HEAD_DIM, CMP_HIDDEN).astype(BF16)
    w2 = cmp_w2.astype(BF16)
    kdim = CMP_STRIDE * NSA_KV_COLS
    return pl.pallas_call(
        _compress_kernel,
        grid=(2, b),
        in_specs=[
            pl.BlockSpec((1, 1, nch, kdim), lambda a, bi: (a, bi, 0, 0)),
            pl.BlockSpec((1, kdim, wbig.shape[2]), lambda a, bi: (a, 0, 0)),
            pl.BlockSpec((1, V7X_SUBLANES, CMP_BLOCK * HEAD_DIM), lambda a, bi: (a, 0, 0)),
            pl.BlockSpec((1, CMP_BLOCK * HEAD_DIM, CMP_HIDDEN), lambda a, bi: (a, 0, 0)),
            pl.BlockSpec((1, CMP_HIDDEN, HEAD_DIM), lambda a, bi: (a, 0, 0)),
        ],
        out_specs=pl.BlockSpec((1, 1, NSA_KV_HEADS, nch, 2 * HEAD_DIM), lambda a, bi: (a, bi, 0, 0, 0)),
        out_shape=jax.ShapeDtypeStruct((2, b, NSA_KV_HEADS, nch, 2 * HEAD_DIM), BF16),
        compiler_params=_params(("parallel", "parallel")),
        name="compress",
    )(x, wbig, pos, w1, w2)


def _normalise_t(acc):
    return acc[:HEAD_DIM] * (1.0 / jnp.maximum(acc[HEAD_DIM:HEAD_DIM + 1], 1e-30))


def _attn_kernel(q_ref, kc_ref, vc_ref, ks_ref, vs_ref, kw_ref, vw_ref, gl_ref, ovt_ref, o_ref, tiles_ref,
                 sel_ref, oc_ref, imp_ref, *, n_pick, tk):
    qt = q_ref.shape[2]
    cols = NSA_GROUP * qt
    t0 = pl.program_id(2) * qt
    q_t = q_ref[0].reshape(cols, 2 * HEAD_DIM).T
    tq = t0 + lax.broadcasted_iota(jnp.int32, (1, qt), 1)

    def per_head(valid, masked):
        bias = jnp.where(valid, 0.0, masked)
        return jnp.concatenate([bias] * NSA_GROUP, axis=1)

    nc = kc_ref.shape[3]

    def compressed(rows):
        cmp_end = lax.broadcasted_iota(jnp.int32, (rows, 1), 0) * CMP_STRIDE + (CMP_BLOCK - 1)
        logit = _dot(kc_ref[0, 0, 0, :rows], q_t) + per_head(cmp_end <= tq, -jnp.inf)
        e = jnp.exp(logit - jnp.maximum(jnp.max(logit, axis=0, keepdims=True), NEG_INF))
        p = e * (1.0 / jnp.maximum(jnp.sum(e, axis=0, keepdims=True), 1e-30))
        oc_ref[...] = _dot(vc_ref[0, 0, 0, :rows].T, p.astype(BF16))[:HEAD_DIM]
        p_sum = p[:, 0:qt]
        for r in range(1, NSA_GROUP):
            p_sum = p_sum + p[:, r * qt:(r + 1) * qt]
        p_hi = p_sum.astype(BF16)
        p_lo = (p_sum - p_hi.astype(F32)).astype(BF16)
        imp_ref[...] = _dot(ovt_ref[:, :rows], p_hi) + _dot(ovt_ref[:, :rows], p_lo)

    n_cmp_live = (t0 + qt - CMP_BLOCK + CMP_STRIDE) // CMP_STRIDE
    cmp_sizes = [nc // 4, nc // 2, nc]
    for lower, rows in zip([None] + cmp_sizes[:-1], cmp_sizes):
        above = True if lower is None else n_cmp_live > lower
        below = True if rows == nc else n_cmp_live <= rows
        pl.when(jnp.logical_and(above, below))(functools.partial(compressed, rows))
    o_c = oc_ref[...]
    imp = imp_ref[...]

    shape = (MAX_SEL_BLOCKS, qt)
    blk = lax.broadcasted_iota(jnp.int32, shape, 0)
    back = ((t0 + lax.broadcasted_iota(jnp.int32, shape, 1)) >> SEL_SHIFT) - blk
    forced = (blk == 0) | ((back >= 0) & (back < N_LOCAL_SEL))
    score = jnp.where(forced, BIG, jnp.where(back >= 0, imp, -BIG))

    def pick(_, work):
        row = lax.broadcasted_iota(jnp.int32, work.shape, 0)
        m = jnp.max(work, axis=0, keepdims=True)
        first = jnp.min(jnp.where(work == m, row, MAX_SEL_BLOCKS), axis=0, keepdims=True)
        return jnp.where(row == first, -jnp.inf, work)

    n_live = (t0 + qt + SEL_BLOCK - 1) >> SEL_SHIFT
    sizes = [MAX_SEL_BLOCKS // 4, MAX_SEL_BLOCKS // 2, MAX_SEL_BLOCKS]
    for lower, rows in zip([0] + sizes[:-1], sizes):
        @pl.when((n_live > lower) & ((n_live <= rows) | (rows == MAX_SEL_BLOCKS)))
        def _(rows=rows):
            left = lax.fori_loop(0, n_pick, pick, score[:rows])
            sel_ref[...] = jnp.zeros(shape, F32)
            sel_ref[:rows] = jnp.where(left == -jnp.inf, 1.0, 0.0)
    sel = sel_ref[...]
    sel_bias = ((sel - 1.0) * -NEG_INF).astype(BF16)
    q_sel = jnp.concatenate([q_t, jnp.concatenate([sel_bias] * NSA_GROUP, axis=1)], axis=0)

    n_own = max(1, qt // tk)
    n_full = t0 // tk
    blocks_per_tile = tk // SEL_BLOCK
    block_used = jnp.max(sel, axis=1, keepdims=True)
    n_visit = jnp.int32(0)
    for j in range(tiles_ref.shape[0]):
        used = jnp.max(block_used[j * blocks_per_tile:(j + 1) * blocks_per_tile]) > 0.0
        tiles_ref[n_visit] = j
        n_visit = n_visit + (used & (j < n_full)).astype(jnp.int32)

    def sweep(tiles, carry, causal):
        m_i, acc = carry
        starts = [pl.multiple_of(j * tk, tk) for j in tiles]
        scores = []
        for k0 in starts:
            s = _dot(ks_ref[0, 0, pl.ds(k0, tk), :], q_sel)
            if causal:
                s = s + per_head(k0 + lax.broadcasted_iota(jnp.int32, (tk, 1), 0) <= tq, NEG_INF)
            scores.append(s)
        m_new = m_i
        for s in scores:
            m_new = jnp.maximum(m_new, jnp.max(s, axis=0, keepdims=True))
        acc = jnp.exp(m_i - m_new) * acc
        for s, k0 in zip(scores, starts):
            acc = acc + _dot(vs_ref[0, 0, 0, pl.ds(k0, tk), :].T, jnp.exp(s - m_new).astype(BF16))
        return m_new, acc

    carry = (jnp.full((1, cols), NEG_INF, F32), jnp.zeros((2 * HEAD_DIM, cols), F32))
    done = jnp.int32(0)
    for width in (1, 2, 4):
        steps = n_visit // width if width == 4 else (n_visit // width) & 1
        carry = lax.fori_loop(
            0, steps,
            lambda i, c, width=width, done=done: sweep(
                [tiles_ref[done + width * i + u] for u in range(width)], c, causal=False),
            carry)
        done = done + steps * width
    carry = sweep([n_full + j for j in range(n_own)], carry, causal=True)
    o_s = _normalise_t(carry[1])

    band = WINDOW + qt
    w0 = pl.multiple_of(jnp.maximum(t0 - WINDOW, 0), math.gcd(qt, WINDOW))
    kpos = w0 + lax.broadcasted_iota(jnp.int32, (band, 1), 0)
    valid_w = (kpos <= tq) & (kpos > tq - WINDOW)
    logit_w = _dot(kw_ref[0, 0, 0, pl.ds(w0, band), :], q_t) + per_head(valid_w, NEG_INF)
    e_w = jnp.exp(logit_w - jnp.max(logit_w, axis=0, keepdims=True))
    o_w = _normalise_t(_dot(vw_ref[0, 0, 0, pl.ds(w0, band), :].T, e_w.astype(BF16)))

    gate_t = jax.nn.sigmoid(gl_ref[0]).T

    def branch_gate(branch):
        first = branch * NSA_GROUP
        return jnp.concatenate([gate_t[first + r:first + r + 1] for r in range(NSA_GROUP)], axis=1)

    o_t = branch_gate(0) * o_c + branch_gate(1) * o_s + branch_gate(2) * o_w
    o = o_t.T
    o_ref[0] = jnp.concatenate([o[r * qt:(r + 1) * qt] for r in range(NSA_GROUP)], axis=1).astype(o_ref.dtype)


def _nsa_attention(qn, kv_cmp, ks, kv3, gl):
    b, _, s, _ = qn.shape
    nc = kv_cmp.shape[3]
    nsel = s // SEL_BLOCK
    n_pick = min(N_SELECT, nsel)
    tk = min(SEL_KV_TILE, s)
    aug = 2 * HEAD_DIM
    qt = min(ATTN_Q_TILE, s)
    assert qt % tk == 0 or tk % qt == 0
    cmp_start = jnp.arange(nc) * CMP_STRIDE
    sel_start = jnp.arange(nsel) * SEL_BLOCK
    overlap = (jnp.minimum(cmp_start[:, None] + CMP_BLOCK, sel_start[None, :] + SEL_BLOCK)
               > jnp.maximum(cmp_start[:, None], sel_start[None, :]))
    overlap_t = jnp.pad(overlap.T.astype(BF16), ((0, MAX_SEL_BLOCKS - nsel), (0, 0)))
    kv_spec = lambda a: pl.BlockSpec((1, 1, 1, s, aug), lambda bi, g, i, a=a: (a, bi, g, 0, 0))
    cmp_spec = lambda a: pl.BlockSpec((1, 1, 1, nc, aug), lambda bi, g, i, a=a: (a, bi, g, 0, 0))
    return pl.pallas_call(
        functools.partial(_attn_kernel, n_pick=n_pick, tk=tk),
        grid=(b, NSA_KV_HEADS, s // qt),
        in_specs=[
            pl.BlockSpec((1, NSA_GROUP, qt, aug), lambda bi, g, i: (bi, g, i, 0)),
            cmp_spec(0), cmp_spec(1),
            pl.BlockSpec((1, 1, s, aug + MAX_SEL_BLOCKS), lambda bi, g, i: (bi, g, 0, 0)),
            kv_spec(0), kv_spec(1), kv_spec(2),
            pl.BlockSpec((1, qt, V7X_LANES), lambda bi, g, i: (bi, i, g)),
            pl.BlockSpec((MAX_SEL_BLOCKS, nc), lambda bi, g, i: (0, 0)),
        ],
        out_specs=pl.BlockSpec((1, qt, NSA_GROUP * HEAD_DIM), lambda bi, g, i: (bi, i, g)),
        out_shape=jax.ShapeDtypeStruct((b, s, NSA_Q_COLS), BF16),
        scratch_shapes=[pltpu.SMEM((s // tk,), jnp.int32), pltpu.VMEM((MAX_SEL_BLOCKS, qt), F32),
                        pltpu.VMEM((HEAD_DIM, NSA_GROUP * qt), F32), pltpu.VMEM((MAX_SEL_BLOCKS, qt), F32)],
        compiler_params=_params(("parallel", "parallel", "arbitrary")),
        name="nsa_attention",
    )(qn, kv_cmp, kv_cmp, ks, kv3, kv3, kv3, gl, overlap_t)


def _retention_kernel(q_ref, k_ref, v_ref, g_ref, decay_ref, xi_ref, zeta_ref, gch_ref, same_ref, gn_ref,
                      o_ref, state_ref):
    @pl.when(pl.program_id(1) == 0)
    def _():
        state_ref[...] = jnp.zeros_like(state_ref)

    width = same_ref.shape[0]
    heads = width // HEAD_DIM
    c = q_ref.shape[1]
    scale = jnp.asarray(HEAD_DIM ** -0.5, BF16)
    lane_head = lax.broadcasted_iota(jnp.int32, (c, width), 1) // HEAD_DIM
    same = same_ref[...]
    mean_w = (same * (1.0 / HEAD_DIM)).astype(BF16)

    def head_mean(x):
        hi = x.astype(BF16)
        lo = (x - hi.astype(F32)).astype(BF16)
        return _dot(hi, mean_w) + _dot(lo, mean_w)

    def stack(x):
        zero = jnp.zeros_like(x)
        return jnp.concatenate([jnp.where(lane_head == j, x, zero) for j in range(heads)], axis=0)

    for i in range(q_ref.shape[0]):
        outs = []
        for p in range(RET_COLS // width):
            cols = slice(p * width, (p + 1) * width)
            q = q_ref[i, :, cols]
            k = k_ref[i, :, cols] * scale
            v = v_ref[i, :, cols]
            state = state_ref[i, p]
            inner = _dot_nt(q, stack(k)) * decay_ref[p]
            o = _dot(inner.astype(BF16), stack(v)) + _dot(q, state.astype(BF16)) * xi_ref[p]
            kz = (k.astype(F32) * zeta_ref[p]).astype(BF16)
            state_ref[i, p] = gch_ref[p] * state + same * _dot_tn(kz, v)
            centred = o - head_mean(o)
            outs.append(centred * lax.rsqrt(head_mean(jnp.square(centred)) + EPS))
        o = jnp.concatenate(outs, axis=1) * gn_ref[...]
        o_ref[i] = (jax.nn.silu(g_ref[i].astype(F32)) * o).astype(o_ref.dtype)


def _retention(ret, gn_gain):
    b, s, _ = ret.shape
    c = RET_CHUNK
    hh = RET_HEADS
    hg = RET_HEAD_GROUP
    width = hg * HEAD_DIM
    log_gamma = jnp.log1p(-jnp.exp2(-5.0 - jnp.arange(hh, dtype=F32)))
    pos = jnp.arange(c, dtype=F32)
    diff = pos[:, None] - pos[None, :]
    decay = jnp.where(diff >= 0, jnp.exp(jnp.maximum(diff, 0.0)[None] * log_gamma[:, None, None]), 0.0)
    xi = jnp.exp((pos + 1.0)[None] * log_gamma[:, None])[..., None]
    zeta = jnp.exp((c - 1.0 - pos)[None] * log_gamma[:, None])[..., None]
    g_chunk = jnp.exp(c * log_gamma)[:, None, None]
    side_by_side = lambda t: jnp.concatenate([t[j::hg] for j in range(hg)], axis=-1)
    decay = side_by_side(decay)
    xi = side_by_side(jnp.broadcast_to(xi, (hh, c, HEAD_DIM)))
    zeta = side_by_side(jnp.broadcast_to(zeta, (hh, c, HEAD_DIM)))
    head_of = jnp.arange(width) // HEAD_DIM
    same = (head_of[:, None] == head_of[None, :]).astype(F32)
    g_lane = side_by_side(jnp.broadcast_to(g_chunk, (hh, 1, HEAD_DIM)))
    g_chunk = same[None] * g_lane
    nb = math.gcd(RET_BATCH, b)
    part = lambda a: pl.BlockSpec((nb, c, RET_COLS), lambda bi, n, a=a: (bi, n, a))
    full = lambda shape: pl.BlockSpec(shape, lambda bi, n: (0,) * len(shape))
    return pl.pallas_call(
        _retention_kernel,
        grid=(b // nb, s // c),
        in_specs=[part(0), part(1), part(2), part(3),
                  full((hh // hg, c, hg * c)), full((hh // hg, c, width)), full((hh // hg, c, width)),
                  full((hh // hg, width, width)), full((width, width)), full((1, RET_COLS))],
        out_specs=pl.BlockSpec((nb, c, RET_COLS), lambda bi, n: (bi, n, 0)),
        out_shape=jax.ShapeDtypeStruct((b, s, RET_COLS), BF16),
        scratch_shapes=[pltpu.VMEM((nb, hh // hg, width, width), F32)],
        compiler_params=_params(("parallel", "arbitrary")),
        name="retention",
    )(ret, ret, ret, ret, decay, xi, zeta, g_chunk, same, gn_gain.reshape(1, RET_COLS).astype(F32))


def _out_proj_rows(h, a, r, w_ref):
    return h + _dot(a, w_ref[:NSA_Q_COLS]) + _dot(r, w_ref[NSA_Q_COLS:])


def _out_proj_kernel(h_ref, a_ref, r_ref, w_ref, g_ref, h_out_ref, hn_ref):
    h = _out_proj_rows(h_ref[...], a_ref[...], r_ref[...], w_ref)
    h_out_ref[...] = h
    hn_ref[...] = _rms(h, g_ref[...]).astype(BF16)


def _out_proj(h, a, r, w, g):
    t, d = h.shape
    tm = min(STREAM_ROW_TILE, t)
    row = lambda n: pl.BlockSpec((tm, n), lambda i: (i, 0))
    return pl.pallas_call(
        _out_proj_kernel,
        grid=(t // tm,),
        in_specs=[row(d), row(NSA_Q_COLS), row(RET_COLS),
                  pl.BlockSpec((MIX_WIDTH, d), lambda i: (0, 0)),
                  pl.BlockSpec((1, d), lambda i: (0, 0))],
        out_specs=[row(d), row(d)],
        out_shape=[jax.ShapeDtypeStruct((t, d), F32), jax.ShapeDtypeStruct((t, d), BF16)],
        compiler_params=_params(("parallel",)),
        name="out_proj",
    )(h, a, r, w.astype(BF16), g.reshape(1, d))


def _swiglu_chunk(x, wg, wu, wd):
    hid = jax.nn.silu(_dot(x, wg)) * _dot(x, wu)
    return _dot(hid.astype(BF16), wd)


def _ple_rows(h, p, g, proj_ref, gate_ref):
    emb = _dot(p.astype(BF16), proj_ref[...])
    return h + emb * jax.nn.sigmoid(_dot(_rms(h, g).astype(BF16), gate_ref[...]))


def _dense_layer_kernel(h_ref, a_ref, r_ref, p_ref, wo_ref, gf_ref, wg_ref, wu_ref, wd_ref, gp_ref, proj_ref, gate_ref,
                        gl_ref, o_ref, *, final_norm):
    h = _out_proj_rows(h_ref[...], a_ref[...], r_ref[...], wo_ref)
    x = _rms(h, gf_ref[...]).astype(BF16)
    acc = None
    for f in range(D_FF // FFN_COL_TILE):
        cols = slice(f * FFN_COL_TILE, (f + 1) * FFN_COL_TILE)
        part = _swiglu_chunk(x, wg_ref[:, cols], wu_ref[:, cols], wd_ref[cols, :])
        acc = part if acc is None else acc + part
    h = _ple_rows(h + acc, p_ref[...], gp_ref[...], proj_ref, gate_ref)
    if final_norm:
        h = _rms(h, gl_ref[...])
    o_ref[...] = h


def _dense_layer(h, a, r, p, w_out, g_ffn, wg, wu, wd, g_ple, proj, gate, g_final, final_norm):
    t, d = h.shape
    tm = min(FFN_ROW_TILE, t)
    row = lambda n: pl.BlockSpec((tm, n), lambda i: (i, 0))
    held = lambda shape: pl.BlockSpec(shape, lambda i: (0, 0), pipeline_mode=pl.Buffered(1))
    vec = held((1, d))
    return pl.pallas_call(
        functools.partial(_dense_layer_kernel, final_norm=final_norm),
        grid=(t // tm,),
        in_specs=[row(d), row(NSA_Q_COLS), row(RET_COLS), row(PLE_DIM),
                  held((MIX_WIDTH, d)), vec, held((d, D_FF)), held((d, D_FF)), held((D_FF, d)),
                  vec, held((PLE_DIM, d)), held((d, d)), vec],
        out_specs=row(d),
        out_shape=jax.ShapeDtypeStruct((t, d), F32),
        compiler_params=_params(("parallel",)),
        name="dense_layer",
    )(h, a, r, p, w_out.astype(BF16), g_ffn.reshape(1, d), wg.astype(BF16), wu.astype(BF16), wd.astype(BF16),
      g_ple.reshape(1, d), proj.astype(BF16), gate.astype(BF16), g_final.reshape(1, d))


def _lane_column(table, lane_index):
    lane = lax.broadcasted_iota(jnp.int32, table.shape, 1)
    col = jnp.sum(jnp.where(lane == lane_index, table, 0.0), axis=-1, keepdims=True)
    return jnp.broadcast_to(col, table.shape)


def _moe_kernel(x_ref, router_ref, wg_ref, wu_ref, wd_ref, o_ref,
                xs_ref, y_ref, slot_ref, gate_ref, slot_row_ref, slot_e_ref, gate_e_ref, count_ref):
    e = pl.program_id(1)
    f = pl.program_id(2)
    tm, d = x_ref.shape
    sub = MOE_GROUP_ROWS

    @pl.when((e == 0) & (f == 0))
    def _route():
        o_ref[...] = jnp.zeros_like(o_ref)
        logits = _dot(x_ref[...], router_ref[...])
        lane = lax.broadcasted_iota(jnp.int32, logits.shape, 1)
        logits = jnp.where(lane < N_EXPERTS, logits, -jnp.inf)
        v1 = jnp.max(logits, axis=-1, keepdims=True)
        i1 = jnp.min(jnp.where(logits == v1, lane, V7X_LANES), axis=-1, keepdims=True)
        rest = jnp.where(lane == i1, -jnp.inf, logits)
        v2 = jnp.max(rest, axis=-1, keepdims=True)
        i2 = jnp.min(jnp.where(rest == v2, lane, V7X_LANES), axis=-1, keepdims=True)
        e2 = jnp.exp(v2 - v1)
        inv = 1.0 / (1.0 + e2)
        gate_ref[...] = jnp.where(lane == i1, inv, 0.0) + jnp.where(lane == i2, e2 * inv, 0.0)
        routed = jnp.where((lane == i1) | (lane == i2), 1.0, 0.0)
        c = MOE_RANK_CHUNK
        before = (lax.broadcasted_iota(jnp.int32, (c, c), 1) < lax.broadcasted_iota(jnp.int32, (c, c), 0))
        before = jnp.where(before, 1.0, 0.0).astype(BF16)
        offset = jnp.zeros((1, V7X_LANES), F32)
        for j in range(tm // c):
            part = routed[j * c:(j + 1) * c]
            rank = _dot(before, part.astype(BF16)) + offset
            slot_ref[j * c:(j + 1) * c, :] = jnp.where(part > 0.0, rank, -1.0)
            offset = offset + jnp.sum(part, axis=0, keepdims=True)
        for ee in range(N_EXPERTS):
            count_ref[ee] = offset[0, ee].astype(jnp.int32)
        slot_row_ref[...] = slot_ref[...].T[:N_EXPERTS]

    n_groups = (count_ref[e] + sub - 1) // sub

    @pl.when(f == 0)
    def _gather():
        slot_e_ref[...] = _lane_column(slot_ref[...], e)
        gate_e_ref[...] = _lane_column(gate_ref[...], e)
        slot_row = slot_row_ref[pl.ds(e, 1), :]

        def body(s, _):
            r0 = pl.multiple_of(s * sub, sub)
            want = (r0 + lax.broadcasted_iota(jnp.int32, (sub, 1), 0)).astype(F32)
            onehot = jnp.where(slot_row == want, 1.0, 0.0).astype(BF16)
            xs_ref[pl.ds(r0, sub), :] = _dot(onehot, x_ref[...]).astype(BF16)
            y_ref[pl.ds(r0, sub), :] = jnp.zeros((sub, d), F32)
            return 0

        lax.fori_loop(0, n_groups, body, 0)

    def expert(i, _, base, n):
        r0 = pl.multiple_of(base + i * n, n)
        y_ref[pl.ds(r0, n), :] += _swiglu_chunk(xs_ref[pl.ds(r0, n), :], wg_ref[0], wu_ref[0], wd_ref[0])
        return 0

    tail = MOE_TAIL_ROWS
    n_filled = count_ref[e] // sub
    rest = count_ref[e] - n_filled * sub
    short = rest <= tail
    n_whole = jnp.where(short, n_filled, n_filled + 1)
    n_tail = jnp.where(short & (rest > 0), 1, 0)
    lax.fori_loop(0, n_whole, functools.partial(expert, base=0, n=sub), 0)
    lax.fori_loop(0, n_tail, functools.partial(expert, base=n_filled * sub, n=tail), 0)

    @pl.when(f == pl.num_programs(2) - 1)
    def _scatter():
        tc = MOE_SCATTER_ROWS

        def body(s, _):
            r0 = pl.multiple_of(s * sub, sub)
            y = y_ref[pl.ds(r0, sub), :].astype(BF16)
            want = (r0 + lax.broadcasted_iota(jnp.int32, (1, sub), 1)).astype(F32)
            for j in range(tm // tc):
                rows = slice(j * tc, (j + 1) * tc)
                slot = jnp.concatenate([slot_e_ref[rows, :]] * (sub // V7X_LANES), axis=1)
                onehot = jnp.where(slot == want, 1.0, 0.0).astype(BF16)
                weight = jnp.concatenate([gate_e_ref[rows, :]] * (d // V7X_LANES), axis=1)
                o_ref[rows, :] += weight * _dot(onehot, y)
            return 0

        lax.fori_loop(0, n_groups, body, 0)


def _moe_ffn(hn, router, wg, wu, wd):
    t, d = hn.shape
    tm = min(MOE_ROW_TILE, t)
    tf = MOE_COL_TILE
    nf = D_FF // tf
    router = jnp.pad(router, ((0, 0), (0, V7X_LANES - N_EXPERTS))).astype(BF16)
    once = pl.Buffered(1)
    return pl.pallas_call(
        _moe_kernel,
        grid=(t // tm, N_EXPERTS, nf),
        in_specs=[pl.BlockSpec((tm, d), lambda i, e, f: (i, 0), pipeline_mode=once),
                  pl.BlockSpec((d, V7X_LANES), lambda i, e, f: (0, 0), pipeline_mode=once),
                  pl.BlockSpec((1, d, tf), lambda i, e, f: (e, 0, f)),
                  pl.BlockSpec((1, d, tf), lambda i, e, f: (e, 0, f)),
                  pl.BlockSpec((1, tf, d), lambda i, e, f: (e, f, 0))],
        out_specs=pl.BlockSpec((tm, d), lambda i, e, f: (i, 0), pipeline_mode=once),
        out_shape=jax.ShapeDtypeStruct((t, d), F32),
        scratch_shapes=[pltpu.VMEM((tm, d), BF16), pltpu.VMEM((tm, d), F32),
                        pltpu.VMEM((tm, V7X_LANES), F32), pltpu.VMEM((tm, V7X_LANES), F32),
                        pltpu.VMEM((N_EXPERTS, tm), F32),
                        pltpu.VMEM((tm, V7X_LANES), F32), pltpu.VMEM((tm, V7X_LANES), F32),
                        pltpu.SMEM((N_EXPERTS,), jnp.int32)],
        compiler_params=_params(("parallel", "arbitrary", "arbitrary"), vmem=MOE_VMEM_LIMIT),
        name="moe_ffn",
    )(hn, router, wg.astype(BF16), wu.astype(BF16), wd.astype(BF16))


def _ple_kernel(h_ref, f_ref, p_ref, g_ref, proj_ref, gate_ref, gf_ref, o_ref, *, final_norm):
    h = _ple_rows(h_ref[...] + f_ref[...], p_ref[...], g_ref[...], proj_ref, gate_ref)
    if final_norm:
        h = _rms(h, gf_ref[...])
    o_ref[...] = h


def _ple(h, f, p, g, proj, gate, g_final, final_norm):
    t, d = h.shape
    tm = min(STREAM_ROW_TILE, t)
    vec = pl.BlockSpec((1, d), lambda i: (0, 0))
    return pl.pallas_call(
        functools.partial(_ple_kernel, final_norm=final_norm),
        grid=(t // tm,),
        in_specs=[pl.BlockSpec((tm, d), lambda i: (i, 0)),
                  pl.BlockSpec((tm, d), lambda i: (i, 0)),
                  pl.BlockSpec((tm, PLE_DIM), lambda i: (i, 0)),
                  vec,
                  pl.BlockSpec((PLE_DIM, d), lambda i: (0, 0)),
                  pl.BlockSpec((d, d), lambda i: (0, 0)),
                  vec],
        out_specs=pl.BlockSpec((tm, d), lambda i: (i, 0)),
        out_shape=jax.ShapeDtypeStruct((t, d), F32),
        compiler_params=_params(("parallel",)),
        name="ple",
    )(h, f, p, g.reshape(1, d), proj.astype(BF16), gate.astype(BF16), g_final.reshape(1, d))


def kernel(x, p, w_in, w_out, g_mix, g_ffn, g_ple, g_final, cmp_pos, cmp_w1, cmp_w2, ret_gn,
           ffn_gate, ffn_up, ffn_down, moe_router, moe_gate, moe_up, moe_down, ple_proj, ple_gate):
    b, s, d = x.shape
    depth = w_in.shape[0]
    t = b * s
    h = x
    for i in range(depth):
        qn, kcv, ks, kv3, gl, ret = _in_proj(h.reshape(b, s, d), g_mix[i], _arrange_w_in(w_in[i]))
        kv_cmp = _compress(kcv, cmp_pos[i], cmp_w1[i], cmp_w2[i])
        a = _nsa_attention(qn, kv_cmp, ks, kv3, gl)
        r = _retention(ret, ret_gn[i])
        h, a, r, p_i = h.reshape(t, d), a.reshape(t, NSA_Q_COLS), r.reshape(t, RET_COLS), p[i].reshape(t, PLE_DIM)
        last = i == depth - 1
        if i % 2 == 0:
            h = _dense_layer(h, a, r, p_i, w_out[i], g_ffn[i], ffn_gate[i // 2], ffn_up[i // 2], ffn_down[i // 2],
                             g_ple[i], ple_proj[i], ple_gate[i], g_final, last)
        else:
            h, hn = _out_proj(h, a, r, w_out[i], g_ffn[i])
            f = _moe_ffn(hn, moe_router[i // 2], moe_gate[i // 2], moe_up[i // 2], moe_down[i // 2])
            h = _ple(h, f, p_i, g_ple[i], ple_proj[i], ple_gate[i], g_final, last)
    return h.reshape(b, s, d)
```

```python
import functools
import math

import jax
import jax.numpy as jnp
from jax import lax
from jax.experimental import pallas as pl
from jax.experimental.pallas import tpu as pltpu

F32 = jnp.float32
BF16 = jnp.bfloat16

D_MODEL = 1024
HEAD_DIM = 64
NSA_HEADS = 8
NSA_KV_HEADS = 2
NSA_GROUP = NSA_HEADS // NSA_KV_HEADS
RET_HEADS = 8
CMP_BLOCK = 32
CMP_STRIDE = 16
CMP_HIDDEN = 256
SEL_BLOCK = 64
N_SELECT = 16
N_LOCAL_SEL = 2
WINDOW = 512
RET_CHUNK = 128
D_FF = 3584
N_EXPERTS = 8
PLE_DIM = 256
EPS = 1e-6
NEG_INF = -1e30
BIG = 1e9

NSA_Q_COLS = NSA_HEADS * HEAD_DIM
NSA_KV_COLS = NSA_KV_HEADS * HEAD_DIM
NSA_GATE_COLS = 3 * NSA_HEADS
RET_COLS = RET_HEADS * HEAD_DIM
MIX_WIDTH = NSA_Q_COLS + RET_COLS

V7X_LANES = 128
V7X_SUBLANES = 8
V7X_VMEM_BYTES = 64 * 1024 * 1024
VMEM_LIMIT = V7X_VMEM_BYTES * 3 // 4

ROW_TILE = 512
STREAM_ROW_TILE = 1024
ATTN_Q_TILE = 256
SEL_KV_TILE = 256
RET_BATCH = 4
RET_HEAD_GROUP = 4
FFN_ROW_TILE = 512
FFN_COL_TILE = 512
MOE_COL_TILE = 1792
MOE_ROW_TILE = 2048
MOE_GROUP_ROWS = 256
MOE_TAIL_ROWS = 64
MOE_RANK_CHUNK = 256
MOE_SCATTER_ROWS = 512
MOE_VMEM_LIMIT = V7X_VMEM_BYTES * 7 // 8

POS_SHIFT = 6
POS_SPLIT = 1 << POS_SHIFT
SEL_SHIFT = SEL_BLOCK.bit_length() - 1
MAX_SEL_BLOCKS = V7X_LANES


def _params(sem, vmem=VMEM_LIMIT):
    return pltpu.CompilerParams(dimension_semantics=sem, vmem_limit_bytes=vmem)


def _dot(a, b):
    return jnp.dot(a, b, preferred_element_type=F32)


def _dot_nt(a, b):
    return lax.dot_general(a, b, (((1,), (1,)), ((), ())), preferred_element_type=F32)


def _dot_tn(a, b):
    return lax.dot_general(a, b, (((0,), (0,)), ((), ())), preferred_element_type=F32)


def _rms(x, g):
    return x * lax.rsqrt(jnp.mean(x * x, axis=-1, keepdims=True) + EPS) * g


def _lane_features(shape, first, second):
    lane = lax.broadcasted_iota(jnp.int32, shape, 1)
    return jnp.where(lane == 0, first, jnp.where(lane == 1, second, 0))


def _key_features(pos, width):
    return _lane_features((pos.shape[0], width), pos >> POS_SHIFT, pos & (POS_SPLIT - 1))


def _ones_features(n, width):
    return _lane_features((n, width), 1, 0)


def _int_to_bf16(x):
    return x.astype(F32).astype(BF16)


def _in_proj_kernel(h_ref, g_ref, w_ref, qn_ref, kcv_ref, ks_ref, kv3_ref, gl_ref, ret_ref):
    tm = h_ref.shape[1]
    xn = _rms(h_ref[0], g_ref[...]).astype(BF16)
    pair = 2 * HEAD_DIM
    pos = pl.program_id(1) * tm + lax.broadcasted_iota(jnp.int32, (tm, 1), 0)
    kfeat = _int_to_bf16(_key_features(pos, HEAD_DIM))
    vfeat = _int_to_bf16(_ones_features(tm, HEAD_DIM))
    lane = lax.broadcasted_iota(jnp.int32, (tm, MAX_SEL_BLOCKS), 1)
    block_onehot = jnp.where(lane == (pos >> SEL_SHIFT), 1.0, 0.0).astype(BF16)

    wide = 2 * pair
    per_dot = wide // HEAD_DIM
    for j in range(NSA_Q_COLS // wide):
        z = (_dot(xn, w_ref[:, j * wide:(j + 1) * wide]) * HEAD_DIM ** -0.5).astype(BF16)
        for k in range(per_dot):
            head = per_dot * j + k
            slope = 2.0 ** -(head + 1)
            qfeat = _lane_features((tm, HEAD_DIM), POS_SPLIT * slope, slope).astype(BF16)
            qn_ref[0, head] = jnp.concatenate([z[:, k * HEAD_DIM:(k + 1) * HEAD_DIM], qfeat], axis=1)
    base = NSA_Q_COLS
    z = _dot(xn, w_ref[:, base:base + wide]).astype(BF16)
    for a in range(2):
        kcv_ref[a, 0] = z[:, a * pair:(a + 1) * pair]
    base += wide
    for c in range(2):
        z = _dot(xn, w_ref[:, base + c * wide:base + (c + 1) * wide]).astype(BF16)
        for half in range(2):
            a = 2 * c + half
            for g in range(NSA_KV_HEADS):
                zg = z[:, half * pair + g * HEAD_DIM:half * pair + (g + 1) * HEAD_DIM]
                if a == 0:
                    ks_ref[0, g] = jnp.concatenate([zg, kfeat, block_onehot], axis=1)
                else:
                    kv3_ref[a - 1, 0, g] = jnp.concatenate([zg, kfeat if a == 2 else vfeat], axis=1)
    base += 2 * wide
    for a in range(4):
        ret_ref[0, :, a * RET_COLS:(a + 1) * RET_COLS] = _dot(
            xn, w_ref[:, base + a * RET_COLS:base + (a + 1) * RET_COLS]).astype(BF16)
    base += 4 * RET_COLS
    gl_ref[0] = _dot(xn, w_ref[:, base:base + NSA_KV_HEADS * V7X_LANES])


def _arrange_w_in(w):
    q_end = NSA_Q_COLS
    kv_end = q_end + 6 * NSA_KV_COLS
    gl_end = kv_end + NSA_GATE_COLS
    gl = w[:, kv_end:gl_end].reshape(D_MODEL, 3, NSA_KV_HEADS, NSA_GROUP)
    gl = jnp.transpose(gl, (0, 2, 1, 3)).reshape(D_MODEL, NSA_KV_HEADS, 3 * NSA_GROUP)
    gl = jnp.pad(gl, ((0, 0), (0, 0), (0, V7X_LANES - 3 * NSA_GROUP)))
    gl = gl.reshape(D_MODEL, NSA_KV_HEADS * V7X_LANES)
    return jnp.concatenate([w[:, :kv_end], w[:, gl_end:], gl], axis=1).astype(BF16)


def _in_proj(h, g, w):
    b, s, d = h.shape
    assert s // SEL_BLOCK <= MAX_SEL_BLOCKS
    tm = min(ROW_TILE, s)
    ncols = w.shape[1]
    aug = 2 * HEAD_DIM
    return pl.pallas_call(
        _in_proj_kernel,
        grid=(b, s // tm),
        in_specs=[
            pl.BlockSpec((1, tm, d), lambda bi, si: (bi, si, 0)),
            pl.BlockSpec((1, d), lambda bi, si: (0, 0)),
            pl.BlockSpec((d, ncols), lambda bi, si: (0, 0)),
        ],
        out_specs=[
            pl.BlockSpec((1, NSA_HEADS, tm, aug), lambda bi, si: (bi, 0, si, 0)),
            pl.BlockSpec((2, 1, tm, NSA_KV_COLS), lambda bi, si: (0, bi, si, 0)),
            pl.BlockSpec((1, NSA_KV_HEADS, tm, aug + MAX_SEL_BLOCKS), lambda bi, si: (bi, 0, si, 0)),
            pl.BlockSpec((3, 1, NSA_KV_HEADS, tm, aug), lambda bi, si: (0, bi, 0, si, 0)),
            pl.BlockSpec((1, tm, NSA_KV_HEADS * V7X_LANES), lambda bi, si: (bi, si, 0)),
            pl.BlockSpec((1, tm, 4 * RET_COLS), lambda bi, si: (bi, si, 0)),
        ],
        out_shape=[
            jax.ShapeDtypeStruct((b, NSA_HEADS, s, aug), BF16),
            jax.ShapeDtypeStruct((2, b, s, NSA_KV_COLS), BF16),
            jax.ShapeDtypeStruct((b, NSA_KV_HEADS, s, aug + MAX_SEL_BLOCKS), BF16),
            jax.ShapeDtypeStruct((3, b, NSA_KV_HEADS, s, aug), BF16),
            jax.ShapeDtypeStruct((b, s, NSA_KV_HEADS * V7X_LANES), F32),
            jax.ShapeDtypeStruct((b, s, 4 * RET_COLS), BF16),
        ],
        compiler_params=_params(("parallel", "parallel")),
        name="in_proj",
    )(h, g.reshape(1, d), w)


def _compress_kernel(x_ref, wbig_ref, pos_ref, w1_ref, w2_ref, o_ref):
    nch = x_ref.shape[2]
    u = _dot(x_ref[0, 0], wbig_ref[0])
    hid0 = _dot(pos_ref[0], w1_ref[0])[0:1]
    w2 = w2_ref[0]
    cmp_end = lax.broadcasted_iota(jnp.int32, (nch, 1), 0) * CMP_STRIDE + (CMP_BLOCK - 1)
    feat = _int_to_bf16(jnp.where(pl.program_id(0) == 0, _key_features(cmp_end, HEAD_DIM),
                                  _ones_features(nch, HEAD_DIM)))
    for g in range(NSA_KV_HEADS):
        c0 = g * 2 * CMP_HIDDEN
        first = u[:, c0:c0 + CMP_HIDDEN]
        second = u[:, c0 + CMP_HIDDEN:c0 + 2 * CMP_HIDDEN]
        hid = first + pltpu.roll(second, nch - 1, 0) + hid0
        out = _dot(jax.nn.gelu(hid).astype(BF16), w2).astype(BF16)
        o_ref[0, 0, g] = jnp.concatenate([out, feat], axis=1)


def _arrange_cmp_w1(w1):
    r = CMP_BLOCK // CMP_STRIDE
    w1r = w1.reshape(2, r, CMP_STRIDE, HEAD_DIM, CMP_HIDDEN)
    eye = jnp.eye(NSA_KV_HEADS, dtype=w1.dtype)
    big = jnp.einsum('krcdh,gf->kcgdfrh', w1r, eye)
    return big.reshape(2, CMP_STRIDE * NSA_KV_COLS, NSA_KV_HEADS * r * CMP_HIDDEN).astype(BF16)


def _compress(kcv, cmp_pos, cmp_w1, cmp_w2):
    _, b, s, _ = kcv.shape
    nch = s // CMP_STRIDE
    x = kcv.reshape(2, b, nch, CMP_STRIDE * NSA_KV_COLS)
    wbig = _arrange_cmp_w1(cmp_w1)
    pos = jnp.broadcast_to(cmp_pos.reshape(2, 1, CMP_BLOCK * HEAD_DIM),
                           (2, V7X_SUBLANES, CMP_BLOCK * HEAD_DIM)).astype(BF16)
    w1 = cmp_w1.reshape(2, CMP_BLOCK * HEAD_DIM, CMP_HIDDEN).astype(BF16)
    w2 = cmp_w2.astype(BF16)
    kdim = CMP_STRIDE * NSA_KV_COLS
    return pl.pallas_call(
        _compress_kernel,
        grid=(2, b),
        in_specs=[
            pl.BlockSpec((1, 1, nch, kdim), lambda a, bi: (a, bi, 0, 0)),
            pl.BlockSpec((1, kdim, wbig.shape[2]), lambda a, bi: (a, 0, 0)),
            pl.BlockSpec((1, V7X_SUBLANES, CMP_BLOCK * HEAD_DIM), lambda a, bi: (a, 0, 0)),
            pl.BlockSpec((1, CMP_BLOCK * HEAD_DIM, CMP_HIDDEN), lambda a, bi: (a, 0, 0)),
            pl.BlockSpec((1, CMP_HIDDEN, HEAD_DIM), lambda a, bi: (a, 0, 0)),
        ],
        out_specs=pl.BlockSpec((1, 1, NSA_KV_HEADS, nch, 2 * HEAD_DIM), lambda a, bi: (a, bi, 0, 0, 0)),
        out_shape=jax.ShapeDtypeStruct((2, b, NSA_KV_HEADS, nch, 2 * HEAD_DIM), BF16),
        compiler_params=_params(("parallel", "parallel")),
        name="compress",
    )(x, wbig, pos, w1, w2)


def _normalise_t(acc):
    return acc[:HEAD_DIM] * (1.0 / jnp.maximum(acc[HEAD_DIM:HEAD_DIM + 1], 1e-30))


def _attn_kernel(q_ref, kc_ref, vc_ref, ks_ref, vs_ref, kw_ref, vw_ref, gl_ref, ovt_ref, o_ref, tiles_ref,
                 sel_ref, oc_ref, imp_ref, *, n_pick, tk):
    qt = q_ref.shape[2]
    cols = NSA_GROUP * qt
    t0 = pl.program_id(2) * qt
    q_t = q_ref[0].reshape(cols, 2 * HEAD_DIM).T
    tq = t0 + lax.broadcasted_iota(jnp.int32, (1, qt), 1)

    def per_head(valid, masked):
        bias = jnp.where(valid, 0.0, masked)
        return jnp.concatenate([bias] * NSA_GROUP, axis=1)

    nc = kc_ref.shape[3]

    def compressed(rows):
        cmp_end = lax.broadcasted_iota(jnp.int32, (rows, 1), 0) * CMP_STRIDE + (CMP_BLOCK - 1)
        logit = _dot(kc_ref[0, 0, 0, :rows], q_t) + per_head(cmp_end <= tq, -jnp.inf)
        e = jnp.exp(logit - jnp.maximum(jnp.max(logit, axis=0, keepdims=True), NEG_INF))
        p = e * (1.0 / jnp.maximum(jnp.sum(e, axis=0, keepdims=True), 1e-30))
        oc_ref[...] = _dot(vc_ref[0, 0, 0, :rows].T, p.astype(BF16))[:HEAD_DIM]
        p_sum = p[:, 0:qt]
        for r in range(1, NSA_GROUP):
            p_sum = p_sum + p[:, r * qt:(r + 1) * qt]
        p_hi = p_sum.astype(BF16)
        p_lo = (p_sum - p_hi.astype(F32)).astype(BF16)
        imp_ref[...] = _dot(ovt_ref[:, :rows], p_hi) + _dot(ovt_ref[:, :rows], p_lo)

    n_cmp_live = (t0 + qt - CMP_BLOCK + CMP_STRIDE) // CMP_STRIDE
    cmp_sizes = [nc // 4, nc // 2, nc]
    for lower, rows in zip([None] + cmp_sizes[:-1], cmp_sizes):
        above = True if lower is None else n_cmp_live > lower
        below = True if rows == nc else n_cmp_live <= rows
        pl.when(jnp.logical_and(above, below))(functools.partial(compressed, rows))
    o_c = oc_ref[...]
    imp = imp_ref[...]

    shape = (MAX_SEL_BLOCKS, qt)
    blk = lax.broadcasted_iota(jnp.int32, shape, 0)
    back = ((t0 + lax.broadcasted_iota(jnp.int32, shape, 1)) >> SEL_SHIFT) - blk
    forced = (blk == 0) | ((back >= 0) & (back < N_LOCAL_SEL))
    score = jnp.where(forced, BIG, jnp.where(back >= 0, imp, -BIG))

    def pick(_, work):
        row = lax.broadcasted_iota(jnp.int32, work.shape, 0)
        m = jnp.max(work, axis=0, keepdims=True)
        first = jnp.min(jnp.where(work == m, row, MAX_SEL_BLOCKS), axis=0, keepdims=True)
        return jnp.where(row == first, -jnp.inf, work)

    n_live = (t0 + qt + SEL_BLOCK - 1) >> SEL_SHIFT
    sizes = [MAX_SEL_BLOCKS // 4, MAX_SEL_BLOCKS // 2, MAX_SEL_BLOCKS]
    for lower, rows in zip([0] + sizes[:-1], sizes):
        @pl.when((n_live > lower) & ((n_live <= rows) | (rows == MAX_SEL_BLOCKS)))
        def _(rows=rows):
            left = lax.fori_loop(0, n_pick, pick, score[:rows])
            sel_ref[...] = jnp.zeros(shape, F32)
            sel_ref[:rows] = jnp.where(left == -jnp.inf, 1.0, 0.0)
    sel = sel_ref[...]
    sel_bias = ((sel - 1.0) * -NEG_INF).astype(BF16)
    q_sel = jnp.concatenate([q_t, jnp.concatenate([sel_bias] * NSA_GROUP, axis=1)], axis=0)

    n_own = max(1, qt // tk)
    n_full = t0 // tk
    blocks_per_tile = tk // SEL_BLOCK
    block_used = jnp.max(sel, axis=1, keepdims=True)
    n_visit = jnp.int32(0)
    for j in range(tiles_ref.shape[0]):
        used = jnp.max(block_used[j * blocks_per_tile:(j + 1) * blocks_per_tile]) > 0.0
        tiles_ref[n_visit] = j
        n_visit = n_visit + (used & (j < n_full)).astype(jnp.int32)

    def sweep(tiles, carry, causal):
        m_i, acc = carry
        starts = [pl.multiple_of(j * tk, tk) for j in tiles]
        scores = []
        for k0 in starts:
            s = _dot(ks_ref[0, 0, pl.ds(k0, tk), :], q_sel)
            if causal:
                s = s + per_head(k0 + lax.broadcasted_iota(jnp.int32, (tk, 1), 0) <= tq, NEG_INF)
            scores.append(s)
        m_new = m_i
        for s in scores:
            m_new = jnp.maximum(m_new, jnp.max(s, axis=0, keepdims=True))
        acc = jnp.exp(m_i - m_new) * acc
        for s, k0 in zip(scores, starts):
            acc = acc + _dot(vs_ref[0, 0, 0, pl.ds(k0, tk), :].T, jnp.exp(s - m_new).astype(BF16))
        return m_new, acc

    carry = (jnp.full((1, cols), NEG_INF, F32), jnp.zeros((2 * HEAD_DIM, cols), F32))
    done = jnp.int32(0)
    for width in (1, 2, 4):
        steps = n_visit // width if width == 4 else (n_visit // width) & 1
        carry = lax.fori_loop(
            0, steps,
            lambda i, c, width=width, done=done: sweep(
                [tiles_ref[done + width * i + u] for u in range(width)], c, causal=False),
            carry)
        done = done + steps * width
    carry = sweep([n_full + j for j in range(n_own)], carry, causal=True)
    o_s = _normalise_t(carry[1])

    band = WINDOW + qt
    w0 = pl.multiple_of(jnp.maximum(t0 - WINDOW, 0), math.gcd(qt, WINDOW))
    kpos = w0 + lax.broadcasted_iota(jnp.int32, (band, 1), 0)
    valid_w = (kpos <= tq) & (kpos > tq - WINDOW)
    logit_w = _dot(kw_ref[0, 0, 0, pl.ds(w0, band), :], q_t) + per_head(valid_w, NEG_INF)
    e_w = jnp.exp(logit_w - jnp.max(logit_w, axis=0, keepdims=True))
    o_w = _normalise_t(_dot(vw_ref[0, 0, 0, pl.ds(w0, band), :].T, e_w.astype(BF16)))

    gate_t = jax.nn.sigmoid(gl_ref[0]).T

    def branch_gate(branch):
        first = branch * NSA_GROUP
        return jnp.concatenate([gate_t[first + r:first + r + 1] for r in range(NSA_GROUP)], axis=1)

    o_t = branch_gate(0) * o_c + branch_gate(1) * o_s + branch_gate(2) * o_w
    o = o_t.T
    o_ref[0] = jnp.concatenate([o[r * qt:(r + 1) * qt] for r in range(NSA_GROUP)], axis=1).astype(o_ref.dtype)


def _nsa_attention(qn, kv_cmp, ks, kv3, gl):
    b, _, s, _ = qn.shape
    nc = kv_cmp.shape[3]
    nsel = s // SEL_BLOCK
    n_pick = min(N_SELECT, nsel)
    tk = min(SEL_KV_TILE, s)
    aug = 2 * HEAD_DIM
    qt = min(ATTN_Q_TILE, s)
    assert qt % tk == 0 or tk % qt == 0
    cmp_start = jnp.arange(nc) * CMP_STRIDE
    sel_start = jnp.arange(nsel) * SEL_BLOCK
    overlap = (jnp.minimum(cmp_start[:, None] + CMP_BLOCK, sel_start[None, :] + SEL_BLOCK)
               > jnp.maximum(cmp_start[:, None], sel_start[None, :]))
    overlap_t = jnp.pad(overlap.T.astype(BF16), ((0, MAX_SEL_BLOCKS - nsel), (0, 0)))
    kv_spec = lambda a: pl.BlockSpec((1, 1, 1, s, aug), lambda bi, g, i, a=a: (a, bi, g, 0, 0))
    cmp_spec = lambda a: pl.BlockSpec((1, 1, 1, nc, aug), lambda bi, g, i, a=a: (a, bi, g, 0, 0))
    return pl.pallas_call(
        functools.partial(_attn_kernel, n_pick=n_pick, tk=tk),
        grid=(b, NSA_KV_HEADS, s // qt),
        in_specs=[
            pl.BlockSpec((1, NSA_GROUP, qt, aug), lambda bi, g, i: (bi, g, i, 0)),
            cmp_spec(0), cmp_spec(1),
            pl.BlockSpec((1, 1, s, aug + MAX_SEL_BLOCKS), lambda bi, g, i: (bi, g, 0, 0)),
            kv_spec(0), kv_spec(1), kv_spec(2),
            pl.BlockSpec((1, qt, V7X_LANES), lambda bi, g, i: (bi, i, g)),
            pl.BlockSpec((MAX_SEL_BLOCKS, nc), lambda bi, g, i: (0, 0)),
        ],
        out_specs=pl.BlockSpec((1, qt, NSA_GROUP * HEAD_DIM), lambda bi, g, i: (bi, i, g)),
        out_shape=jax.ShapeDtypeStruct((b, s, NSA_Q_COLS), BF16),
        scratch_shapes=[pltpu.SMEM((s // tk,), jnp.int32), pltpu.VMEM((MAX_SEL_BLOCKS, qt), F32),
                        pltpu.VMEM((HEAD_DIM, NSA_GROUP * qt), F32), pltpu.VMEM((MAX_SEL_BLOCKS, qt), F32)],
        compiler_params=_params(("parallel", "parallel", "arbitrary")),
        name="nsa_attention",
    )(qn, kv_cmp, kv_cmp, ks, kv3, kv3, kv3, gl, overlap_t)


def _retention_kernel(q_ref, k_ref, v_ref, g_ref, decay_ref, xi_ref, zeta_ref, gch_ref, same_ref, gn_ref,
                      o_ref, state_ref):
    @pl.when(pl.program_id(1) == 0)
    def _():
        state_ref[...] = jnp.zeros_like(state_ref)

    width = same_ref.shape[0]
    heads = width // HEAD_DIM
    c = q_ref.shape[1]
    scale = jnp.asarray(HEAD_DIM ** -0.5, BF16)
    lane_head = lax.broadcasted_iota(jnp.int32, (c, width), 1) // HEAD_DIM
    same = same_ref[...]
    mean_w = (same * (1.0 / HEAD_DIM)).astype(BF16)

    def head_mean(x):
        hi = x.astype(BF16)
        lo = (x - hi.astype(F32)).astype(BF16)
        return _dot(hi, mean_w) + _dot(lo, mean_w)

    def stack(x):
        zero = jnp.zeros_like(x)
        return jnp.concatenate([jnp.where(lane_head == j, x, zero) for j in range(heads)], axis=0)

    for i in range(q_ref.shape[0]):
        outs = []
        for p in range(RET_COLS // width):
            cols = slice(p * width, (p + 1) * width)
            q = q_ref[i, :, cols]
            k = k_ref[i, :, cols] * scale
            v = v_ref[i, :, cols]
            state = state_ref[i, p]
            inner = _dot_nt(q, stack(k)) * decay_ref[p]
            o = _dot(inner.astype(BF16), stack(v)) + _dot(q, state.astype(BF16)) * xi_ref[p]
            kz = (k.astype(F32) * zeta_ref[p]).astype(BF16)
            state_ref[i, p] = gch_ref[p] * state + same * _dot_tn(kz, v)
            centred = o - head_mean(o)
            outs.append(centred * lax.rsqrt(head_mean(jnp.square(centred)) + EPS))
        o = jnp.concatenate(outs, axis=1) * gn_ref[...]
        o_ref[i] = (jax.nn.silu(g_ref[i].astype(F32)) * o).astype(o_ref.dtype)


def _retention(ret, gn_gain):
    b, s, _ = ret.shape
    c = RET_CHUNK
    hh = RET_HEADS
    hg = RET_HEAD_GROUP
    width = hg * HEAD_DIM
    log_gamma = jnp.log1p(-jnp.exp2(-5.0 - jnp.arange(hh, dtype=F32)))
    pos = jnp.arange(c, dtype=F32)
    diff = pos[:, None] - pos[None, :]
    decay = jnp.where(diff >= 0, jnp.exp(jnp.maximum(diff, 0.0)[None] * log_gamma[:, None, None]), 0.0)
    xi = jnp.exp((pos + 1.0)[None] * log_gamma[:, None])[..., None]
    zeta = jnp.exp((c - 1.0 - pos)[None] * log_gamma[:, None])[..., None]
    g_chunk = jnp.exp(c * log_gamma)[:, None, None]
    side_by_side = lambda t: jnp.concatenate([t[j::hg] for j in range(hg)], axis=-1)
    decay = side_by_side(decay)
    xi = side_by_side(jnp.broadcast_to(xi, (hh, c, HEAD_DIM)))
    zeta = side_by_side(jnp.broadcast_to(zeta, (hh, c, HEAD_DIM)))
    head_of = jnp.arange(width) // HEAD_DIM
    same = (head_of[:, None] == head_of[None, :]).astype(F32)
    g_lane = side_by_side(jnp.broadcast_to(g_chunk, (hh, 1, HEAD_DIM)))
    g_chunk = same[None] * g_lane
    nb = math.gcd(RET_BATCH, b)
    part = lambda a: pl.BlockSpec((nb, c, RET_COLS), lambda bi, n, a=a: (bi, n, a))
    full = lambda shape: pl.BlockSpec(shape, lambda bi, n: (0,) * len(shape))
    return pl.pallas_call(
        _retention_kernel,
        grid=(b // nb, s // c),
        in_specs=[part(0), part(1), part(2), part(3),
                  full((hh // hg, c, hg * c)), full((hh // hg, c, width)), full((hh // hg, c, width)),
                  full((hh // hg, width, width)), full((width, width)), full((1, RET_COLS))],
        out_specs=pl.BlockSpec((nb, c, RET_COLS), lambda bi, n: (bi, n, 0)),
        out_shape=jax.ShapeDtypeStruct((b, s, RET_COLS), BF16),
        scratch_shapes=[pltpu.VMEM((nb, hh // hg, width, width), F32)],
        compiler_params=_params(("parallel", "arbitrary")),
        name="retention",
    )(ret, ret, ret, ret, decay, xi, zeta, g_chunk, same, gn_gain.reshape(1, RET_COLS).astype(F32))


def _out_proj_rows(h, a, r, w_ref):
    return h + _dot(a, w_ref[:NSA_Q_COLS]) + _dot(r, w_ref[NSA_Q_COLS:])


def _out_proj_kernel(h_ref, a_ref, r_ref, w_ref, g_ref, h_out_ref, hn_ref):
    h = _out_proj_rows(h_ref[...], a_ref[...], r_ref[...], w_ref)
    h_out_ref[...] = h
    hn_ref[...] = _rms(h, g_ref[...]).astype(BF16)


def _out_proj(h, a, r, w, g):
    t, d = h.shape
    tm = min(STREAM_ROW_TILE, t)
    row = lambda n: pl.BlockSpec((tm, n), lambda i: (i, 0))
    return pl.pallas_call(
        _out_proj_kernel,
        grid=(t // tm,),
        in_specs=[row(d), row(NSA_Q_COLS), row(RET_COLS),
                  pl.BlockSpec((MIX_WIDTH, d), lambda i: (0, 0)),
                  pl.BlockSpec((1, d), lambda i: (0, 0))],
        out_specs=[row(d), row(d)],
        out_shape=[jax.ShapeDtypeStruct((t, d), F32), jax.ShapeDtypeStruct((t, d), BF16)],
        compiler_params=_params(("parallel",)),
        name="out_proj",
    )(h, a, r, w.astype(BF16), g.reshape(1, d))


def _swiglu_chunk(x, wg, wu, wd):
    hid = jax.nn.silu(_dot(x, wg)) * _dot(x, wu)
    return _dot(hid.astype(BF16), wd)


def _ple_rows(h, p, g, proj_ref, gate_ref):
    emb = _dot(p.astype(BF16), proj_ref[...])
    return h + emb * jax.nn.sigmoid(_dot(_rms(h, g).astype(BF16), gate_ref[...]))


def _dense_layer_kernel(h_ref, a_ref, r_ref, p_ref, wo_ref, gf_ref, wg_ref, wu_ref, wd_ref, gp_ref, proj_ref, gate_ref,
                        gl_ref, o_ref, *, final_norm):
    h = _out_proj_rows(h_ref[...], a_ref[...], r_ref[...], wo_ref)
    x = _rms(h, gf_ref[...]).astype(BF16)
    acc = None
    for f in range(D_FF // FFN_COL_TILE):
        cols = slice(f * FFN_COL_TILE, (f + 1) * FFN_COL_TILE)
        part = _swiglu_chunk(x, wg_ref[:, cols], wu_ref[:, cols], wd_ref[cols, :])
        acc = part if acc is None else acc + part
    h = _ple_rows(h + acc, p_ref[...], gp_ref[...], proj_ref, gate_ref)
    if final_norm:
        h = _rms(h, gl_ref[...])
    o_ref[...] = h


def _dense_layer(h, a, r, p, w_out, g_ffn, wg, wu, wd, g_ple, proj, gate, g_final, final_norm):
    t, d = h.shape
    tm = min(FFN_ROW_TILE, t)
    row = lambda n: pl.BlockSpec((tm, n), lambda i: (i, 0))
    held = lambda shape: pl.BlockSpec(shape, lambda i: (0, 0), pipeline_mode=pl.Buffered(1))
    vec = held((1, d))
    return pl.pallas_call(
        functools.partial(_dense_layer_kernel, final_norm=final_norm),
        grid=(t // tm,),
        in_specs=[row(d), row(NSA_Q_COLS), row(RET_COLS), row(PLE_DIM),
                  held((MIX_WIDTH, d)), vec, held((d, D_FF)), held((d, D_FF)), held((D_FF, d)),
                  vec, held((PLE_DIM, d)), held((d, d)), vec],
        out_specs=row(d),
        out_shape=jax.ShapeDtypeStruct((t, d), F32),
        compiler_params=_params(("parallel",)),
        name="dense_layer",
    )(h, a, r, p, w_out.astype(BF16), g_ffn.reshape(1, d), wg.astype(BF16), wu.astype(BF16), wd.astype(BF16),
      g_ple.reshape(1, d), proj.astype(BF16), gate.astype(BF16), g_final.reshape(1, d))


def _lane_column(table, lane_index):
    lane = lax.broadcasted_iota(jnp.int32, table.shape, 1)
    col = jnp.sum(jnp.where(lane == lane_index, table, 0.0), axis=-1, keepdims=True)
    return jnp.broadcast_to(col, table.shape)


def _moe_kernel(x_ref, router_ref, wg_ref, wu_ref, wd_ref, o_ref,
                xs_ref, y_ref, slot_ref, gate_ref, slot_row_ref, slot_e_ref, gate_e_ref, count_ref):
    e = pl.program_id(1)
    f = pl.program_id(2)
    tm, d = x_ref.shape
    sub = MOE_GROUP_ROWS

    @pl.when((e == 0) & (f == 0))
    def _route():
        o_ref[...] = jnp.zeros_like(o_ref)
        logits = _dot(x_ref[...], router_ref[...])
        lane = lax.broadcasted_iota(jnp.int32, logits.shape, 1)
        logits = jnp.where(lane < N_EXPERTS, logits, -jnp.inf)
        v1 = jnp.max(logits, axis=-1, keepdims=True)
        i1 = jnp.min(jnp.where(logits == v1, lane, V7X_LANES), axis=-1, keepdims=True)
        rest = jnp.where(lane == i1, -jnp.inf, logits)
        v2 = jnp.max(rest, axis=-1, keepdims=True)
        i2 = jnp.min(jnp.where(rest == v2, lane, V7X_LANES), axis=-1, keepdims=True)
        e2 = jnp.exp(v2 - v1)
        inv = 1.0 / (1.0 + e2)
        gate_ref[...] = jnp.where(lane == i1, inv, 0.0) + jnp.where(lane == i2, e2 * inv, 0.0)
        routed = jnp.where((lane == i1) | (lane == i2), 1.0, 0.0)
        c = MOE_RANK_CHUNK
        before = (lax.broadcasted_iota(jnp.int32, (c, c), 1) < lax.broadcasted_iota(jnp.int32, (c, c), 0))
        before = jnp.where(before, 1.0, 0.0).astype(BF16)
        offset = jnp.zeros((1, V7X_LANES), F32)
        for j in range(tm // c):
            part = routed[j * c:(j + 1) * c]
            rank = _dot(before, part.astype(BF16)) + offset
            slot_ref[j * c:(j + 1) * c, :] = jnp.where(part > 0.0, rank, -1.0)
            offset = offset + jnp.sum(part, axis=0, keepdims=True)
        for ee in range(N_EXPERTS):
            count_ref[ee] = offset[0, ee].astype(jnp.int32)
        slot_row_ref[...] = slot_ref[...].T[:N_EXPERTS]

    n_groups = (count_ref[e] + sub - 1) // sub

    @pl.when(f == 0)
    def _gather():
        slot_e_ref[...] = _lane_column(slot_ref[...], e)
        gate_e_ref[...] = _lane_column(gate_ref[...], e)
        slot_row = slot_row_ref[pl.ds(e, 1), :]

        def body(s, _):
            r0 = pl.multiple_of(s * sub, sub)
            want = (r0 + lax.broadcasted_iota(jnp.int32, (sub, 1), 0)).astype(F32)
            onehot = jnp.where(slot_row == want, 1.0, 0.0).astype(BF16)
            xs_ref[pl.ds(r0, sub), :] = _dot(onehot, x_ref[...]).astype(BF16)
            y_ref[pl.ds(r0, sub), :] = jnp.zeros((sub, d), F32)
            return 0

        lax.fori_loop(0, n_groups, body, 0)

    def expert(i, _, base, n, align):
        r0 = pl.multiple_of(base + i * n, align)
        y_ref[pl.ds(r0, n), :] += _swiglu_chunk(xs_ref[pl.ds(r0, n), :], wg_ref[0], wu_ref[0], wd_ref[0])
        return 0

    tail = MOE_TAIL_ROWS
    n_filled = count_ref[e] // sub
    rest = count_ref[e] - n_filled * sub
    short = (rest <= tail) & (rest > 0)
    merged = short & (n_filled > 0)
    n_whole = jnp.where(rest > tail, n_filled + 1, jnp.where(merged, n_filled - 1, n_filled))
    lax.fori_loop(0, n_whole, functools.partial(expert, base=0, n=sub, align=sub), 0)
    lax.fori_loop(0, merged.astype(jnp.int32),
                  functools.partial(expert, base=(n_filled - 1) * sub, n=sub + tail, align=sub), 0)
    lax.fori_loop(0, (short & (n_filled == 0)).astype(jnp.int32),
                  functools.partial(expert, base=0, n=tail, align=tail), 0)

    @pl.when(f == pl.num_programs(2) - 1)
    def _scatter():
        tc = MOE_SCATTER_ROWS

        def body(s, _):
            r0 = pl.multiple_of(s * sub, sub)
            y = y_ref[pl.ds(r0, sub), :].astype(BF16)
            want = (r0 + lax.broadcasted_iota(jnp.int32, (1, sub), 1)).astype(F32)
            for j in range(tm // tc):
                rows = slice(j * tc, (j + 1) * tc)
                slot = jnp.concatenate([slot_e_ref[rows, :]] * (sub // V7X_LANES), axis=1)
                onehot = jnp.where(slot == want, 1.0, 0.0).astype(BF16)
                weight = jnp.concatenate([gate_e_ref[rows, :]] * (d // V7X_LANES), axis=1)
                o_ref[rows, :] += weight * _dot(onehot, y)
            return 0

        lax.fori_loop(0, n_groups, body, 0)


def _moe_ffn(hn, router, wg, wu, wd):
    t, d = hn.shape
    tm = min(MOE_ROW_TILE, t)
    tf = MOE_COL_TILE
    nf = D_FF // tf
    router = jnp.pad(router, ((0, 0), (0, V7X_LANES - N_EXPERTS))).astype(BF16)
    once = pl.Buffered(1)
    return pl.pallas_call(
        _moe_kernel,
        grid=(t // tm, N_EXPERTS, nf),
        in_specs=[pl.BlockSpec((tm, d), lambda i, e, f: (i, 0), pipeline_mode=once),
                  pl.BlockSpec((d, V7X_LANES), lambda i, e, f: (0, 0), pipeline_mode=once),
                  pl.BlockSpec((1, d, tf), lambda i, e, f: (e, 0, f)),
                  pl.BlockSpec((1, d, tf), lambda i, e, f: (e, 0, f)),
                  pl.BlockSpec((1, tf, d), lambda i, e, f: (e, f, 0))],
        out_specs=pl.BlockSpec((tm, d), lambda i, e, f: (i, 0), pipeline_mode=once),
        out_shape=jax.ShapeDtypeStruct((t, d), F32),
        scratch_shapes=[pltpu.VMEM((tm, d), BF16), pltpu.VMEM((tm, d), F32),
                        pltpu.VMEM((tm, V7X_LANES), F32), pltpu.VMEM((tm, V7X_LANES), F32),
                        pltpu.VMEM((N_EXPERTS, tm), F32),
                        pltpu.VMEM((tm, V7X_LANES), F32), pltpu.VMEM((tm, V7X_LANES), F32),
                        pltpu.SMEM((N_EXPERTS,), jnp.int32)],
        compiler_params=_params(("parallel", "arbitrary", "arbitrary"), vmem=MOE_VMEM_LIMIT),
        name="moe_ffn",
    )(hn, router, wg.astype(BF16), wu.astype(BF16), wd.astype(BF16))


def _ple_kernel(h_ref, f_ref, p_ref, g_ref, proj_ref, gate_ref, gf_ref, o_ref, *, final_norm):
    h = _ple_rows(h_ref[...] + f_ref[...], p_ref[...], g_ref[...], proj_ref, gate_ref)
    if final_norm:
        h = _rms(h, gf_ref[...])
    o_ref[...] = h


def _ple(h, f, p, g, proj, gate, g_final, final_norm):
    t, d = h.shape
    tm = min(STREAM_ROW_TILE, t)
    vec = pl.BlockSpec((1, d), lambda i: (0, 0))
    return pl.pallas_call(
        functools.partial(_ple_kernel, final_norm=final_norm),
        grid=(t // tm,),
        in_specs=[pl.BlockSpec((tm, d), lambda i: (i, 0)),
                  pl.BlockSpec((tm, d), lambda i: (i, 0)),
                  pl.BlockSpec((tm, PLE_DIM), lambda i: (i, 0)),
                  vec,
                  pl.BlockSpec((PLE_DIM, d), lambda i: (0, 0)),
                  pl.BlockSpec((d, d), lambda i: (0, 0)),
                  vec],
        out_specs=pl.BlockSpec((tm, d), lambda i: (i, 0)),
        out_shape=jax.ShapeDtypeStruct((t, d), F32),
        compiler_params=_params(("parallel",)),
        name="ple",
    )(h, f, p, g.reshape(1, d), proj.astype(BF16), gate.astype(BF16), g_final.reshape(1, d))


def kernel(x, p, w_in, w_out, g_mix, g_ffn, g_ple, g_final, cmp_pos, cmp_w1, cmp_w2, ret_gn,
           ffn_gate, ffn_up, ffn_down, moe_router, moe_gate, moe_up, moe_down, ple_proj, ple_gate):
    b, s, d = x.shape
    depth = w_in.shape[0]
    t = b * s
    h = x
    for i in range(depth):
        qn, kcv, ks, kv3, gl, ret = _in_proj(h.reshape(b, s, d), g_mix[i], _arrange_w_in(w_in[i]))
        kv_cmp = _compress(kcv, cmp_pos[i], cmp_w1[i], cmp_w2[i])
        a = _nsa_attention(qn, kv_cmp, ks, kv3, gl)
        r = _retention(ret, ret_gn[i])
        h, a, r, p_i = h.reshape(t, d), a.reshape(t, NSA_Q_COLS), r.reshape(t, RET_COLS), p[i].reshape(t, PLE_DIM)
        last = i == depth - 1
        if i % 2 == 0:
            h = _dense_layer(h, a, r, p_i, w_out[i], g_ffn[i], ffn_gate[i // 2], ffn_up[i // 2], ffn_down[i // 2],
                             g_ple[i], ple_proj[i], ple_gate[i], g_final, last)
        else:
            h, hn = _out_proj(h, a, r, w_out[i], g_ffn[i])
            f = _moe_ffn(hn, moe_router[i // 2], moe_gate[i // 2], moe_up[i // 2], moe_down[i // 2])
            h = _ple(h, f, p_i, g_ple[i], ple_proj[i], ple_gate[i], g_final, last)
    return h.reshape(b, s, d)
```

```python
import functools
import math

import jax
import jax.numpy as jnp
from jax import lax
from jax.experimental import pallas as pl
from jax.experimental.pallas import tpu as pltpu

F32 = jnp.float32
BF16 = jnp.bfloat16

D_MODEL = 1024
HEAD_DIM = 64
NSA_HEADS = 8
NSA_KV_HEADS = 2
NSA_GROUP = NSA_HEADS // NSA_KV_HEADS
RET_HEADS = 8
CMP_BLOCK = 32
CMP_STRIDE = 16
CMP_HIDDEN = 256
SEL_BLOCK = 64
N_SELECT = 16
N_LOCAL_SEL = 2
WINDOW = 512
RET_CHUNK = 128
D_FF = 3584
N_EXPERTS = 8
PLE_DIM = 256
EPS = 1e-6
NEG_INF = -1e30
BIG = 1e9

NSA_Q_COLS = NSA_HEADS * HEAD_DIM
NSA_KV_COLS = NSA_KV_HEADS * HEAD_DIM
NSA_GATE_COLS = 3 * NSA_HEADS
RET_COLS = RET_HEADS * HEAD_DIM
MIX_WIDTH = NSA_Q_COLS + RET_COLS

V7X_LANES = 128
V7X_SUBLANES = 8
V7X_VMEM_BYTES = 64 * 1024 * 1024
VMEM_LIMIT = V7X_VMEM_BYTES * 3 // 4

ROW_TILE = 512
STREAM_ROW_TILE = 1024
ATTN_Q_TILE = 256
SEL_KV_TILE = 256
RET_BATCH = 4
RET_HEAD_GROUP = 4
FFN_ROW_TILE = 512
FFN_COL_TILE = 512
MOE_COL_TILE = 1792
MOE_ROW_TILE = 2048
MOE_GROUP_ROWS = 256
MOE_TAIL_ROWS = 64
MOE_RANK_CHUNK = 256
MOE_SCATTER_ROWS = 512
MOE_VMEM_LIMIT = V7X_VMEM_BYTES * 7 // 8

POS_SHIFT = 6
POS_SPLIT = 1 << POS_SHIFT
SEL_SHIFT = SEL_BLOCK.bit_length() - 1
MAX_SEL_BLOCKS = V7X_LANES


def _params(sem, vmem=VMEM_LIMIT):
    return pltpu.CompilerParams(dimension_semantics=sem, vmem_limit_bytes=vmem)


def _dot(a, b):
    return jnp.dot(a, b, preferred_element_type=F32)


def _dot_nt(a, b):
    return lax.dot_general(a, b, (((1,), (1,)), ((), ())), preferred_element_type=F32)


def _dot_tn(a, b):
    return lax.dot_general(a, b, (((0,), (0,)), ((), ())), preferred_element_type=F32)


def _rms(x, g):
    return x * lax.rsqrt(jnp.mean(x * x, axis=-1, keepdims=True) + EPS) * g


def _lane_features(shape, first, second):
    lane = lax.broadcasted_iota(jnp.int32, shape, 1)
    return jnp.where(lane == 0, first, jnp.where(lane == 1, second, 0))


def _key_features(pos, width):
    return _lane_features((pos.shape[0], width), pos >> POS_SHIFT, pos & (POS_SPLIT - 1))


def _ones_features(n, width):
    return _lane_features((n, width), 1, 0)


def _int_to_bf16(x):
    return x.astype(F32).astype(BF16)


def _in_proj_kernel(h_ref, g_ref, w_ref, qn_ref, kcv_ref, ks_ref, kv3_ref, gl_ref, ret_ref):
    tm = h_ref.shape[1]
    xn = _rms(h_ref[0], g_ref[...]).astype(BF16)
    pair = 2 * HEAD_DIM
    pos = pl.program_id(1) * tm + lax.broadcasted_iota(jnp.int32, (tm, 1), 0)
    kfeat = _int_to_bf16(_key_features(pos, HEAD_DIM))
    vfeat = _int_to_bf16(_ones_features(tm, HEAD_DIM))
    lane = lax.broadcasted_iota(jnp.int32, (tm, MAX_SEL_BLOCKS), 1)
    block_onehot = jnp.where(lane == (pos >> SEL_SHIFT), 1.0, 0.0).astype(BF16)

    wide = 2 * pair
    per_dot = wide // HEAD_DIM
    for j in range(NSA_Q_COLS // wide):
        z = (_dot(xn, w_ref[:, j * wide:(j + 1) * wide]) * HEAD_DIM ** -0.5).astype(BF16)
        for k in range(per_dot):
            head = per_dot * j + k
            slope = 2.0 ** -(head + 1)
            qfeat = _lane_features((tm, HEAD_DIM), POS_SPLIT * slope, slope).astype(BF16)
            qn_ref[0, head] = jnp.concatenate([z[:, k * HEAD_DIM:(k + 1) * HEAD_DIM], qfeat], axis=1)
    base = NSA_Q_COLS
    z = _dot(xn, w_ref[:, base:base + wide]).astype(BF16)
    for a in range(2):
        kcv_ref[a, 0] = z[:, a * pair:(a + 1) * pair]
    base += wide
    for c in range(2):
        z = _dot(xn, w_ref[:, base + c * wide:base + (c + 1) * wide]).astype(BF16)
        for half in range(2):
            a = 2 * c + half
            for g in range(NSA_KV_HEADS):
                zg = z[:, half * pair + g * HEAD_DIM:half * pair + (g + 1) * HEAD_DIM]
                if a == 0:
                    ks_ref[0, g] = jnp.concatenate([zg, kfeat, block_onehot], axis=1)
                else:
                    kv3_ref[a - 1, 0, g] = jnp.concatenate([zg, kfeat if a == 2 else vfeat], axis=1)
    base += 2 * wide
    for a in range(4):
        ret_ref[0, :, a * RET_COLS:(a + 1) * RET_COLS] = _dot(
            xn, w_ref[:, base + a * RET_COLS:base + (a + 1) * RET_COLS]).astype(BF16)
    base += 4 * RET_COLS
    gl_ref[0] = _dot(xn, w_ref[:, base:base + NSA_KV_HEADS * V7X_LANES])


def _arrange_w_in(w):
    q_end = NSA_Q_COLS
    kv_end = q_end + 6 * NSA_KV_COLS
    gl_end = kv_end + NSA_GATE_COLS
    gl = w[:, kv_end:gl_end].reshape(D_MODEL, 3, NSA_KV_HEADS, NSA_GROUP)
    gl = jnp.transpose(gl, (0, 2, 1, 3)).reshape(D_MODEL, NSA_KV_HEADS, 3 * NSA_GROUP)
    gl = jnp.pad(gl, ((0, 0), (0, 0), (0, V7X_LANES - 3 * NSA_GROUP)))
    gl = gl.reshape(D_MODEL, NSA_KV_HEADS * V7X_LANES)
    return jnp.concatenate([w[:, :kv_end], w[:, gl_end:], gl], axis=1).astype(BF16)


def _in_proj(h, g, w):
    b, s, d = h.shape
    assert s // SEL_BLOCK <= MAX_SEL_BLOCKS
    tm = min(ROW_TILE, s)
    ncols = w.shape[1]
    aug = 2 * HEAD_DIM
    return pl.pallas_call(
        _in_proj_kernel,
        grid=(b, s // tm),
        in_specs=[
            pl.BlockSpec((1, tm, d), lambda bi, si: (bi, si, 0)),
            pl.BlockSpec((1, d), lambda bi, si: (0, 0)),
            pl.BlockSpec((d, ncols), lambda bi, si: (0, 0)),
        ],
        out_specs=[
            pl.BlockSpec((1, NSA_HEADS, tm, aug), lambda bi, si: (bi, 0, si, 0)),
            pl.BlockSpec((2, 1, tm, NSA_KV_COLS), lambda bi, si: (0, bi, si, 0)),
            pl.BlockSpec((1, NSA_KV_HEADS, tm, aug + MAX_SEL_BLOCKS), lambda bi, si: (bi, 0, si, 0)),
            pl.BlockSpec((3, 1, NSA_KV_HEADS, tm, aug), lambda bi, si: (0, bi, 0, si, 0)),
            pl.BlockSpec((1, tm, NSA_KV_HEADS * V7X_LANES), lambda bi, si: (bi, si, 0)),
            pl.BlockSpec((1, tm, 4 * RET_COLS), lambda bi, si: (bi, si, 0)),
        ],
        out_shape=[
            jax.ShapeDtypeStruct((b, NSA_HEADS, s, aug), BF16),
            jax.ShapeDtypeStruct((2, b, s, NSA_KV_COLS), BF16),
            jax.ShapeDtypeStruct((b, NSA_KV_HEADS, s, aug + MAX_SEL_BLOCKS), BF16),
            jax.ShapeDtypeStruct((3, b, NSA_KV_HEADS, s, aug), BF16),
            jax.ShapeDtypeStruct((b, s, NSA_KV_HEADS * V7X_LANES), F32),
            jax.ShapeDtypeStruct((b, s, 4 * RET_COLS), BF16),
        ],
        compiler_params=_params(("parallel", "parallel")),
        name="in_proj",
    )(h, g.reshape(1, d), w)


def _compress_kernel(x_ref, wbig_ref, pos_ref, w1_ref, w2_ref, o_ref):
    nch = x_ref.shape[2]
    u = _dot(x_ref[0, 0], wbig_ref[0])
    hid0 = _dot(pos_ref[0], w1_ref[0])[0:1]
    w2 = w2_ref[0]
    cmp_end = lax.broadcasted_iota(jnp.int32, (nch, 1), 0) * CMP_STRIDE + (CMP_BLOCK - 1)
    feat = _int_to_bf16(jnp.where(pl.program_id(0) == 0, _key_features(cmp_end, HEAD_DIM),
                                  _ones_features(nch, HEAD_DIM)))
    for g in range(NSA_KV_HEADS):
        c0 = g * 2 * CMP_HIDDEN
        first = u[:, c0:c0 + CMP_HIDDEN]
        second = u[:, c0 + CMP_HIDDEN:c0 + 2 * CMP_HIDDEN]
        hid = first + pltpu.roll(second, nch - 1, 0) + hid0
        out = _dot(jax.nn.gelu(hid).astype(BF16), w2).astype(BF16)
        o_ref[0, 0, g] = jnp.concatenate([out, feat], axis=1)


def _arrange_cmp_w1(w1):
    r = CMP_BLOCK // CMP_STRIDE
    w1r = w1.reshape(2, r, CMP_STRIDE, HEAD_DIM, CMP_HIDDEN)
    eye = jnp.eye(NSA_KV_HEADS, dtype=w1.dtype)
    big = jnp.einsum('krcdh,gf->kcgdfrh', w1r, eye)
    return big.reshape(2, CMP_STRIDE * NSA_KV_COLS, NSA_KV_HEADS * r * CMP_HIDDEN).astype(BF16)


def _compress(kcv, cmp_pos, cmp_w1, cmp_w2):
    _, b, s, _ = kcv.shape
    nch = s // CMP_STRIDE
    x = kcv.reshape(2, b, nch, CMP_STRIDE * NSA_KV_COLS)
    wbig = _arrange_cmp_w1(cmp_w1)
    pos = jnp.broadcast_to(cmp_pos.reshape(2, 1, CMP_BLOCK * HEAD_DIM),
                           (2, V7X_SUBLANES, CMP_BLOCK * HEAD_DIM)).astype(BF16)
    w1 = cmp_w1.reshape(2, CMP_BLOCK * HEAD_DIM, CMP_HIDDEN).astype(BF16)
    w2 = cmp_w2.astype(BF16)
    kdim = CMP_STRIDE * NSA_KV_COLS
    return pl.pallas_call(
        _compress_kernel,
        grid=(2, b),
        in_specs=[
            pl.BlockSpec((1, 1, nch, kdim), lambda a, bi: (a, bi, 0, 0)),
            pl.BlockSpec((1, kdim, wbig.shape[2]), lambda a, bi: (a, 0, 0)),
            pl.BlockSpec((1, V7X_SUBLANES, CMP_BLOCK * HEAD_DIM), lambda a, bi: (a, 0, 0)),
            pl.BlockSpec((1, CMP_BLOCK * HEAD_DIM, CMP_HIDDEN), lambda a, bi: (a, 0, 0)),
            pl.BlockSpec((1, CMP_HIDDEN, HEAD_DIM), lambda a, bi: (a, 0, 0)),
        ],
        out_specs=pl.BlockSpec((1, 1, NSA_KV_HEADS, nch, 2 * HEAD_DIM), lambda a, bi: (a, bi, 0, 0, 0)),
        out_shape=jax.ShapeDtypeStruct((2, b, NSA_KV_HEADS, nch, 2 * HEAD_DIM), BF16),
        compiler_params=_params(("parallel", "parallel")),
        name="compress",
    )(x, wbig, pos, w1, w2)


def _normalise_t(acc):
    return acc[:HEAD_DIM] * (1.0 / jnp.maximum(acc[HEAD_DIM:HEAD_DIM + 1], 1e-30))


def _attn_kernel(q_ref, kc_ref, vc_ref, ks_ref, vs_ref, kw_ref, vw_ref, gl_ref, ovt_ref, o_ref, tiles_ref,
                 sel_ref, oc_ref, imp_ref, *, n_pick, tk):
    qt = q_ref.shape[2]
    cols = NSA_GROUP * qt
    t0 = pl.program_id(2) * qt
    q_t = q_ref[0].reshape(cols, 2 * HEAD_DIM).T
    tq = t0 + lax.broadcasted_iota(jnp.int32, (1, qt), 1)

    def per_head(valid, masked):
        bias = jnp.where(valid, 0.0, masked)
        return jnp.concatenate([bias] * NSA_GROUP, axis=1)

    nc = kc_ref.shape[3]

    def compressed(rows):
        cmp_end = lax.broadcasted_iota(jnp.int32, (rows, 1), 0) * CMP_STRIDE + (CMP_BLOCK - 1)
        logit = _dot(kc_ref[0, 0, 0, :rows], q_t) + per_head(cmp_end <= tq, -jnp.inf)
        e = jnp.exp(logit - jnp.maximum(jnp.max(logit, axis=0, keepdims=True), NEG_INF))
        p = e * (1.0 / jnp.maximum(jnp.sum(e, axis=0, keepdims=True), 1e-30))
        oc_ref[...] = _dot(vc_ref[0, 0, 0, :rows].T, p.astype(BF16))[:HEAD_DIM]
        p_sum = p[:, 0:qt]
        for r in range(1, NSA_GROUP):
            p_sum = p_sum + p[:, r * qt:(r + 1) * qt]
        p_hi = p_sum.astype(BF16)
        p_lo = (p_sum - p_hi.astype(F32)).astype(BF16)
        imp_ref[...] = _dot(ovt_ref[:, :rows], p_hi) + _dot(ovt_ref[:, :rows], p_lo)

    n_cmp_live = (t0 + qt - CMP_BLOCK + CMP_STRIDE) // CMP_STRIDE
    cmp_sizes = [nc // 4, nc // 2, nc]
    for lower, rows in zip([None] + cmp_sizes[:-1], cmp_sizes):
        above = True if lower is None else n_cmp_live > lower
        below = True if rows == nc else n_cmp_live <= rows
        pl.when(jnp.logical_and(above, below))(functools.partial(compressed, rows))
    o_c = oc_ref[...]
    imp = imp_ref[...]

    shape = (MAX_SEL_BLOCKS, qt)
    blk = lax.broadcasted_iota(jnp.int32, shape, 0)
    back = ((t0 + lax.broadcasted_iota(jnp.int32, shape, 1)) >> SEL_SHIFT) - blk
    forced = (blk == 0) | ((back >= 0) & (back < N_LOCAL_SEL))
    score = jnp.where(forced, BIG, jnp.where(back >= 0, imp, -BIG))

    def pick(_, work):
        row = lax.broadcasted_iota(jnp.int32, work.shape, 0)
        m = jnp.max(work, axis=0, keepdims=True)
        first = jnp.min(jnp.where(work == m, row, MAX_SEL_BLOCKS), axis=0, keepdims=True)
        return jnp.where(row == first, -jnp.inf, work)

    n_live = (t0 + qt + SEL_BLOCK - 1) >> SEL_SHIFT
    sizes = [MAX_SEL_BLOCKS // 4, MAX_SEL_BLOCKS // 2, MAX_SEL_BLOCKS]
    for lower, rows in zip([0] + sizes[:-1], sizes):
        @pl.when((n_live > lower) & ((n_live <= rows) | (rows == MAX_SEL_BLOCKS)))
        def _(rows=rows):
            left = lax.fori_loop(0, n_pick, pick, score[:rows])
            sel_ref[...] = jnp.zeros(shape, F32)
            sel_ref[:rows] = jnp.where(left == -jnp.inf, 1.0, 0.0)
    sel = sel_ref[...]
    sel_bias = ((sel - 1.0) * -NEG_INF).astype(BF16)
    q_sel = jnp.concatenate([q_t, jnp.concatenate([sel_bias] * NSA_GROUP, axis=1)], axis=0)

    n_own = max(1, qt // tk)
    n_full = t0 // tk
    blocks_per_tile = tk // SEL_BLOCK
    block_used = jnp.max(sel, axis=1, keepdims=True)
    n_visit = jnp.int32(0)
    for j in range(tiles_ref.shape[0]):
        used = jnp.max(block_used[j * blocks_per_tile:(j + 1) * blocks_per_tile]) > 0.0
        tiles_ref[n_visit] = j
        n_visit = n_visit + (used & (j < n_full)).astype(jnp.int32)

    def sweep(tiles, carry, causal):
        m_i, acc = carry
        starts = [pl.multiple_of(j * tk, tk) for j in tiles]
        scores = []
        for k0 in starts:
            s = _dot(ks_ref[0, 0, pl.ds(k0, tk), :], q_sel)
            if causal:
                s = s + per_head(k0 + lax.broadcasted_iota(jnp.int32, (tk, 1), 0) <= tq, NEG_INF)
            scores.append(s)
        m_new = m_i
        for s in scores:
            m_new = jnp.maximum(m_new, jnp.max(s, axis=0, keepdims=True))
        acc = jnp.exp(m_i - m_new) * acc
        for s, k0 in zip(scores, starts):
            acc = acc + _dot(vs_ref[0, 0, 0, pl.ds(k0, tk), :].T, jnp.exp(s - m_new).astype(BF16))
        return m_new, acc

    carry = (jnp.full((1, cols), NEG_INF, F32), jnp.zeros((2 * HEAD_DIM, cols), F32))
    done = jnp.int32(0)
    for width in (1, 2, 4):
        steps = n_visit // width if width == 4 else (n_visit // width) & 1
        carry = lax.fori_loop(
            0, steps,
            lambda i, c, width=width, done=done: sweep(
                [tiles_ref[done + width * i + u] for u in range(width)], c, causal=False),
            carry)
        done = done + steps * width
    carry = sweep([n_full + j for j in range(n_own)], carry, causal=True)
    o_s = _normalise_t(carry[1])

    band = WINDOW + qt
    w0 = pl.multiple_of(jnp.maximum(t0 - WINDOW, 0), math.gcd(qt, WINDOW))
    kpos = w0 + lax.broadcasted_iota(jnp.int32, (band, 1), 0)
    valid_w = (kpos <= tq) & (kpos > tq - WINDOW)
    logit_w = _dot(kw_ref[0, 0, 0, pl.ds(w0, band), :], q_t) + per_head(valid_w, NEG_INF)
    e_w = jnp.exp(logit_w - jnp.max(logit_w, axis=0, keepdims=True))
    o_w = _normalise_t(_dot(vw_ref[0, 0, 0, pl.ds(w0, band), :].T, e_w.astype(BF16)))

    gate_t = jax.nn.sigmoid(gl_ref[0]).T

    def branch_gate(branch):
        first = branch * NSA_GROUP
        return jnp.concatenate([gate_t[first + r:first + r + 1] for r in range(NSA_GROUP)], axis=1)

    o_t = branch_gate(0) * o_c + branch_gate(1) * o_s + branch_gate(2) * o_w
    o = o_t.T
    o_ref[0] = jnp.concatenate([o[r * qt:(r + 1) * qt] for r in range(NSA_GROUP)], axis=1).astype(o_ref.dtype)


def _nsa_attention(qn, kv_cmp, ks, kv3, gl):
    b, _, s, _ = qn.shape
    nc = kv_cmp.shape[3]
    nsel = s // SEL_BLOCK
    n_pick = min(N_SELECT, nsel)
    tk = min(SEL_KV_TILE, s)
    aug = 2 * HEAD_DIM
    qt = min(ATTN_Q_TILE, s)
    assert qt % tk == 0 or tk % qt == 0
    cmp_start = jnp.arange(nc) * CMP_STRIDE
    sel_start = jnp.arange(nsel) * SEL_BLOCK
    overlap = (jnp.minimum(cmp_start[:, None] + CMP_BLOCK, sel_start[None, :] + SEL_BLOCK)
               > jnp.maximum(cmp_start[:, None], sel_start[None, :]))
    overlap_t = jnp.pad(overlap.T.astype(BF16), ((0, MAX_SEL_BLOCKS - nsel), (0, 0)))
    kv_spec = lambda a: pl.BlockSpec((1, 1, 1, s, aug), lambda bi, g, i, a=a: (a, bi, g, 0, 0))
    cmp_spec = lambda a: pl.BlockSpec((1, 1, 1, nc, aug), lambda bi, g, i, a=a: (a, bi, g, 0, 0))
    return pl.pallas_call(
        functools.partial(_attn_kernel, n_pick=n_pick, tk=tk),
        grid=(b, NSA_KV_HEADS, s // qt),
        in_specs=[
            pl.BlockSpec((1, NSA_GROUP, qt, aug), lambda bi, g, i: (bi, g, i, 0)),
            cmp_spec(0), cmp_spec(1),
            pl.BlockSpec((1, 1, s, aug + MAX_SEL_BLOCKS), lambda bi, g, i: (bi, g, 0, 0)),
            kv_spec(0), kv_spec(1), kv_spec(2),
            pl.BlockSpec((1, qt, V7X_LANES), lambda bi, g, i: (bi, i, g)),
            pl.BlockSpec((MAX_SEL_BLOCKS, nc), lambda bi, g, i: (0, 0)),
        ],
        out_specs=pl.BlockSpec((1, qt, NSA_GROUP * HEAD_DIM), lambda bi, g, i: (bi, i, g)),
        out_shape=jax.ShapeDtypeStruct((b, s, NSA_Q_COLS), BF16),
        scratch_shapes=[pltpu.SMEM((s // tk,), jnp.int32), pltpu.VMEM((MAX_SEL_BLOCKS, qt), F32),
                        pltpu.VMEM((HEAD_DIM, NSA_GROUP * qt), F32), pltpu.VMEM((MAX_SEL_BLOCKS, qt), F32)],
        compiler_params=_params(("parallel", "parallel", "arbitrary")),
        name="nsa_attention",
    )(qn, kv_cmp, kv_cmp, ks, kv3, kv3, kv3, gl, overlap_t)


def _retention_kernel(q_ref, k_ref, v_ref, g_ref, decay_ref, xi_ref, zeta_ref, gch_ref, same_ref, gn_ref,
                      o_ref, state_ref):
    @pl.when(pl.program_id(1) == 0)
    def _():
        state_ref[...] = jnp.zeros_like(state_ref)

    width = same_ref.shape[0]
    heads = width // HEAD_DIM
    c = q_ref.shape[1]
    scale = jnp.asarray(HEAD_DIM ** -0.5, BF16)
    lane_head = lax.broadcasted_iota(jnp.int32, (c, width), 1) // HEAD_DIM
    same = same_ref[...]
    mean_w = (same * (1.0 / HEAD_DIM)).astype(BF16)

    def head_mean(x):
        hi = x.astype(BF16)
        lo = (x - hi.astype(F32)).astype(BF16)
        return _dot(hi, mean_w) + _dot(lo, mean_w)

    def stack(x):
        zero = jnp.zeros_like(x)
        return jnp.concatenate([jnp.where(lane_head == j, x, zero) for j in range(heads)], axis=0)

    for i in range(q_ref.shape[0]):
        outs = []
        for p in range(RET_COLS // width):
            cols = slice(p * width, (p + 1) * width)
            q = q_ref[i, :, cols]
            k = k_ref[i, :, cols] * scale
            v = v_ref[i, :, cols]
            state = state_ref[i, p]
            inner = _dot_nt(q, stack(k)) * decay_ref[p]
            o = _dot(inner.astype(BF16), stack(v)) + _dot(q, state.astype(BF16)) * xi_ref[p]
            kz = (k.astype(F32) * zeta_ref[p]).astype(BF16)
            state_ref[i, p] = gch_ref[p] * state + same * _dot_tn(kz, v)
            centred = o - head_mean(o)
            outs.append(centred * lax.rsqrt(head_mean(jnp.square(centred)) + EPS))
        o = jnp.concatenate(outs, axis=1) * gn_ref[...]
        o_ref[i] = (jax.nn.silu(g_ref[i].astype(F32)) * o).astype(o_ref.dtype)


def _retention(ret, gn_gain):
    b, s, _ = ret.shape
    c = RET_CHUNK
    hh = RET_HEADS
    hg = RET_HEAD_GROUP
    width = hg * HEAD_DIM
    log_gamma = jnp.log1p(-jnp.exp2(-5.0 - jnp.arange(hh, dtype=F32)))
    pos = jnp.arange(c, dtype=F32)
    diff = pos[:, None] - pos[None, :]
    decay = jnp.where(diff >= 0, jnp.exp(jnp.maximum(diff, 0.0)[None] * log_gamma[:, None, None]), 0.0)
    xi = jnp.exp((pos + 1.0)[None] * log_gamma[:, None])[..., None]
    zeta = jnp.exp((c - 1.0 - pos)[None] * log_gamma[:, None])[..., None]
    g_chunk = jnp.exp(c * log_gamma)[:, None, None]
    side_by_side = lambda t: jnp.concatenate([t[j::hg] for j in range(hg)], axis=-1)
    decay = side_by_side(decay)
    xi = side_by_side(jnp.broadcast_to(xi, (hh, c, HEAD_DIM)))
    zeta = side_by_side(jnp.broadcast_to(zeta, (hh, c, HEAD_DIM)))
    head_of = jnp.arange(width) // HEAD_DIM
    same = (head_of[:, None] == head_of[None, :]).astype(F32)
    g_lane = side_by_side(jnp.broadcast_to(g_chunk, (hh, 1, HEAD_DIM)))
    g_chunk = same[None] * g_lane
    nb = math.gcd(RET_BATCH, b)
    part = lambda a: pl.BlockSpec((nb, c, RET_COLS), lambda bi, n, a=a: (bi, n, a))
    full = lambda shape: pl.BlockSpec(shape, lambda bi, n: (0,) * len(shape))
    return pl.pallas_call(
        _retention_kernel,
        grid=(b // nb, s // c),
        in_specs=[part(0), part(1), part(2), part(3),
                  full((hh // hg, c, hg * c)), full((hh // hg, c, width)), full((hh // hg, c, width)),
                  full((hh // hg, width, width)), full((width, width)), full((1, RET_COLS))],
        out_specs=pl.BlockSpec((nb, c, RET_COLS), lambda bi, n: (bi, n, 0)),
        out_shape=jax.ShapeDtypeStruct((b, s, RET_COLS), BF16),
        scratch_shapes=[pltpu.VMEM((nb, hh // hg, width, width), F32)],
        compiler_params=_params(("parallel", "arbitrary")),
        name="retention",
    )(ret, ret, ret, ret, decay, xi, zeta, g_chunk, same, gn_gain.reshape(1, RET_COLS).astype(F32))


def _out_proj_rows(h, a, r, w_ref):
    return h + _dot(a, w_ref[:NSA_Q_COLS]) + _dot(r, w_ref[NSA_Q_COLS:])


def _out_proj_kernel(h_ref, a_ref, r_ref, w_ref, g_ref, h_out_ref, hn_ref):
    h = _out_proj_rows(h_ref[...], a_ref[...], r_ref[...], w_ref)
    h_out_ref[...] = h
    hn_ref[...] = _rms(h, g_ref[...]).astype(BF16)


def _out_proj(h, a, r, w, g):
    t, d = h.shape
    tm = min(STREAM_ROW_TILE, t)
    row = lambda n: pl.BlockSpec((tm, n), lambda i: (i, 0))
    return pl.pallas_call(
        _out_proj_kernel,
        grid=(t // tm,),
        in_specs=[row(d), row(NSA_Q_COLS), row(RET_COLS),
                  pl.BlockSpec((MIX_WIDTH, d), lambda i: (0, 0)),
                  pl.BlockSpec((1, d), lambda i: (0, 0))],
        out_specs=[row(d), row(d)],
        out_shape=[jax.ShapeDtypeStruct((t, d), F32), jax.ShapeDtypeStruct((t, d), BF16)],
        compiler_params=_params(("parallel",)),
        name="out_proj",
    )(h, a, r, w.astype(BF16), g.reshape(1, d))


def _swiglu_chunk(x, wg, wu, wd):
    hid = jax.nn.silu(_dot(x, wg)) * _dot(x, wu)
    return _dot(hid.astype(BF16), wd)


def _ple_rows(h, p, g, proj_ref, gate_ref):
    emb = _dot(p.astype(BF16), proj_ref[...])
    return h + emb * jax.nn.sigmoid(_dot(_rms(h, g).astype(BF16), gate_ref[...]))


def _dense_layer_kernel(h_ref, a_ref, r_ref, p_ref, wo_ref, gf_ref, wg_ref, wu_ref, wd_ref, gp_ref, proj_ref, gate_ref,
                        gl_ref, o_ref, *, final_norm):
    h = _out_proj_rows(h_ref[...], a_ref[...], r_ref[...], wo_ref)
    x = _rms(h, gf_ref[...]).astype(BF16)
    acc = None
    for f in range(D_FF // FFN_COL_TILE):
        cols = slice(f * FFN_COL_TILE, (f + 1) * FFN_COL_TILE)
        part = _swiglu_chunk(x, wg_ref[:, cols], wu_ref[:, cols], wd_ref[cols, :])
        acc = part if acc is None else acc + part
    h = _ple_rows(h + acc, p_ref[...], gp_ref[...], proj_ref, gate_ref)
    if final_norm:
        h = _rms(h, gl_ref[...])
    o_ref[...] = h


def _dense_layer(h, a, r, p, w_out, g_ffn, wg, wu, wd, g_ple, proj, gate, g_final, final_norm):
    t, d = h.shape
    tm = min(FFN_ROW_TILE, t)
    row = lambda n: pl.BlockSpec((tm, n), lambda i: (i, 0))
    held = lambda shape: pl.BlockSpec(shape, lambda i: (0, 0), pipeline_mode=pl.Buffered(1))
    vec = held((1, d))
    return pl.pallas_call(
        functools.partial(_dense_layer_kernel, final_norm=final_norm),
        grid=(t // tm,),
        in_specs=[row(d), row(NSA_Q_COLS), row(RET_COLS), row(PLE_DIM),
                  held((MIX_WIDTH, d)), vec, held((d, D_FF)), held((d, D_FF)), held((D_FF, d)),
                  vec, held((PLE_DIM, d)), held((d, d)), vec],
        out_specs=row(d),
        out_shape=jax.ShapeDtypeStruct((t, d), F32),
        compiler_params=_params(("parallel",)),
        name="dense_layer",
    )(h, a, r, p, w_out.astype(BF16), g_ffn.reshape(1, d), wg.astype(BF16), wu.astype(BF16), wd.astype(BF16),
      g_ple.reshape(1, d), proj.astype(BF16), gate.astype(BF16), g_final.reshape(1, d))


def _lane_column(table, lane_index):
    lane = lax.broadcasted_iota(jnp.int32, table.shape, 1)
    col = jnp.sum(jnp.where(lane == lane_index, table, 0.0), axis=-1, keepdims=True)
    return jnp.broadcast_to(col, table.shape)


def _moe_kernel(x_ref, router_ref, wg_ref, wu_ref, wd_ref, o_ref,
                xs_ref, y_ref, slot_ref, gate_ref, slot_row_ref, slot_e_ref, gate_e_ref, count_ref):
    e = pl.program_id(1)
    f = pl.program_id(2)
    tm, d = x_ref.shape
    sub = MOE_GROUP_ROWS

    @pl.when((e == 0) & (f == 0))
    def _route():
        o_ref[...] = jnp.zeros_like(o_ref)
        logits = _dot(x_ref[...], router_ref[...])
        lane = lax.broadcasted_iota(jnp.int32, logits.shape, 1)
        logits = jnp.where(lane < N_EXPERTS, logits, -jnp.inf)
        v1 = jnp.max(logits, axis=-1, keepdims=True)
        i1 = jnp.min(jnp.where(logits == v1, lane, V7X_LANES), axis=-1, keepdims=True)
        rest = jnp.where(lane == i1, -jnp.inf, logits)
        v2 = jnp.max(rest, axis=-1, keepdims=True)
        i2 = jnp.min(jnp.where(rest == v2, lane, V7X_LANES), axis=-1, keepdims=True)
        e2 = jnp.exp(v2 - v1)
        inv = 1.0 / (1.0 + e2)
        gate_ref[...] = jnp.where(lane == i1, inv, 0.0) + jnp.where(lane == i2, e2 * inv, 0.0)
        routed = jnp.where((lane == i1) | (lane == i2), 1.0, 0.0)
        c = MOE_RANK_CHUNK
        before = (lax.broadcasted_iota(jnp.int32, (c, c), 1) < lax.broadcasted_iota(jnp.int32, (c, c), 0))
        before = jnp.where(before, 1.0, 0.0).astype(BF16)
        offset = jnp.zeros((1, V7X_LANES), F32)
        for j in range(tm // c):
            part = routed[j * c:(j + 1) * c]
            rank = _dot(before, part.astype(BF16)) + offset
            slot_ref[j * c:(j + 1) * c, :] = jnp.where(part > 0.0, rank, -1.0)
            offset = offset + jnp.sum(part, axis=0, keepdims=True)
        for ee in range(N_EXPERTS):
            count_ref[ee] = offset[0, ee].astype(jnp.int32)
        slot_row_ref[...] = slot_ref[...].T[:N_EXPERTS]

    tail = MOE_TAIL_ROWS
    n_groups = (count_ref[e] + sub - 1) // sub
    n_filled = count_ref[e] // sub
    rest = count_ref[e] - n_filled * sub
    short = (rest <= tail) & (rest > 0)
    merged = short & (n_filled > 0)
    merged_base = (n_filled - 1) * sub
    n_plain = jnp.where(merged, n_filled - 1, n_groups)
    merged_once = merged.astype(jnp.int32)
    long = sub + tail
    long_pad = -(-long // V7X_LANES) * V7X_LANES

    @pl.when(f == 0)
    def _gather():
        slot_e_ref[...] = _lane_column(slot_ref[...], e)
        gate_e_ref[...] = _lane_column(gate_ref[...], e)
        slot_row = slot_row_ref[pl.ds(e, 1), :]

        def body(i, _, base, n, n_zero):
            r0 = pl.multiple_of(base + i * sub, sub)
            want = (r0 + lax.broadcasted_iota(jnp.int32, (n, 1), 0)).astype(F32)
            onehot = jnp.where(slot_row == want, 1.0, 0.0).astype(BF16)
            xs_ref[pl.ds(r0, n), :] = _dot(onehot, x_ref[...]).astype(BF16)
            y_ref[pl.ds(r0, n_zero), :] = jnp.zeros((n_zero, d), F32)
            return 0

        lax.fori_loop(0, n_plain, functools.partial(body, base=0, n=sub, n_zero=sub), 0)
        lax.fori_loop(0, merged_once, functools.partial(body, base=merged_base, n=long, n_zero=long_pad), 0)

    def expert(i, _, base, n, align):
        r0 = pl.multiple_of(base + i * n, align)
        y_ref[pl.ds(r0, n), :] += _swiglu_chunk(xs_ref[pl.ds(r0, n), :], wg_ref[0], wu_ref[0], wd_ref[0])
        return 0

    n_whole = jnp.where(rest > tail, n_filled + 1, jnp.where(merged, n_filled - 1, n_filled))
    lax.fori_loop(0, n_whole, functools.partial(expert, base=0, n=sub, align=sub), 0)
    lax.fori_loop(0, merged_once, functools.partial(expert, base=merged_base, n=long, align=sub), 0)
    lax.fori_loop(0, (short & (n_filled == 0)).astype(jnp.int32),
                  functools.partial(expert, base=0, n=tail, align=tail), 0)

    @pl.when(f == pl.num_programs(2) - 1)
    def _scatter():
        tc = MOE_SCATTER_ROWS

        def body(i, _, base, n):
            r0 = pl.multiple_of(base + i * sub, sub)
            y = y_ref[pl.ds(r0, n), :].astype(BF16)
            want = (r0 + lax.broadcasted_iota(jnp.int32, (1, n), 1)).astype(F32)
            for j in range(tm // tc):
                rows = slice(j * tc, (j + 1) * tc)
                slot = jnp.concatenate([slot_e_ref[rows, :]] * (n // V7X_LANES), axis=1)
                onehot = jnp.where(slot == want, 1.0, 0.0).astype(BF16)
                weight = jnp.concatenate([gate_e_ref[rows, :]] * (d // V7X_LANES), axis=1)
                o_ref[rows, :] += weight * _dot(onehot, y)
            return 0

        lax.fori_loop(0, n_plain, functools.partial(body, base=0, n=sub), 0)
        lax.fori_loop(0, merged_once, functools.partial(body, base=merged_base, n=long_pad), 0)


def _moe_ffn(hn, router, wg, wu, wd):
    t, d = hn.shape
    tm = min(MOE_ROW_TILE, t)
    tf = MOE_COL_TILE
    nf = D_FF // tf
    router = jnp.pad(router, ((0, 0), (0, V7X_LANES - N_EXPERTS))).astype(BF16)
    once = pl.Buffered(1)
    return pl.pallas_call(
        _moe_kernel,
        grid=(t // tm, N_EXPERTS, nf),
        in_specs=[pl.BlockSpec((tm, d), lambda i, e, f: (i, 0), pipeline_mode=once),
                  pl.BlockSpec((d, V7X_LANES), lambda i, e, f: (0, 0), pipeline_mode=once),
                  pl.BlockSpec((1, d, tf), lambda i, e, f: (e, 0, f)),
                  pl.BlockSpec((1, d, tf), lambda i, e, f: (e, 0, f)),
                  pl.BlockSpec((1, tf, d), lambda i, e, f: (e, f, 0))],
        out_specs=pl.BlockSpec((tm, d), lambda i, e, f: (i, 0), pipeline_mode=once),
        out_shape=jax.ShapeDtypeStruct((t, d), F32),
        scratch_shapes=[pltpu.VMEM((tm, d), BF16), pltpu.VMEM((tm, d), F32),
                        pltpu.VMEM((tm, V7X_LANES), F32), pltpu.VMEM((tm, V7X_LANES), F32),
                        pltpu.VMEM((N_EXPERTS, tm), F32),
                        pltpu.VMEM((tm, V7X_LANES), F32), pltpu.VMEM((tm, V7X_LANES), F32),
                        pltpu.SMEM((N_EXPERTS,), jnp.int32)],
        compiler_params=_params(("parallel", "arbitrary", "arbitrary"), vmem=MOE_VMEM_LIMIT),
        name="moe_ffn",
    )(hn, router, wg.astype(BF16), wu.astype(BF16), wd.astype(BF16))


def _ple_kernel(h_ref, f_ref, p_ref, g_ref, proj_ref, gate_ref, gf_ref, o_ref, *, final_norm):
    h = _ple_rows(h_ref[...] + f_ref[...], p_ref[...], g_ref[...], proj_ref, gate_ref)
    if final_norm:
        h = _rms(h, gf_ref[...])
    o_ref[...] = h


def _ple(h, f, p, g, proj, gate, g_final, final_norm):
    t, d = h.shape
    tm = min(STREAM_ROW_TILE, t)
    vec = pl.BlockSpec((1, d), lambda i: (0, 0))
    return pl.pallas_call(
        functools.partial(_ple_kernel, final_norm=final_norm),
        grid=(t // tm,),
        in_specs=[pl.BlockSpec((tm, d), lambda i: (i, 0)),
                  pl.BlockSpec((tm, d), lambda i: (i, 0)),
                  pl.BlockSpec((tm, PLE_DIM), lambda i: (i, 0)),
                  vec,
                  pl.BlockSpec((PLE_DIM, d), lambda i: (0, 0)),
                  pl.BlockSpec((d, d), lambda i: (0, 0)),
                  vec],
        out_specs=pl.BlockSpec((tm, d), lambda i: (i, 0)),
        out_shape=jax.ShapeDtypeStruct((t, d), F32),
        compiler_params=_params(("parallel",)),
        name="ple",
    )(h, f, p, g.reshape(1, d), proj.astype(BF16), gate.astype(BF16), g_final.reshape(1, d))


def kernel(x, p, w_in, w_out, g_mix, g_ffn, g_ple, g_final, cmp_pos, cmp_w1, cmp_w2, ret_gn,
           ffn_gate, ffn_up, ffn_down, moe_router, moe_gate, moe_up, moe_down, ple_proj, ple_gate):
    b, s, d = x.shape
    depth = w_in.shape[0]
    t = b * s
    h = x
    for i in range(depth):
        qn, kcv, ks, kv3, gl, ret = _in_proj(h.reshape(b, s, d), g_mix[i], _arrange_w_in(w_in[i]))
        kv_cmp = _compress(kcv, cmp_pos[i], cmp_w1[i], cmp_w2[i])
        a = _nsa_attention(qn, kv_cmp, ks, kv3, gl)
        r = _retention(ret, ret_gn[i])
        h, a, r, p_i = h.reshape(t, d), a.reshape(t, NSA_Q_COLS), r.reshape(t, RET_COLS), p[i].reshape(t, PLE_DIM)
        last = i == depth - 1
        if i % 2 == 0:
            h = _dense_layer(h, a, r, p_i, w_out[i], g_ffn[i], ffn_gate[i // 2], ffn_up[i // 2], ffn_down[i // 2],
                             g_ple[i], ple_proj[i], ple_gate[i], g_final, last)
        else:
            h, hn = _out_proj(h, a, r, w_out[i], g_ffn[i])
            f = _moe_ffn(hn, moe_router[i // 2], moe_gate[i // 2], moe_up[i // 2], moe_down[i // 2])
            h = _ple(h, f, p_i, g_ple[i], ple_proj[i], ple_gate[i], g_final, last)
    return h.reshape(b, s, d)
```

```python
import functools
import math

import jax
import jax.numpy as jnp
from jax import lax
from jax.experimental import pallas as pl
from jax.experimental.pallas import tpu as pltpu

F32 = jnp.float32
BF16 = jnp.bfloat16

D_MODEL = 1024
HEAD_DIM = 64
NSA_HEADS = 8
NSA_KV_HEADS = 2
NSA_GROUP = NSA_HEADS // NSA_KV_HEADS
RET_HEADS = 8
CMP_BLOCK = 32
CMP_STRIDE = 16
CMP_HIDDEN = 256
SEL_BLOCK = 64
N_SELECT = 16
N_LOCAL_SEL = 2
WINDOW = 512
RET_CHUNK = 128
D_FF = 3584
N_EXPERTS = 8
PLE_DIM = 256
EPS = 1e-6
NEG_INF = -1e30
BIG = 1e9

NSA_Q_COLS = NSA_HEADS * HEAD_DIM
NSA_KV_COLS = NSA_KV_HEADS * HEAD_DIM
NSA_GATE_COLS = 3 * NSA_HEADS
RET_COLS = RET_HEADS * HEAD_DIM
MIX_WIDTH = NSA_Q_COLS + RET_COLS

V7X_LANES = 128
V7X_SUBLANES = 8
V7X_VMEM_BYTES = 64 * 1024 * 1024
VMEM_LIMIT = V7X_VMEM_BYTES * 3 // 4

ROW_TILE = 512
STREAM_ROW_TILE = 1024
ATTN_Q_TILE = 256
SEL_KV_TILE = 256
SEL_STEP_WIDTHS = (1, 2, 4, 8)
RET_BATCH = 4
RET_HEAD_GROUP = 4
FFN_ROW_TILE = 512
FFN_COL_TILE = 512
MOE_COL_TILE = 1792
MOE_ROW_TILE = 2048
MOE_GROUP_ROWS = 256
MOE_TAIL_ROWS = 64
MOE_RANK_CHUNK = 256
MOE_SCATTER_ROWS = 512
MOE_VMEM_LIMIT = V7X_VMEM_BYTES * 7 // 8

POS_SHIFT = 6
POS_SPLIT = 1 << POS_SHIFT
SEL_SHIFT = SEL_BLOCK.bit_length() - 1
MAX_SEL_BLOCKS = V7X_LANES


def _params(sem, vmem=VMEM_LIMIT):
    return pltpu.CompilerParams(dimension_semantics=sem, vmem_limit_bytes=vmem)


def _dot(a, b):
    return jnp.dot(a, b, preferred_element_type=F32)


def _dot_nt(a, b):
    return lax.dot_general(a, b, (((1,), (1,)), ((), ())), preferred_element_type=F32)


def _dot_tn(a, b):
    return lax.dot_general(a, b, (((0,), (0,)), ((), ())), preferred_element_type=F32)


def _rms(x, g):
    return x * lax.rsqrt(jnp.mean(x * x, axis=-1, keepdims=True) + EPS) * g


def _lane_features(shape, first, second):
    lane = lax.broadcasted_iota(jnp.int32, shape, 1)
    return jnp.where(lane == 0, first, jnp.where(lane == 1, second, 0))


def _key_features(pos, width):
    return _lane_features((pos.shape[0], width), pos >> POS_SHIFT, pos & (POS_SPLIT - 1))


def _ones_features(n, width):
    return _lane_features((n, width), 1, 0)


def _int_to_bf16(x):
    return x.astype(F32).astype(BF16)


def _in_proj_kernel(h_ref, g_ref, w_ref, qn_ref, kcv_ref, ks_ref, kv3_ref, gl_ref, ret_ref):
    tm = h_ref.shape[1]
    xn = _rms(h_ref[0], g_ref[...]).astype(BF16)
    pair = 2 * HEAD_DIM
    pos = pl.program_id(1) * tm + lax.broadcasted_iota(jnp.int32, (tm, 1), 0)
    kfeat = _int_to_bf16(_key_features(pos, HEAD_DIM))
    vfeat = _int_to_bf16(_ones_features(tm, HEAD_DIM))
    lane = lax.broadcasted_iota(jnp.int32, (tm, MAX_SEL_BLOCKS), 1)
    block_onehot = jnp.where(lane == (pos >> SEL_SHIFT), 1.0, 0.0).astype(BF16)

    wide = 2 * pair
    per_dot = wide // HEAD_DIM
    for j in range(NSA_Q_COLS // wide):
        z = (_dot(xn, w_ref[:, j * wide:(j + 1) * wide]) * HEAD_DIM ** -0.5).astype(BF16)
        for k in range(per_dot):
            head = per_dot * j + k
            slope = 2.0 ** -(head + 1)
            qfeat = _lane_features((tm, HEAD_DIM), POS_SPLIT * slope, slope).astype(BF16)
            qn_ref[0, head] = jnp.concatenate([z[:, k * HEAD_DIM:(k + 1) * HEAD_DIM], qfeat], axis=1)
    base = NSA_Q_COLS
    z = _dot(xn, w_ref[:, base:base + wide]).astype(BF16)
    for a in range(2):
        kcv_ref[a, 0] = z[:, a * pair:(a + 1) * pair]
    base += wide
    for c in range(2):
        z = _dot(xn, w_ref[:, base + c * wide:base + (c + 1) * wide]).astype(BF16)
        for half in range(2):
            a = 2 * c + half
            for g in range(NSA_KV_HEADS):
                zg = z[:, half * pair + g * HEAD_DIM:half * pair + (g + 1) * HEAD_DIM]
                if a == 0:
                    ks_ref[0, g] = jnp.concatenate([zg, kfeat, block_onehot], axis=1)
                else:
                    kv3_ref[a - 1, 0, g] = jnp.concatenate([zg, kfeat if a == 2 else vfeat], axis=1)
    base += 2 * wide
    for a in range(4):
        ret_ref[0, :, a * RET_COLS:(a + 1) * RET_COLS] = _dot(
            xn, w_ref[:, base + a * RET_COLS:base + (a + 1) * RET_COLS]).astype(BF16)
    base += 4 * RET_COLS
    gl_ref[0] = _dot(xn, w_ref[:, base:base + NSA_KV_HEADS * V7X_LANES])


def _arrange_w_in(w):
    q_end = NSA_Q_COLS
    kv_end = q_end + 6 * NSA_KV_COLS
    gl_end = kv_end + NSA_GATE_COLS
    gl = w[:, kv_end:gl_end].reshape(D_MODEL, 3, NSA_KV_HEADS, NSA_GROUP)
    gl = jnp.transpose(gl, (0, 2, 1, 3)).reshape(D_MODEL, NSA_KV_HEADS, 3 * NSA_GROUP)
    gl = jnp.pad(gl, ((0, 0), (0, 0), (0, V7X_LANES - 3 * NSA_GROUP)))
    gl = gl.reshape(D_MODEL, NSA_KV_HEADS * V7X_LANES)
    return jnp.concatenate([w[:, :kv_end], w[:, gl_end:], gl], axis=1).astype(BF16)


def _in_proj(h, g, w):
    b, s, d = h.shape
    assert s // SEL_BLOCK <= MAX_SEL_BLOCKS
    tm = min(ROW_TILE, s)
    ncols = w.shape[1]
    aug = 2 * HEAD_DIM
    return pl.pallas_call(
        _in_proj_kernel,
        grid=(b, s // tm),
        in_specs=[
            pl.BlockSpec((1, tm, d), lambda bi, si: (bi, si, 0)),
            pl.BlockSpec((1, d), lambda bi, si: (0, 0)),
            pl.BlockSpec((d, ncols), lambda bi, si: (0, 0)),
        ],
        out_specs=[
            pl.BlockSpec((1, NSA_HEADS, tm, aug), lambda bi, si: (bi, 0, si, 0)),
            pl.BlockSpec((2, 1, tm, NSA_KV_COLS), lambda bi, si: (0, bi, si, 0)),
            pl.BlockSpec((1, NSA_KV_HEADS, tm, aug + MAX_SEL_BLOCKS), lambda bi, si: (bi, 0, si, 0)),
            pl.BlockSpec((3, 1, NSA_KV_HEADS, tm, aug), lambda bi, si: (0, bi, 0, si, 0)),
            pl.BlockSpec((1, tm, NSA_KV_HEADS * V7X_LANES), lambda bi, si: (bi, si, 0)),
            pl.BlockSpec((1, tm, 4 * RET_COLS), lambda bi, si: (bi, si, 0)),
        ],
        out_shape=[
            jax.ShapeDtypeStruct((b, NSA_HEADS, s, aug), BF16),
            jax.ShapeDtypeStruct((2, b, s, NSA_KV_COLS), BF16),
            jax.ShapeDtypeStruct((b, NSA_KV_HEADS, s, aug + MAX_SEL_BLOCKS), BF16),
            jax.ShapeDtypeStruct((3, b, NSA_KV_HEADS, s, aug), BF16),
            jax.ShapeDtypeStruct((b, s, NSA_KV_HEADS * V7X_LANES), F32),
            jax.ShapeDtypeStruct((b, s, 4 * RET_COLS), BF16),
        ],
        compiler_params=_params(("parallel", "parallel")),
        name="in_proj",
    )(h, g.reshape(1, d), w)


def _compress_kernel(x_ref, wbig_ref, pos_ref, w1_ref, w2_ref, o_ref):
    nch = x_ref.shape[2]
    u = _dot(x_ref[0, 0], wbig_ref[0])
    hid0 = _dot(pos_ref[0], w1_ref[0])[0:1]
    w2 = w2_ref[0]
    cmp_end = lax.broadcasted_iota(jnp.int32, (nch, 1), 0) * CMP_STRIDE + (CMP_BLOCK - 1)
    feat = _int_to_bf16(jnp.where(pl.program_id(0) == 0, _key_features(cmp_end, HEAD_DIM),
                                  _ones_features(nch, HEAD_DIM)))
    for g in range(NSA_KV_HEADS):
        c0 = g * 2 * CMP_HIDDEN
        first = u[:, c0:c0 + CMP_HIDDEN]
        second = u[:, c0 + CMP_HIDDEN:c0 + 2 * CMP_HIDDEN]
        hid = first + pltpu.roll(second, nch - 1, 0) + hid0
        out = _dot(jax.nn.gelu(hid).astype(BF16), w2).astype(BF16)
        o_ref[0, 0, g] = jnp.concatenate([out, feat], axis=1)


def _arrange_cmp_w1(w1):
    r = CMP_BLOCK // CMP_STRIDE
    w1r = w1.reshape(2, r, CMP_STRIDE, HEAD_DIM, CMP_HIDDEN)
    eye = jnp.eye(NSA_KV_HEADS, dtype=w1.dtype)
    big = jnp.einsum('krcdh,gf->kcgdfrh', w1r, eye)
    return big.reshape(2, CMP_STRIDE * NSA_KV_COLS, NSA_KV_HEADS * r * CMP_HIDDEN).astype(BF16)


def _compress(kcv, cmp_pos, cmp_w1, cmp_w2):
    _, b, s, _ = kcv.shape
    nch = s // CMP_STRIDE
    x = kcv.reshape(2, b, nch, CMP_STRIDE * NSA_KV_COLS)
    wbig = _arrange_cmp_w1(cmp_w1)
    pos = jnp.broadcast_to(cmp_pos.reshape(2, 1, CMP_BLOCK * HEAD_DIM),
                           (2, V7X_SUBLANES, CMP_BLOCK * HEAD_DIM)).astype(BF16)
    w1 = cmp_w1.reshape(2, CMP_BLOCK * HEAD_DIM, CMP_HIDDEN).astype(BF16)
    w2 = cmp_w2.astype(BF16)
    kdim = CMP_STRIDE * NSA_KV_COLS
    return pl.pallas_call(
        _compress_kernel,
        grid=(2, b),
        in_specs=[
            pl.BlockSpec((1, 1, nch, kdim), lambda a, bi: (a, bi, 0, 0)),
            pl.BlockSpec((1, kdim, wbig.shape[2]), lambda a, bi: (a, 0, 0)),
            pl.BlockSpec((1, V7X_SUBLANES, CMP_BLOCK * HEAD_DIM), lambda a, bi: (a, 0, 0)),
            pl.BlockSpec((1, CMP_BLOCK * HEAD_DIM, CMP_HIDDEN), lambda a, bi: (a, 0, 0)),
            pl.BlockSpec((1, CMP_HIDDEN, HEAD_DIM), lambda a, bi: (a, 0, 0)),
        ],
        out_specs=pl.BlockSpec((1, 1, NSA_KV_HEADS, nch, 2 * HEAD_DIM), lambda a, bi: (a, bi, 0, 0, 0)),
        out_shape=jax.ShapeDtypeStruct((2, b, NSA_KV_HEADS, nch, 2 * HEAD_DIM), BF16),
        compiler_params=_params(("parallel", "parallel")),
        name="compress",
    )(x, wbig, pos, w1, w2)


def _normalise_t(acc):
    return acc[:HEAD_DIM] * (1.0 / jnp.maximum(acc[HEAD_DIM:HEAD_DIM + 1], 1e-30))


def _attn_kernel(q_ref, kc_ref, vc_ref, ks_ref, vs_ref, kw_ref, vw_ref, gl_ref, ovt_ref, o_ref, tiles_ref,
                 sel_ref, oc_ref, imp_ref, *, n_pick, tk):
    qt = q_ref.shape[2]
    cols = NSA_GROUP * qt
    t0 = pl.program_id(2) * qt
    q_t = q_ref[0].reshape(cols, 2 * HEAD_DIM).T
    tq = t0 + lax.broadcasted_iota(jnp.int32, (1, qt), 1)

    def per_head(valid, masked):
        bias = jnp.where(valid, 0.0, masked)
        return jnp.concatenate([bias] * NSA_GROUP, axis=1)

    nc = kc_ref.shape[3]

    def compressed(rows):
        cmp_end = lax.broadcasted_iota(jnp.int32, (rows, 1), 0) * CMP_STRIDE + (CMP_BLOCK - 1)
        logit = _dot(kc_ref[0, 0, 0, :rows], q_t) + per_head(cmp_end <= tq, -jnp.inf)
        e = jnp.exp(logit - jnp.maximum(jnp.max(logit, axis=0, keepdims=True), NEG_INF))
        p = e * (1.0 / jnp.maximum(jnp.sum(e, axis=0, keepdims=True), 1e-30))
        oc_ref[...] = _dot(vc_ref[0, 0, 0, :rows].T, p.astype(BF16))[:HEAD_DIM]
        p_sum = p[:, 0:qt]
        for r in range(1, NSA_GROUP):
            p_sum = p_sum + p[:, r * qt:(r + 1) * qt]
        p_hi = p_sum.astype(BF16)
        p_lo = (p_sum - p_hi.astype(F32)).astype(BF16)
        imp_ref[...] = _dot(ovt_ref[:, :rows], p_hi) + _dot(ovt_ref[:, :rows], p_lo)

    n_cmp_live = (t0 + qt - CMP_BLOCK + CMP_STRIDE) // CMP_STRIDE
    cmp_sizes = [nc // 4, nc // 2, nc]
    for lower, rows in zip([None] + cmp_sizes[:-1], cmp_sizes):
        above = True if lower is None else n_cmp_live > lower
        below = True if rows == nc else n_cmp_live <= rows
        pl.when(jnp.logical_and(above, below))(functools.partial(compressed, rows))
    o_c = oc_ref[...]
    imp = imp_ref[...]

    shape = (MAX_SEL_BLOCKS, qt)
    blk = lax.broadcasted_iota(jnp.int32, shape, 0)
    back = ((t0 + lax.broadcasted_iota(jnp.int32, shape, 1)) >> SEL_SHIFT) - blk
    forced = (blk == 0) | ((back >= 0) & (back < N_LOCAL_SEL))
    score = jnp.where(forced, BIG, jnp.where(back >= 0, imp, -BIG))

    def pick(_, work):
        row = lax.broadcasted_iota(jnp.int32, work.shape, 0)
        m = jnp.max(work, axis=0, keepdims=True)
        first = jnp.min(jnp.where(work == m, row, MAX_SEL_BLOCKS), axis=0, keepdims=True)
        return jnp.where(row == first, -jnp.inf, work)

    n_live = (t0 + qt + SEL_BLOCK - 1) >> SEL_SHIFT
    sizes = [MAX_SEL_BLOCKS // 4, MAX_SEL_BLOCKS // 2, MAX_SEL_BLOCKS]
    for lower, rows in zip([0] + sizes[:-1], sizes):
        @pl.when((n_live > lower) & ((n_live <= rows) | (rows == MAX_SEL_BLOCKS)))
        def _(rows=rows):
            left = lax.fori_loop(0, n_pick, pick, score[:rows])
            sel_ref[...] = jnp.zeros(shape, F32)
            sel_ref[:rows] = jnp.where(left == -jnp.inf, 1.0, 0.0)
    sel = sel_ref[...]
    sel_bias = ((sel - 1.0) * -NEG_INF).astype(BF16)
    q_sel = jnp.concatenate([q_t, jnp.concatenate([sel_bias] * NSA_GROUP, axis=1)], axis=0)

    n_own = max(1, qt // tk)
    n_full = t0 // tk
    blocks_per_tile = tk // SEL_BLOCK
    block_used = jnp.max(sel, axis=1, keepdims=True)
    n_visit = jnp.int32(0)
    for j in range(tiles_ref.shape[0]):
        used = jnp.max(block_used[j * blocks_per_tile:(j + 1) * blocks_per_tile]) > 0.0
        tiles_ref[n_visit] = j
        n_visit = n_visit + (used & (j < n_full)).astype(jnp.int32)

    def sweep(tiles, carry, causal):
        m_i, acc = carry
        starts = [pl.multiple_of(j * tk, tk) for j in tiles]
        scores = []
        for k0 in starts:
            s = _dot(ks_ref[0, 0, pl.ds(k0, tk), :], q_sel)
            if causal:
                s = s + per_head(k0 + lax.broadcasted_iota(jnp.int32, (tk, 1), 0) <= tq, NEG_INF)
            scores.append(s)
        m_new = m_i
        for s in scores:
            m_new = jnp.maximum(m_new, jnp.max(s, axis=0, keepdims=True))
        acc = jnp.exp(m_i - m_new) * acc
        for s, k0 in zip(scores, starts):
            acc = acc + _dot(vs_ref[0, 0, 0, pl.ds(k0, tk), :].T, jnp.exp(s - m_new).astype(BF16))
        return m_new, acc

    carry = (jnp.full((1, cols), NEG_INF, F32), jnp.zeros((2 * HEAD_DIM, cols), F32))
    done = jnp.int32(0)
    for width in SEL_STEP_WIDTHS:
        steps = n_visit // width if width == SEL_STEP_WIDTHS[-1] else (n_visit // width) & 1
        carry = lax.fori_loop(
            0, steps,
            lambda i, c, width=width, done=done: sweep(
                [tiles_ref[done + width * i + u] for u in range(width)], c, causal=False),
            carry)
        done = done + steps * width
    carry = sweep([n_full + j for j in range(n_own)], carry, causal=True)
    o_s = _normalise_t(carry[1])

    band = WINDOW + qt
    w0 = pl.multiple_of(jnp.maximum(t0 - WINDOW, 0), math.gcd(qt, WINDOW))
    kpos = w0 + lax.broadcasted_iota(jnp.int32, (band, 1), 0)
    valid_w = (kpos <= tq) & (kpos > tq - WINDOW)
    logit_w = _dot(kw_ref[0, 0, 0, pl.ds(w0, band), :], q_t) + per_head(valid_w, NEG_INF)
    e_w = jnp.exp(logit_w - jnp.max(logit_w, axis=0, keepdims=True))
    o_w = _normalise_t(_dot(vw_ref[0, 0, 0, pl.ds(w0, band), :].T, e_w.astype(BF16)))

    gate_t = jax.nn.sigmoid(gl_ref[0]).T

    def branch_gate(branch):
        first = branch * NSA_GROUP
        return jnp.concatenate([gate_t[first + r:first + r + 1] for r in range(NSA_GROUP)], axis=1)

    o_t = branch_gate(0) * o_c + branch_gate(1) * o_s + branch_gate(2) * o_w
    o = o_t.T
    o_ref[0] = jnp.concatenate([o[r * qt:(r + 1) * qt] for r in range(NSA_GROUP)], axis=1).astype(o_ref.dtype)


def _nsa_attention(qn, kv_cmp, ks, kv3, gl):
    b, _, s, _ = qn.shape
    nc = kv_cmp.shape[3]
    nsel = s // SEL_BLOCK
    n_pick = min(N_SELECT, nsel)
    tk = min(SEL_KV_TILE, s)
    aug = 2 * HEAD_DIM
    qt = min(ATTN_Q_TILE, s)
    assert qt % tk == 0 or tk % qt == 0
    cmp_start = jnp.arange(nc) * CMP_STRIDE
    sel_start = jnp.arange(nsel) * SEL_BLOCK
    overlap = (jnp.minimum(cmp_start[:, None] + CMP_BLOCK, sel_start[None, :] + SEL_BLOCK)
               > jnp.maximum(cmp_start[:, None], sel_start[None, :]))
    overlap_t = jnp.pad(overlap.T.astype(BF16), ((0, MAX_SEL_BLOCKS - nsel), (0, 0)))
    kv_spec = lambda a: pl.BlockSpec((1, 1, 1, s, aug), lambda bi, g, i, a=a: (a, bi, g, 0, 0))
    cmp_spec = lambda a: pl.BlockSpec((1, 1, 1, nc, aug), lambda bi, g, i, a=a: (a, bi, g, 0, 0))
    return pl.pallas_call(
        functools.partial(_attn_kernel, n_pick=n_pick, tk=tk),
        grid=(b, NSA_KV_HEADS, s // qt),
        in_specs=[
            pl.BlockSpec((1, NSA_GROUP, qt, aug), lambda bi, g, i: (bi, g, i, 0)),
            cmp_spec(0), cmp_spec(1),
            pl.BlockSpec((1, 1, s, aug + MAX_SEL_BLOCKS), lambda bi, g, i: (bi, g, 0, 0)),
            kv_spec(0), kv_spec(1), kv_spec(2),
            pl.BlockSpec((1, qt, V7X_LANES), lambda bi, g, i: (bi, i, g)),
            pl.BlockSpec((MAX_SEL_BLOCKS, nc), lambda bi, g, i: (0, 0)),
        ],
        out_specs=pl.BlockSpec((1, qt, NSA_GROUP * HEAD_DIM), lambda bi, g, i: (bi, i, g)),
        out_shape=jax.ShapeDtypeStruct((b, s, NSA_Q_COLS), BF16),
        scratch_shapes=[pltpu.SMEM((s // tk,), jnp.int32), pltpu.VMEM((MAX_SEL_BLOCKS, qt), F32),
                        pltpu.VMEM((HEAD_DIM, NSA_GROUP * qt), F32), pltpu.VMEM((MAX_SEL_BLOCKS, qt), F32)],
        compiler_params=_params(("parallel", "parallel", "arbitrary")),
        name="nsa_attention",
    )(qn, kv_cmp, kv_cmp, ks, kv3, kv3, kv3, gl, overlap_t)


def _retention_kernel(q_ref, k_ref, v_ref, g_ref, decay_ref, xi_ref, zeta_ref, gch_ref, same_ref, gn_ref,
                      o_ref, state_ref):
    @pl.when(pl.program_id(1) == 0)
    def _():
        state_ref[...] = jnp.zeros_like(state_ref)

    width = same_ref.shape[0]
    heads = width // HEAD_DIM
    c = q_ref.shape[1]
    scale = jnp.asarray(HEAD_DIM ** -0.5, BF16)
    lane_head = lax.broadcasted_iota(jnp.int32, (c, width), 1) // HEAD_DIM
    same = same_ref[...]
    mean_w = (same * (1.0 / HEAD_DIM)).astype(BF16)

    def head_mean(x):
        hi = x.astype(BF16)
        lo = (x - hi.astype(F32)).astype(BF16)
        return _dot(hi, mean_w) + _dot(lo, mean_w)

    def stack(x):
        zero = jnp.zeros_like(x)
        return jnp.concatenate([jnp.where(lane_head == j, x, zero) for j in range(heads)], axis=0)

    for i in range(q_ref.shape[0]):
        outs = []
        for p in range(RET_COLS // width):
            cols = slice(p * width, (p + 1) * width)
            q = q_ref[i, :, cols]
            k = k_ref[i, :, cols] * scale
            v = v_ref[i, :, cols]
            state = state_ref[i, p]
            inner = _dot_nt(q, stack(k)) * decay_ref[p]
            o = _dot(inner.astype(BF16), stack(v)) + _dot(q, state.astype(BF16)) * xi_ref[p]
            kz = (k.astype(F32) * zeta_ref[p]).astype(BF16)
            state_ref[i, p] = gch_ref[p] * state + same * _dot_tn(kz, v)
            centred = o - head_mean(o)
            outs.append(centred * lax.rsqrt(head_mean(jnp.square(centred)) + EPS))
        o = jnp.concatenate(outs, axis=1) * gn_ref[...]
        o_ref[i] = (jax.nn.silu(g_ref[i].astype(F32)) * o).astype(o_ref.dtype)


def _retention(ret, gn_gain):
    b, s, _ = ret.shape
    c = RET_CHUNK
    hh = RET_HEADS
    hg = RET_HEAD_GROUP
    width = hg * HEAD_DIM
    log_gamma = jnp.log1p(-jnp.exp2(-5.0 - jnp.arange(hh, dtype=F32)))
    pos = jnp.arange(c, dtype=F32)
    diff = pos[:, None] - pos[None, :]
    decay = jnp.where(diff >= 0, jnp.exp(jnp.maximum(diff, 0.0)[None] * log_gamma[:, None, None]), 0.0)
    xi = jnp.exp((pos + 1.0)[None] * log_gamma[:, None])[..., None]
    zeta = jnp.exp((c - 1.0 - pos)[None] * log_gamma[:, None])[..., None]
    g_chunk = jnp.exp(c * log_gamma)[:, None, None]
    side_by_side = lambda t: jnp.concatenate([t[j::hg] for j in range(hg)], axis=-1)
    decay = side_by_side(decay)
    xi = side_by_side(jnp.broadcast_to(xi, (hh, c, HEAD_DIM)))
    zeta = side_by_side(jnp.broadcast_to(zeta, (hh, c, HEAD_DIM)))
    head_of = jnp.arange(width) // HEAD_DIM
    same = (head_of[:, None] == head_of[None, :]).astype(F32)
    g_lane = side_by_side(jnp.broadcast_to(g_chunk, (hh, 1, HEAD_DIM)))
    g_chunk = same[None] * g_lane
    nb = math.gcd(RET_BATCH, b)
    part = lambda a: pl.BlockSpec((nb, c, RET_COLS), lambda bi, n, a=a: (bi, n, a))
    full = lambda shape: pl.BlockSpec(shape, lambda bi, n: (0,) * len(shape))
    return pl.pallas_call(
        _retention_kernel,
        grid=(b // nb, s // c),
        in_specs=[part(0), part(1), part(2), part(3),
                  full((hh // hg, c, hg * c)), full((hh // hg, c, width)), full((hh // hg, c, width)),
                  full((hh // hg, width, width)), full((width, width)), full((1, RET_COLS))],
        out_specs=pl.BlockSpec((nb, c, RET_COLS), lambda bi, n: (bi, n, 0)),
        out_shape=jax.ShapeDtypeStruct((b, s, RET_COLS), BF16),
        scratch_shapes=[pltpu.VMEM((nb, hh // hg, width, width), F32)],
        compiler_params=_params(("parallel", "arbitrary")),
        name="retention",
    )(ret, ret, ret, ret, decay, xi, zeta, g_chunk, same, gn_gain.reshape(1, RET_COLS).astype(F32))


def _out_proj_rows(h, a, r, w_ref):
    return h + _dot(a, w_ref[:NSA_Q_COLS]) + _dot(r, w_ref[NSA_Q_COLS:])


def _out_proj_kernel(h_ref, a_ref, r_ref, w_ref, g_ref, h_out_ref, hn_ref):
    h = _out_proj_rows(h_ref[...], a_ref[...], r_ref[...], w_ref)
    h_out_ref[...] = h
    hn_ref[...] = _rms(h, g_ref[...]).astype(BF16)


def _out_proj(h, a, r, w, g):
    t, d = h.shape
    tm = min(STREAM_ROW_TILE, t)
    row = lambda n: pl.BlockSpec((tm, n), lambda i: (i, 0))
    return pl.pallas_call(
        _out_proj_kernel,
        grid=(t // tm,),
        in_specs=[row(d), row(NSA_Q_COLS), row(RET_COLS),
                  pl.BlockSpec((MIX_WIDTH, d), lambda i: (0, 0)),
                  pl.BlockSpec((1, d), lambda i: (0, 0))],
        out_specs=[row(d), row(d)],
        out_shape=[jax.ShapeDtypeStruct((t, d), F32), jax.ShapeDtypeStruct((t, d), BF16)],
        compiler_params=_params(("parallel",)),
        name="out_proj",
    )(h, a, r, w.astype(BF16), g.reshape(1, d))


def _swiglu_chunk(x, wg, wu, wd):
    hid = jax.nn.silu(_dot(x, wg)) * _dot(x, wu)
    return _dot(hid.astype(BF16), wd)


def _ple_rows(h, p, g, proj_ref, gate_ref):
    emb = _dot(p.astype(BF16), proj_ref[...])
    return h + emb * jax.nn.sigmoid(_dot(_rms(h, g).astype(BF16), gate_ref[...]))


def _dense_layer_kernel(h_ref, a_ref, r_ref, p_ref, wo_ref, gf_ref, wg_ref, wu_ref, wd_ref, gp_ref, proj_ref, gate_ref,
                        gl_ref, o_ref, *, final_norm):
    h = _out_proj_rows(h_ref[...], a_ref[...], r_ref[...], wo_ref)
    x = _rms(h, gf_ref[...]).astype(BF16)
    acc = None
    for f in range(D_FF // FFN_COL_TILE):
        cols = slice(f * FFN_COL_TILE, (f + 1) * FFN_COL_TILE)
        part = _swiglu_chunk(x, wg_ref[:, cols], wu_ref[:, cols], wd_ref[cols, :])
        acc = part if acc is None else acc + part
    h = _ple_rows(h + acc, p_ref[...], gp_ref[...], proj_ref, gate_ref)
    if final_norm:
        h = _rms(h, gl_ref[...])
    o_ref[...] = h


def _dense_layer(h, a, r, p, w_out, g_ffn, wg, wu, wd, g_ple, proj, gate, g_final, final_norm):
    t, d = h.shape
    tm = min(FFN_ROW_TILE, t)
    row = lambda n: pl.BlockSpec((tm, n), lambda i: (i, 0))
    held = lambda shape: pl.BlockSpec(shape, lambda i: (0, 0), pipeline_mode=pl.Buffered(1))
    vec = held((1, d))
    return pl.pallas_call(
        functools.partial(_dense_layer_kernel, final_norm=final_norm),
        grid=(t // tm,),
        in_specs=[row(d), row(NSA_Q_COLS), row(RET_COLS), row(PLE_DIM),
                  held((MIX_WIDTH, d)), vec, held((d, D_FF)), held((d, D_FF)), held((D_FF, d)),
                  vec, held((PLE_DIM, d)), held((d, d)), vec],
        out_specs=row(d),
        out_shape=jax.ShapeDtypeStruct((t, d), F32),
        compiler_params=_params(("parallel",)),
        name="dense_layer",
    )(h, a, r, p, w_out.astype(BF16), g_ffn.reshape(1, d), wg.astype(BF16), wu.astype(BF16), wd.astype(BF16),
      g_ple.reshape(1, d), proj.astype(BF16), gate.astype(BF16), g_final.reshape(1, d))


def _lane_column(table, lane_index):
    lane = lax.broadcasted_iota(jnp.int32, table.shape, 1)
    col = jnp.sum(jnp.where(lane == lane_index, table, 0.0), axis=-1, keepdims=True)
    return jnp.broadcast_to(col, table.shape)


def _moe_kernel(x_ref, router_ref, wg_ref, wu_ref, wd_ref, o_ref,
                xs_ref, y_ref, slot_ref, gate_ref, slot_row_ref, slot_e_ref, gate_e_ref, count_ref):
    e = pl.program_id(1)
    f = pl.program_id(2)
    tm, d = x_ref.shape
    sub = MOE_GROUP_ROWS

    @pl.when((e == 0) & (f == 0))
    def _route():
        o_ref[...] = jnp.zeros_like(o_ref)
        logits = _dot(x_ref[...], router_ref[...])
        lane = lax.broadcasted_iota(jnp.int32, logits.shape, 1)
        logits = jnp.where(lane < N_EXPERTS, logits, -jnp.inf)
        v1 = jnp.max(logits, axis=-1, keepdims=True)
        i1 = jnp.min(jnp.where(logits == v1, lane, V7X_LANES), axis=-1, keepdims=True)
        rest = jnp.where(lane == i1, -jnp.inf, logits)
        v2 = jnp.max(rest, axis=-1, keepdims=True)
        i2 = jnp.min(jnp.where(rest == v2, lane, V7X_LANES), axis=-1, keepdims=True)
        e2 = jnp.exp(v2 - v1)
        inv = 1.0 / (1.0 + e2)
        gate_ref[...] = jnp.where(lane == i1, inv, 0.0) + jnp.where(lane == i2, e2 * inv, 0.0)
        routed = jnp.where((lane == i1) | (lane == i2), 1.0, 0.0)
        c = MOE_RANK_CHUNK
        before = (lax.broadcasted_iota(jnp.int32, (c, c), 1) < lax.broadcasted_iota(jnp.int32, (c, c), 0))
        before = jnp.where(before, 1.0, 0.0).astype(BF16)
        offset = jnp.zeros((1, V7X_LANES), F32)
        for j in range(tm // c):
            part = routed[j * c:(j + 1) * c]
            rank = _dot(before, part.astype(BF16)) + offset
            slot_ref[j * c:(j + 1) * c, :] = jnp.where(part > 0.0, rank, -1.0)
            offset = offset + jnp.sum(part, axis=0, keepdims=True)
        for ee in range(N_EXPERTS):
            count_ref[ee] = offset[0, ee].astype(jnp.int32)
        slot_row_ref[...] = slot_ref[...].T[:N_EXPERTS]

    tail = MOE_TAIL_ROWS
    n_groups = (count_ref[e] + sub - 1) // sub
    n_filled = count_ref[e] // sub
    rest = count_ref[e] - n_filled * sub
    short = (rest <= tail) & (rest > 0)
    merged = short & (n_filled > 0)
    merged_base = (n_filled - 1) * sub
    n_plain = jnp.where(merged, n_filled - 1, n_groups)
    merged_once = merged.astype(jnp.int32)
    long = sub + tail
    long_pad = -(-long // V7X_LANES) * V7X_LANES

    @pl.when(f == 0)
    def _gather():
        slot_e_ref[...] = _lane_column(slot_ref[...], e)
        gate_e_ref[...] = _lane_column(gate_ref[...], e)
        slot_row = slot_row_ref[pl.ds(e, 1), :]

        def body(i, _, base, n, n_zero):
            r0 = pl.multiple_of(base + i * sub, sub)
            want = (r0 + lax.broadcasted_iota(jnp.int32, (n, 1), 0)).astype(F32)
            onehot = jnp.where(slot_row == want, 1.0, 0.0).astype(BF16)
            xs_ref[pl.ds(r0, n), :] = _dot(onehot, x_ref[...]).astype(BF16)
            y_ref[pl.ds(r0, n_zero), :] = jnp.zeros((n_zero, d), F32)
            return 0

        lax.fori_loop(0, n_plain, functools.partial(body, base=0, n=sub, n_zero=sub), 0)
        lax.fori_loop(0, merged_once, functools.partial(body, base=merged_base, n=long, n_zero=long_pad), 0)

    def expert(i, _, base, n, align):
        r0 = pl.multiple_of(base + i * n, align)
        y_ref[pl.ds(r0, n), :] += _swiglu_chunk(xs_ref[pl.ds(r0, n), :], wg_ref[0], wu_ref[0], wd_ref[0])
        return 0

    n_whole = jnp.where(rest > tail, n_filled + 1, jnp.where(merged, n_filled - 1, n_filled))
    lax.fori_loop(0, n_whole, functools.partial(expert, base=0, n=sub, align=sub), 0)
    lax.fori_loop(0, merged_once, functools.partial(expert, base=merged_base, n=long, align=sub), 0)
    lax.fori_loop(0, (short & (n_filled == 0)).astype(jnp.int32),
                  functools.partial(expert, base=0, n=tail, align=tail), 0)

    @pl.when(f == pl.num_programs(2) - 1)
    def _scatter():
        tc = MOE_SCATTER_ROWS

        def body(i, _, base, n):
            r0 = pl.multiple_of(base + i * sub, sub)
            y = y_ref[pl.ds(r0, n), :].astype(BF16)
            want = (r0 + lax.broadcasted_iota(jnp.int32, (1, n), 1)).astype(F32)
            for j in range(tm // tc):
                rows = slice(j * tc, (j + 1) * tc)
                slot = jnp.concatenate([slot_e_ref[rows, :]] * (n // V7X_LANES), axis=1)
                onehot = jnp.where(slot == want, 1.0, 0.0).astype(BF16)
                weight = jnp.concatenate([gate_e_ref[rows, :]] * (d // V7X_LANES), axis=1)
                o_ref[rows, :] += weight * _dot(onehot, y)
            return 0

        lax.fori_loop(0, n_plain, functools.partial(body, base=0, n=sub), 0)
        lax.fori_loop(0, merged_once, functools.partial(body, base=merged_base, n=long_pad), 0)


def _moe_ffn(hn, router, wg, wu, wd):
    t, d = hn.shape
    tm = min(MOE_ROW_TILE, t)
    tf = MOE_COL_TILE
    nf = D_FF // tf
    router = jnp.pad(router, ((0, 0), (0, V7X_LANES - N_EXPERTS))).astype(BF16)
    once = pl.Buffered(1)
    return pl.pallas_call(
        _moe_kernel,
        grid=(t // tm, N_EXPERTS, nf),
        in_specs=[pl.BlockSpec((tm, d), lambda i, e, f: (i, 0), pipeline_mode=once),
                  pl.BlockSpec((d, V7X_LANES), lambda i, e, f: (0, 0), pipeline_mode=once),
                  pl.BlockSpec((1, d, tf), lambda i, e, f: (e, 0, f)),
                  pl.BlockSpec((1, d, tf), lambda i, e, f: (e, 0, f)),
                  pl.BlockSpec((1, tf, d), lambda i, e, f: (e, f, 0))],
        out_specs=pl.BlockSpec((tm, d), lambda i, e, f: (i, 0), pipeline_mode=once),
        out_shape=jax.ShapeDtypeStruct((t, d), F32),
        scratch_shapes=[pltpu.VMEM((tm, d), BF16), pltpu.VMEM((tm, d), F32),
                        pltpu.VMEM((tm, V7X_LANES), F32), pltpu.VMEM((tm, V7X_LANES), F32),
                        pltpu.VMEM((N_EXPERTS, tm), F32),
                        pltpu.VMEM((tm, V7X_LANES), F32), pltpu.VMEM((tm, V7X_LANES), F32),
                        pltpu.SMEM((N_EXPERTS,), jnp.int32)],
        compiler_params=_params(("parallel", "arbitrary", "arbitrary"), vmem=MOE_VMEM_LIMIT),
        name="moe_ffn",
    )(hn, router, wg.astype(BF16), wu.astype(BF16), wd.astype(BF16))


def _ple_kernel(h_ref, f_ref, p_ref, g_ref, proj_ref, gate_ref, gf_ref, o_ref, *, final_norm):
    h = _ple_rows(h_ref[...] + f_ref[...], p_ref[...], g_ref[...], proj_ref, gate_ref)
    if final_norm:
        h = _rms(h, gf_ref[...])
    o_ref[...] = h


def _ple(h, f, p, g, proj, gate, g_final, final_norm):
    t, d = h.shape
    tm = min(STREAM_ROW_TILE, t)
    vec = pl.BlockSpec((1, d), lambda i: (0, 0))
    return pl.pallas_call(
        functools.partial(_ple_kernel, final_norm=final_norm),
        grid=(t // tm,),
        in_specs=[pl.BlockSpec((tm, d), lambda i: (i, 0)),
                  pl.BlockSpec((tm, d), lambda i: (i, 0)),
                  pl.BlockSpec((tm, PLE_DIM), lambda i: (i, 0)),
                  vec,
                  pl.BlockSpec((PLE_DIM, d), lambda i: (0, 0)),
                  pl.BlockSpec((d, d), lambda i: (0, 0)),
                  vec],
        out_specs=pl.BlockSpec((tm, d), lambda i: (i, 0)),
        out_shape=jax.ShapeDtypeStruct((t, d), F32),
        compiler_params=_params(("parallel",)),
        name="ple",
    )(h, f, p, g.reshape(1, d), proj.astype(BF16), gate.astype(BF16), g_final.reshape(1, d))


def kernel(x, p, w_in, w_out, g_mix, g_ffn, g_ple, g_final, cmp_pos, cmp_w1, cmp_w2, ret_gn,
           ffn_gate, ffn_up, ffn_down, moe_router, moe_gate, moe_up, moe_down, ple_proj, ple_gate):
    b, s, d = x.shape
    depth = w_in.shape[0]
    t = b * s
    h = x
    for i in range(depth):
        qn, kcv, ks, kv3, gl, ret = _in_proj(h.reshape(b, s, d), g_mix[i], _arrange_w_in(w_in[i]))
        kv_cmp = _compress(kcv, cmp_pos[i], cmp_w1[i], cmp_w2[i])
        a = _nsa_attention(qn, kv_cmp, ks, kv3, gl)
        r = _retention(ret, ret_gn[i])
        h, a, r, p_i = h.reshape(t, d), a.reshape(t, NSA_Q_COLS), r.reshape(t, RET_COLS), p[i].reshape(t, PLE_DIM)
        last = i == depth - 1
        if i % 2 == 0:
            h = _dense_layer(h, a, r, p_i, w_out[i], g_ffn[i], ffn_gate[i // 2], ffn_up[i // 2], ffn_down[i // 2],
                             g_ple[i], ple_proj[i], ple_gate[i], g_final, last)
        else:
            h, hn = _out_proj(h, a, r, w_out[i], g_ffn[i])
            f = _moe_ffn(hn, moe_router[i // 2], moe_gate[i // 2], moe_up[i // 2], moe_down[i // 2])
            h = _ple(h, f, p_i, g_ple[i], ple_proj[i], ple_gate[i], g_final, last)
    return h.reshape(b, s, d)
```

```python
import functools
import math

import jax
import jax.numpy as jnp
from jax import lax
from jax.experimental import pallas as pl
from jax.experimental.pallas import tpu as pltpu

F32 = jnp.float32
BF16 = jnp.bfloat16

D_MODEL = 1024
HEAD_DIM = 64
NSA_HEADS = 8
NSA_KV_HEADS = 2
NSA_GROUP = NSA_HEADS // NSA_KV_HEADS
RET_HEADS = 8
CMP_BLOCK = 32
CMP_STRIDE = 16
CMP_HIDDEN = 256
SEL_BLOCK = 64
N_SELECT = 16
N_LOCAL_SEL = 2
WINDOW = 512
RET_CHUNK = 128
D_FF = 3584
N_EXPERTS = 8
PLE_DIM = 256
EPS = 1e-6
NEG_INF = -1e30
BIG = 1e9

NSA_Q_COLS = NSA_HEADS * HEAD_DIM
NSA_KV_COLS = NSA_KV_HEADS * HEAD_DIM
NSA_GATE_COLS = 3 * NSA_HEADS
RET_COLS = RET_HEADS * HEAD_DIM
MIX_WIDTH = NSA_Q_COLS + RET_COLS

V7X_LANES = 128
V7X_SUBLANES = 8
V7X_VMEM_BYTES = 64 * 1024 * 1024
VMEM_LIMIT = V7X_VMEM_BYTES * 3 // 4

ROW_TILE = 512
STREAM_ROW_TILE = 1024
ATTN_Q_TILE = 256
SEL_KV_TILE = 256
SEL_STEP_WIDTHS = (1, 2, 4, 8)
RET_BATCH = 4
RET_HEAD_GROUP = 4
FFN_ROW_TILE = 512
FFN_COL_TILE = 512
MOE_COL_TILE = 1792
MOE_ROW_TILE = 2048
MOE_GROUP_ROWS = 256
MOE_TAIL_ROWS = 64
MOE_RANK_CHUNK = 256
MOE_SCATTER_ROWS = 512
MOE_VMEM_LIMIT = V7X_VMEM_BYTES * 7 // 8

POS_SHIFT = 6
POS_SPLIT = 1 << POS_SHIFT
SEL_SHIFT = SEL_BLOCK.bit_length() - 1
MAX_SEL_BLOCKS = V7X_LANES


def _params(sem, vmem=VMEM_LIMIT):
    return pltpu.CompilerParams(dimension_semantics=sem, vmem_limit_bytes=vmem)


def _dot(a, b):
    return jnp.dot(a, b, preferred_element_type=F32)


def _dot_nt(a, b):
    return lax.dot_general(a, b, (((1,), (1,)), ((), ())), preferred_element_type=F32)


def _dot_tn(a, b):
    return lax.dot_general(a, b, (((0,), (0,)), ((), ())), preferred_element_type=F32)


def _rms(x, g):
    return x * lax.rsqrt(jnp.mean(x * x, axis=-1, keepdims=True) + EPS) * g


def _lane_features(shape, first, second):
    lane = lax.broadcasted_iota(jnp.int32, shape, 1)
    return jnp.where(lane == 0, first, jnp.where(lane == 1, second, 0))


def _key_features(pos, width):
    return _lane_features((pos.shape[0], width), pos >> POS_SHIFT, pos & (POS_SPLIT - 1))


def _ones_features(n, width):
    return _lane_features((n, width), 1, 0)


def _int_to_bf16(x):
    return x.astype(F32).astype(BF16)


def _in_proj_kernel(h_ref, g_ref, w_ref, qn_ref, kcv_ref, ks_ref, kv3_ref, gl_ref, ret_ref):
    tm = h_ref.shape[1]
    xn = _rms(h_ref[0], g_ref[...]).astype(BF16)
    pair = 2 * HEAD_DIM
    pos = pl.program_id(1) * tm + lax.broadcasted_iota(jnp.int32, (tm, 1), 0)
    kfeat = _int_to_bf16(_key_features(pos, HEAD_DIM))
    vfeat = _int_to_bf16(_ones_features(tm, HEAD_DIM))
    lane = lax.broadcasted_iota(jnp.int32, (tm, MAX_SEL_BLOCKS), 1)
    block_onehot = jnp.where(lane == (pos >> SEL_SHIFT), 1.0, 0.0).astype(BF16)

    wide = 2 * pair
    per_dot = wide // HEAD_DIM
    for j in range(NSA_Q_COLS // wide):
        z = (_dot(xn, w_ref[:, j * wide:(j + 1) * wide]) * HEAD_DIM ** -0.5).astype(BF16)
        for k in range(per_dot):
            head = per_dot * j + k
            slope = 2.0 ** -(head + 1)
            qfeat = _lane_features((tm, HEAD_DIM), POS_SPLIT * slope, slope).astype(BF16)
            qn_ref[0, head] = jnp.concatenate([z[:, k * HEAD_DIM:(k + 1) * HEAD_DIM], qfeat], axis=1)
    base = NSA_Q_COLS
    z = _dot(xn, w_ref[:, base:base + wide]).astype(BF16)
    for a in range(2):
        kcv_ref[a, 0] = z[:, a * pair:(a + 1) * pair]
    base += wide
    for c in range(2):
        z = _dot(xn, w_ref[:, base + c * wide:base + (c + 1) * wide]).astype(BF16)
        for half in range(2):
            a = 2 * c + half
            for g in range(NSA_KV_HEADS):
                zg = z[:, half * pair + g * HEAD_DIM:half * pair + (g + 1) * HEAD_DIM]
                if a == 0:
                    ks_ref[0, g] = jnp.concatenate([zg, kfeat, block_onehot], axis=1)
                else:
                    kv3_ref[a - 1, 0, g] = jnp.concatenate([zg, kfeat if a == 2 else vfeat], axis=1)
    base += 2 * wide
    for a in range(4):
        ret_ref[0, :, a * RET_COLS:(a + 1) * RET_COLS] = _dot(
            xn, w_ref[:, base + a * RET_COLS:base + (a + 1) * RET_COLS]).astype(BF16)
    base += 4 * RET_COLS
    gl_ref[0] = _dot(xn, w_ref[:, base:base + NSA_KV_HEADS * V7X_LANES])


def _arrange_w_in(w):
    q_end = NSA_Q_COLS
    kv_end = q_end + 6 * NSA_KV_COLS
    gl_end = kv_end + NSA_GATE_COLS
    gl = w[:, kv_end:gl_end].reshape(D_MODEL, 3, NSA_KV_HEADS, NSA_GROUP)
    gl = jnp.transpose(gl, (0, 2, 1, 3)).reshape(D_MODEL, NSA_KV_HEADS, 3 * NSA_GROUP)
    gl = jnp.pad(gl, ((0, 0), (0, 0), (0, V7X_LANES - 3 * NSA_GROUP)))
    gl = gl.reshape(D_MODEL, NSA_KV_HEADS * V7X_LANES)
    return jnp.concatenate([w[:, :kv_end], w[:, gl_end:], gl], axis=1).astype(BF16)


def _in_proj(h, g, w):
    b, s, d = h.shape
    assert s // SEL_BLOCK <= MAX_SEL_BLOCKS
    tm = min(ROW_TILE, s)
    ncols = w.shape[1]
    aug = 2 * HEAD_DIM
    return pl.pallas_call(
        _in_proj_kernel,
        grid=(b, s // tm),
        in_specs=[
            pl.BlockSpec((1, tm, d), lambda bi, si: (bi, si, 0)),
            pl.BlockSpec((1, d), lambda bi, si: (0, 0)),
            pl.BlockSpec((d, ncols), lambda bi, si: (0, 0)),
        ],
        out_specs=[
            pl.BlockSpec((1, NSA_HEADS, tm, aug), lambda bi, si: (bi, 0, si, 0)),
            pl.BlockSpec((2, 1, tm, NSA_KV_COLS), lambda bi, si: (0, bi, si, 0)),
            pl.BlockSpec((1, NSA_KV_HEADS, tm, aug + MAX_SEL_BLOCKS), lambda bi, si: (bi, 0, si, 0)),
            pl.BlockSpec((3, 1, NSA_KV_HEADS, tm, aug), lambda bi, si: (0, bi, 0, si, 0)),
            pl.BlockSpec((1, tm, NSA_KV_HEADS * V7X_LANES), lambda bi, si: (bi, si, 0)),
            pl.BlockSpec((1, tm, 4 * RET_COLS), lambda bi, si: (bi, si, 0)),
        ],
        out_shape=[
            jax.ShapeDtypeStruct((b, NSA_HEADS, s, aug), BF16),
            jax.ShapeDtypeStruct((2, b, s, NSA_KV_COLS), BF16),
            jax.ShapeDtypeStruct((b, NSA_KV_HEADS, s, aug + MAX_SEL_BLOCKS), BF16),
            jax.ShapeDtypeStruct((3, b, NSA_KV_HEADS, s, aug), BF16),
            jax.ShapeDtypeStruct((b, s, NSA_KV_HEADS * V7X_LANES), F32),
            jax.ShapeDtypeStruct((b, s, 4 * RET_COLS), BF16),
        ],
        compiler_params=_params(("parallel", "parallel")),
        name="in_proj",
    )(h, g.reshape(1, d), w)


def _compress_kernel(x_ref, wbig_ref, pos_ref, w1_ref, w2_ref, o_ref):
    nch = x_ref.shape[2]
    u = _dot(x_ref[0, 0], wbig_ref[0])
    hid0 = _dot(pos_ref[0], w1_ref[0])[0:1]
    w2 = w2_ref[0]
    cmp_end = lax.broadcasted_iota(jnp.int32, (nch, 1), 0) * CMP_STRIDE + (CMP_BLOCK - 1)
    feat = _int_to_bf16(jnp.where(pl.program_id(0) == 0, _key_features(cmp_end, HEAD_DIM),
                                  _ones_features(nch, HEAD_DIM)))
    for g in range(NSA_KV_HEADS):
        c0 = g * 2 * CMP_HIDDEN
        first = u[:, c0:c0 + CMP_HIDDEN]
        second = u[:, c0 + CMP_HIDDEN:c0 + 2 * CMP_HIDDEN]
        hid = first + pltpu.roll(second, nch - 1, 0) + hid0
        out = _dot(jax.nn.gelu(hid).astype(BF16), w2).astype(BF16)
        o_ref[0, 0, g] = jnp.concatenate([out, feat], axis=1)


def _arrange_cmp_w1(w1):
    r = CMP_BLOCK // CMP_STRIDE
    w1r = w1.reshape(2, r, CMP_STRIDE, HEAD_DIM, CMP_HIDDEN)
    eye = jnp.eye(NSA_KV_HEADS, dtype=w1.dtype)
    big = jnp.einsum('krcdh,gf->kcgdfrh', w1r, eye)
    return big.reshape(2, CMP_STRIDE * NSA_KV_COLS, NSA_KV_HEADS * r * CMP_HIDDEN).astype(BF16)


def _compress(kcv, cmp_pos, cmp_w1, cmp_w2):
    _, b, s, _ = kcv.shape
    nch = s // CMP_STRIDE
    x = kcv.reshape(2, b, nch, CMP_STRIDE * NSA_KV_COLS)
    wbig = _arrange_cmp_w1(cmp_w1)
    pos = jnp.broadcast_to(cmp_pos.reshape(2, 1, CMP_BLOCK * HEAD_DIM),
                           (2, V7X_SUBLANES, CMP_BLOCK * HEAD_DIM)).astype(BF16)
    w1 = cmp_w1.reshape(2, CMP_BLOCK * HEAD_DIM, CMP_HIDDEN).astype(BF16)
    w2 = cmp_w2.astype(BF16)
    kdim = CMP_STRIDE * NSA_KV_COLS
    return pl.pallas_call(
        _compress_kernel,
        grid=(2, b),
        in_specs=[
            pl.BlockSpec((1, 1, nch, kdim), lambda a, bi: (a, bi, 0, 0)),
            pl.BlockSpec((1, kdim, wbig.shape[2]), lambda a, bi: (a, 0, 0)),
            pl.BlockSpec((1, V7X_SUBLANES, CMP_BLOCK * HEAD_DIM), lambda a, bi: (a, 0, 0)),
            pl.BlockSpec((1, CMP_BLOCK * HEAD_DIM, CMP_HIDDEN), lambda a, bi: (a, 0, 0)),
            pl.BlockSpec((1, CMP_HIDDEN, HEAD_DIM), lambda a, bi: (a, 0, 0)),
        ],
        out_specs=pl.BlockSpec((1, 1, NSA_KV_HEADS, nch, 2 * HEAD_DIM), lambda a, bi: (a, bi, 0, 0, 0)),
        out_shape=jax.ShapeDtypeStruct((2, b, NSA_KV_HEADS, nch, 2 * HEAD_DIM), BF16),
        compiler_params=_params(("parallel", "parallel")),
        name="compress",
    )(x, wbig, pos, w1, w2)


def _normalise_t(acc):
    return acc[:HEAD_DIM] * (1.0 / jnp.maximum(acc[HEAD_DIM:HEAD_DIM + 1], 1e-30))


def _attn_kernel(q_ref, kc_ref, vc_ref, ks_ref, vs_ref, kw_ref, vw_ref, gl_ref, ovt_ref, o_ref, tiles_ref,
                 sel_ref, oc_ref, imp_ref, *, n_pick, tk):
    qt = q_ref.shape[2]
    cols = NSA_GROUP * qt
    t0 = pl.program_id(2) * qt
    q_t = q_ref[0].reshape(cols, 2 * HEAD_DIM).T
    tq = t0 + lax.broadcasted_iota(jnp.int32, (1, qt), 1)

    def per_head(valid, masked):
        bias = jnp.where(valid, 0.0, masked)
        return jnp.concatenate([bias] * NSA_GROUP, axis=1)

    nc = kc_ref.shape[3]

    def compressed(rows):
        cmp_end = lax.broadcasted_iota(jnp.int32, (rows, 1), 0) * CMP_STRIDE + (CMP_BLOCK - 1)
        logit = _dot(kc_ref[0, 0, 0, :rows], q_t) + per_head(cmp_end <= tq, -jnp.inf)
        e = jnp.exp(logit - jnp.maximum(jnp.max(logit, axis=0, keepdims=True), NEG_INF))
        p = e * (1.0 / jnp.maximum(jnp.sum(e, axis=0, keepdims=True), 1e-30))
        oc_ref[...] = _dot(vc_ref[0, 0, 0, :rows].T, p.astype(BF16))[:HEAD_DIM]
        p_sum = p[:, 0:qt]
        for r in range(1, NSA_GROUP):
            p_sum = p_sum + p[:, r * qt:(r + 1) * qt]
        p_hi = p_sum.astype(BF16)
        p_lo = (p_sum - p_hi.astype(F32)).astype(BF16)
        imp_ref[...] = _dot(ovt_ref[:, :rows], p_hi) + _dot(ovt_ref[:, :rows], p_lo)

    n_cmp_live = (t0 + qt - CMP_BLOCK + CMP_STRIDE) // CMP_STRIDE
    cmp_sizes = [nc // 4, nc // 2, nc]
    for lower, rows in zip([None] + cmp_sizes[:-1], cmp_sizes):
        above = True if lower is None else n_cmp_live > lower
        below = True if rows == nc else n_cmp_live <= rows
        pl.when(jnp.logical_and(above, below))(functools.partial(compressed, rows))
    o_c = oc_ref[...]
    imp = imp_ref[...]

    shape = (MAX_SEL_BLOCKS, qt)
    blk = lax.broadcasted_iota(jnp.int32, shape, 0)
    back = ((t0 + lax.broadcasted_iota(jnp.int32, shape, 1)) >> SEL_SHIFT) - blk
    forced = (blk == 0) | ((back >= 0) & (back < N_LOCAL_SEL))
    score = jnp.where(forced, BIG, jnp.where(back >= 0, imp, -BIG))

    def pick(_, work):
        row = lax.broadcasted_iota(jnp.int32, work.shape, 0)
        m = jnp.max(work, axis=0, keepdims=True)
        first = jnp.min(jnp.where(work == m, row, MAX_SEL_BLOCKS), axis=0, keepdims=True)
        return jnp.where(row == first, -jnp.inf, work)

    n_live = (t0 + qt + SEL_BLOCK - 1) >> SEL_SHIFT
    sizes = [MAX_SEL_BLOCKS // 4, MAX_SEL_BLOCKS // 2, MAX_SEL_BLOCKS]
    for lower, rows in zip([0] + sizes[:-1], sizes):
        @pl.when((n_live > lower) & ((n_live <= rows) | (rows == MAX_SEL_BLOCKS)))
        def _(rows=rows):
            left = lax.fori_loop(0, n_pick, pick, score[:rows])
            sel_ref[...] = jnp.zeros(shape, F32)
            sel_ref[:rows] = jnp.where(left == -jnp.inf, 1.0, 0.0)
    sel = sel_ref[...]
    sel_bias = ((sel - 1.0) * -NEG_INF).astype(BF16)
    q_sel = jnp.concatenate([q_t, jnp.concatenate([sel_bias] * NSA_GROUP, axis=1)], axis=0)

    n_own = max(1, qt // tk)
    n_full = t0 // tk
    blocks_per_tile = tk // SEL_BLOCK
    block_used = jnp.max(sel, axis=1, keepdims=True)
    n_visit = jnp.int32(0)
    for j in range(tiles_ref.shape[0]):
        used = jnp.max(block_used[j * blocks_per_tile:(j + 1) * blocks_per_tile]) > 0.0
        tiles_ref[n_visit] = j
        n_visit = n_visit + (used & (j < n_full)).astype(jnp.int32)

    def sweep(tiles, carry, causal):
        m_i, acc = carry
        starts = [pl.multiple_of(j * tk, tk) for j in tiles]
        keys = jnp.concatenate([ks_ref[0, 0, pl.ds(k0, tk), :] for k0 in starts], axis=0)
        s = _dot(keys, q_sel)
        if causal:
            kpos = jnp.concatenate([k0 + lax.broadcasted_iota(jnp.int32, (tk, 1), 0) for k0 in starts], axis=0)
            s = s + per_head(kpos <= tq, NEG_INF)
        m_new = jnp.maximum(m_i, jnp.max(s, axis=0, keepdims=True))
        values_t = jnp.concatenate([vs_ref[0, 0, 0, pl.ds(k0, tk), :].T for k0 in starts], axis=1)
        acc = jnp.exp(m_i - m_new) * acc + _dot(values_t, jnp.exp(s - m_new).astype(BF16))
        return m_new, acc

    carry = (jnp.full((1, cols), NEG_INF, F32), jnp.zeros((2 * HEAD_DIM, cols), F32))
    done = jnp.int32(0)
    for width in SEL_STEP_WIDTHS:
        steps = n_visit // width if width == SEL_STEP_WIDTHS[-1] else (n_visit // width) & 1
        carry = lax.fori_loop(
            0, steps,
            lambda i, c, width=width, done=done: sweep(
                [tiles_ref[done + width * i + u] for u in range(width)], c, causal=False),
            carry)
        done = done + steps * width
    carry = sweep([n_full + j for j in range(n_own)], carry, causal=True)
    o_s = _normalise_t(carry[1])

    band = WINDOW + qt
    w0 = pl.multiple_of(jnp.maximum(t0 - WINDOW, 0), math.gcd(qt, WINDOW))
    kpos = w0 + lax.broadcasted_iota(jnp.int32, (band, 1), 0)
    valid_w = (kpos <= tq) & (kpos > tq - WINDOW)
    logit_w = _dot(kw_ref[0, 0, 0, pl.ds(w0, band), :], q_t) + per_head(valid_w, NEG_INF)
    e_w = jnp.exp(logit_w - jnp.max(logit_w, axis=0, keepdims=True))
    o_w = _normalise_t(_dot(vw_ref[0, 0, 0, pl.ds(w0, band), :].T, e_w.astype(BF16)))

    gate_t = jax.nn.sigmoid(gl_ref[0]).T

    def branch_gate(branch):
        first = branch * NSA_GROUP
        return jnp.concatenate([gate_t[first + r:first + r + 1] for r in range(NSA_GROUP)], axis=1)

    o_t = branch_gate(0) * o_c + branch_gate(1) * o_s + branch_gate(2) * o_w
    o = o_t.T
    o_ref[0] = jnp.concatenate([o[r * qt:(r + 1) * qt] for r in range(NSA_GROUP)], axis=1).astype(o_ref.dtype)


def _nsa_attention(qn, kv_cmp, ks, kv3, gl):
    b, _, s, _ = qn.shape
    nc = kv_cmp.shape[3]
    nsel = s // SEL_BLOCK
    n_pick = min(N_SELECT, nsel)
    tk = min(SEL_KV_TILE, s)
    aug = 2 * HEAD_DIM
    qt = min(ATTN_Q_TILE, s)
    assert qt % tk == 0 or tk % qt == 0
    cmp_start = jnp.arange(nc) * CMP_STRIDE
    sel_start = jnp.arange(nsel) * SEL_BLOCK
    overlap = (jnp.minimum(cmp_start[:, None] + CMP_BLOCK, sel_start[None, :] + SEL_BLOCK)
               > jnp.maximum(cmp_start[:, None], sel_start[None, :]))
    overlap_t = jnp.pad(overlap.T.astype(BF16), ((0, MAX_SEL_BLOCKS - nsel), (0, 0)))
    kv_spec = lambda a: pl.BlockSpec((1, 1, 1, s, aug), lambda bi, g, i, a=a: (a, bi, g, 0, 0))
    cmp_spec = lambda a: pl.BlockSpec((1, 1, 1, nc, aug), lambda bi, g, i, a=a: (a, bi, g, 0, 0))
    return pl.pallas_call(
        functools.partial(_attn_kernel, n_pick=n_pick, tk=tk),
        grid=(b, NSA_KV_HEADS, s // qt),
        in_specs=[
            pl.BlockSpec((1, NSA_GROUP, qt, aug), lambda bi, g, i: (bi, g, i, 0)),
            cmp_spec(0), cmp_spec(1),
            pl.BlockSpec((1, 1, s, aug + MAX_SEL_BLOCKS), lambda bi, g, i: (bi, g, 0, 0)),
            kv_spec(0), kv_spec(1), kv_spec(2),
            pl.BlockSpec((1, qt, V7X_LANES), lambda bi, g, i: (bi, i, g)),
            pl.BlockSpec((MAX_SEL_BLOCKS, nc), lambda bi, g, i: (0, 0)),
        ],
        out_specs=pl.BlockSpec((1, qt, NSA_GROUP * HEAD_DIM), lambda bi, g, i: (bi, i, g)),
        out_shape=jax.ShapeDtypeStruct((b, s, NSA_Q_COLS), BF16),
        scratch_shapes=[pltpu.SMEM((s // tk,), jnp.int32), pltpu.VMEM((MAX_SEL_BLOCKS, qt), F32),
                        pltpu.VMEM((HEAD_DIM, NSA_GROUP * qt), F32), pltpu.VMEM((MAX_SEL_BLOCKS, qt), F32)],
        compiler_params=_params(("parallel", "parallel", "arbitrary")),
        name="nsa_attention",
    )(qn, kv_cmp, kv_cmp, ks, kv3, kv3, kv3, gl, overlap_t)


def _retention_kernel(q_ref, k_ref, v_ref, g_ref, decay_ref, xi_ref, zeta_ref, gch_ref, same_ref, gn_ref,
                      o_ref, state_ref):
    @pl.when(pl.program_id(1) == 0)
    def _():
        state_ref[...] = jnp.zeros_like(state_ref)

    width = same_ref.shape[0]
    heads = width // HEAD_DIM
    c = q_ref.shape[1]
    scale = jnp.asarray(HEAD_DIM ** -0.5, BF16)
    lane_head = lax.broadcasted_iota(jnp.int32, (c, width), 1) // HEAD_DIM
    same = same_ref[...]
    mean_w = (same * (1.0 / HEAD_DIM)).astype(BF16)

    def head_mean(x):
        hi = x.astype(BF16)
        lo = (x - hi.astype(F32)).astype(BF16)
        return _dot(hi, mean_w) + _dot(lo, mean_w)

    def stack(x):
        zero = jnp.zeros_like(x)
        return jnp.concatenate([jnp.where(lane_head == j, x, zero) for j in range(heads)], axis=0)

    for i in range(q_ref.shape[0]):
        outs = []
        for p in range(RET_COLS // width):
            cols = slice(p * width, (p + 1) * width)
            q = q_ref[i, :, cols]
            k = k_ref[i, :, cols] * scale
            v = v_ref[i, :, cols]
            state = state_ref[i, p]
            inner = _dot_nt(q, stack(k)) * decay_ref[p]
            o = _dot(inner.astype(BF16), stack(v)) + _dot(q, state.astype(BF16)) * xi_ref[p]
            kz = (k.astype(F32) * zeta_ref[p]).astype(BF16)
            state_ref[i, p] = gch_ref[p] * state + same * _dot_tn(kz, v)
            centred = o - head_mean(o)
            outs.append(centred * lax.rsqrt(head_mean(jnp.square(centred)) + EPS))
        o = jnp.concatenate(outs, axis=1) * gn_ref[...]
        o_ref[i] = (jax.nn.silu(g_ref[i].astype(F32)) * o).astype(o_ref.dtype)


def _retention(ret, gn_gain):
    b, s, _ = ret.shape
    c = RET_CHUNK
    hh = RET_HEADS
    hg = RET_HEAD_GROUP
    width = hg * HEAD_DIM
    log_gamma = jnp.log1p(-jnp.exp2(-5.0 - jnp.arange(hh, dtype=F32)))
    pos = jnp.arange(c, dtype=F32)
    diff = pos[:, None] - pos[None, :]
    decay = jnp.where(diff >= 0, jnp.exp(jnp.maximum(diff, 0.0)[None] * log_gamma[:, None, None]), 0.0)
    xi = jnp.exp((pos + 1.0)[None] * log_gamma[:, None])[..., None]
    zeta = jnp.exp((c - 1.0 - pos)[None] * log_gamma[:, None])[..., None]
    g_chunk = jnp.exp(c * log_gamma)[:, None, None]
    side_by_side = lambda t: jnp.concatenate([t[j::hg] for j in range(hg)], axis=-1)
    decay = side_by_side(decay)
    xi = side_by_side(jnp.broadcast_to(xi, (hh, c, HEAD_DIM)))
    zeta = side_by_side(jnp.broadcast_to(zeta, (hh, c, HEAD_DIM)))
    head_of = jnp.arange(width) // HEAD_DIM
    same = (head_of[:, None] == head_of[None, :]).astype(F32)
    g_lane = side_by_side(jnp.broadcast_to(g_chunk, (hh, 1, HEAD_DIM)))
    g_chunk = same[None] * g_lane
    nb = math.gcd(RET_BATCH, b)
    part = lambda a: pl.BlockSpec((nb, c, RET_COLS), lambda bi, n, a=a: (bi, n, a))
    full = lambda shape: pl.BlockSpec(shape, lambda bi, n: (0,) * len(shape))
    return pl.pallas_call(
        _retention_kernel,
        grid=(b // nb, s // c),
        in_specs=[part(0), part(1), part(2), part(3),
                  full((hh // hg, c, hg * c)), full((hh // hg, c, width)), full((hh // hg, c, width)),
                  full((hh // hg, width, width)), full((width, width)), full((1, RET_COLS))],
        out_specs=pl.BlockSpec((nb, c, RET_COLS), lambda bi, n: (bi, n, 0)),
        out_shape=jax.ShapeDtypeStruct((b, s, RET_COLS), BF16),
        scratch_shapes=[pltpu.VMEM((nb, hh // hg, width, width), F32)],
        compiler_params=_params(("parallel", "arbitrary")),
        name="retention",
    )(ret, ret, ret, ret, decay, xi, zeta, g_chunk, same, gn_gain.reshape(1, RET_COLS).astype(F32))


def _out_proj_rows(h, a, r, w_ref):
    return h + _dot(a, w_ref[:NSA_Q_COLS]) + _dot(r, w_ref[NSA_Q_COLS:])


def _out_proj_kernel(h_ref, a_ref, r_ref, w_ref, g_ref, h_out_ref, hn_ref):
    h = _out_proj_rows(h_ref[...], a_ref[...], r_ref[...], w_ref)
    h_out_ref[...] = h
    hn_ref[...] = _rms(h, g_ref[...]).astype(BF16)


def _out_proj(h, a, r, w, g):
    t, d = h.shape
    tm = min(STREAM_ROW_TILE, t)
    row = lambda n: pl.BlockSpec((tm, n), lambda i: (i, 0))
    return pl.pallas_call(
        _out_proj_kernel,
        grid=(t // tm,),
        in_specs=[row(d), row(NSA_Q_COLS), row(RET_COLS),
                  pl.BlockSpec((MIX_WIDTH, d), lambda i: (0, 0)),
                  pl.BlockSpec((1, d), lambda i: (0, 0))],
        out_specs=[row(d), row(d)],
        out_shape=[jax.ShapeDtypeStruct((t, d), F32), jax.ShapeDtypeStruct((t, d), BF16)],
        compiler_params=_params(("parallel",)),
        name="out_proj",
    )(h, a, r, w.astype(BF16), g.reshape(1, d))


def _swiglu_chunk(x, wg, wu, wd):
    hid = jax.nn.silu(_dot(x, wg)) * _dot(x, wu)
    return _dot(hid.astype(BF16), wd)


def _ple_rows(h, p, g, proj_ref, gate_ref):
    emb = _dot(p.astype(BF16), proj_ref[...])
    return h + emb * jax.nn.sigmoid(_dot(_rms(h, g).astype(BF16), gate_ref[...]))


def _dense_layer_kernel(h_ref, a_ref, r_ref, p_ref, wo_ref, gf_ref, wg_ref, wu_ref, wd_ref, gp_ref, proj_ref, gate_ref,
                        gl_ref, o_ref, *, final_norm):
    h = _out_proj_rows(h_ref[...], a_ref[...], r_ref[...], wo_ref)
    x = _rms(h, gf_ref[...]).astype(BF16)
    acc = None
    for f in range(D_FF // FFN_COL_TILE):
        cols = slice(f * FFN_COL_TILE, (f + 1) * FFN_COL_TILE)
        part = _swiglu_chunk(x, wg_ref[:, cols], wu_ref[:, cols], wd_ref[cols, :])
        acc = part if acc is None else acc + part
    h = _ple_rows(h + acc, p_ref[...], gp_ref[...], proj_ref, gate_ref)
    if final_norm:
        h = _rms(h, gl_ref[...])
    o_ref[...] = h


def _dense_layer(h, a, r, p, w_out, g_ffn, wg, wu, wd, g_ple, proj, gate, g_final, final_norm):
    t, d = h.shape
    tm = min(FFN_ROW_TILE, t)
    row = lambda n: pl.BlockSpec((tm, n), lambda i: (i, 0))
    held = lambda shape: pl.BlockSpec(shape, lambda i: (0, 0), pipeline_mode=pl.Buffered(1))
    vec = held((1, d))
    return pl.pallas_call(
        functools.partial(_dense_layer_kernel, final_norm=final_norm),
        grid=(t // tm,),
        in_specs=[row(d), row(NSA_Q_COLS), row(RET_COLS), row(PLE_DIM),
                  held((MIX_WIDTH, d)), vec, held((d, D_FF)), held((d, D_FF)), held((D_FF, d)),
                  vec, held((PLE_DIM, d)), held((d, d)), vec],
        out_specs=row(d),
        out_shape=jax.ShapeDtypeStruct((t, d), F32),
        compiler_params=_params(("parallel",)),
        name="dense_layer",
    )(h, a, r, p, w_out.astype(BF16), g_ffn.reshape(1, d), wg.astype(BF16), wu.astype(BF16), wd.astype(BF16),
      g_ple.reshape(1, d), proj.astype(BF16), gate.astype(BF16), g_final.reshape(1, d))


def _lane_column(table, lane_index):
    lane = lax.broadcasted_iota(jnp.int32, table.shape, 1)
    col = jnp.sum(jnp.where(lane == lane_index, table, 0.0), axis=-1, keepdims=True)
    return jnp.broadcast_to(col, table.shape)


def _moe_kernel(x_ref, router_ref, wg_ref, wu_ref, wd_ref, o_ref,
                xs_ref, y_ref, slot_ref, gate_ref, slot_row_ref, slot_e_ref, gate_e_ref, count_ref):
    e = pl.program_id(1)
    f = pl.program_id(2)
    tm, d = x_ref.shape
    sub = MOE_GROUP_ROWS

    @pl.when((e == 0) & (f == 0))
    def _route():
        o_ref[...] = jnp.zeros_like(o_ref)
        logits = _dot(x_ref[...], router_ref[...])
        lane = lax.broadcasted_iota(jnp.int32, logits.shape, 1)
        logits = jnp.where(lane < N_EXPERTS, logits, -jnp.inf)
        v1 = jnp.max(logits, axis=-1, keepdims=True)
        i1 = jnp.min(jnp.where(logits == v1, lane, V7X_LANES), axis=-1, keepdims=True)
        rest = jnp.where(lane == i1, -jnp.inf, logits)
        v2 = jnp.max(rest, axis=-1, keepdims=True)
        i2 = jnp.min(jnp.where(rest == v2, lane, V7X_LANES), axis=-1, keepdims=True)
        e2 = jnp.exp(v2 - v1)
        inv = 1.0 / (1.0 + e2)
        gate_ref[...] = jnp.where(lane == i1, inv, 0.0) + jnp.where(lane == i2, e2 * inv, 0.0)
        routed = jnp.where((lane == i1) | (lane == i2), 1.0, 0.0)
        c = MOE_RANK_CHUNK
        before = (lax.broadcasted_iota(jnp.int32, (c, c), 1) < lax.broadcasted_iota(jnp.int32, (c, c), 0))
        before = jnp.where(before, 1.0, 0.0).astype(BF16)
        offset = jnp.zeros((1, V7X_LANES), F32)
        for j in range(tm // c):
            part = routed[j * c:(j + 1) * c]
            rank = _dot(before, part.astype(BF16)) + offset
            slot_ref[j * c:(j + 1) * c, :] = jnp.where(part > 0.0, rank, -1.0)
            offset = offset + jnp.sum(part, axis=0, keepdims=True)
        for ee in range(N_EXPERTS):
            count_ref[ee] = offset[0, ee].astype(jnp.int32)
        slot_row_ref[...] = slot_ref[...].T[:N_EXPERTS]

    tail = MOE_TAIL_ROWS
    n_groups = (count_ref[e] + sub - 1) // sub
    n_filled = count_ref[e] // sub
    rest = count_ref[e] - n_filled * sub
    short = (rest <= tail) & (rest > 0)
    merged = short & (n_filled > 0)
    merged_base = (n_filled - 1) * sub
    n_plain = jnp.where(merged, n_filled - 1, n_groups)
    merged_once = merged.astype(jnp.int32)
    long = sub + tail
    long_pad = -(-long // V7X_LANES) * V7X_LANES

    @pl.when(f == 0)
    def _gather():
        slot_e_ref[...] = _lane_column(slot_ref[...], e)
        gate_e_ref[...] = _lane_column(gate_ref[...], e)
        slot_row = slot_row_ref[pl.ds(e, 1), :]

        def body(i, _, base, n, n_zero):
            r0 = pl.multiple_of(base + i * sub, sub)
            want = (r0 + lax.broadcasted_iota(jnp.int32, (n, 1), 0)).astype(F32)
            onehot = jnp.where(slot_row == want, 1.0, 0.0).astype(BF16)
            xs_ref[pl.ds(r0, n), :] = _dot(onehot, x_ref[...]).astype(BF16)
            y_ref[pl.ds(r0, n_zero), :] = jnp.zeros((n_zero, d), F32)
            return 0

        lax.fori_loop(0, n_plain, functools.partial(body, base=0, n=sub, n_zero=sub), 0)
        lax.fori_loop(0, merged_once, functools.partial(body, base=merged_base, n=long, n_zero=long_pad), 0)

    def expert(i, _, base, n, align):
        r0 = pl.multiple_of(base + i * n, align)
        y_ref[pl.ds(r0, n), :] += _swiglu_chunk(xs_ref[pl.ds(r0, n), :], wg_ref[0], wu_ref[0], wd_ref[0])
        return 0

    n_whole = jnp.where(rest > tail, n_filled + 1, jnp.where(merged, n_filled - 1, n_filled))
    lax.fori_loop(0, n_whole, functools.partial(expert, base=0, n=sub, align=sub), 0)
    lax.fori_loop(0, merged_once, functools.partial(expert, base=merged_base, n=long, align=sub), 0)
    lax.fori_loop(0, (short & (n_filled == 0)).astype(jnp.int32),
                  functools.partial(expert, base=0, n=tail, align=tail), 0)

    @pl.when(f == pl.num_programs(2) - 1)
    def _scatter():
        tc = MOE_SCATTER_ROWS

        def body(i, _, base, n):
            r0 = pl.multiple_of(base + i * sub, sub)
            y = y_ref[pl.ds(r0, n), :].astype(BF16)
            want = (r0 + lax.broadcasted_iota(jnp.int32, (1, n), 1)).astype(F32)
            for j in range(tm // tc):
                rows = slice(j * tc, (j + 1) * tc)
                slot = jnp.concatenate([slot_e_ref[rows, :]] * (n // V7X_LANES), axis=1)
                onehot = jnp.where(slot == want, 1.0, 0.0).astype(BF16)
                weight = jnp.concatenate([gate_e_ref[rows, :]] * (d // V7X_LANES), axis=1)
                o_ref[rows, :] += weight * _dot(onehot, y)
            return 0

        lax.fori_loop(0, n_plain, functools.partial(body, base=0, n=sub), 0)
        lax.fori_loop(0, merged_once, functools.partial(body, base=merged_base, n=long_pad), 0)


def _moe_ffn(hn, router, wg, wu, wd):
    t, d = hn.shape
    tm = min(MOE_ROW_TILE, t)
    tf = MOE_COL_TILE
    nf = D_FF // tf
    router = jnp.pad(router, ((0, 0), (0, V7X_LANES - N_EXPERTS))).astype(BF16)
    once = pl.Buffered(1)
    return pl.pallas_call(
        _moe_kernel,
        grid=(t // tm, N_EXPERTS, nf),
        in_specs=[pl.BlockSpec((tm, d), lambda i, e, f: (i, 0), pipeline_mode=once),
                  pl.BlockSpec((d, V7X_LANES), lambda i, e, f: (0, 0), pipeline_mode=once),
                  pl.BlockSpec((1, d, tf), lambda i, e, f: (e, 0, f)),
                  pl.BlockSpec((1, d, tf), lambda i, e, f: (e, 0, f)),
                  pl.BlockSpec((1, tf, d), lambda i, e, f: (e, f, 0))],
        out_specs=pl.BlockSpec((tm, d), lambda i, e, f: (i, 0), pipeline_mode=once),
        out_shape=jax.ShapeDtypeStruct((t, d), F32),
        scratch_shapes=[pltpu.VMEM((tm, d), BF16), pltpu.VMEM((tm, d), F32),
                        pltpu.VMEM((tm, V7X_LANES), F32), pltpu.VMEM((tm, V7X_LANES), F32),
                        pltpu.VMEM((N_EXPERTS, tm), F32),
                        pltpu.VMEM((tm, V7X_LANES), F32), pltpu.VMEM((tm, V7X_LANES), F32),
                        pltpu.SMEM((N_EXPERTS,), jnp.int32)],
        compiler_params=_params(("parallel", "arbitrary", "arbitrary"), vmem=MOE_VMEM_LIMIT),
        name="moe_ffn",
    )(hn, router, wg.astype(BF16), wu.astype(BF16), wd.astype(BF16))


def _ple_kernel(h_ref, f_ref, p_ref, g_ref, proj_ref, gate_ref, gf_ref, o_ref, *, final_norm):
    h = _ple_rows(h_ref[...] + f_ref[...], p_ref[...], g_ref[...], proj_ref, gate_ref)
    if final_norm:
        h = _rms(h, gf_ref[...])
    o_ref[...] = h


def _ple(h, f, p, g, proj, gate, g_final, final_norm):
    t, d = h.shape
    tm = min(STREAM_ROW_TILE, t)
    vec = pl.BlockSpec((1, d), lambda i: (0, 0))
    return pl.pallas_call(
        functools.partial(_ple_kernel, final_norm=final_norm),
        grid=(t // tm,),
        in_specs=[pl.BlockSpec((tm, d), lambda i: (i, 0)),
                  pl.BlockSpec((tm, d), lambda i: (i, 0)),
                  pl.BlockSpec((tm, PLE_DIM), lambda i: (i, 0)),
                  vec,
                  pl.BlockSpec((PLE_DIM, d), lambda i: (0, 0)),
                  pl.BlockSpec((d, d), lambda i: (0, 0)),
                  vec],
        out_specs=pl.BlockSpec((tm, d), lambda i: (i, 0)),
        out_shape=jax.ShapeDtypeStruct((t, d), F32),
        compiler_params=_params(("parallel",)),
        name="ple",
    )(h, f, p, g.reshape(1, d), proj.astype(BF16), gate.astype(BF16), g_final.reshape(1, d))


def kernel(x, p, w_in, w_out, g_mix, g_ffn, g_ple, g_final, cmp_pos, cmp_w1, cmp_w2, ret_gn,
           ffn_gate, ffn_up, ffn_down, moe_router, moe_gate, moe_up, moe_down, ple_proj, ple_gate):
    b, s, d = x.shape
    depth = w_in.shape[0]
    t = b * s
    h = x
    for i in range(depth):
        qn, kcv, ks, kv3, gl, ret = _in_proj(h.reshape(b, s, d), g_mix[i], _arrange_w_in(w_in[i]))
        kv_cmp = _compress(kcv, cmp_pos[i], cmp_w1[i], cmp_w2[i])
        a = _nsa_attention(qn, kv_cmp, ks, kv3, gl)
        r = _retention(ret, ret_gn[i])
        h, a, r, p_i = h.reshape(t, d), a.reshape(t, NSA_Q_COLS), r.reshape(t, RET_COLS), p[i].reshape(t, PLE_DIM)
        last = i == depth - 1
        if i % 2 == 0:
            h = _dense_layer(h, a, r, p_i, w_out[i], g_ffn[i], ffn_gate[i // 2], ffn_up[i // 2], ffn_down[i // 2],
                             g_ple[i], ple_proj[i], ple_gate[i], g_final, last)
        else:
            h, hn = _out_proj(h, a, r, w_out[i], g_ffn[i])
            f = _moe_ffn(hn, moe_router[i // 2], moe_gate[i // 2], moe_up[i // 2], moe_down[i // 2])
            h = _ple(h, f, p_i, g_ple[i], ple_proj[i], ple_gate[i], g_final, last)
    return h.reshape(b, s, d)
```

```python
import functools
import math

import jax
import jax.numpy as jnp
from jax import lax
from jax.experimental import pallas as pl
from jax.experimental.pallas import tpu as pltpu

F32 = jnp.float32
BF16 = jnp.bfloat16

D_MODEL = 1024
HEAD_DIM = 64
NSA_HEADS = 8
NSA_KV_HEADS = 2
NSA_GROUP = NSA_HEADS // NSA_KV_HEADS
RET_HEADS = 8
CMP_BLOCK = 32
CMP_STRIDE = 16
CMP_HIDDEN = 256
SEL_BLOCK = 64
N_SELECT = 16
N_LOCAL_SEL = 2
WINDOW = 512
RET_CHUNK = 128
D_FF = 3584
N_EXPERTS = 8
PLE_DIM = 256
EPS = 1e-6
NEG_INF = -1e30
BIG = 1e9

NSA_Q_COLS = NSA_HEADS * HEAD_DIM
NSA_KV_COLS = NSA_KV_HEADS * HEAD_DIM
NSA_GATE_COLS = 3 * NSA_HEADS
RET_COLS = RET_HEADS * HEAD_DIM
MIX_WIDTH = NSA_Q_COLS + RET_COLS

V7X_LANES = 128
V7X_SUBLANES = 8
V7X_VMEM_BYTES = 64 * 1024 * 1024
VMEM_LIMIT = V7X_VMEM_BYTES * 3 // 4

ROW_TILE = 512
STREAM_ROW_TILE = 1024
ATTN_Q_TILE = 256
SEL_KV_TILE = 256
SEL_STEP_WIDTHS = (1, 2, 4, 8)
RET_BATCH = 4
RET_HEAD_GROUP = 4
FFN_ROW_TILE = 512
FFN_COL_TILE = 512
MOE_COL_TILE = 1792
MOE_ROW_TILE = 2048
MOE_GROUP_ROWS = 256
MOE_TAIL_ROWS = 64
MOE_RANK_CHUNK = 256
MOE_SCATTER_ROWS = 512
MOE_VMEM_LIMIT = V7X_VMEM_BYTES * 7 // 8

POS_SHIFT = 6
POS_SPLIT = 1 << POS_SHIFT
SEL_SHIFT = SEL_BLOCK.bit_length() - 1
MAX_SEL_BLOCKS = V7X_LANES


def _params(sem, vmem=VMEM_LIMIT):
    return pltpu.CompilerParams(dimension_semantics=sem, vmem_limit_bytes=vmem)


def _dot(a, b):
    return jnp.dot(a, b, preferred_element_type=F32)


def _dot_nt(a, b):
    return lax.dot_general(a, b, (((1,), (1,)), ((), ())), preferred_element_type=F32)


def _dot_tn(a, b):
    return lax.dot_general(a, b, (((0,), (0,)), ((), ())), preferred_element_type=F32)


def _rms(x, g):
    return x * lax.rsqrt(jnp.mean(x * x, axis=-1, keepdims=True) + EPS) * g


def _lane_features(shape, first, second):
    lane = lax.broadcasted_iota(jnp.int32, shape, 1)
    return jnp.where(lane == 0, first, jnp.where(lane == 1, second, 0))


def _key_features(pos, width):
    return _lane_features((pos.shape[0], width), pos >> POS_SHIFT, pos & (POS_SPLIT - 1))


def _ones_features(n, width):
    return _lane_features((n, width), 1, 0)


def _int_to_bf16(x):
    return x.astype(F32).astype(BF16)


def _in_proj_kernel(h_ref, g_ref, w_ref, qn_ref, kcv_ref, ks_ref, kv3_ref, gl_ref, ret_ref):
    tm = h_ref.shape[1]
    xn = _rms(h_ref[0], g_ref[...]).astype(BF16)
    pair = 2 * HEAD_DIM
    pos = pl.program_id(1) * tm + lax.broadcasted_iota(jnp.int32, (tm, 1), 0)
    kfeat = _int_to_bf16(_key_features(pos, HEAD_DIM))
    vfeat = _int_to_bf16(_ones_features(tm, HEAD_DIM))
    lane = lax.broadcasted_iota(jnp.int32, (tm, MAX_SEL_BLOCKS), 1)
    block_onehot = jnp.where(lane == (pos >> SEL_SHIFT), 1.0, 0.0).astype(BF16)

    wide = 2 * pair
    per_dot = wide // HEAD_DIM
    for j in range(NSA_Q_COLS // wide):
        z = (_dot(xn, w_ref[:, j * wide:(j + 1) * wide]) * HEAD_DIM ** -0.5).astype(BF16)
        for k in range(per_dot):
            head = per_dot * j + k
            slope = 2.0 ** -(head + 1)
            qfeat = _lane_features((tm, HEAD_DIM), POS_SPLIT * slope, slope).astype(BF16)
            qn_ref[0, head] = jnp.concatenate([z[:, k * HEAD_DIM:(k + 1) * HEAD_DIM], qfeat], axis=1)
    base = NSA_Q_COLS
    z = _dot(xn, w_ref[:, base:base + wide]).astype(BF16)
    for a in range(2):
        kcv_ref[a, 0] = z[:, a * pair:(a + 1) * pair]
    base += wide
    for c in range(2):
        z = _dot(xn, w_ref[:, base + c * wide:base + (c + 1) * wide]).astype(BF16)
        for half in range(2):
            a = 2 * c + half
            for g in range(NSA_KV_HEADS):
                zg = z[:, half * pair + g * HEAD_DIM:half * pair + (g + 1) * HEAD_DIM]
                if a == 0:
                    ks_ref[0, g] = jnp.concatenate([zg, kfeat, block_onehot], axis=1)
                else:
                    kv3_ref[a - 1, 0, g] = jnp.concatenate([zg, kfeat if a == 2 else vfeat], axis=1)
    base += 2 * wide
    for a in range(4):
        ret_ref[0, :, a * RET_COLS:(a + 1) * RET_COLS] = _dot(
            xn, w_ref[:, base + a * RET_COLS:base + (a + 1) * RET_COLS]).astype(BF16)
    base += 4 * RET_COLS
    gl_ref[0] = _dot(xn, w_ref[:, base:base + NSA_KV_HEADS * V7X_LANES])


def _arrange_w_in(w):
    q_end = NSA_Q_COLS
    kv_end = q_end + 6 * NSA_KV_COLS
    gl_end = kv_end + NSA_GATE_COLS
    gl = w[:, kv_end:gl_end].reshape(D_MODEL, 3, NSA_KV_HEADS, NSA_GROUP)
    gl = jnp.transpose(gl, (0, 2, 1, 3)).reshape(D_MODEL, NSA_KV_HEADS, 3 * NSA_GROUP)
    gl = jnp.pad(gl, ((0, 0), (0, 0), (0, V7X_LANES - 3 * NSA_GROUP)))
    gl = gl.reshape(D_MODEL, NSA_KV_HEADS * V7X_LANES)
    return jnp.concatenate([w[:, :kv_end], w[:, gl_end:], gl], axis=1).astype(BF16)


def _in_proj(h, g, w):
    b, s, d = h.shape
    assert s // SEL_BLOCK <= MAX_SEL_BLOCKS
    tm = min(ROW_TILE, s)
    ncols = w.shape[1]
    aug = 2 * HEAD_DIM
    return pl.pallas_call(
        _in_proj_kernel,
        grid=(b, s // tm),
        in_specs=[
            pl.BlockSpec((1, tm, d), lambda bi, si: (bi, si, 0)),
            pl.BlockSpec((1, d), lambda bi, si: (0, 0)),
            pl.BlockSpec((d, ncols), lambda bi, si: (0, 0)),
        ],
        out_specs=[
            pl.BlockSpec((1, NSA_HEADS, tm, aug), lambda bi, si: (bi, 0, si, 0)),
            pl.BlockSpec((2, 1, tm, NSA_KV_COLS), lambda bi, si: (0, bi, si, 0)),
            pl.BlockSpec((1, NSA_KV_HEADS, tm, aug + MAX_SEL_BLOCKS), lambda bi, si: (bi, 0, si, 0)),
            pl.BlockSpec((3, 1, NSA_KV_HEADS, tm, aug), lambda bi, si: (0, bi, 0, si, 0)),
            pl.BlockSpec((1, tm, NSA_KV_HEADS * V7X_LANES), lambda bi, si: (bi, si, 0)),
            pl.BlockSpec((1, tm, 4 * RET_COLS), lambda bi, si: (bi, si, 0)),
        ],
        out_shape=[
            jax.ShapeDtypeStruct((b, NSA_HEADS, s, aug), BF16),
            jax.ShapeDtypeStruct((2, b, s, NSA_KV_COLS), BF16),
            jax.ShapeDtypeStruct((b, NSA_KV_HEADS, s, aug + MAX_SEL_BLOCKS), BF16),
            jax.ShapeDtypeStruct((3, b, NSA_KV_HEADS, s, aug), BF16),
            jax.ShapeDtypeStruct((b, s, NSA_KV_HEADS * V7X_LANES), F32),
            jax.ShapeDtypeStruct((b, s, 4 * RET_COLS), BF16),
        ],
        compiler_params=_params(("parallel", "parallel")),
        name="in_proj",
    )(h, g.reshape(1, d), w)


def _compress_kernel(x_ref, wbig_ref, pos_ref, w1_ref, w2_ref, o_ref):
    nch = x_ref.shape[2]
    u = _dot(x_ref[0, 0], wbig_ref[0])
    hid0 = _dot(pos_ref[0], w1_ref[0])[0:1]
    w2 = w2_ref[0]
    cmp_end = lax.broadcasted_iota(jnp.int32, (nch, 1), 0) * CMP_STRIDE + (CMP_BLOCK - 1)
    feat = _int_to_bf16(jnp.where(pl.program_id(0) == 0, _key_features(cmp_end, HEAD_DIM),
                                  _ones_features(nch, HEAD_DIM)))
    for g in range(NSA_KV_HEADS):
        c0 = g * 2 * CMP_HIDDEN
        first = u[:, c0:c0 + CMP_HIDDEN]
        second = u[:, c0 + CMP_HIDDEN:c0 + 2 * CMP_HIDDEN]
        hid = first + pltpu.roll(second, nch - 1, 0) + hid0
        out = _dot(jax.nn.gelu(hid).astype(BF16), w2).astype(BF16)
        o_ref[0, 0, g] = jnp.concatenate([out, feat], axis=1)


def _arrange_cmp_w1(w1):
    r = CMP_BLOCK // CMP_STRIDE
    w1r = w1.reshape(2, r, CMP_STRIDE, HEAD_DIM, CMP_HIDDEN)
    eye = jnp.eye(NSA_KV_HEADS, dtype=w1.dtype)
    big = jnp.einsum('krcdh,gf->kcgdfrh', w1r, eye)
    return big.reshape(2, CMP_STRIDE * NSA_KV_COLS, NSA_KV_HEADS * r * CMP_HIDDEN).astype(BF16)


def _compress(kcv, cmp_pos, cmp_w1, cmp_w2):
    _, b, s, _ = kcv.shape
    nch = s // CMP_STRIDE
    x = kcv.reshape(2, b, nch, CMP_STRIDE * NSA_KV_COLS)
    wbig = _arrange_cmp_w1(cmp_w1)
    pos = jnp.broadcast_to(cmp_pos.reshape(2, 1, CMP_BLOCK * HEAD_DIM),
                           (2, V7X_SUBLANES, CMP_BLOCK * HEAD_DIM)).astype(BF16)
    w1 = cmp_w1.reshape(2, CMP_BLOCK * HEAD_DIM, CMP_HIDDEN).astype(BF16)
    w2 = cmp_w2.astype(BF16)
    kdim = CMP_STRIDE * NSA_KV_COLS
    return pl.pallas_call(
        _compress_kernel,
        grid=(2, b),
        in_specs=[
            pl.BlockSpec((1, 1, nch, kdim), lambda a, bi: (a, bi, 0, 0)),
            pl.BlockSpec((1, kdim, wbig.shape[2]), lambda a, bi: (a, 0, 0)),
            pl.BlockSpec((1, V7X_SUBLANES, CMP_BLOCK * HEAD_DIM), lambda a, bi: (a, 0, 0)),
            pl.BlockSpec((1, CMP_BLOCK * HEAD_DIM, CMP_HIDDEN), lambda a, bi: (a, 0, 0)),
            pl.BlockSpec((1, CMP_HIDDEN, HEAD_DIM), lambda a, bi: (a, 0, 0)),
        ],
        out_specs=pl.BlockSpec((1, 1, NSA_KV_HEADS, nch, 2 * HEAD_DIM), lambda a, bi: (a, bi, 0, 0, 0)),
        out_shape=jax.ShapeDtypeStruct((2, b, NSA_KV_HEADS, nch, 2 * HEAD_DIM), BF16),
        compiler_params=_params(("parallel", "parallel")),
        name="compress",
    )(x, wbig, pos, w1, w2)


def _normalise_t(acc):
    return acc[:HEAD_DIM] * (1.0 / jnp.maximum(acc[HEAD_DIM:HEAD_DIM + 1], 1e-30))


def _attn_kernel(q_ref, kc_ref, vc_ref, ks_ref, vs_ref, kw_ref, vw_ref, gl_ref, ovt_ref, o_ref, tiles_ref,
                 sel_ref, oc_ref, imp_ref, *, n_pick, tk):
    qt = q_ref.shape[2]
    cols = NSA_GROUP * qt
    t0 = pl.program_id(2) * qt
    q_t = q_ref[0].reshape(cols, 2 * HEAD_DIM).T
    tq = t0 + lax.broadcasted_iota(jnp.int32, (1, qt), 1)

    def per_head(valid, masked):
        bias = jnp.where(valid, 0.0, masked)
        return jnp.concatenate([bias] * NSA_GROUP, axis=1)

    nc = kc_ref.shape[3]

    def compressed(rows):
        cmp_end = lax.broadcasted_iota(jnp.int32, (rows, 1), 0) * CMP_STRIDE + (CMP_BLOCK - 1)
        logit = _dot(kc_ref[0, 0, 0, :rows], q_t) + per_head(cmp_end <= tq, -jnp.inf)
        e = jnp.exp(logit - jnp.maximum(jnp.max(logit, axis=0, keepdims=True), NEG_INF))
        p = e * (1.0 / jnp.maximum(jnp.sum(e, axis=0, keepdims=True), 1e-30))
        oc_ref[...] = _dot(vc_ref[0, 0, 0, :rows].T, p.astype(BF16))[:HEAD_DIM]
        p_sum = p[:, 0:qt]
        for r in range(1, NSA_GROUP):
            p_sum = p_sum + p[:, r * qt:(r + 1) * qt]
        p_hi = p_sum.astype(BF16)
        p_lo = (p_sum - p_hi.astype(F32)).astype(BF16)
        imp_ref[...] = _dot(ovt_ref[:, :rows], p_hi) + _dot(ovt_ref[:, :rows], p_lo)

    n_cmp_live = (t0 + qt - CMP_BLOCK + CMP_STRIDE) // CMP_STRIDE
    cmp_sizes = [nc // 4, nc // 2, nc]
    for lower, rows in zip([None] + cmp_sizes[:-1], cmp_sizes):
        above = True if lower is None else n_cmp_live > lower
        below = True if rows == nc else n_cmp_live <= rows
        pl.when(jnp.logical_and(above, below))(functools.partial(compressed, rows))
    o_c = oc_ref[...]
    imp = imp_ref[...]

    shape = (MAX_SEL_BLOCKS, qt)
    blk = lax.broadcasted_iota(jnp.int32, shape, 0)
    back = ((t0 + lax.broadcasted_iota(jnp.int32, shape, 1)) >> SEL_SHIFT) - blk
    forced = (blk == 0) | ((back >= 0) & (back < N_LOCAL_SEL))
    score = jnp.where(forced, BIG, jnp.where(back >= 0, imp, -BIG))

    def pick(_, work):
        row = lax.broadcasted_iota(jnp.int32, work.shape, 0)
        m = jnp.max(work, axis=0, keepdims=True)
        first = jnp.min(jnp.where(work == m, row, MAX_SEL_BLOCKS), axis=0, keepdims=True)
        return jnp.where(row == first, -jnp.inf, work)

    n_live = (t0 + qt + SEL_BLOCK - 1) >> SEL_SHIFT
    sizes = [MAX_SEL_BLOCKS // 4, MAX_SEL_BLOCKS // 2, MAX_SEL_BLOCKS]
    for lower, rows in zip([0] + sizes[:-1], sizes):
        @pl.when((n_live > lower) & ((n_live <= rows) | (rows == MAX_SEL_BLOCKS)))
        def _(rows=rows):
            left = lax.fori_loop(0, n_pick, pick, score[:rows])
            sel_ref[...] = jnp.zeros(shape, F32)
            sel_ref[:rows] = jnp.where(left == -jnp.inf, 1.0, 0.0)
    sel = sel_ref[...]
    sel_bias = ((sel - 1.0) * -NEG_INF).astype(BF16)
    q_sel = jnp.concatenate([q_t, jnp.concatenate([sel_bias] * NSA_GROUP, axis=1)], axis=0)

    n_own = max(1, qt // tk)
    n_full = t0 // tk
    blocks_per_tile = tk // SEL_BLOCK
    block_used = jnp.max(sel, axis=1, keepdims=True)
    n_visit = jnp.int32(0)
    for j in range(tiles_ref.shape[0]):
        used = jnp.max(block_used[j * blocks_per_tile:(j + 1) * blocks_per_tile]) > 0.0
        tiles_ref[n_visit] = j
        n_visit = n_visit + (used & (j < n_full)).astype(jnp.int32)

    def sweep(tiles, carry, causal):
        m_i, acc = carry
        starts = [pl.multiple_of(j * tk, tk) for j in tiles]
        scores = []
        for k0 in starts:
            s = _dot(ks_ref[0, 0, pl.ds(k0, tk), :], q_sel)
            if causal:
                s = s + per_head(k0 + lax.broadcasted_iota(jnp.int32, (tk, 1), 0) <= tq, NEG_INF)
            scores.append(s)
        m_new = m_i
        for s in scores:
            m_new = jnp.maximum(m_new, jnp.max(s, axis=0, keepdims=True))
        acc = jnp.exp(m_i - m_new) * acc
        for s, k0 in zip(scores, starts):
            acc = acc + _dot(vs_ref[0, 0, 0, pl.ds(k0, tk), :].T, jnp.exp(s - m_new).astype(BF16))
        return m_new, acc

    carry = (jnp.full((1, cols), NEG_INF, F32), jnp.zeros((2 * HEAD_DIM, cols), F32))
    done = jnp.int32(0)
    for width in SEL_STEP_WIDTHS:
        steps = n_visit // width if width == SEL_STEP_WIDTHS[-1] else (n_visit // width) & 1
        carry = lax.fori_loop(
            0, steps,
            lambda i, c, width=width, done=done: sweep(
                [tiles_ref[done + width * i + u] for u in range(width)], c, causal=False),
            carry)
        done = done + steps * width
    carry = sweep([n_full + j for j in range(n_own)], carry, causal=True)
    o_s = _normalise_t(carry[1])

    band = WINDOW + qt
    w0 = pl.multiple_of(jnp.maximum(t0 - WINDOW, 0), math.gcd(qt, WINDOW))
    kpos = w0 + lax.broadcasted_iota(jnp.int32, (band, 1), 0)
    valid_w = (kpos <= tq) & (kpos > tq - WINDOW)
    logit_w = _dot(kw_ref[0, 0, 0, pl.ds(w0, band), :], q_t) + per_head(valid_w, NEG_INF)
    e_w = jnp.exp(logit_w - jnp.max(logit_w, axis=0, keepdims=True))
    o_w = _normalise_t(_dot(vw_ref[0, 0, 0, pl.ds(w0, band), :].T, e_w.astype(BF16)))

    gate_t = jax.nn.sigmoid(gl_ref[0]).T

    def branch_gate(branch):
        first = branch * NSA_GROUP
        return jnp.concatenate([gate_t[first + r:first + r + 1] for r in range(NSA_GROUP)], axis=1)

    o_t = branch_gate(0) * o_c + branch_gate(1) * o_s + branch_gate(2) * o_w
    o = o_t.T
    o_ref[0] = jnp.concatenate([o[r * qt:(r + 1) * qt] for r in range(NSA_GROUP)], axis=1).astype(o_ref.dtype)


def _nsa_attention(qn, kv_cmp, ks, kv3, gl):
    b, _, s, _ = qn.shape
    nc = kv_cmp.shape[3]
    nsel = s // SEL_BLOCK
    n_pick = min(N_SELECT, nsel)
    tk = min(SEL_KV_TILE, s)
    aug = 2 * HEAD_DIM
    qt = min(ATTN_Q_TILE, s)
    assert qt % tk == 0 or tk % qt == 0
    cmp_start = jnp.arange(nc) * CMP_STRIDE
    sel_start = jnp.arange(nsel) * SEL_BLOCK
    overlap = (jnp.minimum(cmp_start[:, None] + CMP_BLOCK, sel_start[None, :] + SEL_BLOCK)
               > jnp.maximum(cmp_start[:, None], sel_start[None, :]))
    overlap_t = jnp.pad(overlap.T.astype(BF16), ((0, MAX_SEL_BLOCKS - nsel), (0, 0)))
    kv_spec = lambda a: pl.BlockSpec((1, 1, 1, s, aug), lambda bi, g, i, a=a: (a, bi, g, 0, 0))
    cmp_spec = lambda a: pl.BlockSpec((1, 1, 1, nc, aug), lambda bi, g, i, a=a: (a, bi, g, 0, 0))
    return pl.pallas_call(
        functools.partial(_attn_kernel, n_pick=n_pick, tk=tk),
        grid=(b, NSA_KV_HEADS, s // qt),
        in_specs=[
            pl.BlockSpec((1, NSA_GROUP, qt, aug), lambda bi, g, i: (bi, g, i, 0)),
            cmp_spec(0), cmp_spec(1),
            pl.BlockSpec((1, 1, s, aug + MAX_SEL_BLOCKS), lambda bi, g, i: (bi, g, 0, 0)),
            kv_spec(0), kv_spec(1), kv_spec(2),
            pl.BlockSpec((1, qt, V7X_LANES), lambda bi, g, i: (bi, i, g)),
            pl.BlockSpec((MAX_SEL_BLOCKS, nc), lambda bi, g, i: (0, 0)),
        ],
        out_specs=pl.BlockSpec((1, qt, NSA_GROUP * HEAD_DIM), lambda bi, g, i: (bi, i, g)),
        out_shape=jax.ShapeDtypeStruct((b, s, NSA_Q_COLS), BF16),
        scratch_shapes=[pltpu.SMEM((s // tk,), jnp.int32), pltpu.VMEM((MAX_SEL_BLOCKS, qt), F32),
                        pltpu.VMEM((HEAD_DIM, NSA_GROUP * qt), F32), pltpu.VMEM((MAX_SEL_BLOCKS, qt), F32)],
        compiler_params=_params(("parallel", "parallel", "arbitrary")),
        name="nsa_attention",
    )(qn, kv_cmp, kv_cmp, ks, kv3, kv3, kv3, gl, overlap_t)


def _retention_kernel(q_ref, k_ref, v_ref, g_ref, decay_ref, xi_ref, zeta_ref, gch_ref, same_ref, gn_ref,
                      o_ref, state_ref):
    @pl.when(pl.program_id(1) == 0)
    def _():
        state_ref[...] = jnp.zeros_like(state_ref)

    width = same_ref.shape[0]
    heads = width // HEAD_DIM
    c = q_ref.shape[1]
    scale = jnp.asarray(HEAD_DIM ** -0.5, BF16)
    lane_head = lax.broadcasted_iota(jnp.int32, (c, width), 1) // HEAD_DIM
    same = same_ref[...]
    mean_w = (same * (1.0 / HEAD_DIM)).astype(BF16)

    def head_mean(x):
        hi = x.astype(BF16)
        lo = (x - hi.astype(F32)).astype(BF16)
        both = _dot(jnp.concatenate([hi, lo], axis=0), mean_w)
        return both[:c] + both[c:]

    def stack(x):
        zero = jnp.zeros_like(x)
        return jnp.concatenate([jnp.where(lane_head == j, x, zero) for j in range(heads)], axis=0)

    for i in range(q_ref.shape[0]):
        outs = []
        for p in range(RET_COLS // width):
            cols = slice(p * width, (p + 1) * width)
            q = q_ref[i, :, cols]
            k = k_ref[i, :, cols] * scale
            v = v_ref[i, :, cols]
            state = state_ref[i, p]
            inner = _dot_nt(q, stack(k)) * decay_ref[p]
            o = _dot(inner.astype(BF16), stack(v)) + _dot(q, state.astype(BF16)) * xi_ref[p]
            kz = (k.astype(F32) * zeta_ref[p]).astype(BF16)
            state_ref[i, p] = gch_ref[p] * state + same * _dot_tn(kz, v)
            centred = o - head_mean(o)
            outs.append(centred * lax.rsqrt(head_mean(jnp.square(centred)) + EPS))
        o = jnp.concatenate(outs, axis=1) * gn_ref[...]
        o_ref[i] = (jax.nn.silu(g_ref[i].astype(F32)) * o).astype(o_ref.dtype)


def _retention(ret, gn_gain):
    b, s, _ = ret.shape
    c = RET_CHUNK
    hh = RET_HEADS
    hg = RET_HEAD_GROUP
    width = hg * HEAD_DIM
    log_gamma = jnp.log1p(-jnp.exp2(-5.0 - jnp.arange(hh, dtype=F32)))
    pos = jnp.arange(c, dtype=F32)
    diff = pos[:, None] - pos[None, :]
    decay = jnp.where(diff >= 0, jnp.exp(jnp.maximum(diff, 0.0)[None] * log_gamma[:, None, None]), 0.0)
    xi = jnp.exp((pos + 1.0)[None] * log_gamma[:, None])[..., None]
    zeta = jnp.exp((c - 1.0 - pos)[None] * log_gamma[:, None])[..., None]
    g_chunk = jnp.exp(c * log_gamma)[:, None, None]
    side_by_side = lambda t: jnp.concatenate([t[j::hg] for j in range(hg)], axis=-1)
    decay = side_by_side(decay)
    xi = side_by_side(jnp.broadcast_to(xi, (hh, c, HEAD_DIM)))
    zeta = side_by_side(jnp.broadcast_to(zeta, (hh, c, HEAD_DIM)))
    head_of = jnp.arange(width) // HEAD_DIM
    same = (head_of[:, None] == head_of[None, :]).astype(F32)
    g_lane = side_by_side(jnp.broadcast_to(g_chunk, (hh, 1, HEAD_DIM)))
    g_chunk = same[None] * g_lane
    nb = math.gcd(RET_BATCH, b)
    part = lambda a: pl.BlockSpec((nb, c, RET_COLS), lambda bi, n, a=a: (bi, n, a))
    full = lambda shape: pl.BlockSpec(shape, lambda bi, n: (0,) * len(shape))
    return pl.pallas_call(
        _retention_kernel,
        grid=(b // nb, s // c),
        in_specs=[part(0), part(1), part(2), part(3),
                  full((hh // hg, c, hg * c)), full((hh // hg, c, width)), full((hh // hg, c, width)),
                  full((hh // hg, width, width)), full((width, width)), full((1, RET_COLS))],
        out_specs=pl.BlockSpec((nb, c, RET_COLS), lambda bi, n: (bi, n, 0)),
        out_shape=jax.ShapeDtypeStruct((b, s, RET_COLS), BF16),
        scratch_shapes=[pltpu.VMEM((nb, hh // hg, width, width), F32)],
        compiler_params=_params(("parallel", "arbitrary")),
        name="retention",
    )(ret, ret, ret, ret, decay, xi, zeta, g_chunk, same, gn_gain.reshape(1, RET_COLS).astype(F32))


def _out_proj_rows(h, a, r, w_ref):
    return h + _dot(a, w_ref[:NSA_Q_COLS]) + _dot(r, w_ref[NSA_Q_COLS:])


def _out_proj_kernel(h_ref, a_ref, r_ref, w_ref, g_ref, h_out_ref, hn_ref):
    h = _out_proj_rows(h_ref[...], a_ref[...], r_ref[...], w_ref)
    h_out_ref[...] = h
    hn_ref[...] = _rms(h, g_ref[...]).astype(BF16)


def _out_proj(h, a, r, w, g):
    t, d = h.shape
    tm = min(STREAM_ROW_TILE, t)
    row = lambda n: pl.BlockSpec((tm, n), lambda i: (i, 0))
    return pl.pallas_call(
        _out_proj_kernel,
        grid=(t // tm,),
        in_specs=[row(d), row(NSA_Q_COLS), row(RET_COLS),
                  pl.BlockSpec((MIX_WIDTH, d), lambda i: (0, 0)),
                  pl.BlockSpec((1, d), lambda i: (0, 0))],
        out_specs=[row(d), row(d)],
        out_shape=[jax.ShapeDtypeStruct((t, d), F32), jax.ShapeDtypeStruct((t, d), BF16)],
        compiler_params=_params(("parallel",)),
        name="out_proj",
    )(h, a, r, w.astype(BF16), g.reshape(1, d))


def _swiglu_chunk(x, wg, wu, wd):
    hid = jax.nn.silu(_dot(x, wg)) * _dot(x, wu)
    return _dot(hid.astype(BF16), wd)


def _ple_rows(h, p, g, proj_ref, gate_ref):
    emb = _dot(p.astype(BF16), proj_ref[...])
    return h + emb * jax.nn.sigmoid(_dot(_rms(h, g).astype(BF16), gate_ref[...]))


def _dense_layer_kernel(h_ref, a_ref, r_ref, p_ref, wo_ref, gf_ref, wg_ref, wu_ref, wd_ref, gp_ref, proj_ref, gate_ref,
                        gl_ref, o_ref, *, final_norm):
    h = _out_proj_rows(h_ref[...], a_ref[...], r_ref[...], wo_ref)
    x = _rms(h, gf_ref[...]).astype(BF16)
    acc = None
    for f in range(D_FF // FFN_COL_TILE):
        cols = slice(f * FFN_COL_TILE, (f + 1) * FFN_COL_TILE)
        part = _swiglu_chunk(x, wg_ref[:, cols], wu_ref[:, cols], wd_ref[cols, :])
        acc = part if acc is None else acc + part
    h = _ple_rows(h + acc, p_ref[...], gp_ref[...], proj_ref, gate_ref)
    if final_norm:
        h = _rms(h, gl_ref[...])
    o_ref[...] = h


def _dense_layer(h, a, r, p, w_out, g_ffn, wg, wu, wd, g_ple, proj, gate, g_final, final_norm):
    t, d = h.shape
    tm = min(FFN_ROW_TILE, t)
    row = lambda n: pl.BlockSpec((tm, n), lambda i: (i, 0))
    held = lambda shape: pl.BlockSpec(shape, lambda i: (0, 0), pipeline_mode=pl.Buffered(1))
    vec = held((1, d))
    return pl.pallas_call(
        functools.partial(_dense_layer_kernel, final_norm=final_norm),
        grid=(t // tm,),
        in_specs=[row(d), row(NSA_Q_COLS), row(RET_COLS), row(PLE_DIM),
                  held((MIX_WIDTH, d)), vec, held((d, D_FF)), held((d, D_FF)), held((D_FF, d)),
                  vec, held((PLE_DIM, d)), held((d, d)), vec],
        out_specs=row(d),
        out_shape=jax.ShapeDtypeStruct((t, d), F32),
        compiler_params=_params(("parallel",)),
        name="dense_layer",
    )(h, a, r, p, w_out.astype(BF16), g_ffn.reshape(1, d), wg.astype(BF16), wu.astype(BF16), wd.astype(BF16),
      g_ple.reshape(1, d), proj.astype(BF16), gate.astype(BF16), g_final.reshape(1, d))


def _lane_column(table, lane_index):
    lane = lax.broadcasted_iota(jnp.int32, table.shape, 1)
    col = jnp.sum(jnp.where(lane == lane_index, table, 0.0), axis=-1, keepdims=True)
    return jnp.broadcast_to(col, table.shape)


def _moe_kernel(x_ref, router_ref, wg_ref, wu_ref, wd_ref, o_ref,
                xs_ref, y_ref, slot_ref, gate_ref, slot_row_ref, slot_e_ref, gate_e_ref, count_ref):
    e = pl.program_id(1)
    f = pl.program_id(2)
    tm, d = x_ref.shape
    sub = MOE_GROUP_ROWS

    @pl.when((e == 0) & (f == 0))
    def _route():
        o_ref[...] = jnp.zeros_like(o_ref)
        logits = _dot(x_ref[...], router_ref[...])
        lane = lax.broadcasted_iota(jnp.int32, logits.shape, 1)
        logits = jnp.where(lane < N_EXPERTS, logits, -jnp.inf)
        v1 = jnp.max(logits, axis=-1, keepdims=True)
        i1 = jnp.min(jnp.where(logits == v1, lane, V7X_LANES), axis=-1, keepdims=True)
        rest = jnp.where(lane == i1, -jnp.inf, logits)
        v2 = jnp.max(rest, axis=-1, keepdims=True)
        i2 = jnp.min(jnp.where(rest == v2, lane, V7X_LANES), axis=-1, keepdims=True)
        e2 = jnp.exp(v2 - v1)
        inv = 1.0 / (1.0 + e2)
        gate_ref[...] = jnp.where(lane == i1, inv, 0.0) + jnp.where(lane == i2, e2 * inv, 0.0)
        routed = jnp.where((lane == i1) | (lane == i2), 1.0, 0.0)
        c = MOE_RANK_CHUNK
        before = (lax.broadcasted_iota(jnp.int32, (c, c), 1) < lax.broadcasted_iota(jnp.int32, (c, c), 0))
        before = jnp.where(before, 1.0, 0.0).astype(BF16)
        offset = jnp.zeros((1, V7X_LANES), F32)
        for j in range(tm // c):
            part = routed[j * c:(j + 1) * c]
            rank = _dot(before, part.astype(BF16)) + offset
            slot_ref[j * c:(j + 1) * c, :] = jnp.where(part > 0.0, rank, -1.0)
            offset = offset + jnp.sum(part, axis=0, keepdims=True)
        for ee in range(N_EXPERTS):
            count_ref[ee] = offset[0, ee].astype(jnp.int32)
        slot_row_ref[...] = slot_ref[...].T[:N_EXPERTS]

    tail = MOE_TAIL_ROWS
    n_groups = (count_ref[e] + sub - 1) // sub
    n_filled = count_ref[e] // sub
    rest = count_ref[e] - n_filled * sub
    short = (rest <= tail) & (rest > 0)
    merged = short & (n_filled > 0)
    merged_base = (n_filled - 1) * sub
    n_plain = jnp.where(merged, n_filled - 1, n_groups)
    merged_once = merged.astype(jnp.int32)
    long = sub + tail
    long_pad = -(-long // V7X_LANES) * V7X_LANES

    @pl.when(f == 0)
    def _gather():
        slot_e_ref[...] = _lane_column(slot_ref[...], e)
        gate_e_ref[...] = _lane_column(gate_ref[...], e)
        slot_row = slot_row_ref[pl.ds(e, 1), :]

        def body(i, _, base, n, n_zero):
            r0 = pl.multiple_of(base + i * sub, sub)
            want = (r0 + lax.broadcasted_iota(jnp.int32, (n, 1), 0)).astype(F32)
            onehot = jnp.where(slot_row == want, 1.0, 0.0).astype(BF16)
            xs_ref[pl.ds(r0, n), :] = _dot(onehot, x_ref[...]).astype(BF16)
            y_ref[pl.ds(r0, n_zero), :] = jnp.zeros((n_zero, d), F32)
            return 0

        lax.fori_loop(0, n_plain, functools.partial(body, base=0, n=sub, n_zero=sub), 0)
        lax.fori_loop(0, merged_once, functools.partial(body, base=merged_base, n=long, n_zero=long_pad), 0)

    def expert(i, _, base, n, align):
        r0 = pl.multiple_of(base + i * n, align)
        y_ref[pl.ds(r0, n), :] += _swiglu_chunk(xs_ref[pl.ds(r0, n), :], wg_ref[0], wu_ref[0], wd_ref[0])
        return 0

    n_whole = jnp.where(rest > tail, n_filled + 1, jnp.where(merged, n_filled - 1, n_filled))
    lax.fori_loop(0, n_whole, functools.partial(expert, base=0, n=sub, align=sub), 0)
    lax.fori_loop(0, merged_once, functools.partial(expert, base=merged_base, n=long, align=sub), 0)
    lax.fori_loop(0, (short & (n_filled == 0)).astype(jnp.int32),
                  functools.partial(expert, base=0, n=tail, align=tail), 0)

    @pl.when(f == pl.num_programs(2) - 1)
    def _scatter():
        tc = MOE_SCATTER_ROWS

        def body(i, _, base, n):
            r0 = pl.multiple_of(base + i * sub, sub)
            y = y_ref[pl.ds(r0, n), :].astype(BF16)
            want = (r0 + lax.broadcasted_iota(jnp.int32, (1, n), 1)).astype(F32)
            for j in range(tm // tc):
                rows = slice(j * tc, (j + 1) * tc)
                slot = jnp.concatenate([slot_e_ref[rows, :]] * (n // V7X_LANES), axis=1)
                onehot = jnp.where(slot == want, 1.0, 0.0).astype(BF16)
                weight = jnp.concatenate([gate_e_ref[rows, :]] * (d // V7X_LANES), axis=1)
                o_ref[rows, :] += weight * _dot(onehot, y)
            return 0

        lax.fori_loop(0, n_plain, functools.partial(body, base=0, n=sub), 0)
        lax.fori_loop(0, merged_once, functools.partial(body, base=merged_base, n=long_pad), 0)


def _moe_ffn(hn, router, wg, wu, wd):
    t, d = hn.shape
    tm = min(MOE_ROW_TILE, t)
    tf = MOE_COL_TILE
    nf = D_FF // tf
    router = jnp.pad(router, ((0, 0), (0, V7X_LANES - N_EXPERTS))).astype(BF16)
    once = pl.Buffered(1)
    return pl.pallas_call(
        _moe_kernel,
        grid=(t // tm, N_EXPERTS, nf),
        in_specs=[pl.BlockSpec((tm, d), lambda i, e, f: (i, 0), pipeline_mode=once),
                  pl.BlockSpec((d, V7X_LANES), lambda i, e, f: (0, 0), pipeline_mode=once),
                  pl.BlockSpec((1, d, tf), lambda i, e, f: (e, 0, f)),
                  pl.BlockSpec((1, d, tf), lambda i, e, f: (e, 0, f)),
                  pl.BlockSpec((1, tf, d), lambda i, e, f: (e, f, 0))],
        out_specs=pl.BlockSpec((tm, d), lambda i, e, f: (i, 0), pipeline_mode=once),
        out_shape=jax.ShapeDtypeStruct((t, d), F32),
        scratch_shapes=[pltpu.VMEM((tm, d), BF16), pltpu.VMEM((tm, d), F32),
                        pltpu.VMEM((tm, V7X_LANES), F32), pltpu.VMEM((tm, V7X_LANES), F32),
                        pltpu.VMEM((N_EXPERTS, tm), F32),
                        pltpu.VMEM((tm, V7X_LANES), F32), pltpu.VMEM((tm, V7X_LANES), F32),
                        pltpu.SMEM((N_EXPERTS,), jnp.int32)],
        compiler_params=_params(("parallel", "arbitrary", "arbitrary"), vmem=MOE_VMEM_LIMIT),
        name="moe_ffn",
    )(hn, router, wg.astype(BF16), wu.astype(BF16), wd.astype(BF16))


def _ple_kernel(h_ref, f_ref, p_ref, g_ref, proj_ref, gate_ref, gf_ref, o_ref, *, final_norm):
    h = _ple_rows(h_ref[...] + f_ref[...], p_ref[...], g_ref[...], proj_ref, gate_ref)
    if final_norm:
        h = _rms(h, gf_ref[...])
    o_ref[...] = h


def _ple(h, f, p, g, proj, gate, g_final, final_norm):
    t, d = h.shape
    tm = min(STREAM_ROW_TILE, t)
    vec = pl.BlockSpec((1, d), lambda i: (0, 0))
    return pl.pallas_call(
        functools.partial(_ple_kernel, final_norm=final_norm),
        grid=(t // tm,),
        in_specs=[pl.BlockSpec((tm, d), lambda i: (i, 0)),
                  pl.BlockSpec((tm, d), lambda i: (i, 0)),
                  pl.BlockSpec((tm, PLE_DIM), lambda i: (i, 0)),
                  vec,
                  pl.BlockSpec((PLE_DIM, d), lambda i: (0, 0)),
                  pl.BlockSpec((d, d), lambda i: (0, 0)),
                  vec],
        out_specs=pl.BlockSpec((tm, d), lambda i: (i, 0)),
        out_shape=jax.ShapeDtypeStruct((t, d), F32),
        compiler_params=_params(("parallel",)),
        name="ple",
    )(h, f, p, g.reshape(1, d), proj.astype(BF16), gate.astype(BF16), g_final.reshape(1, d))


def kernel(x, p, w_in, w_out, g_mix, g_ffn, g_ple, g_final, cmp_pos, cmp_w1, cmp_w2, ret_gn,
           ffn_gate, ffn_up, ffn_down, moe_router, moe_gate, moe_up, moe_down, ple_proj, ple_gate):
    b, s, d = x.shape
    depth = w_in.shape[0]
    t = b * s
    h = x
    for i in range(depth):
        qn, kcv, ks, kv3, gl, ret = _in_proj(h.reshape(b, s, d), g_mix[i], _arrange_w_in(w_in[i]))
        kv_cmp = _compress(kcv, cmp_pos[i], cmp_w1[i], cmp_w2[i])
        a = _nsa_attention(qn, kv_cmp, ks, kv3, gl)
        r = _retention(ret, ret_gn[i])
        h, a, r, p_i = h.reshape(t, d), a.reshape(t, NSA_Q_COLS), r.reshape(t, RET_COLS), p[i].reshape(t, PLE_DIM)
        last = i == depth - 1
        if i % 2 == 0:
            h = _dense_layer(h, a, r, p_i, w_out[i], g_ffn[i], ffn_gate[i // 2], ffn_up[i // 2], ffn_down[i // 2],
                             g_ple[i], ple_proj[i], ple_gate[i], g_final, last)
        else:
            h, hn = _out_proj(h, a, r, w_out[i], g_ffn[i])
            f = _moe_ffn(hn, moe_router[i // 2], moe_gate[i // 2], moe_up[i // 2], moe_down[i // 2])
            h = _ple(h, f, p_i, g_ple[i], ple_proj[i], ple_gate[i], g_final, last)
    return h.reshape(b, s, d)
```
